```python
import math
import jax, jax.numpy as jnp
from jax import lax
import numpy as np

D_MODEL = 1024
BATCH = 4
SEQ = 4096
DEPTH = 2

N_MEM = 256
DN_HEADS = 4
DN_HEAD_DIM = 128
DN_WIDTH = DN_HEADS * DN_HEAD_DIM
DN_CONV = 4
DN_CHUNK = 64
DA_HEADS = 8
DA_HEAD_DIM = 64
DA_WIDTH = DA_HEADS * DA_HEAD_DIM
DA_BRANCHES = ((128, 1), (512, 4), (2048, 16))
DA_BLOCK = 128
MIX_WIDTH = DN_WIDTH + DA_WIDTH
AB_IN = 4 * DN_WIDTH + 2 * DN_HEADS + 3 * DA_WIDTH
SC_WIDTH = D_MODEL
SC_CONV = 3
XA_HEADS = 4
XA_HEAD_DIM = D_MODEL // XA_HEADS
N_EXPERTS = 16
N_GROUPS = 4
EXPERTS_PER_GROUP = N_EXPERTS // N_GROUPS
TOP_K = 2
D_EXPERT = D_MODEL // 2
DEEPNORM_ALPHA = (2 * DEPTH) ** 0.25
DEEPNORM_BETA = (8 * DEPTH) ** -0.25
LN_EPS = 1e-5
RMS_EPS = 1e-6
N_EVEN = (DEPTH + 1) // 2
N_ODD = DEPTH // 2

kernel_name = 'hybrid_deltanet_dilated_shortconv_moe'


def layer_norm(x, g, b):
    xf = x.astype(jnp.float32)
    mu = jnp.mean(xf, -1, keepdims=True)
    var = jnp.mean(jnp.square(xf - mu), -1, keepdims=True)
    return ((xf - mu) * lax.rsqrt(var + LN_EPS) * g.astype(jnp.float32) + b.astype(jnp.float32)).astype(x.dtype)


def causal_dwconv(x, w):
    k, c = w.shape
    return lax.conv_general_dilated(x, w[:, None, :].astype(x.dtype), window_strides=(1,), padding=[(k - 1, 0)], dimension_numbers=('NWC', 'WIO', 'NWC'), feature_group_count=c)


def l2norm(t):
    return t * lax.rsqrt(jnp.sum(t * t, -1, keepdims=True) + RMS_EPS)


def gated_delta_rule_chunked(q, k, v, g, beta):
    b, h, s, dk = q.shape
    dv = v.shape[-1]
    n = s // DN_CHUNK
    q, k, v = (t.reshape(b, h, n, DN_CHUNK, t.shape[-1]) for t in (q, k, v))
    g = jnp.cumsum(g.reshape(b, h, n, DN_CHUNK), -1)
    beta = beta.reshape(b, h, n, DN_CHUNK)
    incl = jnp.tril(jnp.ones((DN_CHUNK, DN_CHUNK), bool))
    strict = jnp.tril(jnp.ones((DN_CHUNK, DN_CHUNK), bool), -1)
    decay = jnp.exp(jnp.where(incl, g[..., :, None] - g[..., None, :], -jnp.inf))
    kb = k * beta[..., None]
    a_mat = jnp.where(strict, jnp.einsum('bhnik,bhnjk->bhnij', kb, k) * decay, 0.0)
    lower = a_mat + jnp.eye(DN_CHUNK, dtype=q.dtype)
    rhs = jnp.concatenate([v * beta[..., None], kb * jnp.exp(g)[..., None]], -1)
    sol = lax.linalg.triangular_solve(lower, rhs, left_side=True, lower=True, unit_diagonal=True)
    u, w = sol[..., :dv], sol[..., dv:]
    attn = jnp.where(incl, jnp.einsum('bhnik,bhnjk->bhnij', q, k) * decay, 0.0)
    q_dec = q * jnp.exp(g)[..., None]
    k_tail = k * jnp.exp(g[..., -1:] - g)[..., None]
    g_last = jnp.exp(g[..., -1])

    def step(state, xs):
        q_i, k_i, u_i, w_i, a_i, gl_i = xs
        v_new = u_i - jnp.einsum('bhck,bhkv->bhcv', w_i, state)
        o_i = jnp.einsum('bhck,bhkv->bhcv', q_i, state) + jnp.einsum('bhcj,bhjv->bhcv', a_i, v_new)
        state = state * gl_i[..., None, None] + jnp.einsum('bhck,bhcv->bhkv', k_i, v_new)
        return state, o_i

    xs = tuple(jnp.moveaxis(t, 2, 0) for t in (q_dec, k_tail, u, w, attn, g_last))
    _, o = lax.scan(step, jnp.zeros((b, h, dk, dv), q.dtype), xs)
    return jnp.moveaxis(o, 0, 2).reshape(b, h, s, dv)


def dilated_branch(q, k, v, window, dilation):
    b, h, s, dh = q.shape
    length = s // dilation
    span = window // dilation
    n_blk = -(-length // DA_BLOCK)
    padded = n_blk * DA_BLOCK

    def to_blocks(t):
        t = t.reshape(b, h, length, dilation, dh).transpose(0, 1, 3, 2, 4)
        t = jnp.pad(t, ((0, 0), (0, 0), (0, 0), (0, padded - length), (0, 0)))
        return t.reshape(b, h, dilation, n_blk, DA_BLOCK, dh)

    def with_prev(t):
        prev = jnp.pad(t, ((0, 0), (0, 0), (0, 0), (1, 0), (0, 0), (0, 0)))[:, :, :, :-1]
        return jnp.concatenate([prev, t], axis=4)

    qb = to_blocks(q)
    kw = with_prev(to_blocks(k))
    vw = with_prev(to_blocks(v))
    sc = jnp.einsum('bhrnqd,bhrnkd->bhrnqk', qb, kw) * (dh ** -0.5)
    qpos = jnp.arange(n_blk)[:, None] * DA_BLOCK + jnp.arange(DA_BLOCK)[None, :]
    kpos = (jnp.arange(n_blk)[:, None] - 1) * DA_BLOCK + jnp.arange(2 * DA_BLOCK)[None, :]
    dist = qpos[:, :, None] - kpos[:, None, :]
    valid = (dist >= 0) & (dist <= span) & (kpos[:, None, :] >= 0)
    sc = jnp.where(valid, sc, -jnp.inf)
    m = jnp.max(sc, -1, keepdims=True)
    p = jnp.exp(sc - m)
    den = jnp.sum(p, -1)
    o = jnp.einsum('bhrnqk,bhrnkd->bhrnqd', p, vw) / den[..., None]
    lse = m[..., 0] + jnp.log(den)
    o = o.reshape(b, h, dilation, padded, dh)[:, :, :, :length].transpose(0, 1, 3, 2, 4).reshape(b, h, s, dh)
    lse = lse.reshape(b, h, dilation, padded)[..., :length].transpose(0, 1, 3, 2).reshape(b, h, s)
    return o, lse


def dilated_attention(q, k, v):
    outs, lses = zip(*[dilated_branch(q, k, v, wnd, dil) for wnd, dil in DA_BRANCHES])
    wts = jax.nn.softmax(jnp.stack(lses), axis=0)
    return jnp.einsum('ibhs,ibhsd->bhsd', wts, jnp.stack(outs))


def mixer_ab(x, w_in, conv_w, a_log, dt_bias, norm_w, w_out):
    b, s, _ = x.shape
    f32 = jnp.float32
    qkv, z, beta_in, a_in, qkv_d = jnp.split(x @ w_in, [3 * DN_WIDTH, 4 * DN_WIDTH, 4 * DN_WIDTH + DN_HEADS, 4 * DN_WIDTH + 2 * DN_HEADS], axis=-1)
    qkv = jax.nn.silu(causal_dwconv(qkv, conv_w)).astype(f32)
    q, k, v = (t.reshape(b, s, DN_HEADS, DN_HEAD_DIM).transpose(0, 2, 1, 3) for t in jnp.split(qkv, 3, -1))
    q = l2norm(q) * (DN_HEAD_DIM ** -0.5)
    k = l2norm(k)
    beta = jax.nn.sigmoid(beta_in.astype(f32)).transpose(0, 2, 1)
    g = (-jnp.exp(a_log.astype(f32)) * jax.nn.softplus(a_in.astype(f32) + dt_bias.astype(f32))).transpose(0, 2, 1)
    o = gated_delta_rule_chunked(q, k, v, g, beta).transpose(0, 2, 1, 3)
    o = o * lax.rsqrt(jnp.mean(o * o, -1, keepdims=True) + RMS_EPS) * norm_w.astype(f32)
    o_a = (o * jax.nn.silu(z.astype(f32).reshape(b, s, DN_HEADS, DN_HEAD_DIM))).reshape(b, s, DN_WIDTH)
    qd, kd, vd = (t.astype(f32).reshape(b, s, DA_HEADS, DA_HEAD_DIM).transpose(0, 2, 1, 3) for t in jnp.split(qkv_d, 3, -1))
    o_b = dilated_attention(qd, kd, vd).transpose(0, 2, 1, 3).reshape(b, s, DA_WIDTH)
    return jnp.concatenate([o_a, o_b], -1).astype(x.dtype) @ w_out


def short_conv_mixer(x, w_in, conv_w, w_out):
    gate_b, gate_c, hid = jnp.split(x @ w_in, 3, axis=-1)
    return (gate_b * causal_dwconv(gate_c * hid, conv_w)) @ w_out


def memory_cross_attention(x, mem, w_q, w_kv, w_o):
    b, s, d = x.shape
    q = (x @ w_q).astype(jnp.float32).reshape(b, s, XA_HEADS, XA_HEAD_DIM)
    k, v = (t.astype(jnp.float32).reshape(b, -1, XA_HEADS, XA_HEAD_DIM) for t in jnp.split(mem @ w_kv, 2, -1))
    p = jax.nn.softmax(jnp.einsum('bshd,bmhd->bhsm', q, k) * (XA_HEAD_DIM ** -0.5), axis=-1)
    o = jnp.einsum('bhsm,bmhd->bshd', p, v).reshape(b, s, d).astype(x.dtype)
    return o @ w_o


def grouped_moe(x, router_w, router_b, w_gate, w_up, w_down):
    b, s, d = x.shape
    xt = x.reshape(b * s, d)
    scores = jax.nn.softmax((xt @ router_w).astype(jnp.float32), axis=-1)
    sel = scores + router_b.astype(jnp.float32)
    group_score = jnp.sum(lax.top_k(sel.reshape(-1, N_GROUPS, EXPERTS_PER_GROUP), TOP_K)[0], -1)
    group = jnp.argmax(group_score, -1)
    in_group = (jnp.arange(N_EXPERTS) // EXPERTS_PER_GROUP)[None, :] == group[:, None]
    _, idx = lax.top_k(jnp.where(in_group, sel, -jnp.inf), TOP_K)
    wts = jnp.take_along_axis(scores, idx, -1)
    wts = wts / jnp.sum(wts, -1, keepdims=True)
    gates = jnp.einsum('tk,tke->te', wts, jax.nn.one_hot(idx, N_EXPERTS, dtype=jnp.float32)).astype(x.dtype)
    y = jnp.zeros_like(xt)
    for e in range(N_EXPERTS):
        hdn = jax.nn.silu(xt @ w_gate[e]) * (xt @ w_up[e])
        y = y + gates[:, e:e + 1] * (hdn @ w_down[e])
    return y.reshape(b, s, d)


def setup_inputs(seed: int = 0) -> dict:
    key = jax.random.key(seed)
    ks = jax.random.split(key, 24)
    f32 = jnp.float32

    def nrm(k, shape, scale):
        return scale * jax.random.normal(k, shape, f32)

    d = D_MODEL
    dt = jnp.exp(jax.random.uniform(ks[5], (N_EVEN, DN_HEADS), f32, math.log(1e-3), math.log(1e-1)))
    return {
        'x': nrm(ks[0], (BATCH, SEQ, d), 1.0),
        'mem': nrm(ks[1], (BATCH, N_MEM, d), 1.0),
        'ab_w_in': nrm(ks[2], (N_EVEN, d, AB_IN), d ** -0.5),
        'ab_conv_w': nrm(ks[3], (N_EVEN, DN_CONV, 3 * DN_WIDTH), DN_CONV ** -0.5),
        'ab_a_log': jnp.log(jax.random.uniform(ks[4], (N_EVEN, DN_HEADS), f32, 1.0, 16.0)),
        'ab_dt_bias': dt + jnp.log(-jnp.expm1(-dt)),
        'ab_norm_w': 1.0 + nrm(ks[6], (N_EVEN, DN_HEAD_DIM), 0.02),
        'ab_w_out': nrm(ks[7], (N_EVEN, MIX_WIDTH, d), DEEPNORM_BETA * MIX_WIDTH ** -0.5),
        'sc_w_in': nrm(ks[8], (N_ODD, d, 3 * SC_WIDTH), d ** -0.5),
        'sc_conv_w': nrm(ks[9], (N_ODD, SC_CONV, SC_WIDTH), SC_CONV ** -0.5),
        'sc_w_out': nrm(ks[10], (N_ODD, SC_WIDTH, d), DEEPNORM_BETA * SC_WIDTH ** -0.5),
        'xa_w_q': nrm(ks[11], (DEPTH, d, d), d ** -0.5),
        'xa_w_kv': nrm(ks[12], (DEPTH, d, 2 * d), d ** -0.5),
        'xa_w_o': nrm(ks[13], (DEPTH, d, d), DEEPNORM_BETA * d ** -0.5),
        'router_w': nrm(ks[14], (d, N_EXPERTS), d ** -0.5),
        'router_b': nrm(ks[15], (N_EXPERTS,), 0.01),
        'moe_w_gate': nrm(ks[16], (DEPTH, N_EXPERTS, d, D_EXPERT), d ** -0.5),
        'moe_w_up': nrm(ks[17], (DEPTH, N_EXPERTS, d, D_EXPERT), d ** -0.5),
        'moe_w_down': nrm(ks[18], (DEPTH, N_EXPERTS, D_EXPERT, d), DEEPNORM_BETA * D_EXPERT ** -0.5),
        'ln_g': 1.0 + nrm(ks[19], (DEPTH, 3, d), 0.02),
        'ln_b': nrm(ks[20], (DEPTH, 3, d), 0.02),
    }


def reference(x, mem, ab_w_in, ab_conv_w, ab_a_log, ab_dt_bias, ab_norm_w, ab_w_out, sc_w_in, sc_conv_w, sc_w_out, xa_w_q, xa_w_kv, xa_w_o, router_w, router_b, moe_w_gate, moe_w_up, moe_w_down, ln_g, ln_b):
    h = x
    for layer in range(DEPTH):
        i = layer // 2
        if layer % 2 == 0:
            y = mixer_ab(h, ab_w_in[i], ab_conv_w[i], ab_a_log[i], ab_dt_bias[i], ab_norm_w[i], ab_w_out[i])
        else:
            y = short_conv_mixer(h, sc_w_in[i], sc_conv_w[i], sc_w_out[i])
        h = layer_norm(DEEPNORM_ALPHA * h + y, ln_g[layer, 0], ln_b[layer, 0])
        y = memory_cross_attention(h, mem, xa_w_q[layer], xa_w_kv[layer], xa_w_o[layer])
        h = layer_norm(DEEPNORM_ALPHA * h + y, ln_g[layer, 1], ln_b[layer, 1])
        y = grouped_moe(h, router_w, router_b, moe_w_gate[layer], moe_w_up[layer], moe_w_down[layer])
        h = layer_norm(DEEPNORM_ALPHA * h + y, ln_g[layer, 2], ln_b[layer, 2])
    return h
```

```python
import functools

import jax
import jax.numpy as jnp
from jax import lax
from jax.experimental import pallas as pl
from jax.experimental.pallas import tpu as pltpu

D_MODEL = 1024
DEPTH = 2
DN_HEADS = 4
DN_HEAD_DIM = 128
DN_WIDTH = DN_HEADS * DN_HEAD_DIM
DN_CONV = 4
DN_CHUNK = 64
DA_HEADS = 8
DA_HEAD_DIM = 64
DA_WIDTH = DA_HEADS * DA_HEAD_DIM
DA_BRANCHES = ((128, 1), (512, 4), (2048, 16))
DA_BLOCK = 128
SC_CONV = 3
XA_HEADS = 4
XA_HEAD_DIM = D_MODEL // XA_HEADS
N_EXPERTS = 16
N_GROUPS = 4
EXPERTS_PER_GROUP = N_EXPERTS // N_GROUPS
D_EXPERT = D_MODEL // 2
DEEPNORM_ALPHA = (2 * DEPTH) ** 0.25
LN_EPS = 1e-5
RMS_EPS = 1e-6

LANES = 128
SUBLANES = 8
VMEM_LIMIT_BYTES = 48 * 1024 * 1024

TOKEN_TILE = 512
MOE_TILE = 256
COMBINE_TILE = 256
DISPATCH_CHUNK = 512
GATE_LANES = LANES
ROW_EXT = D_MODEL + GATE_LANES

_PAIRS = tuple((i, j) for i in range(EXPERTS_PER_GROUP) for j in range(i + 1, EXPERTS_PER_GROUP))
N_BUCKETS = N_GROUPS * len(_PAIRS)
BUCKET_ROWS = 32

NEG_BIG = -1e30

bf16 = jnp.bfloat16
f32 = jnp.float32


def _params(semantics):
    return pltpu.CompilerParams(dimension_semantics=semantics, vmem_limit_bytes=VMEM_LIMIT_BYTES)


def _dot(a, b):
    return jnp.dot(a.astype(bf16), b.astype(bf16), preferred_element_type=f32)


def _dot_nt(a, b):
    return lax.dot_general(a.astype(bf16), b.astype(bf16), (((1,), (1,)), ((), ())), preferred_element_type=f32)


def _dot_tn(a, b):
    return lax.dot_general(a.astype(bf16), b.astype(bf16), (((0,), (0,)), ((), ())), preferred_element_type=f32)


def _split(a, parts):
    out = []
    rem = a
    for _ in range(parts):
        p = rem.astype(bf16)
        out.append(p)
        rem = rem - p.astype(f32)
    return out


def _layer_norm(v, g, b):
    mu = jnp.mean(v, -1, keepdims=True)
    c = v - mu
    var = jnp.mean(c * c, -1, keepdims=True)
    return c * lax.rsqrt(var + LN_EPS) * g + b


def _silu(v):
    return v * jax.nn.sigmoid(v)


def _softplus(v):
    return jnp.maximum(v, 0.0) + jnp.log(1.0 + jnp.exp(-jnp.abs(v)))


def _proj_kernel(x_ref, *refs, n_out):
    w_refs, o_refs = refs[:n_out], refs[n_out:]
    xb = x_ref[...].astype(bf16)
    for w_ref, o_ref in zip(w_refs, o_refs):
        n = w_ref.shape[1]
        for c0 in range(0, n, 512):
            c1 = min(c0 + 512, n)
            o_ref[:, c0:c1] = jnp.dot(xb, w_ref[:, c0:c1], preferred_element_type=f32).astype(o_ref.dtype)


def _project(x, ws, out_dtypes, tm):
    t, k = x.shape
    n_out = len(ws)
    return pl.pallas_call(
        functools.partial(_proj_kernel, n_out=n_out),
        grid=(t // tm,),
        in_specs=[pl.BlockSpec((tm, k), lambda i: (i, 0))]
        + [pl.BlockSpec(w.shape, lambda i: (0, 0)) for w in ws],
        out_specs=[pl.BlockSpec((tm, w.shape[1]), lambda i: (i, 0)) for w in ws],
        out_shape=[jax.ShapeDtypeStruct((t, w.shape[1]), dt) for w, dt in zip(ws, out_dtypes)],
        compiler_params=_params(("parallel",)),
        name="project",
    )(x, *ws)


def _deltanet_kernel(qkv_ref, ba_ref, z_ref, cw_ref, alog_ref, dtb_ref, nw_ref, o_ref, state_ref, tail_ref):
    c = pl.program_id(1)

    @pl.when(c == 0)
    def _():
        state_ref[...] = jnp.zeros_like(state_ref)
        tail_ref[...] = jnp.zeros_like(tail_ref)

    ch, hd, width = DN_CHUNK, DN_HEAD_DIM, DN_WIDTH
    x = qkv_ref[...]
    xx = jnp.concatenate([tail_ref[...], x], axis=0)
    cw = cw_ref[...]
    conv = x * cw[DN_CONV - 1:DN_CONV, :]
    for j in range(DN_CONV - 1):
        back = DN_CONV - 1 - j
        conv = conv + xx[SUBLANES - back:SUBLANES - back + ch, :] * cw[j:j + 1, :]
    tail_ref[...] = x[ch - SUBLANES:, :]
    act = _silu(conv)

    ba = ba_ref[...]
    beta_all = jax.nn.sigmoid(ba)
    g_all = -jnp.exp(alog_ref[...]) * _softplus(ba + dtb_ref[...])

    ri = lax.broadcasted_iota(jnp.int32, (ch, ch), 0)
    ci = lax.broadcasted_iota(jnp.int32, (ch, ch), 1)
    incl = ri >= ci
    strict = ri > ci
    lower_ones = jnp.where(incl, 1.0, 0.0).astype(bf16)
    eye = jnp.where(ri == ci, 1.0, 0.0)
    lane0 = lax.broadcasted_iota(jnp.int32, (ch, hd), 1) == 0
    ones_nt = jnp.ones((ch, hd), bf16)
    z = z_ref[...]
    nw = nw_ref[...]

    for h in range(DN_HEADS):
        q = act[:, h * hd:(h + 1) * hd]
        k = act[:, width + h * hd:width + (h + 1) * hd]
        v = act[:, 2 * width + h * hd:2 * width + (h + 1) * hd]
        q = q * lax.rsqrt(jnp.sum(q * q, -1, keepdims=True) + RMS_EPS) * (hd ** -0.5)
        k = k * lax.rsqrt(jnp.sum(k * k, -1, keepdims=True) + RMS_EPS)
        beta_b = jnp.broadcast_to(beta_all[:, h:h + 1], (ch, hd))
        g_b = jnp.broadcast_to(g_all[:, DN_HEADS + h:DN_HEADS + h + 1], (ch, hd))
        g_cum = sum(jnp.dot(lower_ones, p, preferred_element_type=f32) for p in _split(g_b, 2))
        gj = sum(
            lax.dot_general(ones_nt, jnp.where(lane0, p, jnp.zeros_like(p)), (((1,), (1,)), ((), ())),
                            preferred_element_type=f32)
            for p in _split(g_cum, 3))
        decay = jnp.where(incl, jnp.exp(jnp.minimum(g_cum[:, :ch] - gj, 0.0)), 0.0)
        e_g = jnp.exp(g_cum)
        kb = k * beta_b
        a_mat = jnp.where(strict, _dot_nt(kb, k) * decay, 0.0)
        t_inv = eye - a_mat
        pw = a_mat
        for _ in range(5):
            pw = _dot(pw, pw)
            t_inv = t_inv + _dot(t_inv, pw)
        sol = _dot(t_inv, jnp.concatenate([v * beta_b, kb * e_g], axis=1))
        u, w = sol[:, :hd], sol[:, hd:]
        attn = jnp.where(incl, _dot_nt(q, k) * decay, 0.0)
        g_last = g_cum[ch - 1:ch, :]
        k_tail = k * jnp.exp(g_last - g_cum)
        state = state_ref[h]
        v_new = u - _dot(w, state)
        o = _dot(q * e_g, state) + _dot(attn, v_new)
        state_ref[h] = state * jnp.exp(g_last) + _dot_tn(k_tail, v_new)
        o = o * lax.rsqrt(jnp.mean(o * o, -1, keepdims=True) + RMS_EPS) * nw
        o_ref[:, h * hd:(h + 1) * hd] = (o * _silu(z[:, h * hd:(h + 1) * hd])).astype(o_ref.dtype)


def _deltanet(qkv, ba, z, conv_w, alog_row, dtb_row, norm_w, batch, seq):
    nc = seq // DN_CHUNK
    row = lambda b, c: (b * nc + c, 0)
    fixed = lambda b, c: (0, 0)
    return pl.pallas_call(
        _deltanet_kernel,
        grid=(batch, nc),
        in_specs=[
            pl.BlockSpec((DN_CHUNK, 3 * DN_WIDTH), row),
            pl.BlockSpec((DN_CHUNK, LANES), row),
            pl.BlockSpec((DN_CHUNK, DN_WIDTH), row),
            pl.BlockSpec(conv_w.shape, fixed),
            pl.BlockSpec((1, LANES), fixed),
            pl.BlockSpec((1, LANES), fixed),
            pl.BlockSpec((1, DN_HEAD_DIM), fixed),
        ],
        out_specs=pl.BlockSpec((DN_CHUNK, DN_WIDTH), row),
        out_shape=jax.ShapeDtypeStruct((batch * seq, DN_WIDTH), bf16),
        scratch_shapes=[
            pltpu.VMEM((DN_HEADS, DN_HEAD_DIM, DN_HEAD_DIM), f32),
            pltpu.VMEM((SUBLANES, 3 * DN_WIDTH), f32),
        ],
        compiler_params=_params(("arbitrary", "arbitrary")),
        name="deltanet",
    )(qkv, ba, z, conv_w, alog_row, dtb_row, norm_w)


def _dilated_kernel(q_ref, k_ref, v_ref, o_ref, acc_ref, m_ref, l_ref, *, seq):
    blk = DA_BLOCK
    acc_ref[...] = jnp.zeros_like(acc_ref)
    l_ref[...] = jnp.zeros_like(l_ref)
    m_ref[...] = jnp.full(m_ref.shape, NEG_BIG, f32)
    head0 = lax.broadcasted_iota(jnp.int32, (blk, LANES), 1) < DA_HEAD_DIM
    ri = lax.broadcasted_iota(jnp.int32, (blk, blk), 0)
    ci = lax.broadcasted_iota(jnp.int32, (blk, blk), 1)
    cur_ok = ci <= ri
    prev_ok = ci >= ri
    scale = DA_HEAD_DIM ** -0.5

    for window, dil in DA_BRANCHES:
        assert window // dil == blk
        nblk = seq // (dil * blk)

        def body(idx, carry, dil=dil, nblk=nblk):
            r = idx // nblk
            n = idx - r * nblk
            start = r + n * (dil * blk)
            pstart = r + jnp.maximum(n - 1, 0) * (dil * blk)
            if dil == 1:
                cur, prev = pl.ds(start, blk), pl.ds(pstart, blk)
            else:
                cur, prev = pl.ds(start, blk, stride=dil), pl.ds(pstart, blk, stride=dil)
            q = q_ref[0, cur, :].astype(bf16)
            kc = k_ref[0, cur, :].astype(bf16)
            vc = v_ref[0, cur, :].astype(bf16)
            kp = k_ref[0, prev, :].astype(bf16)
            vp = v_ref[0, prev, :].astype(bf16)
            has_prev = n > 0
            stats = []
            for first in (True, False):
                hm = head0 if first else jnp.logical_not(head0)
                zero = jnp.zeros_like(kc)
                s_c = _dot_nt(q, jnp.where(hm, kc, zero)) * scale
                s_p = _dot_nt(q, jnp.where(hm, kp, zero)) * scale
                s_c = jnp.where(cur_ok, s_c, NEG_BIG)
                s_p = jnp.where(jnp.logical_and(prev_ok, has_prev), s_p, NEG_BIG)
                mb = jnp.maximum(jnp.max(s_c, -1, keepdims=True), jnp.max(s_p, -1, keepdims=True))
                p_c = jnp.exp(s_c - mb)
                p_p = jnp.exp(s_p - mb)
                lb = jnp.sum(p_c, -1, keepdims=True) + jnp.sum(p_p, -1, keepdims=True)
                ob = _dot(p_c, vc) + _dot(p_p, vp)
                stats.append((mb, lb, ob))
            m_blk = jnp.where(head0, stats[0][0], stats[1][0])
            l_blk = jnp.where(head0, stats[0][1], stats[1][1])
            o_blk = jnp.where(head0, stats[0][2], stats[1][2])
            m_old = m_ref[cur, :]
            m_new = jnp.maximum(m_old, m_blk)
            a_old = jnp.exp(m_old - m_new)
            a_blk = jnp.exp(m_blk - m_new)
            acc_ref[cur, :] = acc_ref[cur, :] * a_old + o_blk * a_blk
            l_ref[cur, :] = l_ref[cur, :] * a_old + l_blk * a_blk
            m_ref[cur, :] = m_new
            return carry

        lax.fori_loop(0, dil * nblk, body, 0)

    o_ref[0] = (acc_ref[...] / l_ref[...]).astype(o_ref.dtype)


def _dilated_attention(qkv3, batch, seq):
    pairs = DA_WIDTH // LANES
    spec = lambda off: pl.BlockSpec((1, seq, LANES), lambda b, p, off=off: (b, 0, off + p))
    return pl.pallas_call(
        functools.partial(_dilated_kernel, seq=seq),
        grid=(batch, pairs),
        in_specs=[spec(0), spec(pairs), spec(2 * pairs)],
        out_specs=pl.BlockSpec((1, seq, LANES), lambda b, p: (b, 0, p)),
        out_shape=jax.ShapeDtypeStruct((batch, seq, DA_WIDTH), bf16),
        scratch_shapes=[pltpu.VMEM((seq, LANES), f32)] * 3,
        compiler_params=_params(("parallel", "parallel")),
        name="dilated_attention",
    )(qkv3, qkv3, qkv3)


def _mix_out_kernel(oa_ref, ob_ref, h_ref, w_ref, g_ref, b_ref, o_ref):
    w = w_ref[...]
    y = jnp.dot(oa_ref[...], w[:DN_WIDTH], preferred_element_type=f32)
    y = y + jnp.dot(ob_ref[...], w[DN_WIDTH:], preferred_element_type=f32)
    o_ref[...] = _layer_norm(DEEPNORM_ALPHA * h_ref[...] + y, g_ref[...], b_ref[...])


def _mix_out(o_a, o_b, h, w_out, g, b):
    t = h.shape[0]
    tm = TOKEN_TILE
    fixed = lambda i: (0, 0)
    return pl.pallas_call(
        _mix_out_kernel,
        grid=(t // tm,),
        in_specs=[
            pl.BlockSpec((tm, DN_WIDTH), lambda i: (i, 0)),
            pl.BlockSpec((tm, DA_WIDTH), lambda i: (i, 0)),
            pl.BlockSpec((tm, D_MODEL), lambda i: (i, 0)),
            pl.BlockSpec(w_out.shape, fixed),
            pl.BlockSpec((1, D_MODEL), fixed),
            pl.BlockSpec((1, D_MODEL), fixed),
        ],
        out_specs=pl.BlockSpec((tm, D_MODEL), lambda i: (i, 0)),
        out_shape=jax.ShapeDtypeStruct((t, D_MODEL), f32),
        compiler_params=_params(("parallel",)),
        name="mix_out",
    )(o_a, o_b, h, w_out, g, b)


def _shortconv_kernel(h_ref, win_ref, cw_ref, wout_ref, g_ref, b_ref, o_ref, tail_ref):
    @pl.when(pl.program_id(1) == 0)
    def _():
        tail_ref[...] = jnp.zeros_like(tail_ref)

    tm = h_ref.shape[0]
    d = D_MODEL
    h = h_ref[...]
    hb = h.astype(bf16)
    gate_b = jnp.dot(hb, win_ref[:, :d], preferred_element_type=f32)
    gate_c = jnp.dot(hb, win_ref[:, d:2 * d], preferred_element_type=f32)
    hid = jnp.dot(hb, win_ref[:, 2 * d:], preferred_element_type=f32)
    u = gate_c * hid
    uu = jnp.concatenate([tail_ref[...], u], axis=0)
    cw = cw_ref[...]
    conv = u * cw[SC_CONV - 1:SC_CONV, :]
    for j in range(SC_CONV - 1):
        back = SC_CONV - 1 - j
        conv = conv + uu[SUBLANES - back:SUBLANES - back + tm, :] * cw[j:j + 1, :]
    tail_ref[...] = u[tm - SUBLANES:, :]
    y = jnp.dot((gate_b * conv).astype(bf16), wout_ref[...], preferred_element_type=f32)
    o_ref[...] = _layer_norm(DEEPNORM_ALPHA * h + y, g_ref[...], b_ref[...])


def _shortconv(h, w_in, conv_w, w_out, g, b, batch, seq):
    tm = TOKEN_TILE
    ns = seq // tm
    row = lambda bb, s: (bb * ns + s, 0)
    fixed = lambda bb, s: (0, 0)
    return pl.pallas_call(
        _shortconv_kernel,
        grid=(batch, ns),
        in_specs=[
            pl.BlockSpec((tm, D_MODEL), row),
            pl.BlockSpec(w_in.shape, fixed),
            pl.BlockSpec(conv_w.shape, fixed),
            pl.BlockSpec(w_out.shape, fixed),
            pl.BlockSpec((1, D_MODEL), fixed),
            pl.BlockSpec((1, D_MODEL), fixed),
        ],
        out_specs=pl.BlockSpec((tm, D_MODEL), row),
        out_shape=jax.ShapeDtypeStruct((batch * seq, D_MODEL), f32),
        scratch_shapes=[pltpu.VMEM((SUBLANES, D_MODEL), f32)],
        compiler_params=_params(("arbitrary", "arbitrary")),
        name="shortconv",
    )(h, w_in, conv_w, w_out, g, b)


def _route(h2, rwt_ref, rb_ref, cnt_ref):
    tm = h2.shape[0]
    w_hi, w_lo = _split(rwt_ref[...], 2)
    x_hi, x_lo = _split(h2, 2)
    nt = (((1,), (1,)), ((), ()))
    logits = (lax.dot_general(w_hi, x_hi, nt, preferred_element_type=f32)
              + lax.dot_general(w_hi, x_lo, nt, preferred_element_type=f32)
              + lax.dot_general(w_lo, x_hi, nt, preferred_element_type=f32))
    mx = jnp.max(logits, axis=0, keepdims=True)
    ex = jnp.exp(logits - mx)
    scores = ex / jnp.sum(ex, axis=0, keepdims=True)
    sel = scores + rb_ref[...]
    best = jnp.full((1, tm), -jnp.inf, f32)
    bucket = jnp.zeros((1, tm), jnp.int32)
    s_a = jnp.zeros((1, tm), f32)
    s_b = jnp.zeros((1, tm), f32)
    for grp in range(N_GROUPS):
        for p, (i, j) in enumerate(_PAIRS):
            a, b = grp * EXPERTS_PER_GROUP + i, grp * EXPERTS_PER_GROUP + j
            ps = sel[a:a + 1, :] + sel[b:b + 1, :]
            upd = ps > best
            best = jnp.where(upd, ps, best)
            bucket = jnp.where(upd, grp * len(_PAIRS) + p, bucket)
            s_a = jnp.where(upd, scores[a:a + 1, :], s_a)
            s_b = jnp.where(upd, scores[b:b + 1, :], s_b)
    denom = s_a + s_b
    gate_rows = jnp.concatenate([s_a / denom, s_b / denom, jnp.zeros((GATE_LANES - 2, tm), f32)], axis=0)
    gates = jnp.transpose(gate_rows)
    onehot = jnp.where(lax.broadcasted_iota(jnp.int32, (BUCKET_ROWS, tm), 0) == bucket, 1.0, 0.0)
    before = lax.broadcasted_iota(jnp.int32, (tm, tm), 0) < lax.broadcasted_iota(jnp.int32, (tm, tm), 1)
    prefix = jnp.dot(onehot.astype(bf16), jnp.where(before, 1.0, 0.0).astype(bf16), preferred_element_type=f32)
    cnt = cnt_ref[...]
    rank = jnp.sum(onehot * (prefix + cnt[:, 0:1]), axis=0, keepdims=True)
    cnt_ref[...] = cnt + jnp.sum(onehot, axis=1, keepdims=True)
    return gates, bucket, rank.astype(jnp.int32)


def _xattn_kernel(h_ref, kv_ref, wq_ref, wo_ref, g_ref, b_ref, rwt_ref, rb_ref,
                  hext_ref, route_ref, cnt_out_ref, cnt_ref):
    @pl.when(jnp.logical_and(pl.program_id(0) == 0, pl.program_id(1) == 0))
    def _():
        cnt_ref[...] = jnp.zeros_like(cnt_ref)

    h = h_ref[...]
    tm = h.shape[0]
    q = jnp.dot(h.astype(bf16), wq_ref[...], preferred_element_type=f32).astype(bf16)
    outs = []
    for hd in range(XA_HEADS):
        lo, hi = hd * XA_HEAD_DIM, (hd + 1) * XA_HEAD_DIM
        s = _dot_nt(q[:, lo:hi], kv_ref[:, lo:hi]) * (XA_HEAD_DIM ** -0.5)
        p = jnp.exp(s - jnp.max(s, -1, keepdims=True))
        o = _dot(p, kv_ref[:, D_MODEL + lo:D_MODEL + hi]) / jnp.sum(p, -1, keepdims=True)
        outs.append(o.astype(bf16))
    y = jnp.dot(jnp.concatenate(outs, axis=1), wo_ref[...], preferred_element_type=f32)
    h2 = _layer_norm(DEEPNORM_ALPHA * h + y, g_ref[...], b_ref[...])
    gates, bucket, rank = _route(h2, rwt_ref, rb_ref, cnt_ref)
    hext_ref[:, :D_MODEL] = h2
    hext_ref[:, D_MODEL:] = gates
    route_ref[...] = jnp.concatenate([bucket, rank, jnp.zeros((SUBLANES - 2, tm), jnp.int32)], axis=0)
    cnt_out_ref[...] = cnt_ref[...]


def _xattn_route(h, kv, w_q, w_o, g, b, rwt, rb, batch, seq, n_mem):
    tm = TOKEN_TILE
    ns = seq // tm
    t = batch * seq
    row = lambda bb, s: (bb * ns + s, 0)
    fixed = lambda bb, s: (0, 0)
    return pl.pallas_call(
        _xattn_kernel,
        grid=(batch, ns),
        in_specs=[
            pl.BlockSpec((tm, D_MODEL), row),
            pl.BlockSpec((n_mem, 2 * D_MODEL), lambda bb, s: (bb, 0)),
            pl.BlockSpec(w_q.shape, fixed),
            pl.BlockSpec(w_o.shape, fixed),
            pl.BlockSpec((1, D_MODEL), fixed),
            pl.BlockSpec((1, D_MODEL), fixed),
            pl.BlockSpec(rwt.shape, fixed),
            pl.BlockSpec(rb.shape, fixed),
        ],
        out_specs=[
            pl.BlockSpec((tm, ROW_EXT), row),
            pl.BlockSpec((SUBLANES, tm), lambda bb, s: (0, bb * ns + s)),
            pl.BlockSpec((BUCKET_ROWS, LANES), fixed),
        ],
        out_shape=[
            jax.ShapeDtypeStruct((t, ROW_EXT), f32),
            jax.ShapeDtypeStruct((SUBLANES, t), jnp.int32),
            jax.ShapeDtypeStruct((BUCKET_ROWS, LANES), f32),
        ],
        scratch_shapes=[pltpu.VMEM((BUCKET_ROWS, LANES), f32)],
        compiler_params=_params(("arbitrary", "arbitrary")),
        name="xattn_route",
    )(h, kv, w_q, w_o, g, b, rwt, rb)


def _dispatch_kernel(dest_ref, hext_ref, xs_in_ref, xs_ref, sem):
    del xs_in_ref
    base = pl.program_id(0) * DISPATCH_CHUNK

    def issue(j, carry):
        t = base + j
        pltpu.make_async_copy(hext_ref.at[pl.ds(t, 1)], xs_ref.at[pl.ds(dest_ref[t], 1)], sem).start()
        return carry

    lax.fori_loop(0, DISPATCH_CHUNK, issue, 0, unroll=8)
    pltpu.make_async_copy(hext_ref.at[pl.ds(0, DISPATCH_CHUNK)], xs_ref.at[pl.ds(0, DISPATCH_CHUNK)], sem).wait()


def _dispatch(dest, hext, n_rows):
    t = hext.shape[0]
    xs0 = jnp.zeros((n_rows, ROW_EXT), f32)
    return pl.pallas_call(
        _dispatch_kernel,
        grid_spec=pltpu.PrefetchScalarGridSpec(
            num_scalar_prefetch=1,
            grid=(t // DISPATCH_CHUNK,),
            in_specs=[pl.BlockSpec(memory_space=pl.ANY), pl.BlockSpec(memory_space=pl.ANY)],
            out_specs=pl.BlockSpec(memory_space=pl.ANY),
            scratch_shapes=[pltpu.SemaphoreType.DMA(())],
        ),
        out_shape=jax.ShapeDtypeStruct((n_rows, ROW_EXT), f32),
        input_output_aliases={2: 0},
        compiler_params=_params(("arbitrary",)),
        name="moe_dispatch",
    )(dest, hext, xs0)


def _experts_kernel(ea_ref, eb_ref, used_ref, xs_ref, wga_ref, wua_ref, wda_ref, wgb_ref, wub_ref, wdb_ref, y_ref):
    i = pl.program_id(0)

    @pl.when(used_ref[i] > 0)
    def _():
        x = xs_ref[:, :D_MODEL].astype(bf16)
        gates = xs_ref[:, D_MODEL:]

        def ffn(wg_ref, wu_ref, wd_ref, gate):
            hg = jnp.dot(x, wg_ref[0].astype(bf16), preferred_element_type=f32)
            hu = jnp.dot(x, wu_ref[0].astype(bf16), preferred_element_type=f32)
            hdn = _silu(hg) * hu * gate
            return jnp.dot(hdn.astype(bf16), wd_ref[0].astype(bf16), preferred_element_type=f32)

        y_ref[...] = ffn(wga_ref, wua_ref, wda_ref, gates[:, 0:1]) + ffn(wgb_ref, wub_ref, wdb_ref, gates[:, 1:2])

    @pl.when(used_ref[i] == 0)
    def _():
        y_ref[...] = jnp.zeros_like(y_ref)


def _experts(ea, eb, used, xs, w_gate, w_up, w_down):
    n_rows = xs.shape[0]
    up_a = pl.BlockSpec((1, D_MODEL, D_EXPERT), lambda i, ea, eb, used: (ea[i], 0, 0))
    up_b = pl.BlockSpec((1, D_MODEL, D_EXPERT), lambda i, ea, eb, used: (eb[i], 0, 0))
    dn_a = pl.BlockSpec((1, D_EXPERT, D_MODEL), lambda i, ea, eb, used: (ea[i], 0, 0))
    dn_b = pl.BlockSpec((1, D_EXPERT, D_MODEL), lambda i, ea, eb, used: (eb[i], 0, 0))
    return pl.pallas_call(
        _experts_kernel,
        grid_spec=pltpu.PrefetchScalarGridSpec(
            num_scalar_prefetch=3,
            grid=(n_rows // MOE_TILE,),
            in_specs=[pl.BlockSpec((MOE_TILE, ROW_EXT), lambda i, ea, eb, used: (i, 0)),
                      up_a, up_a, dn_a, up_b, up_b, dn_b],
            out_specs=pl.BlockSpec((MOE_TILE, D_MODEL), lambda i, ea, eb, used: (i, 0)),
        ),
        out_shape=jax.ShapeDtypeStruct((n_rows, D_MODEL), f32),
        compiler_params=_params(("arbitrary",)),
        name="moe_experts",
    )(ea, eb, used, xs, w_gate, w_up, w_down, w_gate, w_up, w_down)


def _combine_kernel(dest_ref, y_ref, h_ref, g_ref, b_ref, o_ref, buf_ref, sem_ref):
    i = pl.program_id(0)
    n = pl.num_programs(0)
    tm = COMBINE_TILE

    def row_copy(tile, slot, j):
        return pltpu.make_async_copy(y_ref.at[pl.ds(dest_ref[tile * tm + j], 1)],
                                     buf_ref.at[slot, pl.ds(j, 1)], sem_ref.at[slot])

    def gather(tile, slot):
        def issue(j, carry):
            row_copy(tile, slot, j).start()
            return carry
        lax.fori_loop(0, tm, issue, 0, unroll=8)

    @pl.when(i == 0)
    def _():
        gather(0, 0)

    @pl.when(i + 1 < n)
    def _():
        gather(i + 1, (i + 1) % 2)

    slot = i % 2
    pltpu.make_async_copy(y_ref.at[pl.ds(0, tm)], buf_ref.at[slot], sem_ref.at[slot]).wait()
    o_ref[...] = _layer_norm(DEEPNORM_ALPHA * h_ref[...] + buf_ref[slot], g_ref[...], b_ref[...])


def _combine(dest, y, hext, g, b):
    t = hext.shape[0]
    tm = COMBINE_TILE
    fixed = lambda i, dest: (0, 0)
    return pl.pallas_call(
        _combine_kernel,
        grid_spec=pltpu.PrefetchScalarGridSpec(
            num_scalar_prefetch=1,
            grid=(t // tm,),
            in_specs=[
                pl.BlockSpec(memory_space=pl.ANY),
                pl.BlockSpec((tm, D_MODEL), lambda i, dest: (i, 0)),
                pl.BlockSpec((1, D_MODEL), fixed),
                pl.BlockSpec((1, D_MODEL), fixed),
            ],
            out_specs=pl.BlockSpec((tm, D_MODEL), lambda i, dest: (i, 0)),
            scratch_shapes=[pltpu.VMEM((2, tm, D_MODEL), f32), pltpu.SemaphoreType.DMA((2,))],
        ),
        out_shape=jax.ShapeDtypeStruct((t, D_MODEL), f32),
        compiler_params=_params(("arbitrary",)),
        name="moe_combine",
    )(dest, y, hext, g, b)


def _moe_plan(route, counts):
    t = route.shape[1]
    n_rows = t + N_BUCKETS * MOE_TILE
    n_tiles = n_rows // MOE_TILE
    cnt = counts[:N_BUCKETS, 0].astype(jnp.int32)
    padded = (cnt + MOE_TILE - 1) // MOE_TILE * MOE_TILE
    ends = jnp.cumsum(padded)
    starts = ends - padded
    dest = starts[route[0]] + route[1]
    tile_start = jnp.arange(n_tiles, dtype=jnp.int32) * MOE_TILE
    tile_bucket = jnp.sum(tile_start[:, None] >= ends[None, :], axis=1).astype(jnp.int32)
    used = (tile_bucket < N_BUCKETS).astype(jnp.int32)
    tile_bucket = jnp.minimum(tile_bucket, N_BUCKETS - 1)
    pair_a = jnp.array([g * EXPERTS_PER_GROUP + i for g in range(N_GROUPS) for i, _ in _PAIRS], jnp.int32)
    pair_b = jnp.array([g * EXPERTS_PER_GROUP + j for g in range(N_GROUPS) for _, j in _PAIRS], jnp.int32)
    return dest, pair_a[tile_bucket], pair_b[tile_bucket], used, n_rows


def _moe(hext, route, counts, w_gate, w_up, w_down, g, b):
    dest, ea, eb, used, n_rows = _moe_plan(route, counts)
    xs = _dispatch(dest, hext, n_rows)
    y = _experts(ea, eb, used, xs, w_gate, w_up, w_down)
    return _combine(dest, y, hext, g, b)


def kernel(x, mem, ab_w_in, ab_conv_w, ab_a_log, ab_dt_bias, ab_norm_w, ab_w_out, sc_w_in, sc_conv_w, sc_w_out, xa_w_q, xa_w_kv, xa_w_o, router_w, router_b, moe_w_gate, moe_w_up, moe_w_down, ln_g, ln_b):
    batch, seq, d = x.shape
    n_mem = mem.shape[1]
    t = batch * seq
    h = x.reshape(t, d)
    mem2 = mem.reshape(batch * n_mem, d)
    rwt = jnp.transpose(router_w)
    rb = router_b.reshape(N_EXPERTS, 1)
    row = lambda v: v.reshape(1, -1)

    for layer in range(DEPTH):
        i = layer // 2
        if layer % 2 == 0:
            w_in = ab_w_in[i]
            c0, c1, c2 = 3 * DN_WIDTH, 4 * DN_WIDTH, 4 * DN_WIDTH + 2 * DN_HEADS
            w_ba = jnp.pad(w_in[:, c1:c2], ((0, 0), (0, LANES - 2 * DN_HEADS)))
            ws = [w_in[:, :c0].astype(bf16), w_in[:, c0:c1].astype(bf16), w_ba.astype(bf16), w_in[:, c2:].astype(bf16)]
            qkv, z, ba, qkv_d = _project(h, ws, [f32, f32, f32, f32], TOKEN_TILE)
            lane_pad = lambda v: jnp.pad(v.reshape(1, DN_HEADS), ((0, 0), (DN_HEADS, LANES - 2 * DN_HEADS)))
            o_a = _deltanet(qkv, ba, z, ab_conv_w[i], lane_pad(ab_a_log[i]), lane_pad(ab_dt_bias[i]),
                            row(ab_norm_w[i]), batch, seq)
            o_b = _dilated_attention(qkv_d.reshape(batch, seq, 3 * DA_WIDTH), batch, seq).reshape(t, DA_WIDTH)
            h = _mix_out(o_a, o_b, h, ab_w_out[i].astype(bf16), row(ln_g[layer, 0]), row(ln_b[layer, 0]))
        else:
            h = _shortconv(h, sc_w_in[i].astype(bf16), sc_conv_w[i], sc_w_out[i].astype(bf16),
                           row(ln_g[layer, 0]), row(ln_b[layer, 0]), batch, seq)
        (kv,) = _project(mem2, [xa_w_kv[layer].astype(bf16)], [bf16], n_mem)
        hext, route, counts = _xattn_route(h, kv, xa_w_q[layer].astype(bf16), xa_w_o[layer].astype(bf16),
                                           row(ln_g[layer, 1]), row(ln_b[layer, 1]), rwt, rb, batch, seq, n_mem)
        h = _moe(hext, route, counts, moe_w_gate[layer], moe_w_up[layer], moe_w_down[layer],
                 row(ln_g[layer, 2]), row(ln_b[layer, 2]))
    return h.reshape(batch, seq, d)
```

```python
import functools

import jax
import jax.numpy as jnp
from jax import lax
from jax.experimental import pallas as pl
from jax.experimental.pallas import tpu as pltpu

D_MODEL = 1024
DEPTH = 2
DN_HEADS = 4
DN_HEAD_DIM = 128
DN_WIDTH = DN_HEADS * DN_HEAD_DIM
DN_CONV = 4
DN_CHUNK = 64
DA_HEADS = 8
DA_HEAD_DIM = 64
DA_WIDTH = DA_HEADS * DA_HEAD_DIM
DA_BRANCHES = ((128, 1), (512, 4), (2048, 16))
DA_BLOCK = 128
SC_CONV = 3
XA_HEADS = 4
XA_HEAD_DIM = D_MODEL // XA_HEADS
N_EXPERTS = 16
N_GROUPS = 4
EXPERTS_PER_GROUP = N_EXPERTS // N_GROUPS
D_EXPERT = D_MODEL // 2
DEEPNORM_ALPHA = (2 * DEPTH) ** 0.25
LN_EPS = 1e-5
RMS_EPS = 1e-6

LANES = 128
SUBLANES = 8
VMEM_LIMIT_BYTES = 48 * 1024 * 1024

TOKEN_TILE = 512
MOE_TILE = 256
COMBINE_TILE = 256
DISPATCH_CHUNK = 512
GATE_LANES = LANES
ROW_EXT = D_MODEL + GATE_LANES

_PAIRS = tuple((i, j) for i in range(EXPERTS_PER_GROUP) for j in range(i + 1, EXPERTS_PER_GROUP))
N_BUCKETS = N_GROUPS * len(_PAIRS)
BUCKET_ROWS = 32

NEG_BIG = -1e30

bf16 = jnp.bfloat16
f32 = jnp.float32


def _params(semantics):
    return pltpu.CompilerParams(dimension_semantics=semantics, vmem_limit_bytes=VMEM_LIMIT_BYTES)


def _dot(a, b):
    return jnp.dot(a.astype(bf16), b.astype(bf16), preferred_element_type=f32)


def _dot_nt(a, b):
    return lax.dot_general(a.astype(bf16), b.astype(bf16), (((1,), (1,)), ((), ())), preferred_element_type=f32)


def _dot_tn(a, b):
    return lax.dot_general(a.astype(bf16), b.astype(bf16), (((0,), (0,)), ((), ())), preferred_element_type=f32)


def _split(a, parts):
    out = []
    rem = a
    for _ in range(parts):
        p = rem.astype(bf16)
        out.append(p)
        rem = rem - p.astype(f32)
    return out


def _layer_norm(v, g, b):
    mu = jnp.mean(v, -1, keepdims=True)
    c = v - mu
    var = jnp.mean(c * c, -1, keepdims=True)
    return c * lax.rsqrt(var + LN_EPS) * g + b


def _silu(v):
    return v * jax.nn.sigmoid(v)


def _softplus(v):
    return jnp.maximum(v, 0.0) + jnp.log(1.0 + jnp.exp(-jnp.abs(v)))


def _proj_kernel(x_ref, *refs, n_out):
    w_refs, o_refs = refs[:n_out], refs[n_out:]
    xb = x_ref[...].astype(bf16)
    for w_ref, o_ref in zip(w_refs, o_refs):
        n = w_ref.shape[1]
        for c0 in range(0, n, 512):
            c1 = min(c0 + 512, n)
            o_ref[:, c0:c1] = jnp.dot(xb, w_ref[:, c0:c1], preferred_element_type=f32).astype(o_ref.dtype)


def _project(x, ws, out_dtypes, tm):
    t, k = x.shape
    n_out = len(ws)
    return pl.pallas_call(
        functools.partial(_proj_kernel, n_out=n_out),
        grid=(t // tm,),
        in_specs=[pl.BlockSpec((tm, k), lambda i: (i, 0))]
        + [pl.BlockSpec(w.shape, lambda i: (0, 0)) for w in ws],
        out_specs=[pl.BlockSpec((tm, w.shape[1]), lambda i: (i, 0)) for w in ws],
        out_shape=[jax.ShapeDtypeStruct((t, w.shape[1]), dt) for w, dt in zip(ws, out_dtypes)],
        compiler_params=_params(("parallel",)),
        name="project",
    )(x, *ws)


def _deltanet_kernel(qkv_ref, ba_ref, z_ref, cw_ref, alog_ref, dtb_ref, nw_ref, o_ref, state_ref, tail_ref):
    c = pl.program_id(1)

    @pl.when(c == 0)
    def _():
        state_ref[...] = jnp.zeros_like(state_ref)
        tail_ref[...] = jnp.zeros_like(tail_ref)

    ch, hd, width = DN_CHUNK, DN_HEAD_DIM, DN_WIDTH
    x = qkv_ref[...]
    xx = jnp.concatenate([tail_ref[...], x], axis=0)
    cw = cw_ref[...]
    conv = x * cw[DN_CONV - 1:DN_CONV, :]
    for j in range(DN_CONV - 1):
        back = DN_CONV - 1 - j
        conv = conv + xx[SUBLANES - back:SUBLANES - back + ch, :] * cw[j:j + 1, :]
    tail_ref[...] = x[ch - SUBLANES:, :]
    act = _silu(conv)

    ba = ba_ref[...]
    beta_all = jax.nn.sigmoid(ba)
    g_all = -jnp.exp(alog_ref[...]) * _softplus(ba + dtb_ref[...])

    ri = lax.broadcasted_iota(jnp.int32, (ch, ch), 0)
    ci = lax.broadcasted_iota(jnp.int32, (ch, ch), 1)
    incl = ri >= ci
    strict = ri > ci
    lower_ones = jnp.where(incl, 1.0, 0.0).astype(bf16)
    eye = jnp.where(ri == ci, 1.0, 0.0)
    lane0 = lax.broadcasted_iota(jnp.int32, (ch, hd), 1) == 0
    ones_nt = jnp.ones((ch, hd), bf16)
    z = z_ref[...]
    nw = nw_ref[...]

    for h in range(DN_HEADS):
        q = act[:, h * hd:(h + 1) * hd]
        k = act[:, width + h * hd:width + (h + 1) * hd]
        v = act[:, 2 * width + h * hd:2 * width + (h + 1) * hd]
        q = q * lax.rsqrt(jnp.sum(q * q, -1, keepdims=True) + RMS_EPS) * (hd ** -0.5)
        k = k * lax.rsqrt(jnp.sum(k * k, -1, keepdims=True) + RMS_EPS)
        beta_b = jnp.broadcast_to(beta_all[:, h:h + 1], (ch, hd))
        g_b = jnp.broadcast_to(g_all[:, DN_HEADS + h:DN_HEADS + h + 1], (ch, hd))
        g_cum = sum(jnp.dot(lower_ones, p, preferred_element_type=f32) for p in _split(g_b, 2))
        gj = sum(
            lax.dot_general(ones_nt, jnp.where(lane0, p, jnp.zeros_like(p)), (((1,), (1,)), ((), ())),
                            preferred_element_type=f32)
            for p in _split(g_cum, 3))
        decay = jnp.where(incl, jnp.exp(jnp.minimum(g_cum[:, :ch] - gj, 0.0)), 0.0)
        e_g = jnp.exp(g_cum)
        kb = k * beta_b
        a_mat = jnp.where(strict, _dot_nt(kb, k) * decay, 0.0)
        t_inv = eye - a_mat
        pw = a_mat
        for _ in range(5):
            pw = _dot(pw, pw)
            t_inv = t_inv + _dot(t_inv, pw)
        sol = _dot(t_inv, jnp.concatenate([v * beta_b, kb * e_g], axis=1))
        u, w = sol[:, :hd], sol[:, hd:]
        attn = jnp.where(incl, _dot_nt(q, k) * decay, 0.0)
        g_last = g_cum[ch - 1:ch, :]
        k_tail = k * jnp.exp(g_last - g_cum)
        state = state_ref[h]
        v_new = u - _dot(w, state)
        o = _dot(q * e_g, state) + _dot(attn, v_new)
        state_ref[h] = state * jnp.exp(g_last) + _dot_tn(k_tail, v_new)
        o = o * lax.rsqrt(jnp.mean(o * o, -1, keepdims=True) + RMS_EPS) * nw
        o_ref[:, h * hd:(h + 1) * hd] = (o * _silu(z[:, h * hd:(h + 1) * hd])).astype(o_ref.dtype)


def _deltanet(qkv, ba, z, conv_w, alog_row, dtb_row, norm_w, batch, seq):
    nc = seq // DN_CHUNK
    row = lambda b, c: (b * nc + c, 0)
    fixed = lambda b, c: (0, 0)
    return pl.pallas_call(
        _deltanet_kernel,
        grid=(batch, nc),
        in_specs=[
            pl.BlockSpec((DN_CHUNK, 3 * DN_WIDTH), row),
            pl.BlockSpec((DN_CHUNK, LANES), row),
            pl.BlockSpec((DN_CHUNK, DN_WIDTH), row),
            pl.BlockSpec(conv_w.shape, fixed),
            pl.BlockSpec((1, LANES), fixed),
            pl.BlockSpec((1, LANES), fixed),
            pl.BlockSpec((1, DN_HEAD_DIM), fixed),
        ],
        out_specs=pl.BlockSpec((DN_CHUNK, DN_WIDTH), row),
        out_shape=jax.ShapeDtypeStruct((batch * seq, DN_WIDTH), bf16),
        scratch_shapes=[
            pltpu.VMEM((DN_HEADS, DN_HEAD_DIM, DN_HEAD_DIM), f32),
            pltpu.VMEM((SUBLANES, 3 * DN_WIDTH), f32),
        ],
        compiler_params=_params(("arbitrary", "arbitrary")),
        name="deltanet",
    )(qkv, ba, z, conv_w, alog_row, dtb_row, norm_w)


def _dilated_kernel(q_ref, k_ref, v_ref, o_ref, acc_ref, m_ref, l_ref, *, seq):
    blk = DA_BLOCK
    acc_ref[...] = jnp.zeros_like(acc_ref)
    l_ref[...] = jnp.zeros_like(l_ref)
    m_ref[...] = jnp.full(m_ref.shape, NEG_BIG, f32)
    head0 = lax.broadcasted_iota(jnp.int32, (blk, LANES), 1) < DA_HEAD_DIM
    ri = lax.broadcasted_iota(jnp.int32, (blk, blk), 0)
    ci = lax.broadcasted_iota(jnp.int32, (blk, blk), 1)
    cur_ok = ci <= ri
    prev_ok = ci >= ri
    scale = DA_HEAD_DIM ** -0.5

    for window, dil in DA_BRANCHES:
        assert window // dil == blk
        nblk = seq // (dil * blk)

        def body(idx, carry, dil=dil, nblk=nblk):
            r = idx // nblk
            n = idx - r * nblk
            start = r + n * (dil * blk)
            pstart = r + jnp.maximum(n - 1, 0) * (dil * blk)
            if dil == 1:
                cur, prev = pl.ds(start, blk), pl.ds(pstart, blk)
            else:
                cur, prev = pl.ds(start, blk, stride=dil), pl.ds(pstart, blk, stride=dil)
            q = q_ref[0, cur, :].astype(bf16)
            kc = k_ref[0, cur, :].astype(bf16)
            vc = v_ref[0, cur, :].astype(bf16)
            kp = k_ref[0, prev, :].astype(bf16)
            vp = v_ref[0, prev, :].astype(bf16)
            has_prev = n > 0
            stats = []
            for first in (True, False):
                hm = head0 if first else jnp.logical_not(head0)
                zero = jnp.zeros_like(kc)
                s_c = _dot_nt(q, jnp.where(hm, kc, zero)) * scale
                s_p = _dot_nt(q, jnp.where(hm, kp, zero)) * scale
                s_c = jnp.where(cur_ok, s_c, NEG_BIG)
                s_p = jnp.where(jnp.logical_and(prev_ok, has_prev), s_p, NEG_BIG)
                mb = jnp.maximum(jnp.max(s_c, -1, keepdims=True), jnp.max(s_p, -1, keepdims=True))
                p_c = jnp.exp(s_c - mb)
                p_p = jnp.exp(s_p - mb)
                lb = jnp.sum(p_c, -1, keepdims=True) + jnp.sum(p_p, -1, keepdims=True)
                ob = _dot(p_c, vc) + _dot(p_p, vp)
                stats.append((mb, lb, ob))
            m_blk = jnp.where(head0, stats[0][0], stats[1][0])
            l_blk = jnp.where(head0, stats[0][1], stats[1][1])
            o_blk = jnp.where(head0, stats[0][2], stats[1][2])
            m_old = m_ref[cur, :]
            m_new = jnp.maximum(m_old, m_blk)
            a_old = jnp.exp(m_old - m_new)
            a_blk = jnp.exp(m_blk - m_new)
            acc_ref[cur, :] = acc_ref[cur, :] * a_old + o_blk * a_blk
            l_ref[cur, :] = l_ref[cur, :] * a_old + l_blk * a_blk
            m_ref[cur, :] = m_new
            return carry

        lax.fori_loop(0, dil * nblk, body, 0)

    o_ref[0] = (acc_ref[...] / l_ref[...]).astype(o_ref.dtype)


def _dilated_attention(qkv3, batch, seq):
    pairs = DA_WIDTH // LANES
    spec = lambda off: pl.BlockSpec((1, seq, LANES), lambda b, p, off=off: (b, 0, off + p))
    return pl.pallas_call(
        functools.partial(_dilated_kernel, seq=seq),
        grid=(batch, pairs),
        in_specs=[spec(0), spec(pairs), spec(2 * pairs)],
        out_specs=pl.BlockSpec((1, seq, LANES), lambda b, p: (b, 0, p)),
        out_shape=jax.ShapeDtypeStruct((batch, seq, DA_WIDTH), bf16),
        scratch_shapes=[pltpu.VMEM((seq, LANES), f32)] * 3,
        compiler_params=_params(("parallel", "parallel")),
        name="dilated_attention",
    )(qkv3, qkv3, qkv3)


def _mix_out_kernel(oa_ref, ob_ref, h_ref, w_ref, g_ref, b_ref, o_ref):
    w = w_ref[...]
    y = jnp.dot(oa_ref[...], w[:DN_WIDTH], preferred_element_type=f32)
    y = y + jnp.dot(ob_ref[...], w[DN_WIDTH:], preferred_element_type=f32)
    o_ref[...] = _layer_norm(DEEPNORM_ALPHA * h_ref[...] + y, g_ref[...], b_ref[...])


def _mix_out(o_a, o_b, h, w_out, g, b):
    t = h.shape[0]
    tm = TOKEN_TILE
    fixed = lambda i: (0, 0)
    return pl.pallas_call(
        _mix_out_kernel,
        grid=(t // tm,),
        in_specs=[
            pl.BlockSpec((tm, DN_WIDTH), lambda i: (i, 0)),
            pl.BlockSpec((tm, DA_WIDTH), lambda i: (i, 0)),
            pl.BlockSpec((tm, D_MODEL), lambda i: (i, 0)),
            pl.BlockSpec(w_out.shape, fixed),
            pl.BlockSpec((1, D_MODEL), fixed),
            pl.BlockSpec((1, D_MODEL), fixed),
        ],
        out_specs=pl.BlockSpec((tm, D_MODEL), lambda i: (i, 0)),
        out_shape=jax.ShapeDtypeStruct((t, D_MODEL), f32),
        compiler_params=_params(("parallel",)),
        name="mix_out",
    )(o_a, o_b, h, w_out, g, b)


def _shortconv_kernel(h_ref, win_ref, cw_ref, wout_ref, g_ref, b_ref, o_ref, tail_ref):
    @pl.when(pl.program_id(1) == 0)
    def _():
        tail_ref[...] = jnp.zeros_like(tail_ref)

    tm = h_ref.shape[0]
    d = D_MODEL
    h = h_ref[...]
    hb = h.astype(bf16)
    gate_b = jnp.dot(hb, win_ref[:, :d], preferred_element_type=f32)
    gate_c = jnp.dot(hb, win_ref[:, d:2 * d], preferred_element_type=f32)
    hid = jnp.dot(hb, win_ref[:, 2 * d:], preferred_element_type=f32)
    u = gate_c * hid
    uu = jnp.concatenate([tail_ref[...], u], axis=0)
    cw = cw_ref[...]
    conv = u * cw[SC_CONV - 1:SC_CONV, :]
    for j in range(SC_CONV - 1):
        back = SC_CONV - 1 - j
        conv = conv + uu[SUBLANES - back:SUBLANES - back + tm, :] * cw[j:j + 1, :]
    tail_ref[...] = u[tm - SUBLANES:, :]
    y = jnp.dot((gate_b * conv).astype(bf16), wout_ref[...], preferred_element_type=f32)
    o_ref[...] = _layer_norm(DEEPNORM_ALPHA * h + y, g_ref[...], b_ref[...])


def _shortconv(h, w_in, conv_w, w_out, g, b, batch, seq):
    tm = TOKEN_TILE
    ns = seq // tm
    row = lambda bb, s: (bb * ns + s, 0)
    fixed = lambda bb, s: (0, 0)
    return pl.pallas_call(
        _shortconv_kernel,
        grid=(batch, ns),
        in_specs=[
            pl.BlockSpec((tm, D_MODEL), row),
            pl.BlockSpec(w_in.shape, fixed),
            pl.BlockSpec(conv_w.shape, fixed),
            pl.BlockSpec(w_out.shape, fixed),
            pl.BlockSpec((1, D_MODEL), fixed),
            pl.BlockSpec((1, D_MODEL), fixed),
        ],
        out_specs=pl.BlockSpec((tm, D_MODEL), row),
        out_shape=jax.ShapeDtypeStruct((batch * seq, D_MODEL), f32),
        scratch_shapes=[pltpu.VMEM((SUBLANES, D_MODEL), f32)],
        compiler_params=_params(("arbitrary", "arbitrary")),
        name="shortconv",
    )(h, w_in, conv_w, w_out, g, b)


def _route(h2, rwt_ref, rb_ref, cnt_ref):
    tm = h2.shape[0]
    w_hi, w_lo = _split(rwt_ref[...], 2)
    x_hi, x_lo = _split(h2, 2)
    nt = (((1,), (1,)), ((), ()))
    logits = (lax.dot_general(w_hi, x_hi, nt, preferred_element_type=f32)
              + lax.dot_general(w_hi, x_lo, nt, preferred_element_type=f32)
              + lax.dot_general(w_lo, x_hi, nt, preferred_element_type=f32))
    mx = jnp.max(logits, axis=0, keepdims=True)
    ex = jnp.exp(logits - mx)
    scores = ex / jnp.sum(ex, axis=0, keepdims=True)
    sel = scores + rb_ref[...]
    best = jnp.full((1, tm), -jnp.inf, f32)
    bucket = jnp.zeros((1, tm), jnp.int32)
    s_a = jnp.zeros((1, tm), f32)
    s_b = jnp.zeros((1, tm), f32)
    for grp in range(N_GROUPS):
        for p, (i, j) in enumerate(_PAIRS):
            a, b = grp * EXPERTS_PER_GROUP + i, grp * EXPERTS_PER_GROUP + j
            ps = sel[a:a + 1, :] + sel[b:b + 1, :]
            upd = ps > best
            best = jnp.where(upd, ps, best)
            bucket = jnp.where(upd, grp * len(_PAIRS) + p, bucket)
            s_a = jnp.where(upd, scores[a:a + 1, :], s_a)
            s_b = jnp.where(upd, scores[b:b + 1, :], s_b)
    denom = s_a + s_b
    gate_rows = jnp.concatenate([s_a / denom, s_b / denom, jnp.zeros((GATE_LANES - 2, tm), f32)], axis=0)
    gates = jnp.transpose(gate_rows)
    onehot = jnp.where(lax.broadcasted_iota(jnp.int32, (BUCKET_ROWS, tm), 0) == bucket, 1.0, 0.0)
    before = lax.broadcasted_iota(jnp.int32, (tm, tm), 0) < lax.broadcasted_iota(jnp.int32, (tm, tm), 1)
    prefix = jnp.dot(onehot.astype(bf16), jnp.where(before, 1.0, 0.0).astype(bf16), preferred_element_type=f32)
    cnt = cnt_ref[...]
    rank = jnp.sum(onehot * (prefix + cnt[:, 0:1]), axis=0, keepdims=True)
    cnt_ref[...] = cnt + jnp.sum(onehot, axis=1, keepdims=True)
    return gates, bucket, rank.astype(jnp.int32)


def _xattn_kernel(h_ref, kv_ref, wq_ref, wo_ref, g_ref, b_ref, rwt_ref, rb_ref,
                  hext_ref, route_ref, cnt_out_ref, cnt_ref):
    @pl.when(jnp.logical_and(pl.program_id(0) == 0, pl.program_id(1) == 0))
    def _():
        cnt_ref[...] = jnp.zeros_like(cnt_ref)

    h = h_ref[...]
    tm = h.shape[0]
    q = jnp.dot(h.astype(bf16), wq_ref[...], preferred_element_type=f32).astype(bf16)
    outs = []
    for hd in range(XA_HEADS):
        lo, hi = hd * XA_HEAD_DIM, (hd + 1) * XA_HEAD_DIM
        s = _dot_nt(q[:, lo:hi], kv_ref[:, lo:hi]) * (XA_HEAD_DIM ** -0.5)
        p = jnp.exp(s - jnp.max(s, -1, keepdims=True))
        o = _dot(p, kv_ref[:, D_MODEL + lo:D_MODEL + hi]) / jnp.sum(p, -1, keepdims=True)
        outs.append(o.astype(bf16))
    y = jnp.dot(jnp.concatenate(outs, axis=1), wo_ref[...], preferred_element_type=f32)
    h2 = _layer_norm(DEEPNORM_ALPHA * h + y, g_ref[...], b_ref[...])
    gates, bucket, rank = _route(h2, rwt_ref, rb_ref, cnt_ref)
    hext_ref[:, :D_MODEL] = h2
    hext_ref[:, D_MODEL:] = gates
    route_ref[...] = jnp.concatenate([bucket, rank, jnp.zeros((SUBLANES - 2, tm), jnp.int32)], axis=0)
    cnt_out_ref[...] = cnt_ref[...]


def _xattn_route(h, kv, w_q, w_o, g, b, rwt, rb, batch, seq, n_mem):
    tm = TOKEN_TILE
    ns = seq // tm
    t = batch * seq
    row = lambda bb, s: (bb * ns + s, 0)
    fixed = lambda bb, s: (0, 0)
    return pl.pallas_call(
        _xattn_kernel,
        grid=(batch, ns),
        in_specs=[
            pl.BlockSpec((tm, D_MODEL), row),
            pl.BlockSpec((n_mem, 2 * D_MODEL), lambda bb, s: (bb, 0)),
            pl.BlockSpec(w_q.shape, fixed),
            pl.BlockSpec(w_o.shape, fixed),
            pl.BlockSpec((1, D_MODEL), fixed),
            pl.BlockSpec((1, D_MODEL), fixed),
            pl.BlockSpec(rwt.shape, fixed),
            pl.BlockSpec(rb.shape, fixed),
        ],
        out_specs=[
            pl.BlockSpec((tm, ROW_EXT), row),
            pl.BlockSpec((SUBLANES, tm), lambda bb, s: (0, bb * ns + s)),
            pl.BlockSpec((BUCKET_ROWS, LANES), fixed),
        ],
        out_shape=[
            jax.ShapeDtypeStruct((t, ROW_EXT), f32),
            jax.ShapeDtypeStruct((SUBLANES, t), jnp.int32),
            jax.ShapeDtypeStruct((BUCKET_ROWS, LANES), f32),
        ],
        scratch_shapes=[pltpu.VMEM((BUCKET_ROWS, LANES), f32)],
        compiler_params=_params(("arbitrary", "arbitrary")),
        name="xattn_route",
    )(h, kv, w_q, w_o, g, b, rwt, rb)


def _dispatch_kernel(dest_ref, hext_ref, xs_in_ref, xs_ref, sem):
    del xs_in_ref
    base = pl.program_id(0) * DISPATCH_CHUNK

    def issue(j, carry):
        pltpu.make_async_copy(hext_ref.at[pl.ds(j, 1)], xs_ref.at[pl.ds(dest_ref[base + j], 1)], sem).start()
        return carry

    lax.fori_loop(0, DISPATCH_CHUNK, issue, 0, unroll=8)
    pltpu.make_async_copy(hext_ref, xs_ref.at[pl.ds(0, DISPATCH_CHUNK)], sem).wait()


def _dispatch(dest, hext, n_rows):
    t = hext.shape[0]
    xs0 = jnp.zeros((n_rows, ROW_EXT), f32)
    return pl.pallas_call(
        _dispatch_kernel,
        grid_spec=pltpu.PrefetchScalarGridSpec(
            num_scalar_prefetch=1,
            grid=(t // DISPATCH_CHUNK,),
            in_specs=[pl.BlockSpec((DISPATCH_CHUNK, ROW_EXT), lambda i, dest: (i, 0)),
                      pl.BlockSpec(memory_space=pl.ANY)],
            out_specs=pl.BlockSpec(memory_space=pl.ANY),
            scratch_shapes=[pltpu.SemaphoreType.DMA(())],
        ),
        out_shape=jax.ShapeDtypeStruct((n_rows, ROW_EXT), f32),
        input_output_aliases={2: 0},
        compiler_params=_params(("arbitrary",)),
        name="moe_dispatch",
    )(dest, hext, xs0)


def _experts_kernel(ea_ref, eb_ref, used_ref, xs_ref, wga_ref, wua_ref, wda_ref, wgb_ref, wub_ref, wdb_ref, y_ref):
    i = pl.program_id(0)

    @pl.when(used_ref[i] > 0)
    def _():
        x = xs_ref[:, :D_MODEL].astype(bf16)
        gates = xs_ref[:, D_MODEL:]

        def ffn(wg_ref, wu_ref, wd_ref, gate):
            hg = jnp.dot(x, wg_ref[0].astype(bf16), preferred_element_type=f32)
            hu = jnp.dot(x, wu_ref[0].astype(bf16), preferred_element_type=f32)
            hdn = _silu(hg) * hu * gate
            return jnp.dot(hdn.astype(bf16), wd_ref[0].astype(bf16), preferred_element_type=f32)

        y_ref[...] = ffn(wga_ref, wua_ref, wda_ref, gates[:, 0:1]) + ffn(wgb_ref, wub_ref, wdb_ref, gates[:, 1:2])

    @pl.when(used_ref[i] == 0)
    def _():
        y_ref[...] = jnp.zeros_like(y_ref)


def _experts(ea, eb, used, xs, w_gate, w_up, w_down):
    n_rows = xs.shape[0]
    up_a = pl.BlockSpec((1, D_MODEL, D_EXPERT), lambda i, ea, eb, used: (ea[i], 0, 0))
    up_b = pl.BlockSpec((1, D_MODEL, D_EXPERT), lambda i, ea, eb, used: (eb[i], 0, 0))
    dn_a = pl.BlockSpec((1, D_EXPERT, D_MODEL), lambda i, ea, eb, used: (ea[i], 0, 0))
    dn_b = pl.BlockSpec((1, D_EXPERT, D_MODEL), lambda i, ea, eb, used: (eb[i], 0, 0))
    return pl.pallas_call(
        _experts_kernel,
        grid_spec=pltpu.PrefetchScalarGridSpec(
            num_scalar_prefetch=3,
            grid=(n_rows // MOE_TILE,),
            in_specs=[pl.BlockSpec((MOE_TILE, ROW_EXT), lambda i, ea, eb, used: (i, 0)),
                      up_a, up_a, dn_a, up_b, up_b, dn_b],
            out_specs=pl.BlockSpec((MOE_TILE, D_MODEL), lambda i, ea, eb, used: (i, 0)),
        ),
        out_shape=jax.ShapeDtypeStruct((n_rows, D_MODEL), f32),
        compiler_params=_params(("arbitrary",)),
        name="moe_experts",
    )(ea, eb, used, xs, w_gate, w_up, w_down, w_gate, w_up, w_down)


def _combine_kernel(dest_ref, y_ref, h_ref, g_ref, b_ref, o_ref, buf_ref, sem_ref):
    i = pl.program_id(0)
    n = pl.num_programs(0)
    tm = COMBINE_TILE

    def row_copy(tile, slot, j):
        return pltpu.make_async_copy(y_ref.at[pl.ds(dest_ref[tile * tm + j], 1)],
                                     buf_ref.at[slot, pl.ds(j, 1)], sem_ref.at[slot])

    def gather(tile, slot):
        def issue(j, carry):
            row_copy(tile, slot, j).start()
            return carry
        lax.fori_loop(0, tm, issue, 0, unroll=8)

    @pl.when(i == 0)
    def _():
        gather(0, 0)

    @pl.when(i + 1 < n)
    def _():
        gather(i + 1, (i + 1) % 2)

    slot = i % 2
    pltpu.make_async_copy(y_ref.at[pl.ds(0, tm)], buf_ref.at[slot], sem_ref.at[slot]).wait()
    o_ref[...] = _layer_norm(DEEPNORM_ALPHA * h_ref[...] + buf_ref[slot], g_ref[...], b_ref[...])


def _combine(dest, y, hext, g, b):
    t = hext.shape[0]
    tm = COMBINE_TILE
    fixed = lambda i, dest: (0, 0)
    return pl.pallas_call(
        _combine_kernel,
        grid_spec=pltpu.PrefetchScalarGridSpec(
            num_scalar_prefetch=1,
            grid=(t // tm,),
            in_specs=[
                pl.BlockSpec(memory_space=pl.ANY),
                pl.BlockSpec((tm, D_MODEL), lambda i, dest: (i, 0)),
                pl.BlockSpec((1, D_MODEL), fixed),
                pl.BlockSpec((1, D_MODEL), fixed),
            ],
            out_specs=pl.BlockSpec((tm, D_MODEL), lambda i, dest: (i, 0)),
            scratch_shapes=[pltpu.VMEM((2, tm, D_MODEL), f32), pltpu.SemaphoreType.DMA((2,))],
        ),
        out_shape=jax.ShapeDtypeStruct((t, D_MODEL), f32),
        compiler_params=_params(("arbitrary",)),
        name="moe_combine",
    )(dest, y, hext, g, b)


def _moe_plan(route, counts):
    t = route.shape[1]
    n_rows = t + N_BUCKETS * MOE_TILE
    n_tiles = n_rows // MOE_TILE
    cnt = counts[:N_BUCKETS, 0].astype(jnp.int32)
    padded = (cnt + MOE_TILE - 1) // MOE_TILE * MOE_TILE
    ends = jnp.cumsum(padded)
    starts = ends - padded
    dest = starts[route[0]] + route[1]
    tile_start = jnp.arange(n_tiles, dtype=jnp.int32) * MOE_TILE
    tile_bucket = jnp.sum(tile_start[:, None] >= ends[None, :], axis=1).astype(jnp.int32)
    used = (tile_bucket < N_BUCKETS).astype(jnp.int32)
    tile_bucket = jnp.minimum(tile_bucket, N_BUCKETS - 1)
    pair_a = jnp.array([g * EXPERTS_PER_GROUP + i for g in range(N_GROUPS) for i, _ in _PAIRS], jnp.int32)
    pair_b = jnp.array([g * EXPERTS_PER_GROUP + j for g in range(N_GROUPS) for _, j in _PAIRS], jnp.int32)
    return dest, pair_a[tile_bucket], pair_b[tile_bucket], used, n_rows


def _moe(hext, route, counts, w_gate, w_up, w_down, g, b):
    dest, ea, eb, used, n_rows = _moe_plan(route, counts)
    xs = _dispatch(dest, hext, n_rows)
    y = _experts(ea, eb, used, xs, w_gate, w_up, w_down)
    return _combine(dest, y, hext, g, b)


def kernel(x, mem, ab_w_in, ab_conv_w, ab_a_log, ab_dt_bias, ab_norm_w, ab_w_out, sc_w_in, sc_conv_w, sc_w_out, xa_w_q, xa_w_kv, xa_w_o, router_w, router_b, moe_w_gate, moe_w_up, moe_w_down, ln_g, ln_b):
    batch, seq, d = x.shape
    n_mem = mem.shape[1]
    t = batch * seq
    h = x.reshape(t, d)
    mem2 = mem.reshape(batch * n_mem, d)
    rwt = jnp.transpose(router_w)
    rb = router_b.reshape(N_EXPERTS, 1)
    row = lambda v: v.reshape(1, -1)

    for layer in range(DEPTH):
        i = layer // 2
        if layer % 2 == 0:
            w_in = ab_w_in[i]
            c0, c1, c2 = 3 * DN_WIDTH, 4 * DN_WIDTH, 4 * DN_WIDTH + 2 * DN_HEADS
            w_ba = jnp.pad(w_in[:, c1:c2], ((0, 0), (0, LANES - 2 * DN_HEADS)))
            ws = [w_in[:, :c0].astype(bf16), w_in[:, c0:c1].astype(bf16), w_ba.astype(bf16), w_in[:, c2:].astype(bf16)]
            qkv, z, ba, qkv_d = _project(h, ws, [f32, f32, f32, f32], TOKEN_TILE)
            lane_pad = lambda v: jnp.pad(v.reshape(1, DN_HEADS), ((0, 0), (DN_HEADS, LANES - 2 * DN_HEADS)))
            o_a = _deltanet(qkv, ba, z, ab_conv_w[i], lane_pad(ab_a_log[i]), lane_pad(ab_dt_bias[i]),
                            row(ab_norm_w[i]), batch, seq)
            o_b = _dilated_attention(qkv_d.reshape(batch, seq, 3 * DA_WIDTH), batch, seq).reshape(t, DA_WIDTH)
            h = _mix_out(o_a, o_b, h, ab_w_out[i].astype(bf16), row(ln_g[layer, 0]), row(ln_b[layer, 0]))
        else:
            h = _shortconv(h, sc_w_in[i].astype(bf16), sc_conv_w[i], sc_w_out[i].astype(bf16),
                           row(ln_g[layer, 0]), row(ln_b[layer, 0]), batch, seq)
        (kv,) = _project(mem2, [xa_w_kv[layer].astype(bf16)], [bf16], n_mem)
        hext, route, counts = _xattn_route(h, kv, xa_w_q[layer].astype(bf16), xa_w_o[layer].astype(bf16),
                                           row(ln_g[layer, 1]), row(ln_b[layer, 1]), rwt, rb, batch, seq, n_mem)
        h = _moe(hext, route, counts, moe_w_gate[layer], moe_w_up[layer], moe_w_down[layer],
                 row(ln_g[layer, 2]), row(ln_b[layer, 2]))
    return h.reshape(batch, seq, d)
```

```python
import functools

import jax
import jax.numpy as jnp
from jax import lax
from jax.experimental import pallas as pl
from jax.experimental.pallas import tpu as pltpu

D_MODEL = 1024
DEPTH = 2
DN_HEADS = 4
DN_HEAD_DIM = 128
DN_WIDTH = DN_HEADS * DN_HEAD_DIM
DN_CONV = 4
DN_CHUNK = 64
DA_HEADS = 8
DA_HEAD_DIM = 64
DA_WIDTH = DA_HEADS * DA_HEAD_DIM
DA_BRANCHES = ((128, 1), (512, 4), (2048, 16))
DA_BLOCK = 128
DA_RESIDUES = max(d for _, d in DA_BRANCHES)
SC_CONV = 3
XA_HEADS = 4
XA_HEAD_DIM = D_MODEL // XA_HEADS
N_EXPERTS = 16
N_GROUPS = 4
EXPERTS_PER_GROUP = N_EXPERTS // N_GROUPS
D_EXPERT = D_MODEL // 2
DEEPNORM_ALPHA = (2 * DEPTH) ** 0.25
LN_EPS = 1e-5
RMS_EPS = 1e-6

LANES = 128
SUBLANES = 8
VMEM_LIMIT_BYTES = 48 * 1024 * 1024

TOKEN_TILE = 512
MOE_TILE = 256
COMBINE_TILE = 256
DISPATCH_CHUNK = 512
DA_UNROLL = 2
GATE_LANES = LANES
ROW_EXT = D_MODEL + GATE_LANES

_PAIRS = tuple((i, j) for i in range(EXPERTS_PER_GROUP) for j in range(i + 1, EXPERTS_PER_GROUP))
N_BUCKETS = N_GROUPS * len(_PAIRS)
BUCKET_ROWS = 32

NEG_BIG = -1e30
LOG2_E = 1.4426950408889634

bf16 = jnp.bfloat16
f32 = jnp.float32


def _params(semantics):
    return pltpu.CompilerParams(dimension_semantics=semantics, vmem_limit_bytes=VMEM_LIMIT_BYTES)


def _dot(a, b):
    return jnp.dot(a.astype(bf16), b.astype(bf16), preferred_element_type=f32)


def _dot_nt(a, b):
    return lax.dot_general(a.astype(bf16), b.astype(bf16), (((1,), (1,)), ((), ())), preferred_element_type=f32)


def _dot_tn(a, b):
    return lax.dot_general(a.astype(bf16), b.astype(bf16), (((0,), (0,)), ((), ())), preferred_element_type=f32)


def _split(a, parts):
    out = []
    rem = a
    for _ in range(parts):
        p = rem.astype(bf16)
        out.append(p)
        rem = rem - p.astype(f32)
    return out


def _layer_norm(v, g, b):
    mu = jnp.mean(v, -1, keepdims=True)
    c = v - mu
    var = jnp.mean(c * c, -1, keepdims=True)
    return c * lax.rsqrt(var + LN_EPS) * g + b


def _silu(v):
    return v * jax.nn.sigmoid(v)


def _softplus(v):
    return jnp.maximum(v, 0.0) + jnp.log(1.0 + jnp.exp(-jnp.abs(v)))


def _proj_kernel(x_ref, *refs, n_out):
    w_refs, o_refs = refs[:n_out], refs[n_out:]
    xb = x_ref[...].astype(bf16)
    for w_ref, o_ref in zip(w_refs, o_refs):
        n = w_ref.shape[1]
        for c0 in range(0, n, 512):
            c1 = min(c0 + 512, n)
            o_ref[:, c0:c1] = jnp.dot(xb, w_ref[:, c0:c1], preferred_element_type=f32).astype(o_ref.dtype)


def _project(x, ws, out_dtypes, tm):
    t, k = x.shape
    n_out = len(ws)
    return pl.pallas_call(
        functools.partial(_proj_kernel, n_out=n_out),
        grid=(t // tm,),
        in_specs=[pl.BlockSpec((tm, k), lambda i: (i, 0))]
        + [pl.BlockSpec(w.shape, lambda i: (0, 0)) for w in ws],
        out_specs=[pl.BlockSpec((tm, w.shape[1]), lambda i: (i, 0)) for w in ws],
        out_shape=[jax.ShapeDtypeStruct((t, w.shape[1]), dt) for w, dt in zip(ws, out_dtypes)],
        compiler_params=_params(("parallel",)),
        name="project",
    )(x, *ws)


def _deltanet_kernel(qkv_ref, ba_ref, z_ref, cw_ref, alog_ref, dtb_ref, nw_ref, o_ref, state_ref, tail_ref):
    c = pl.program_id(1)

    @pl.when(c == 0)
    def _():
        state_ref[...] = jnp.zeros_like(state_ref)
        tail_ref[...] = jnp.zeros_like(tail_ref)

    ch, hd, width = DN_CHUNK, DN_HEAD_DIM, DN_WIDTH
    x = qkv_ref[...]
    xx = jnp.concatenate([tail_ref[...], x], axis=0)
    cw = cw_ref[...]
    conv = x * cw[DN_CONV - 1:DN_CONV, :]
    for j in range(DN_CONV - 1):
        back = DN_CONV - 1 - j
        conv = conv + xx[SUBLANES - back:SUBLANES - back + ch, :] * cw[j:j + 1, :]
    tail_ref[...] = x[ch - SUBLANES:, :]
    act = _silu(conv)

    ba = ba_ref[...]
    beta_all = jax.nn.sigmoid(ba)
    g_all = -jnp.exp(alog_ref[...]) * _softplus(ba + dtb_ref[...])

    ri = lax.broadcasted_iota(jnp.int32, (ch, ch), 0)
    ci = lax.broadcasted_iota(jnp.int32, (ch, ch), 1)
    incl = ri >= ci
    strict = ri > ci
    lower_ones = jnp.where(incl, 1.0, 0.0).astype(bf16)
    eye = jnp.where(ri == ci, 1.0, 0.0)
    lane0 = lax.broadcasted_iota(jnp.int32, (ch, hd), 1) == 0
    ones_nt = jnp.ones((ch, hd), bf16)
    z = z_ref[...]
    nw = nw_ref[...]

    for h in range(DN_HEADS):
        q = act[:, h * hd:(h + 1) * hd]
        k = act[:, width + h * hd:width + (h + 1) * hd]
        v = act[:, 2 * width + h * hd:2 * width + (h + 1) * hd]
        q = q * lax.rsqrt(jnp.sum(q * q, -1, keepdims=True) + RMS_EPS) * (hd ** -0.5)
        k = k * lax.rsqrt(jnp.sum(k * k, -1, keepdims=True) + RMS_EPS)
        beta_b = jnp.broadcast_to(beta_all[:, h:h + 1], (ch, hd))
        g_b = jnp.broadcast_to(g_all[:, DN_HEADS + h:DN_HEADS + h + 1], (ch, hd))
        g_cum = sum(jnp.dot(lower_ones, p, preferred_element_type=f32) for p in _split(g_b, 2))
        gj = sum(
            lax.dot_general(ones_nt, jnp.where(lane0, p, jnp.zeros_like(p)), (((1,), (1,)), ((), ())),
                            preferred_element_type=f32)
            for p in _split(g_cum, 3))
        decay = jnp.where(incl, jnp.exp(jnp.minimum(g_cum[:, :ch] - gj, 0.0)), 0.0)
        e_g = jnp.exp(g_cum)
        kb = k * beta_b
        a_mat = jnp.where(strict, _dot_nt(kb, k) * decay, 0.0)
        t_inv = eye - a_mat
        pw = a_mat
        for _ in range(5):
            pw = _dot(pw, pw)
            t_inv = t_inv + _dot(t_inv, pw)
        sol = _dot(t_inv, jnp.concatenate([v * beta_b, kb * e_g], axis=1))
        u, w = sol[:, :hd], sol[:, hd:]
        attn = jnp.where(incl, _dot_nt(q, k) * decay, 0.0)
        g_last = g_cum[ch - 1:ch, :]
        k_tail = k * jnp.exp(g_last - g_cum)
        state = state_ref[h]
        v_new = u - _dot(w, state)
        o = _dot(q * e_g, state) + _dot(attn, v_new)
        state_ref[h] = state * jnp.exp(g_last) + _dot_tn(k_tail, v_new)
        o = o * lax.rsqrt(jnp.mean(o * o, -1, keepdims=True) + RMS_EPS) * nw
        o_ref[:, h * hd:(h + 1) * hd] = (o * _silu(z[:, h * hd:(h + 1) * hd])).astype(o_ref.dtype)


def _deltanet(qkv, ba, z, conv_w, alog_row, dtb_row, norm_w, batch, seq):
    nc = seq // DN_CHUNK
    row = lambda b, c: (b * nc + c, 0)
    fixed = lambda b, c: (0, 0)
    return pl.pallas_call(
        _deltanet_kernel,
        grid=(batch, nc),
        in_specs=[
            pl.BlockSpec((DN_CHUNK, 3 * DN_WIDTH), row),
            pl.BlockSpec((DN_CHUNK, LANES), row),
            pl.BlockSpec((DN_CHUNK, DN_WIDTH), row),
            pl.BlockSpec(conv_w.shape, fixed),
            pl.BlockSpec((1, LANES), fixed),
            pl.BlockSpec((1, LANES), fixed),
            pl.BlockSpec((1, DN_HEAD_DIM), fixed),
        ],
        out_specs=pl.BlockSpec((DN_CHUNK, DN_WIDTH), row),
        out_shape=jax.ShapeDtypeStruct((batch * seq, DN_WIDTH), bf16),
        scratch_shapes=[
            pltpu.VMEM((DN_HEADS, DN_HEAD_DIM, DN_HEAD_DIM), f32),
            pltpu.VMEM((SUBLANES, 3 * DN_WIDTH), f32),
        ],
        compiler_params=_params(("arbitrary", "arbitrary")),
        name="deltanet",
    )(qkv, ba, z, conv_w, alog_row, dtb_row, norm_w)


def _dilated_kernel(q_ref, k_ref, v_ref, o_ref, qs_ref, ks_ref, vs_ref, ob_ref, lse_ref, s_ref, p_ref, st_ref, *, seq):
    blk, res = DA_BLOCK, DA_RESIDUES
    per = seq // res
    head0 = lax.broadcasted_iota(jnp.int32, (blk, LANES), 1) < DA_HEAD_DIM
    ai = lax.broadcasted_iota(jnp.int32, (2 * blk, blk), 0) & (blk - 1)
    bi = lax.broadcasted_iota(jnp.int32, (2 * blk, blk), 1)
    ones_blk = jnp.ones((blk, LANES), bf16)
    lane_head0 = lax.broadcasted_iota(jnp.int32, (LANES, 2 * LANES), 0) < DA_HEAD_DIM
    col_head0 = lax.broadcasted_iota(jnp.int32, (LANES, 2 * LANES), 1) < LANES
    head_sum = jnp.where(lane_head0 == col_head0, 1.0, 0.0).astype(bf16)

    q_scale = DA_HEAD_DIM ** -0.5 * LOG2_E
    for r in range(res):
        rows = pl.ds(r, per, stride=res)
        qs_ref[r * per:(r + 1) * per, :] = q_ref[0, rows, :] * q_scale
        ks_ref[r * per:(r + 1) * per, :] = k_ref[0, rows, :]
        vs_ref[r * per:(r + 1) * per, :] = v_ref[0, rows, :]

    for br, (window, dil) in enumerate(DA_BRANCHES):
        assert window // dil == blk and res % dil == 0
        runs = res // dil
        run_len = blk // runs
        nblk = seq // (dil * blk)
        shift = run_len.bit_length() - 1

        def pos(a, runs=runs, run_len=run_len, shift=shift):
            return runs * (a & (run_len - 1)) + (a >> shift)

        from_cur = pos(bi) <= pos(ai)

        n_iter = dil * nblk // DA_UNROLL
        assert n_iter * DA_UNROLL == dil * nblk

        def tile(ref, offs, run_len=run_len, runs=runs):
            parts = [ref[pl.ds(o, run_len), :] for o in offs]
            return parts[0] if runs == 1 else jnp.concatenate(parts, axis=0)

        def blocks_of(it, dil=dil, runs=runs, run_len=run_len, nblk=nblk):
            out = []
            for u in range(DA_UNROLL):
                idx = it * DA_UNROLL + u
                cls = idx // nblk
                n = idx - cls * nblk
                n_prev = jnp.maximum(n - 1, 0)
                cur = [pl.multiple_of((dil * m + cls) * per + run_len * n, SUBLANES) for m in range(runs)]
                prev = [pl.multiple_of((dil * m + cls) * per + run_len * n_prev, SUBLANES) for m in range(runs)]
                out.append((cur, prev, n > 0))
            return out

        def masked_scores(blocks, tile=tile, from_cur=from_cur):
            out = []
            for cur, prev, has_prev in blocks:
                q = tile(qs_ref, cur).astype(bf16)
                zero = jnp.zeros_like(q)
                q2 = jnp.concatenate([jnp.where(head0, q, zero), jnp.where(head0, zero, q)], axis=0)
                s_c = _dot_nt(q2, tile(ks_ref, cur))
                s_p = _dot_nt(q2, tile(ks_ref, prev))
                out.append(jnp.where(from_cur, s_c, jnp.where(has_prev, s_p, NEG_BIG)))
            return out

        def softmax_stage(blocks, scores, tile=tile):
            out = []
            for (cur, prev, has_prev), s in zip(blocks, scores):
                sd = jnp.dot((tile(qs_ref, cur) * tile(ks_ref, prev)).astype(bf16), head_sum, preferred_element_type=f32)
                row_max = jnp.max(s, -1, keepdims=True)
                m0 = jnp.maximum(row_max[:blk], sd[:, :LANES])
                m1 = jnp.maximum(row_max[blk:], sd[:, LANES:])
                p = jnp.exp2(s - jnp.concatenate([m0, m1], axis=0))
                top = jnp.where(head0, m0, m1)
                p_d = jnp.where(has_prev, jnp.exp2(jnp.where(head0, sd[:, :LANES], sd[:, LANES:]) - top), 0.0)
                out.append((p.astype(bf16), p_d, top))
            return out

        def output_stage(blocks, probs, br=br, run_len=run_len, from_cur=from_cur, tile=tile):
            for (cur, prev, has_prev), (pb, p_d, top) in zip(blocks, probs):
                zb = jnp.zeros_like(pb)
                acc = (jnp.dot(jnp.where(from_cur, pb, zb), jnp.concatenate([tile(vs_ref, cur).astype(bf16), ones_blk], axis=1),
                               preferred_element_type=f32)
                       + jnp.dot(jnp.where(from_cur, zb, pb), jnp.concatenate([tile(vs_ref, prev).astype(bf16), ones_blk], axis=1),
                                 preferred_element_type=f32))
                den = jnp.where(head0, acc[:blk, LANES:], acc[blk:, LANES:]) + p_d
                o_blk = (jnp.where(head0, acc[:blk, :LANES], acc[blk:, :LANES]) + p_d * tile(vs_ref, prev)) / den
                lse_blk = top + jnp.log2(den)
                for m, o in enumerate(cur):
                    ob_ref[br, pl.ds(o, run_len), :] = o_blk[m * run_len:(m + 1) * run_len]
                    lse_ref[br, pl.ds(o, run_len), :] = lse_blk[m * run_len:(m + 1) * run_len]

        def put_scores(scores):
            for u, sc in enumerate(scores):
                s_ref[u] = sc

        def put_probs(probs):
            for u, (pb, p_d, top) in enumerate(probs):
                p_ref[u] = pb
                st_ref[u, 0] = p_d
                st_ref[u, 1] = top

        put_probs(softmax_stage(blocks_of(0), masked_scores(blocks_of(0))))
        put_scores(masked_scores(blocks_of(min(1, n_iter - 1))))

        def body(it, carry, n_iter=n_iter, blocks_of=blocks_of, masked_scores=masked_scores,
                 softmax_stage=softmax_stage, output_stage=output_stage, put_scores=put_scores, put_probs=put_probs):
            scores = [s_ref[u] for u in range(DA_UNROLL)]
            probs = [(p_ref[u], st_ref[u, 0], st_ref[u, 1]) for u in range(DA_UNROLL)]
            output_stage(blocks_of(it), probs)
            next_probs = softmax_stage(blocks_of(jnp.minimum(it + 1, n_iter - 1)), scores)
            next_scores = masked_scores(blocks_of(jnp.minimum(it + 2, n_iter - 1)))
            put_scores(next_scores)
            put_probs(next_probs)
            return carry

        lax.fori_loop(0, n_iter, body, 0)

    for r in range(res):
        rows = slice(r * per, (r + 1) * per)
        lses = [lse_ref[br, rows, :] for br in range(len(DA_BRANCHES))]
        top = functools.reduce(jnp.maximum, lses)
        wts = [jnp.exp2(l - top) for l in lses]
        num = sum(w * ob_ref[br, rows, :] for br, w in enumerate(wts))
        o_ref[0, pl.ds(r, per, stride=res), :] = num / sum(wts)


def _dilated_attention(qkv3, batch, seq):
    pairs = DA_WIDTH // LANES
    n_br = len(DA_BRANCHES)
    spec = lambda off: pl.BlockSpec((1, seq, LANES), lambda b, p, off=off: (b, 0, off + p))
    return pl.pallas_call(
        functools.partial(_dilated_kernel, seq=seq),
        grid=(batch, pairs),
        in_specs=[spec(0), spec(pairs), spec(2 * pairs)],
        out_specs=pl.BlockSpec((1, seq, LANES), lambda b, p: (b, 0, p)),
        out_shape=jax.ShapeDtypeStruct((batch, seq, DA_WIDTH), f32),
        scratch_shapes=[pltpu.VMEM((seq, LANES), f32)] * 3
        + [pltpu.VMEM((n_br, seq, LANES), f32)] * 2
        + [pltpu.VMEM((DA_UNROLL, 2 * DA_BLOCK, DA_BLOCK), f32),
           pltpu.VMEM((DA_UNROLL, 2 * DA_BLOCK, DA_BLOCK), bf16),
           pltpu.VMEM((DA_UNROLL, 2, DA_BLOCK, LANES), f32)],
        compiler_params=_params(("parallel", "parallel")),
        name="dilated_attention",
    )(qkv3, qkv3, qkv3)


def _mix_out_kernel(oa_ref, ob_ref, h_ref, w_ref, g_ref, b_ref, o_ref):
    w = w_ref[...]
    y = jnp.dot(oa_ref[...], w[:DN_WIDTH], preferred_element_type=f32)
    y = y + jnp.dot(ob_ref[...].astype(bf16), w[DN_WIDTH:], preferred_element_type=f32)
    o_ref[...] = _layer_norm(DEEPNORM_ALPHA * h_ref[...] + y, g_ref[...], b_ref[...])


def _mix_out(o_a, o_b, h, w_out, g, b):
    t = h.shape[0]
    tm = TOKEN_TILE
    fixed = lambda i: (0, 0)
    return pl.pallas_call(
        _mix_out_kernel,
        grid=(t // tm,),
        in_specs=[
            pl.BlockSpec((tm, DN_WIDTH), lambda i: (i, 0)),
            pl.BlockSpec((tm, DA_WIDTH), lambda i: (i, 0)),
            pl.BlockSpec((tm, D_MODEL), lambda i: (i, 0)),
            pl.BlockSpec(w_out.shape, fixed),
            pl.BlockSpec((1, D_MODEL), fixed),
            pl.BlockSpec((1, D_MODEL), fixed),
        ],
        out_specs=pl.BlockSpec((tm, D_MODEL), lambda i: (i, 0)),
        out_shape=jax.ShapeDtypeStruct((t, D_MODEL), f32),
        compiler_params=_params(("parallel",)),
        name="mix_out",
    )(o_a, o_b, h, w_out, g, b)


def _shortconv_kernel(h_ref, win_ref, cw_ref, wout_ref, g_ref, b_ref, o_ref, tail_ref):
    @pl.when(pl.program_id(1) == 0)
    def _():
        tail_ref[...] = jnp.zeros_like(tail_ref)

    tm = h_ref.shape[0]
    d = D_MODEL
    h = h_ref[...]
    hb = h.astype(bf16)
    gate_b = jnp.dot(hb, win_ref[:, :d], preferred_element_type=f32)
    gate_c = jnp.dot(hb, win_ref[:, d:2 * d], preferred_element_type=f32)
    hid = jnp.dot(hb, win_ref[:, 2 * d:], preferred_element_type=f32)
    u = gate_c * hid
    uu = jnp.concatenate([tail_ref[...], u], axis=0)
    cw = cw_ref[...]
    conv = u * cw[SC_CONV - 1:SC_CONV, :]
    for j in range(SC_CONV - 1):
        back = SC_CONV - 1 - j
        conv = conv + uu[SUBLANES - back:SUBLANES - back + tm, :] * cw[j:j + 1, :]
    tail_ref[...] = u[tm - SUBLANES:, :]
    y = jnp.dot((gate_b * conv).astype(bf16), wout_ref[...], preferred_element_type=f32)
    o_ref[...] = _layer_norm(DEEPNORM_ALPHA * h + y, g_ref[...], b_ref[...])


def _shortconv(h, w_in, conv_w, w_out, g, b, batch, seq):
    tm = TOKEN_TILE
    ns = seq // tm
    row = lambda bb, s: (bb * ns + s, 0)
    fixed = lambda bb, s: (0, 0)
    return pl.pallas_call(
        _shortconv_kernel,
        grid=(batch, ns),
        in_specs=[
            pl.BlockSpec((tm, D_MODEL), row),
            pl.BlockSpec(w_in.shape, fixed),
            pl.BlockSpec(conv_w.shape, fixed),
            pl.BlockSpec(w_out.shape, fixed),
            pl.BlockSpec((1, D_MODEL), fixed),
            pl.BlockSpec((1, D_MODEL), fixed),
        ],
        out_specs=pl.BlockSpec((tm, D_MODEL), row),
        out_shape=jax.ShapeDtypeStruct((batch * seq, D_MODEL), f32),
        scratch_shapes=[pltpu.VMEM((SUBLANES, D_MODEL), f32)],
        compiler_params=_params(("arbitrary", "arbitrary")),
        name="shortconv",
    )(h, w_in, conv_w, w_out, g, b)


def _route(h2, rwt_ref, rb_ref, cnt_ref):
    tm = h2.shape[0]
    w_hi, w_lo = _split(rwt_ref[...], 2)
    x_hi, x_lo = _split(h2, 2)
    nt = (((1,), (1,)), ((), ()))
    logits = (lax.dot_general(w_hi, x_hi, nt, preferred_element_type=f32)
              + lax.dot_general(w_hi, x_lo, nt, preferred_element_type=f32)
              + lax.dot_general(w_lo, x_hi, nt, preferred_element_type=f32))
    mx = jnp.max(logits, axis=0, keepdims=True)
    ex = jnp.exp(logits - mx)
    scores = ex / jnp.sum(ex, axis=0, keepdims=True)
    sel = scores + rb_ref[...]
    best = jnp.full((1, tm), -jnp.inf, f32)
    bucket = jnp.zeros((1, tm), jnp.int32)
    s_a = jnp.zeros((1, tm), f32)
    s_b = jnp.zeros((1, tm), f32)
    for grp in range(N_GROUPS):
        for p, (i, j) in enumerate(_PAIRS):
            a, b = grp * EXPERTS_PER_GROUP + i, grp * EXPERTS_PER_GROUP + j
            ps = sel[a:a + 1, :] + sel[b:b + 1, :]
            upd = ps > best
            best = jnp.where(upd, ps, best)
            bucket = jnp.where(upd, grp * len(_PAIRS) + p, bucket)
            s_a = jnp.where(upd, scores[a:a + 1, :], s_a)
            s_b = jnp.where(upd, scores[b:b + 1, :], s_b)
    denom = s_a + s_b
    gate_rows = jnp.concatenate([s_a / denom, s_b / denom, jnp.zeros((GATE_LANES - 2, tm), f32)], axis=0)
    gates = jnp.transpose(gate_rows)
    onehot = jnp.where(lax.broadcasted_iota(jnp.int32, (BUCKET_ROWS, tm), 0) == bucket, 1.0, 0.0)
    before = lax.broadcasted_iota(jnp.int32, (tm, tm), 0) < lax.broadcasted_iota(jnp.int32, (tm, tm), 1)
    prefix = jnp.dot(onehot.astype(bf16), jnp.where(before, 1.0, 0.0).astype(bf16), preferred_element_type=f32)
    cnt = cnt_ref[...]
    rank = jnp.sum(onehot * (prefix + cnt[:, 0:1]), axis=0, keepdims=True)
    cnt_ref[...] = cnt + jnp.sum(onehot, axis=1, keepdims=True)
    return gates, bucket, rank.astype(jnp.int32)


def _xattn_kernel(h_ref, kv_ref, wq_ref, wo_ref, g_ref, b_ref, rwt_ref, rb_ref,
                  hext_ref, route_ref, cnt_out_ref, cnt_ref):
    @pl.when(jnp.logical_and(pl.program_id(0) == 0, pl.program_id(1) == 0))
    def _():
        cnt_ref[...] = jnp.zeros_like(cnt_ref)

    h = h_ref[...]
    tm = h.shape[0]
    q = jnp.dot(h.astype(bf16), wq_ref[...], preferred_element_type=f32).astype(bf16)
    outs = []
    for hd in range(XA_HEADS):
        lo, hi = hd * XA_HEAD_DIM, (hd + 1) * XA_HEAD_DIM
        s = _dot_nt(q[:, lo:hi], kv_ref[:, lo:hi]) * (XA_HEAD_DIM ** -0.5)
        p = jnp.exp(s - jnp.max(s, -1, keepdims=True))
        o = _dot(p, kv_ref[:, D_MODEL + lo:D_MODEL + hi]) / jnp.sum(p, -1, keepdims=True)
        outs.append(o.astype(bf16))
    y = jnp.dot(jnp.concatenate(outs, axis=1), wo_ref[...], preferred_element_type=f32)
    h2 = _layer_norm(DEEPNORM_ALPHA * h + y, g_ref[...], b_ref[...])
    gates, bucket, rank = _route(h2, rwt_ref, rb_ref, cnt_ref)
    hext_ref[:, :D_MODEL] = h2
    hext_ref[:, D_MODEL:] = gates
    route_ref[...] = jnp.concatenate([bucket, rank, jnp.zeros((SUBLANES - 2, tm), jnp.int32)], axis=0)
    cnt_out_ref[...] = cnt_ref[...]


def _xattn_route(h, kv, w_q, w_o, g, b, rwt, rb, batch, seq, n_mem):
    tm = TOKEN_TILE
    ns = seq // tm
    t = batch * seq
    row = lambda bb, s: (bb * ns + s, 0)
    fixed = lambda bb, s: (0, 0)
    return pl.pallas_call(
        _xattn_kernel,
        grid=(batch, ns),
        in_specs=[
            pl.BlockSpec((tm, D_MODEL), row),
            pl.BlockSpec((n_mem, 2 * D_MODEL), lambda bb, s: (bb, 0)),
            pl.BlockSpec(w_q.shape, fixed),
            pl.BlockSpec(w_o.shape, fixed),
            pl.BlockSpec((1, D_MODEL), fixed),
            pl.BlockSpec((1, D_MODEL), fixed),
            pl.BlockSpec(rwt.shape, fixed),
            pl.BlockSpec(rb.shape, fixed),
        ],
        out_specs=[
            pl.BlockSpec((tm, ROW_EXT), row),
            pl.BlockSpec((SUBLANES, tm), lambda bb, s: (0, bb * ns + s)),
            pl.BlockSpec((BUCKET_ROWS, LANES), fixed),
        ],
        out_shape=[
            jax.ShapeDtypeStruct((t, ROW_EXT), f32),
            jax.ShapeDtypeStruct((SUBLANES, t), jnp.int32),
            jax.ShapeDtypeStruct((BUCKET_ROWS, LANES), f32),
        ],
        scratch_shapes=[pltpu.VMEM((BUCKET_ROWS, LANES), f32)],
        compiler_params=_params(("arbitrary", "arbitrary")),
        name="xattn_route",
    )(h, kv, w_q, w_o, g, b, rwt, rb)


def _dispatch_kernel(dest_ref, hext_ref, xs_in_ref, xs_ref, sem):
    del xs_in_ref
    base = pl.program_id(0) * DISPATCH_CHUNK

    def issue(j, carry):
        pltpu.make_async_copy(hext_ref.at[pl.ds(j, 1)], xs_ref.at[pl.ds(dest_ref[base + j], 1)], sem).start()
        return carry

    lax.fori_loop(0, DISPATCH_CHUNK, issue, 0, unroll=8)
    pltpu.make_async_copy(hext_ref, xs_ref.at[pl.ds(0, DISPATCH_CHUNK)], sem).wait()


def _dispatch(dest, hext, n_rows):
    t = hext.shape[0]
    xs0 = jnp.zeros((n_rows, ROW_EXT), f32)
    return pl.pallas_call(
        _dispatch_kernel,
        grid_spec=pltpu.PrefetchScalarGridSpec(
            num_scalar_prefetch=1,
            grid=(t // DISPATCH_CHUNK,),
            in_specs=[pl.BlockSpec((DISPATCH_CHUNK, ROW_EXT), lambda i, dest: (i, 0)),
                      pl.BlockSpec(memory_space=pl.ANY)],
            out_specs=pl.BlockSpec(memory_space=pl.ANY),
            scratch_shapes=[pltpu.SemaphoreType.DMA(())],
        ),
        out_shape=jax.ShapeDtypeStruct((n_rows, ROW_EXT), f32),
        input_output_aliases={2: 0},
        compiler_params=_params(("arbitrary",)),
        name="moe_dispatch",
    )(dest, hext, xs0)


def _experts_kernel(ea_ref, eb_ref, used_ref, xs_ref, wga_ref, wua_ref, wda_ref, wgb_ref, wub_ref, wdb_ref, y_ref):
    i = pl.program_id(0)

    @pl.when(used_ref[i] > 0)
    def _():
        x = xs_ref[:, :D_MODEL].astype(bf16)
        gates = xs_ref[:, D_MODEL:]

        def ffn(wg_ref, wu_ref, wd_ref, gate):
            hg = jnp.dot(x, wg_ref[0].astype(bf16), preferred_element_type=f32)
            hu = jnp.dot(x, wu_ref[0].astype(bf16), preferred_element_type=f32)
            hdn = _silu(hg) * hu * gate
            return jnp.dot(hdn.astype(bf16), wd_ref[0].astype(bf16), preferred_element_type=f32)

        y_ref[...] = ffn(wga_ref, wua_ref, wda_ref, gates[:, 0:1]) + ffn(wgb_ref, wub_ref, wdb_ref, gates[:, 1:2])

    @pl.when(used_ref[i] == 0)
    def _():
        y_ref[...] = jnp.zeros_like(y_ref)


def _experts(ea, eb, used, xs, w_gate, w_up, w_down):
    n_rows = xs.shape[0]
    up_a = pl.BlockSpec((1, D_MODEL, D_EXPERT), lambda i, ea, eb, used: (ea[i], 0, 0))
    up_b = pl.BlockSpec((1, D_MODEL, D_EXPERT), lambda i, ea, eb, used: (eb[i], 0, 0))
    dn_a = pl.BlockSpec((1, D_EXPERT, D_MODEL), lambda i, ea, eb, used: (ea[i], 0, 0))
    dn_b = pl.BlockSpec((1, D_EXPERT, D_MODEL), lambda i, ea, eb, used: (eb[i], 0, 0))
    return pl.pallas_call(
        _experts_kernel,
        grid_spec=pltpu.PrefetchScalarGridSpec(
            num_scalar_prefetch=3,
            grid=(n_rows // MOE_TILE,),
            in_specs=[pl.BlockSpec((MOE_TILE, ROW_EXT), lambda i, ea, eb, used: (i, 0)),
                      up_a, up_a, dn_a, up_b, up_b, dn_b],
            out_specs=pl.BlockSpec((MOE_TILE, D_MODEL), lambda i, ea, eb, used: (i, 0)),
        ),
        out_shape=jax.ShapeDtypeStruct((n_rows, D_MODEL), f32),
        compiler_params=_params(("arbitrary",)),
        name="moe_experts",
    )(ea, eb, used, xs, w_gate, w_up, w_down, w_gate, w_up, w_down)


def _combine_kernel(dest_ref, y_ref, h_ref, g_ref, b_ref, o_ref, buf_ref, sem_ref):
    i = pl.program_id(0)
    n = pl.num_programs(0)
    tm = COMBINE_TILE

    def row_copy(tile, slot, j):
        return pltpu.make_async_copy(y_ref.at[pl.ds(dest_ref[tile * tm + j], 1)],
                                     buf_ref.at[slot, pl.ds(j, 1)], sem_ref.at[slot])

    def gather(tile, slot):
        def issue(j, carry):
            row_copy(tile, slot, j).start()
            return carry
        lax.fori_loop(0, tm, issue, 0, unroll=8)

    @pl.when(i == 0)
    def _():
        gather(0, 0)

    @pl.when(i + 1 < n)
    def _():
        gather(i + 1, (i + 1) % 2)

    slot = i % 2
    pltpu.make_async_copy(y_ref.at[pl.ds(0, tm)], buf_ref.at[slot], sem_ref.at[slot]).wait()
    o_ref[...] = _layer_norm(DEEPNORM_ALPHA * h_ref[...] + buf_ref[slot], g_ref[...], b_ref[...])


def _combine(dest, y, hext, g, b):
    t = hext.shape[0]
    tm = COMBINE_TILE
    fixed = lambda i, dest: (0, 0)
    return pl.pallas_call(
        _combine_kernel,
        grid_spec=pltpu.PrefetchScalarGridSpec(
            num_scalar_prefetch=1,
            grid=(t // tm,),
            in_specs=[
                pl.BlockSpec(memory_space=pl.ANY),
                pl.BlockSpec((tm, D_MODEL), lambda i, dest: (i, 0)),
                pl.BlockSpec((1, D_MODEL), fixed),
                pl.BlockSpec((1, D_MODEL), fixed),
            ],
            out_specs=pl.BlockSpec((tm, D_MODEL), lambda i, dest: (i, 0)),
            scratch_shapes=[pltpu.VMEM((2, tm, D_MODEL), f32), pltpu.SemaphoreType.DMA((2,))],
        ),
        out_shape=jax.ShapeDtypeStruct((t, D_MODEL), f32),
        compiler_params=_params(("arbitrary",)),
        name="moe_combine",
    )(dest, y, hext, g, b)


def _moe_plan(route, counts):
    t = route.shape[1]
    n_rows = t + N_BUCKETS * MOE_TILE
    n_tiles = n_rows // MOE_TILE
    cnt = counts[:N_BUCKETS, 0].astype(jnp.int32)
    padded = (cnt + MOE_TILE - 1) // MOE_TILE * MOE_TILE
    ends = jnp.cumsum(padded)
    starts = ends - padded
    dest = starts[route[0]] + route[1]
    tile_start = jnp.arange(n_tiles, dtype=jnp.int32) * MOE_TILE
    tile_bucket = jnp.sum(tile_start[:, None] >= ends[None, :], axis=1).astype(jnp.int32)
    used = (tile_bucket < N_BUCKETS).astype(jnp.int32)
    tile_bucket = jnp.minimum(tile_bucket, N_BUCKETS - 1)
    pair_a = jnp.array([g * EXPERTS_PER_GROUP + i for g in range(N_GROUPS) for i, _ in _PAIRS], jnp.int32)
    pair_b = jnp.array([g * EXPERTS_PER_GROUP + j for g in range(N_GROUPS) for _, j in _PAIRS], jnp.int32)
    return dest, pair_a[tile_bucket], pair_b[tile_bucket], used, n_rows


def _moe(hext, route, counts, w_gate, w_up, w_down, g, b):
    dest, ea, eb, used, n_rows = _moe_plan(route, counts)
    xs = _dispatch(dest, hext, n_rows)
    y = _experts(ea, eb, used, xs, w_gate, w_up, w_down)
    return _combine(dest, y, hext, g, b)


def kernel(x, mem, ab_w_in, ab_conv_w, ab_a_log, ab_dt_bias, ab_norm_w, ab_w_out, sc_w_in, sc_conv_w, sc_w_out, xa_w_q, xa_w_kv, xa_w_o, router_w, router_b, moe_w_gate, moe_w_up, moe_w_down, ln_g, ln_b):
    batch, seq, d = x.shape
    n_mem = mem.shape[1]
    t = batch * seq
    h = x.reshape(t, d)
    mem2 = mem.reshape(batch * n_mem, d)
    rwt = jnp.transpose(router_w)
    rb = router_b.reshape(N_EXPERTS, 1)
    row = lambda v: v.reshape(1, -1)

    for layer in range(DEPTH):
        i = layer // 2
        if layer % 2 == 0:
            w_in = ab_w_in[i]
            c0, c1, c2 = 3 * DN_WIDTH, 4 * DN_WIDTH, 4 * DN_WIDTH + 2 * DN_HEADS
            w_ba = jnp.pad(w_in[:, c1:c2], ((0, 0), (0, LANES - 2 * DN_HEADS)))
            ws = [w_in[:, :c0].astype(bf16), w_in[:, c0:c1].astype(bf16), w_ba.astype(bf16), w_in[:, c2:].astype(bf16)]
            qkv, z, ba, qkv_d = _project(h, ws, [f32, f32, f32, f32], TOKEN_TILE)
            lane_pad = lambda v: jnp.pad(v.reshape(1, DN_HEADS), ((0, 0), (DN_HEADS, LANES - 2 * DN_HEADS)))
            o_a = _deltanet(qkv, ba, z, ab_conv_w[i], lane_pad(ab_a_log[i]), lane_pad(ab_dt_bias[i]),
                            row(ab_norm_w[i]), batch, seq)
            o_b = _dilated_attention(qkv_d.reshape(batch, seq, 3 * DA_WIDTH), batch, seq).reshape(t, DA_WIDTH)
            h = _mix_out(o_a, o_b, h, ab_w_out[i].astype(bf16), row(ln_g[layer, 0]), row(ln_b[layer, 0]))
        else:
            h = _shortconv(h, sc_w_in[i].astype(bf16), sc_conv_w[i], sc_w_out[i].astype(bf16),
                           row(ln_g[layer, 0]), row(ln_b[layer, 0]), batch, seq)
        (kv,) = _project(mem2, [xa_w_kv[layer].astype(bf16)], [bf16], n_mem)
        hext, route, counts = _xattn_route(h, kv, xa_w_q[layer].astype(bf16), xa_w_o[layer].astype(bf16),
                                           row(ln_g[layer, 1]), row(ln_b[layer, 1]), rwt, rb, batch, seq, n_mem)
        h = _moe(hext, route, counts, moe_w_gate[layer], moe_w_up[layer], moe_w_down[layer],
                 row(ln_g[layer, 2]), row(ln_b[layer, 2]))
    return h.reshape(batch, seq, d)
```

```python
import functools

import jax
import jax.numpy as jnp
from jax import lax
from jax.experimental import pallas as pl
from jax.experimental.pallas import tpu as pltpu

D_MODEL = 1024
DEPTH = 2
DN_HEADS = 4
DN_HEAD_DIM = 128
DN_WIDTH = DN_HEADS * DN_HEAD_DIM
DN_CONV = 4
DN_CHUNK = 64
DA_HEADS = 8
DA_HEAD_DIM = 64
DA_WIDTH = DA_HEADS * DA_HEAD_DIM
DA_BRANCHES = ((128, 1), (512, 4), (2048, 16))
DA_BLOCK = 128
DA_RESIDUES = max(d for _, d in DA_BRANCHES)
SC_CONV = 3
XA_HEADS = 4
XA_HEAD_DIM = D_MODEL // XA_HEADS
N_EXPERTS = 16
N_GROUPS = 4
EXPERTS_PER_GROUP = N_EXPERTS // N_GROUPS
D_EXPERT = D_MODEL // 2
DEEPNORM_ALPHA = (2 * DEPTH) ** 0.25
LN_EPS = 1e-5
RMS_EPS = 1e-6

LANES = 128
SUBLANES = 8
VMEM_LIMIT_BYTES = 48 * 1024 * 1024

TOKEN_TILE = 512
DN_BLOCK = 256
MOE_TILE = 256
COMBINE_TILE = 256
DISPATCH_CHUNK = 512
DA_UNROLL = 2
GATE_LANES = LANES
ROW_EXT = D_MODEL + GATE_LANES

_PAIRS = tuple((i, j) for i in range(EXPERTS_PER_GROUP) for j in range(i + 1, EXPERTS_PER_GROUP))
N_BUCKETS = N_GROUPS * len(_PAIRS)
BUCKET_ROWS = 32

NEG_BIG = -1e30
LOG2_E = 1.4426950408889634

bf16 = jnp.bfloat16
f32 = jnp.float32


def _params(semantics):
    return pltpu.CompilerParams(dimension_semantics=semantics, vmem_limit_bytes=VMEM_LIMIT_BYTES)


def _dot(a, b):
    return jnp.dot(a.astype(bf16), b.astype(bf16), preferred_element_type=f32)


def _dot_nt(a, b):
    return lax.dot_general(a.astype(bf16), b.astype(bf16), (((1,), (1,)), ((), ())), preferred_element_type=f32)


def _dot_tn(a, b):
    return lax.dot_general(a.astype(bf16), b.astype(bf16), (((0,), (0,)), ((), ())), preferred_element_type=f32)


def _split(a, parts):
    out = []
    rem = a
    for _ in range(parts):
        p = rem.astype(bf16)
        out.append(p)
        rem = rem - p.astype(f32)
    return out


def _layer_norm(v, g, b):
    mu = jnp.mean(v, -1, keepdims=True)
    c = v - mu
    var = jnp.mean(c * c, -1, keepdims=True)
    return c * lax.rsqrt(var + LN_EPS) * g + b


def _silu(v):
    return v * jax.nn.sigmoid(v)


def _softplus(v):
    return jnp.maximum(v, 0.0) + jnp.log(1.0 + jnp.exp(-jnp.abs(v)))


def _proj_kernel(x_ref, *refs, n_out):
    w_refs, o_refs = refs[:n_out], refs[n_out:]
    xb = x_ref[...].astype(bf16)
    for w_ref, o_ref in zip(w_refs, o_refs):
        n = w_ref.shape[1]
        for c0 in range(0, n, 512):
            c1 = min(c0 + 512, n)
            o_ref[:, c0:c1] = jnp.dot(xb, w_ref[:, c0:c1], preferred_element_type=f32).astype(o_ref.dtype)


def _project(x, ws, out_dtypes, tm):
    t, k = x.shape
    n_out = len(ws)
    return pl.pallas_call(
        functools.partial(_proj_kernel, n_out=n_out),
        grid=(t // tm,),
        in_specs=[pl.BlockSpec((tm, k), lambda i: (i, 0))]
        + [pl.BlockSpec(w.shape, lambda i: (0, 0)) for w in ws],
        out_specs=[pl.BlockSpec((tm, w.shape[1]), lambda i: (i, 0)) for w in ws],
        out_shape=[jax.ShapeDtypeStruct((t, w.shape[1]), dt) for w, dt in zip(ws, out_dtypes)],
        compiler_params=_params(("parallel",)),
        name="project",
    )(x, *ws)


def _deltanet_kernel(qkv_ref, ba_ref, z_ref, cw_ref, alog_ref, dtb_ref, nw_ref, o_ref, state_ref, xx_ref):
    @pl.when(pl.program_id(1) == 0)
    def _():
        state_ref[...] = jnp.zeros_like(state_ref)
        xx_ref[0:SUBLANES, :] = jnp.zeros((SUBLANES, xx_ref.shape[1]), f32)

    rows, ch, hd, width = DN_BLOCK, DN_CHUNK, DN_HEAD_DIM, DN_WIDTH
    sup = 2 * ch
    n_sup = rows // sup

    x = qkv_ref[...]
    xx_ref[SUBLANES:, :] = x
    cw = cw_ref[...]
    conv = x * cw[DN_CONV - 1:DN_CONV, :]
    for j in range(DN_CONV - 1):
        back = DN_CONV - 1 - j
        conv = conv + xx_ref[pl.ds(SUBLANES - back, rows), :] * cw[j:j + 1, :]
    xx_ref[0:SUBLANES, :] = x[rows - SUBLANES:, :]
    act = _silu(conv)

    ba = ba_ref[...]
    beta_all = jax.nn.sigmoid(ba)
    g_all = -jnp.exp(alog_ref[...]) * _softplus(ba + dtb_ref[...])

    ri = lax.broadcasted_iota(jnp.int32, (sup, sup), 0)
    ci = lax.broadcasted_iota(jnp.int32, (sup, sup), 1)
    same = (ri >= ch) == (ci >= ch)
    incl = jnp.logical_and(ri >= ci, same)
    strict = jnp.logical_and(ri > ci, same)
    lower_ones = jnp.where(incl, 1.0, 0.0).astype(bf16)
    eye = jnp.where(ri == ci, 1.0, 0.0)
    first_rows = lax.broadcasted_iota(jnp.int32, (sup, hd), 0) < ch
    z = z_ref[...]
    nw = nw_ref[...]

    chains = [(s, h) for s in range(n_sup) for h in range(DN_HEADS)]
    g_cum_all, g_cum_t = [], []
    for s in range(n_sup):
        g_blk = g_all[s * sup:(s + 1) * sup, :]
        gc = sum(jnp.dot(lower_ones, p, preferred_element_type=f32) for p in _split(g_blk, 2))
        g_cum_all.append(gc)
        g_cum_t.append(jnp.transpose(gc))

    st = {}
    for s, h in chains:
        r0 = s * sup
        q = act[r0:r0 + sup, h * hd:(h + 1) * hd]
        k = act[r0:r0 + sup, width + h * hd:width + (h + 1) * hd]
        v = act[r0:r0 + sup, 2 * width + h * hd:2 * width + (h + 1) * hd]
        q = q * (lax.rsqrt(jnp.sum(q * q, -1, keepdims=True) + RMS_EPS) * (hd ** -0.5))
        k = k * lax.rsqrt(jnp.sum(k * k, -1, keepdims=True) + RMS_EPS)
        beta_b = jnp.broadcast_to(beta_all[r0:r0 + sup, h:h + 1], (sup, hd))
        g_i = jnp.broadcast_to(g_cum_all[s][:, DN_HEADS + h:DN_HEADS + h + 1], (sup, hd))
        g_j = jnp.broadcast_to(g_cum_t[s][DN_HEADS + h:DN_HEADS + h + 1, :], (sup, sup))
        decay = jnp.where(incl, jnp.exp(jnp.minimum(g_i - g_j, 0.0)), 0.0)
        e_g = jnp.exp(g_i)
        g_last = jnp.where(first_rows, g_i[ch - 1:ch, :], g_i[sup - 1:sup, :])
        st[s, h] = dict(q=q, k=k, kb=k * beta_b, vb=v * beta_b, decay=decay, e_g=e_g,
                        k_tail=k * jnp.exp(g_last - g_i),
                        gl=(jnp.exp(g_i[ch - 1:ch, :]), jnp.exp(g_i[sup - 1:sup, :])))

    for c in chains:
        d = st[c]
        d["a"] = jnp.where(strict, _dot_nt(d["kb"], d["k"]) * d["decay"], 0.0)
        d["attn"] = jnp.where(incl, _dot_nt(d["q"], d["k"]) * d["decay"], 0.0)
    for c in chains:
        st[c]["t_inv"] = eye - st[c]["a"]
        st[c]["pw"] = st[c]["a"]
    for _ in range(5):
        for c in chains:
            st[c]["pw"] = _dot(st[c]["pw"], st[c]["pw"])
        for c in chains:
            st[c]["t_inv"] = st[c]["t_inv"] + _dot(st[c]["t_inv"], st[c]["pw"])
    for c in chains:
        d = st[c]
        sol = _dot(d["t_inv"], jnp.concatenate([d["vb"], d["kb"] * d["e_g"]], axis=1))
        d["u"], d["w"] = sol[:, :hd], sol[:, hd:]
    for c in chains:
        d = st[c]
        uw = jnp.concatenate([d["u"], d["w"]], axis=1).astype(bf16)
        mix = jnp.dot(d["attn"].astype(bf16), uw, preferred_element_type=f32)
        d["au"] = mix[:, :hd]
        d["qe"] = d["q"] * d["e_g"] - mix[:, hd:]
        kt = d["k_tail"].astype(bf16)
        zero = jnp.zeros_like(kt)
        d["nu_pw"] = [_dot_tn(jnp.where(first_rows == first, kt, zero), uw)
                      for first in (True, False)]

    for s in range(n_sup):
        for half in range(2):
            for h in range(DN_HEADS):
                d = st[s, h]
                state = state_ref[h]
                lo = half * ch
                o = _dot(d["qe"][lo:lo + ch], state) + d["au"][lo:lo + ch]
                nu_pw = d["nu_pw"][half]
                state_ref[h] = state * d["gl"][half] - _dot(nu_pw[:, hd:], state) + nu_pw[:, :hd]
                o = o * lax.rsqrt(jnp.mean(o * o, -1, keepdims=True) + RMS_EPS) * nw
                r0 = s * sup + lo
                o_ref[r0:r0 + ch, h * hd:(h + 1) * hd] = (o * _silu(z[r0:r0 + ch, h * hd:(h + 1) * hd])).astype(o_ref.dtype)


def _deltanet(qkv, ba, z, conv_w, alog_row, dtb_row, norm_w, batch, seq):
    nb = seq // DN_BLOCK
    row = lambda b, c: (b * nb + c, 0)
    fixed = lambda b, c: (0, 0)
    return pl.pallas_call(
        _deltanet_kernel,
        grid=(batch, nb),
        in_specs=[
            pl.BlockSpec((DN_BLOCK, 3 * DN_WIDTH), row),
            pl.BlockSpec((DN_BLOCK, LANES), row),
            pl.BlockSpec((DN_BLOCK, DN_WIDTH), row),
            pl.BlockSpec(conv_w.shape, fixed),
            pl.BlockSpec((1, LANES), fixed),
            pl.BlockSpec((1, LANES), fixed),
            pl.BlockSpec((1, DN_HEAD_DIM), fixed),
        ],
        out_specs=pl.BlockSpec((DN_BLOCK, DN_WIDTH), row),
        out_shape=jax.ShapeDtypeStruct((batch * seq, DN_WIDTH), bf16),
        scratch_shapes=[
            pltpu.VMEM((DN_HEADS, DN_HEAD_DIM, DN_HEAD_DIM), f32),
            pltpu.VMEM((SUBLANES + DN_BLOCK, 3 * DN_WIDTH), f32),
        ],
        compiler_params=_params(("arbitrary", "arbitrary")),
        name="deltanet",
    )(qkv, ba, z, conv_w, alog_row, dtb_row, norm_w)


def _dilated_kernel(q_ref, k_ref, v_ref, o_ref, qs_ref, ks_ref, vs_ref, ob_ref, lse_ref, s_ref, p_ref, st_ref, *, seq):
    blk, res = DA_BLOCK, DA_RESIDUES
    per = seq // res
    head0 = lax.broadcasted_iota(jnp.int32, (blk, LANES), 1) < DA_HEAD_DIM
    ai = lax.broadcasted_iota(jnp.int32, (2 * blk, blk), 0) & (blk - 1)
    bi = lax.broadcasted_iota(jnp.int32, (2 * blk, blk), 1)
    ones_blk = jnp.ones((blk, LANES), bf16)
    lane_head0 = lax.broadcasted_iota(jnp.int32, (LANES, 2 * LANES), 0) < DA_HEAD_DIM
    col_head0 = lax.broadcasted_iota(jnp.int32, (LANES, 2 * LANES), 1) < LANES
    head_sum = jnp.where(lane_head0 == col_head0, 1.0, 0.0).astype(bf16)

    q_scale = DA_HEAD_DIM ** -0.5 * LOG2_E
    for r in range(res):
        rows = pl.ds(r, per, stride=res)
        qs_ref[r * per:(r + 1) * per, :] = q_ref[0, rows, :] * q_scale
        ks_ref[r * per:(r + 1) * per, :] = k_ref[0, rows, :]
        vs_ref[r * per:(r + 1) * per, :] = v_ref[0, rows, :]

    for br, (window, dil) in enumerate(DA_BRANCHES):
        assert window // dil == blk and res % dil == 0
        runs = res // dil
        run_len = blk // runs
        nblk = seq // (dil * blk)
        shift = run_len.bit_length() - 1

        def pos(a, runs=runs, run_len=run_len, shift=shift):
            return runs * (a & (run_len - 1)) + (a >> shift)

        from_cur = pos(bi) <= pos(ai)

        n_iter = dil * nblk // DA_UNROLL
        assert n_iter * DA_UNROLL == dil * nblk

        def tile(ref, offs, run_len=run_len, runs=runs):
            parts = [ref[pl.ds(o, run_len), :] for o in offs]
            return parts[0] if runs == 1 else jnp.concatenate(parts, axis=0)

        def blocks_of(it, dil=dil, runs=runs, run_len=run_len, nblk=nblk):
            out = []
            for u in range(DA_UNROLL):
                idx = it * DA_UNROLL + u
                cls = idx // nblk
                n = idx - cls * nblk
                n_prev = jnp.maximum(n - 1, 0)
                cur = [pl.multiple_of((dil * m + cls) * per + run_len * n, SUBLANES) for m in range(runs)]
                prev = [pl.multiple_of((dil * m + cls) * per + run_len * n_prev, SUBLANES) for m in range(runs)]
                out.append((cur, prev, n > 0))
            return out

        def masked_scores(blocks, tile=tile, from_cur=from_cur):
            out = []
            for cur, prev, has_prev in blocks:
                q = tile(qs_ref, cur).astype(bf16)
                zero = jnp.zeros_like(q)
                q2 = jnp.concatenate([jnp.where(head0, q, zero), jnp.where(head0, zero, q)], axis=0)
                s_c = _dot_nt(q2, tile(ks_ref, cur))
                s_p = _dot_nt(q2, tile(ks_ref, prev))
                out.append(jnp.where(from_cur, s_c, jnp.where(has_prev, s_p, NEG_BIG)))
            return out

        def softmax_stage(blocks, scores, tile=tile):
            out = []
            for (cur, prev, has_prev), s in zip(blocks, scores):
                sd = jnp.dot((tile(qs_ref, cur) * tile(ks_ref, prev)).astype(bf16), head_sum, preferred_element_type=f32)
                row_max = jnp.max(s, -1, keepdims=True)
                m0 = jnp.maximum(row_max[:blk], sd[:, :LANES])
                m1 = jnp.maximum(row_max[blk:], sd[:, LANES:])
                p = jnp.exp2(s - jnp.concatenate([m0, m1], axis=0))
                top = jnp.where(head0, m0, m1)
                p_d = jnp.where(has_prev, jnp.exp2(jnp.where(head0, sd[:, :LANES], sd[:, LANES:]) - top), 0.0)
                out.append((p.astype(bf16), p_d, top))
            return out

        def output_stage(blocks, probs, br=br, run_len=run_len, from_cur=from_cur, tile=tile):
            for (cur, prev, has_prev), (pb, p_d, top) in zip(blocks, probs):
                zb = jnp.zeros_like(pb)
                acc = (jnp.dot(jnp.where(from_cur, pb, zb), jnp.concatenate([tile(vs_ref, cur).astype(bf16), ones_blk], axis=1),
                               preferred_element_type=f32)
                       + jnp.dot(jnp.where(from_cur, zb, pb), jnp.concatenate([tile(vs_ref, prev).astype(bf16), ones_blk], axis=1),
                                 preferred_element_type=f32))
                den = jnp.where(head0, acc[:blk, LANES:], acc[blk:, LANES:]) + p_d
                o_blk = (jnp.where(head0, acc[:blk, :LANES], acc[blk:, :LANES]) + p_d * tile(vs_ref, prev)) / den
                lse_blk = top + jnp.log2(den)
                for m, o in enumerate(cur):
                    ob_ref[br, pl.ds(o, run_len), :] = o_blk[m * run_len:(m + 1) * run_len]
                    lse_ref[br, pl.ds(o, run_len), :] = lse_blk[m * run_len:(m + 1) * run_len]

        def put_scores(scores):
            for u, sc in enumerate(scores):
                s_ref[u] = sc

        def put_probs(probs):
            for u, (pb, p_d, top) in enumerate(probs):
                p_ref[u] = pb
                st_ref[u, 0] = p_d
                st_ref[u, 1] = top

        put_probs(softmax_stage(blocks_of(0), masked_scores(blocks_of(0))))
        put_scores(masked_scores(blocks_of(min(1, n_iter - 1))))

        def body(it, carry, n_iter=n_iter, blocks_of=blocks_of, masked_scores=masked_scores,
                 softmax_stage=softmax_stage, output_stage=output_stage, put_scores=put_scores, put_probs=put_probs):
            scores = [s_ref[u] for u in range(DA_UNROLL)]
            probs = [(p_ref[u], st_ref[u, 0], st_ref[u, 1]) for u in range(DA_UNROLL)]
            output_stage(blocks_of(it), probs)
            next_probs = softmax_stage(blocks_of(jnp.minimum(it + 1, n_iter - 1)), scores)
            next_scores = masked_scores(blocks_of(jnp.minimum(it + 2, n_iter - 1)))
            put_scores(next_scores)
            put_probs(next_probs)
            return carry

        lax.fori_loop(0, n_iter, body, 0)

    for r in range(res):
        rows = slice(r * per, (r + 1) * per)
        lses = [lse_ref[br, rows, :] for br in range(len(DA_BRANCHES))]
        top = functools.reduce(jnp.maximum, lses)
        wts = [jnp.exp2(l - top) for l in lses]
        num = sum(w * ob_ref[br, rows, :] for br, w in enumerate(wts))
        o_ref[0, pl.ds(r, per, stride=res), :] = num / sum(wts)


def _dilated_attention(qkv3, batch, seq):
    pairs = DA_WIDTH // LANES
    n_br = len(DA_BRANCHES)
    spec = lambda off: pl.BlockSpec((1, seq, LANES), lambda b, p, off=off: (b, 0, off + p))
    return pl.pallas_call(
        functools.partial(_dilated_kernel, seq=seq),
        grid=(batch, pairs),
        in_specs=[spec(0), spec(pairs), spec(2 * pairs)],
        out_specs=pl.BlockSpec((1, seq, LANES), lambda b, p: (b, 0, p)),
        out_shape=jax.ShapeDtypeStruct((batch, seq, DA_WIDTH), f32),
        scratch_shapes=[pltpu.VMEM((seq, LANES), f32)] * 3
        + [pltpu.VMEM((n_br, seq, LANES), f32)] * 2
        + [pltpu.VMEM((DA_UNROLL, 2 * DA_BLOCK, DA_BLOCK), f32),
           pltpu.VMEM((DA_UNROLL, 2 * DA_BLOCK, DA_BLOCK), bf16),
           pltpu.VMEM((DA_UNROLL, 2, DA_BLOCK, LANES), f32)],
        compiler_params=_params(("parallel", "parallel")),
        name="dilated_attention",
    )(qkv3, qkv3, qkv3)


def _mix_out_kernel(oa_ref, ob_ref, h_ref, w_ref, g_ref, b_ref, o_ref):
    w = w_ref[...]
    y = jnp.dot(oa_ref[...], w[:DN_WIDTH], preferred_element_type=f32)
    y = y + jnp.dot(ob_ref[...].astype(bf16), w[DN_WIDTH:], preferred_element_type=f32)
    o_ref[...] = _layer_norm(DEEPNORM_ALPHA * h_ref[...] + y, g_ref[...], b_ref[...])


def _mix_out(o_a, o_b, h, w_out, g, b):
    t = h.shape[0]
    tm = TOKEN_TILE
    fixed = lambda i: (0, 0)
    return pl.pallas_call(
        _mix_out_kernel,
        grid=(t // tm,),
        in_specs=[
            pl.BlockSpec((tm, DN_WIDTH), lambda i: (i, 0)),
            pl.BlockSpec((tm, DA_WIDTH), lambda i: (i, 0)),
            pl.BlockSpec((tm, D_MODEL), lambda i: (i, 0)),
            pl.BlockSpec(w_out.shape, fixed),
            pl.BlockSpec((1, D_MODEL), fixed),
            pl.BlockSpec((1, D_MODEL), fixed),
        ],
        out_specs=pl.BlockSpec((tm, D_MODEL), lambda i: (i, 0)),
        out_shape=jax.ShapeDtypeStruct((t, D_MODEL), f32),
        compiler_params=_params(("parallel",)),
        name="mix_out",
    )(o_a, o_b, h, w_out, g, b)


def _shortconv_kernel(h_ref, win_ref, cw_ref, wout_ref, g_ref, b_ref, o_ref, tail_ref):
    @pl.when(pl.program_id(1) == 0)
    def _():
        tail_ref[...] = jnp.zeros_like(tail_ref)

    tm = h_ref.shape[0]
    d = D_MODEL
    h = h_ref[...]
    hb = h.astype(bf16)
    gate_b = jnp.dot(hb, win_ref[:, :d], preferred_element_type=f32)
    gate_c = jnp.dot(hb, win_ref[:, d:2 * d], preferred_element_type=f32)
    hid = jnp.dot(hb, win_ref[:, 2 * d:], preferred_element_type=f32)
    u = gate_c * hid
    uu = jnp.concatenate([tail_ref[...], u], axis=0)
    cw = cw_ref[...]
    conv = u * cw[SC_CONV - 1:SC_CONV, :]
    for j in range(SC_CONV - 1):
        back = SC_CONV - 1 - j
        conv = conv + uu[SUBLANES - back:SUBLANES - back + tm, :] * cw[j:j + 1, :]
    tail_ref[...] = u[tm - SUBLANES:, :]
    y = jnp.dot((gate_b * conv).astype(bf16), wout_ref[...], preferred_element_type=f32)
    o_ref[...] = _layer_norm(DEEPNORM_ALPHA * h + y, g_ref[...], b_ref[...])


def _shortconv(h, w_in, conv_w, w_out, g, b, batch, seq):
    tm = TOKEN_TILE
    ns = seq // tm
    row = lambda bb, s: (bb * ns + s, 0)
    fixed = lambda bb, s: (0, 0)
    return pl.pallas_call(
        _shortconv_kernel,
        grid=(batch, ns),
        in_specs=[
            pl.BlockSpec((tm, D_MODEL), row),
            pl.BlockSpec(w_in.shape, fixed),
            pl.BlockSpec(conv_w.shape, fixed),
            pl.BlockSpec(w_out.shape, fixed),
            pl.BlockSpec((1, D_MODEL), fixed),
            pl.BlockSpec((1, D_MODEL), fixed),
        ],
        out_specs=pl.BlockSpec((tm, D_MODEL), row),
        out_shape=jax.ShapeDtypeStruct((batch * seq, D_MODEL), f32),
        scratch_shapes=[pltpu.VMEM((SUBLANES, D_MODEL), f32)],
        compiler_params=_params(("arbitrary", "arbitrary")),
        name="shortconv",
    )(h, w_in, conv_w, w_out, g, b)


def _route(h2, rwt_ref, rb_ref, cnt_ref):
    tm = h2.shape[0]
    w_hi, w_lo = _split(rwt_ref[...], 2)
    x_hi, x_lo = _split(h2, 2)
    nt = (((1,), (1,)), ((), ()))
    logits = (lax.dot_general(w_hi, x_hi, nt, preferred_element_type=f32)
              + lax.dot_general(w_hi, x_lo, nt, preferred_element_type=f32)
              + lax.dot_general(w_lo, x_hi, nt, preferred_element_type=f32))
    mx = jnp.max(logits, axis=0, keepdims=True)
    ex = jnp.exp(logits - mx)
    scores = ex / jnp.sum(ex, axis=0, keepdims=True)
    sel = scores + rb_ref[...]
    best = jnp.full((1, tm), -jnp.inf, f32)
    bucket = jnp.zeros((1, tm), jnp.int32)
    s_a = jnp.zeros((1, tm), f32)
    s_b = jnp.zeros((1, tm), f32)
    for grp in range(N_GROUPS):
        for p, (i, j) in enumerate(_PAIRS):
            a, b = grp * EXPERTS_PER_GROUP + i, grp * EXPERTS_PER_GROUP + j
            ps = sel[a:a + 1, :] + sel[b:b + 1, :]
            upd = ps > best
            best = jnp.where(upd, ps, best)
            bucket = jnp.where(upd, grp * len(_PAIRS) + p, bucket)
            s_a = jnp.where(upd, scores[a:a + 1, :], s_a)
            s_b = jnp.where(upd, scores[b:b + 1, :], s_b)
    denom = s_a + s_b
    gate_rows = jnp.concatenate([s_a / denom, s_b / denom, jnp.zeros((GATE_LANES - 2, tm), f32)], axis=0)
    gates = jnp.transpose(gate_rows)
    onehot = jnp.where(lax.broadcasted_iota(jnp.int32, (BUCKET_ROWS, tm), 0) == bucket, 1.0, 0.0)
    before = lax.broadcasted_iota(jnp.int32, (tm, tm), 0) < lax.broadcasted_iota(jnp.int32, (tm, tm), 1)
    prefix = jnp.dot(onehot.astype(bf16), jnp.where(before, 1.0, 0.0).astype(bf16), preferred_element_type=f32)
    cnt = cnt_ref[...]
    rank = jnp.sum(onehot * (prefix + cnt[:, 0:1]), axis=0, keepdims=True)
    cnt_ref[...] = cnt + jnp.sum(onehot, axis=1, keepdims=True)
    return gates, bucket, rank.astype(jnp.int32)


def _xattn_kernel(h_ref, kv_ref, wq_ref, wo_ref, g_ref, b_ref, rwt_ref, rb_ref,
                  hext_ref, route_ref, cnt_out_ref, cnt_ref):
    @pl.when(jnp.logical_and(pl.program_id(0) == 0, pl.program_id(1) == 0))
    def _():
        cnt_ref[...] = jnp.zeros_like(cnt_ref)

    h = h_ref[...]
    tm = h.shape[0]
    q = jnp.dot(h.astype(bf16), wq_ref[...], preferred_element_type=f32).astype(bf16)
    outs = []
    for hd in range(XA_HEADS):
        lo, hi = hd * XA_HEAD_DIM, (hd + 1) * XA_HEAD_DIM
        s = _dot_nt(q[:, lo:hi], kv_ref[:, lo:hi]) * (XA_HEAD_DIM ** -0.5)
        p = jnp.exp(s - jnp.max(s, -1, keepdims=True))
        o = _dot(p, kv_ref[:, D_MODEL + lo:D_MODEL + hi]) / jnp.sum(p, -1, keepdims=True)
        outs.append(o.astype(bf16))
    y = jnp.dot(jnp.concatenate(outs, axis=1), wo_ref[...], preferred_element_type=f32)
    h2 = _layer_norm(DEEPNORM_ALPHA * h + y, g_ref[...], b_ref[...])
    gates, bucket, rank = _route(h2, rwt_ref, rb_ref, cnt_ref)
    hext_ref[:, :D_MODEL] = h2
    hext_ref[:, D_MODEL:] = gates
    route_ref[...] = jnp.concatenate([bucket, rank, jnp.zeros((SUBLANES - 2, tm), jnp.int32)], axis=0)
    cnt_out_ref[...] = cnt_ref[...]


def _xattn_route(h, kv, w_q, w_o, g, b, rwt, rb, batch, seq, n_mem):
    tm = TOKEN_TILE
    ns = seq // tm
    t = batch * seq
    row = lambda bb, s: (bb * ns + s, 0)
    fixed = lambda bb, s: (0, 0)
    return pl.pallas_call(
        _xattn_kernel,
        grid=(batch, ns),
        in_specs=[
            pl.BlockSpec((tm, D_MODEL), row),
            pl.BlockSpec((n_mem, 2 * D_MODEL), lambda bb, s: (bb, 0)),
            pl.BlockSpec(w_q.shape, fixed),
            pl.BlockSpec(w_o.shape, fixed),
            pl.BlockSpec((1, D_MODEL), fixed),
            pl.BlockSpec((1, D_MODEL), fixed),
            pl.BlockSpec(rwt.shape, fixed),
            pl.BlockSpec(rb.shape, fixed),
        ],
        out_specs=[
            pl.BlockSpec((tm, ROW_EXT), row),
            pl.BlockSpec((SUBLANES, tm), lambda bb, s: (0, bb * ns + s)),
            pl.BlockSpec((BUCKET_ROWS, LANES), fixed),
        ],
        out_shape=[
            jax.ShapeDtypeStruct((t, ROW_EXT), f32),
            jax.ShapeDtypeStruct((SUBLANES, t), jnp.int32),
            jax.ShapeDtypeStruct((BUCKET_ROWS, LANES), f32),
        ],
        scratch_shapes=[pltpu.VMEM((BUCKET_ROWS, LANES), f32)],
        compiler_params=_params(("arbitrary", "arbitrary")),
        name="xattn_route",
    )(h, kv, w_q, w_o, g, b, rwt, rb)


def _dispatch_kernel(dest_ref, hext_ref, xs_in_ref, xs_ref, sem):
    del xs_in_ref
    base = pl.program_id(0) * DISPATCH_CHUNK

    def issue(j, carry):
        pltpu.make_async_copy(hext_ref.at[pl.ds(j, 1)], xs_ref.at[pl.ds(dest_ref[base + j], 1)], sem).start()
        return carry

    lax.fori_loop(0, DISPATCH_CHUNK, issue, 0, unroll=8)
    pltpu.make_async_copy(hext_ref, xs_ref.at[pl.ds(0, DISPATCH_CHUNK)], sem).wait()


def _dispatch(dest, hext, n_rows):
    t = hext.shape[0]
    xs0 = jnp.zeros((n_rows, ROW_EXT), f32)
    return pl.pallas_call(
        _dispatch_kernel,
        grid_spec=pltpu.PrefetchScalarGridSpec(
            num_scalar_prefetch=1,
            grid=(t // DISPATCH_CHUNK,),
            in_specs=[pl.BlockSpec((DISPATCH_CHUNK, ROW_EXT), lambda i, dest: (i, 0)),
                      pl.BlockSpec(memory_space=pl.ANY)],
            out_specs=pl.BlockSpec(memory_space=pl.ANY),
            scratch_shapes=[pltpu.SemaphoreType.DMA(())],
        ),
        out_shape=jax.ShapeDtypeStruct((n_rows, ROW_EXT), f32),
        input_output_aliases={2: 0},
        compiler_params=_params(("arbitrary",)),
        name="moe_dispatch",
    )(dest, hext, xs0)


def _experts_kernel(ea_ref, eb_ref, used_ref, xs_ref, wga_ref, wua_ref, wda_ref, wgb_ref, wub_ref, wdb_ref, y_ref):
    i = pl.program_id(0)

    @pl.when(used_ref[i] > 0)
    def _():
        x = xs_ref[:, :D_MODEL].astype(bf16)
        gates = xs_ref[:, D_MODEL:]

        def ffn(wg_ref, wu_ref, wd_ref, gate):
            hg = jnp.dot(x, wg_ref[0].astype(bf16), preferred_element_type=f32)
            hu = jnp.dot(x, wu_ref[0].astype(bf16), preferred_element_type=f32)
            hdn = _silu(hg) * hu * gate
            return jnp.dot(hdn.astype(bf16), wd_ref[0].astype(bf16), preferred_element_type=f32)

        y_ref[...] = ffn(wga_ref, wua_ref, wda_ref, gates[:, 0:1]) + ffn(wgb_ref, wub_ref, wdb_ref, gates[:, 1:2])

    @pl.when(used_ref[i] == 0)
    def _():
        y_ref[...] = jnp.zeros_like(y_ref)


def _experts(ea, eb, used, xs, w_gate, w_up, w_down):
    n_rows = xs.shape[0]
    up_a = pl.BlockSpec((1, D_MODEL, D_EXPERT), lambda i, ea, eb, used: (ea[i], 0, 0))
    up_b = pl.BlockSpec((1, D_MODEL, D_EXPERT), lambda i, ea, eb, used: (eb[i], 0, 0))
    dn_a = pl.BlockSpec((1, D_EXPERT, D_MODEL), lambda i, ea, eb, used: (ea[i], 0, 0))
    dn_b = pl.BlockSpec((1, D_EXPERT, D_MODEL), lambda i, ea, eb, used: (eb[i], 0, 0))
    return pl.pallas_call(
        _experts_kernel,
        grid_spec=pltpu.PrefetchScalarGridSpec(
            num_scalar_prefetch=3,
            grid=(n_rows // MOE_TILE,),
            in_specs=[pl.BlockSpec((MOE_TILE, ROW_EXT), lambda i, ea, eb, used: (i, 0)),
                      up_a, up_a, dn_a, up_b, up_b, dn_b],
            out_specs=pl.BlockSpec((MOE_TILE, D_MODEL), lambda i, ea, eb, used: (i, 0)),
        ),
        out_shape=jax.ShapeDtypeStruct((n_rows, D_MODEL), f32),
        compiler_params=_params(("arbitrary",)),
        name="moe_experts",
    )(ea, eb, used, xs, w_gate, w_up, w_down, w_gate, w_up, w_down)


def _combine_kernel(dest_ref, y_ref, h_ref, g_ref, b_ref, o_ref, buf_ref, sem_ref):
    i = pl.program_id(0)
    n = pl.num_programs(0)
    tm = COMBINE_TILE

    def row_copy(tile, slot, j):
        return pltpu.make_async_copy(y_ref.at[pl.ds(dest_ref[tile * tm + j], 1)],
                                     buf_ref.at[slot, pl.ds(j, 1)], sem_ref.at[slot])

    def gather(tile, slot):
        def issue(j, carry):
            row_copy(tile, slot, j).start()
            return carry
        lax.fori_loop(0, tm, issue, 0, unroll=8)

    @pl.when(i == 0)
    def _():
        gather(0, 0)

    @pl.when(i + 1 < n)
    def _():
        gather(i + 1, (i + 1) % 2)

    slot = i % 2
    pltpu.make_async_copy(y_ref.at[pl.ds(0, tm)], buf_ref.at[slot], sem_ref.at[slot]).wait()
    o_ref[...] = _layer_norm(DEEPNORM_ALPHA * h_ref[...] + buf_ref[slot], g_ref[...], b_ref[...])


def _combine(dest, y, hext, g, b):
    t = hext.shape[0]
    tm = COMBINE_TILE
    fixed = lambda i, dest: (0, 0)
    return pl.pallas_call(
        _combine_kernel,
        grid_spec=pltpu.PrefetchScalarGridSpec(
            num_scalar_prefetch=1,
            grid=(t // tm,),
            in_specs=[
                pl.BlockSpec(memory_space=pl.ANY),
                pl.BlockSpec((tm, D_MODEL), lambda i, dest: (i, 0)),
                pl.BlockSpec((1, D_MODEL), fixed),
                pl.BlockSpec((1, D_MODEL), fixed),
            ],
            out_specs=pl.BlockSpec((tm, D_MODEL), lambda i, dest: (i, 0)),
            scratch_shapes=[pltpu.VMEM((2, tm, D_MODEL), f32), pltpu.SemaphoreType.DMA((2,))],
        ),
        out_shape=jax.ShapeDtypeStruct((t, D_MODEL), f32),
        compiler_params=_params(("arbitrary",)),
        name="moe_combine",
    )(dest, y, hext, g, b)


def _moe_plan(route, counts):
    t = route.shape[1]
    n_rows = t + N_BUCKETS * MOE_TILE
    n_tiles = n_rows // MOE_TILE
    cnt = counts[:N_BUCKETS, 0].astype(jnp.int32)
    padded = (cnt + MOE_TILE - 1) // MOE_TILE * MOE_TILE
    ends = jnp.cumsum(padded)
    starts = ends - padded
    dest = starts[route[0]] + route[1]
    tile_start = jnp.arange(n_tiles, dtype=jnp.int32) * MOE_TILE
    tile_bucket = jnp.sum(tile_start[:, None] >= ends[None, :], axis=1).astype(jnp.int32)
    used = (tile_bucket < N_BUCKETS).astype(jnp.int32)
    tile_bucket = jnp.minimum(tile_bucket, N_BUCKETS - 1)
    pair_a = jnp.array([g * EXPERTS_PER_GROUP + i for g in range(N_GROUPS) for i, _ in _PAIRS], jnp.int32)
    pair_b = jnp.array([g * EXPERTS_PER_GROUP + j for g in range(N_GROUPS) for _, j in _PAIRS], jnp.int32)
    return dest, pair_a[tile_bucket], pair_b[tile_bucket], used, n_rows


def _moe(hext, route, counts, w_gate, w_up, w_down, g, b):
    dest, ea, eb, used, n_rows = _moe_plan(route, counts)
    xs = _dispatch(dest, hext, n_rows)
    y = _experts(ea, eb, used, xs, w_gate, w_up, w_down)
    return _combine(dest, y, hext, g, b)


def kernel(x, mem, ab_w_in, ab_conv_w, ab_a_log, ab_dt_bias, ab_norm_w, ab_w_out, sc_w_in, sc_conv_w, sc_w_out, xa_w_q, xa_w_kv, xa_w_o, router_w, router_b, moe_w_gate, moe_w_up, moe_w_down, ln_g, ln_b):
    batch, seq, d = x.shape
    n_mem = mem.shape[1]
    t = batch * seq
    h = x.reshape(t, d)
    mem2 = mem.reshape(batch * n_mem, d)
    rwt = jnp.transpose(router_w)
    rb = router_b.reshape(N_EXPERTS, 1)
    row = lambda v: v.reshape(1, -1)

    for layer in range(DEPTH):
        i = layer // 2
        if layer % 2 == 0:
            w_in = ab_w_in[i]
            c0, c1, c2 = 3 * DN_WIDTH, 4 * DN_WIDTH, 4 * DN_WIDTH + 2 * DN_HEADS
            w_ba = jnp.pad(w_in[:, c1:c2], ((0, 0), (0, LANES - 2 * DN_HEADS)))
            ws = [w_in[:, :c0].astype(bf16), w_in[:, c0:c1].astype(bf16), w_ba.astype(bf16), w_in[:, c2:].astype(bf16)]
            qkv, z, ba, qkv_d = _project(h, ws, [f32, f32, f32, f32], TOKEN_TILE)
            lane_pad = lambda v: jnp.pad(v.reshape(1, DN_HEADS), ((0, 0), (DN_HEADS, LANES - 2 * DN_HEADS)))
            o_a = _deltanet(qkv, ba, z, ab_conv_w[i], lane_pad(ab_a_log[i]), lane_pad(ab_dt_bias[i]),
                            row(ab_norm_w[i]), batch, seq)
            o_b = _dilated_attention(qkv_d.reshape(batch, seq, 3 * DA_WIDTH), batch, seq).reshape(t, DA_WIDTH)
            h = _mix_out(o_a, o_b, h, ab_w_out[i].astype(bf16), row(ln_g[layer, 0]), row(ln_b[layer, 0]))
        else:
            h = _shortconv(h, sc_w_in[i].astype(bf16), sc_conv_w[i], sc_w_out[i].astype(bf16),
                           row(ln_g[layer, 0]), row(ln_b[layer, 0]), batch, seq)
        (kv,) = _project(mem2, [xa_w_kv[layer].astype(bf16)], [bf16], n_mem)
        hext, route, counts = _xattn_route(h, kv, xa_w_q[layer].astype(bf16), xa_w_o[layer].astype(bf16),
                                           row(ln_g[layer, 1]), row(ln_b[layer, 1]), rwt, rb, batch, seq, n_mem)
        h = _moe(hext, route, counts, moe_w_gate[layer], moe_w_up[layer], moe_w_down[layer],
                 row(ln_g[layer, 2]), row(ln_b[layer, 2]))
    return h.reshape(batch, seq, d)
```

```python
import functools

import jax
import jax.numpy as jnp
from jax import lax
from jax.experimental import pallas as pl
from jax.experimental.pallas import tpu as pltpu

D_MODEL = 1024
DEPTH = 2
DN_HEADS = 4
DN_HEAD_DIM = 128
DN_WIDTH = DN_HEADS * DN_HEAD_DIM
DN_CONV = 4
DN_CHUNK = 64
DA_HEADS = 8
DA_HEAD_DIM = 64
DA_WIDTH = DA_HEADS * DA_HEAD_DIM
DA_BRANCHES = ((128, 1), (512, 4), (2048, 16))
DA_BLOCK = 128
DA_RESIDUES = max(d for _, d in DA_BRANCHES)
SC_CONV = 3
XA_HEADS = 4
XA_HEAD_DIM = D_MODEL // XA_HEADS
N_EXPERTS = 16
N_GROUPS = 4
EXPERTS_PER_GROUP = N_EXPERTS // N_GROUPS
D_EXPERT = D_MODEL // 2
DEEPNORM_ALPHA = (2 * DEPTH) ** 0.25
LN_EPS = 1e-5
RMS_EPS = 1e-6

LANES = 128
SUBLANES = 8
VMEM_LIMIT_BYTES = 48 * 1024 * 1024

TOKEN_TILE = 512
DN_BLOCK = 256
MOE_TILE = 256
DISPATCH_CHUNK = 512
DA_UNROLL = 2
GATE_LANES = LANES
ROW_EXT = D_MODEL + GATE_LANES

_PAIRS = tuple((i, j) for i in range(EXPERTS_PER_GROUP) for j in range(i + 1, EXPERTS_PER_GROUP))
N_BUCKETS = N_GROUPS * len(_PAIRS)
BUCKET_ROWS = 32

NEG_BIG = -1e30
LOG2_E = 1.4426950408889634

bf16 = jnp.bfloat16
f32 = jnp.float32


def _params(semantics):
    return pltpu.CompilerParams(dimension_semantics=semantics, vmem_limit_bytes=VMEM_LIMIT_BYTES)


def _dot(a, b):
    return jnp.dot(a.astype(bf16), b.astype(bf16), preferred_element_type=f32)


def _dot_nt(a, b):
    return lax.dot_general(a.astype(bf16), b.astype(bf16), (((1,), (1,)), ((), ())), preferred_element_type=f32)


def _dot_tn(a, b):
    return lax.dot_general(a.astype(bf16), b.astype(bf16), (((0,), (0,)), ((), ())), preferred_element_type=f32)


def _split(a, parts):
    out = []
    rem = a
    for _ in range(parts):
        p = rem.astype(bf16)
        out.append(p)
        rem = rem - p.astype(f32)
    return out


def _layer_norm(v, g, b):
    mu = jnp.mean(v, -1, keepdims=True)
    c = v - mu
    var = jnp.mean(c * c, -1, keepdims=True)
    return c * lax.rsqrt(var + LN_EPS) * g + b


def _silu(v):
    return v * jax.nn.sigmoid(v)


def _softplus(v):
    return jnp.maximum(v, 0.0) + jnp.log(1.0 + jnp.exp(-jnp.abs(v)))


def _proj_kernel(x_ref, *refs, n_out):
    w_refs, o_refs = refs[:n_out], refs[n_out:]
    xb = x_ref[...].astype(bf16)
    for w_ref, o_ref in zip(w_refs, o_refs):
        n = w_ref.shape[1]
        for c0 in range(0, n, 512):
            c1 = min(c0 + 512, n)
            o_ref[:, c0:c1] = jnp.dot(xb, w_ref[:, c0:c1], preferred_element_type=f32).astype(o_ref.dtype)


def _project(x, ws, out_dtypes, tm):
    t, k = x.shape
    n_out = len(ws)
    return pl.pallas_call(
        functools.partial(_proj_kernel, n_out=n_out),
        grid=(t // tm,),
        in_specs=[pl.BlockSpec((tm, k), lambda i: (i, 0))]
        + [pl.BlockSpec(w.shape, lambda i: (0, 0)) for w in ws],
        out_specs=[pl.BlockSpec((tm, w.shape[1]), lambda i: (i, 0)) for w in ws],
        out_shape=[jax.ShapeDtypeStruct((t, w.shape[1]), dt) for w, dt in zip(ws, out_dtypes)],
        compiler_params=_params(("parallel",)),
        name="project",
    )(x, *ws)


def _deltanet_kernel(qkv_ref, ba_ref, z_ref, cw_ref, alog_ref, dtb_ref, nw_ref, o_ref, state_ref, xx_ref):
    @pl.when(pl.program_id(1) == 0)
    def _():
        state_ref[...] = jnp.zeros_like(state_ref)
        xx_ref[0:SUBLANES, :] = jnp.zeros((SUBLANES, xx_ref.shape[1]), f32)

    rows, ch, hd, width = DN_BLOCK, DN_CHUNK, DN_HEAD_DIM, DN_WIDTH
    sup = 2 * ch
    n_sup = rows // sup

    x = qkv_ref[...]
    xx_ref[SUBLANES:, :] = x
    cw = cw_ref[...]
    conv = x * cw[DN_CONV - 1:DN_CONV, :]
    for j in range(DN_CONV - 1):
        back = DN_CONV - 1 - j
        conv = conv + xx_ref[pl.ds(SUBLANES - back, rows), :] * cw[j:j + 1, :]
    xx_ref[0:SUBLANES, :] = x[rows - SUBLANES:, :]
    act = _silu(conv)

    ba = ba_ref[...]
    beta_all = jax.nn.sigmoid(ba)
    g_all = -jnp.exp(alog_ref[...]) * _softplus(ba + dtb_ref[...])

    ri = lax.broadcasted_iota(jnp.int32, (sup, sup), 0)
    ci = lax.broadcasted_iota(jnp.int32, (sup, sup), 1)
    same = (ri >= ch) == (ci >= ch)
    incl = jnp.logical_and(ri >= ci, same)
    strict = jnp.logical_and(ri > ci, same)
    lower_ones = jnp.where(incl, 1.0, 0.0).astype(bf16)
    eye = jnp.where(ri == ci, 1.0, 0.0)
    first_rows = lax.broadcasted_iota(jnp.int32, (sup, hd), 0) < ch
    z = z_ref[...]
    nw = nw_ref[...]

    chains = [(s, h) for s in range(n_sup) for h in range(DN_HEADS)]
    g_cum_all, g_cum_t = [], []
    for s in range(n_sup):
        g_blk = g_all[s * sup:(s + 1) * sup, :]
        gc = sum(jnp.dot(lower_ones, p, preferred_element_type=f32) for p in _split(g_blk, 2))
        g_cum_all.append(gc)
        g_cum_t.append(jnp.transpose(gc))

    st = {}
    for s, h in chains:
        r0 = s * sup
        q = act[r0:r0 + sup, h * hd:(h + 1) * hd]
        k = act[r0:r0 + sup, width + h * hd:width + (h + 1) * hd]
        v = act[r0:r0 + sup, 2 * width + h * hd:2 * width + (h + 1) * hd]
        q = q * (lax.rsqrt(jnp.sum(q * q, -1, keepdims=True) + RMS_EPS) * (hd ** -0.5))
        k = k * lax.rsqrt(jnp.sum(k * k, -1, keepdims=True) + RMS_EPS)
        beta_b = jnp.broadcast_to(beta_all[r0:r0 + sup, h:h + 1], (sup, hd))
        g_i = jnp.broadcast_to(g_cum_all[s][:, DN_HEADS + h:DN_HEADS + h + 1], (sup, hd))
        g_j = jnp.broadcast_to(g_cum_t[s][DN_HEADS + h:DN_HEADS + h + 1, :], (sup, sup))
        decay = jnp.where(incl, jnp.exp(jnp.minimum(g_i - g_j, 0.0)), 0.0)
        e_g = jnp.exp(g_i)
        g_last = jnp.where(first_rows, g_i[ch - 1:ch, :], g_i[sup - 1:sup, :])
        st[s, h] = dict(q=q, k=k, kb=k * beta_b, vb=v * beta_b, decay=decay, e_g=e_g,
                        k_tail=k * jnp.exp(g_last - g_i),
                        gl=(jnp.exp(g_i[ch - 1:ch, :]), jnp.exp(g_i[sup - 1:sup, :])))

    for c in chains:
        d = st[c]
        d["a"] = jnp.where(strict, _dot_nt(d["kb"], d["k"]) * d["decay"], 0.0)
        d["attn"] = jnp.where(incl, _dot_nt(d["q"], d["k"]) * d["decay"], 0.0)
    for c in chains:
        st[c]["t_inv"] = eye - st[c]["a"]
        st[c]["pw"] = st[c]["a"]
    for _ in range(5):
        for c in chains:
            st[c]["pw"] = _dot(st[c]["pw"], st[c]["pw"])
        for c in chains:
            st[c]["t_inv"] = st[c]["t_inv"] + _dot(st[c]["t_inv"], st[c]["pw"])
    for c in chains:
        d = st[c]
        sol = _dot(d["t_inv"], jnp.concatenate([d["vb"], d["kb"] * d["e_g"]], axis=1))
        d["u"], d["w"] = sol[:, :hd], sol[:, hd:]
    for c in chains:
        d = st[c]
        uw = jnp.concatenate([d["u"], d["w"]], axis=1).astype(bf16)
        mix = jnp.dot(d["attn"].astype(bf16), uw, preferred_element_type=f32)
        d["au"] = mix[:, :hd]
        d["qe"] = d["q"] * d["e_g"] - mix[:, hd:]
        kt = d["k_tail"].astype(bf16)
        zero = jnp.zeros_like(kt)
        d["nu_pw"] = [_dot_tn(jnp.where(first_rows == first, kt, zero), uw)
                      for first in (True, False)]

    for s in range(n_sup):
        for half in range(2):
            for h in range(DN_HEADS):
                d = st[s, h]
                state = state_ref[h]
                lo = half * ch
                o = _dot(d["qe"][lo:lo + ch], state) + d["au"][lo:lo + ch]
                nu_pw = d["nu_pw"][half]
                state_ref[h] = state * d["gl"][half] - _dot(nu_pw[:, hd:], state) + nu_pw[:, :hd]
                o = o * lax.rsqrt(jnp.mean(o * o, -1, keepdims=True) + RMS_EPS) * nw
                r0 = s * sup + lo
                o_ref[r0:r0 + ch, h * hd:(h + 1) * hd] = (o * _silu(z[r0:r0 + ch, h * hd:(h + 1) * hd])).astype(o_ref.dtype)


def _deltanet(qkv, ba, z, conv_w, alog_row, dtb_row, norm_w, batch, seq):
    nb = seq // DN_BLOCK
    row = lambda b, c: (b * nb + c, 0)
    fixed = lambda b, c: (0, 0)
    return pl.pallas_call(
        _deltanet_kernel,
        grid=(batch, nb),
        in_specs=[
            pl.BlockSpec((DN_BLOCK, 3 * DN_WIDTH), row),
            pl.BlockSpec((DN_BLOCK, LANES), row),
            pl.BlockSpec((DN_BLOCK, DN_WIDTH), row),
            pl.BlockSpec(conv_w.shape, fixed),
            pl.BlockSpec((1, LANES), fixed),
            pl.BlockSpec((1, LANES), fixed),
            pl.BlockSpec((1, DN_HEAD_DIM), fixed),
        ],
        out_specs=pl.BlockSpec((DN_BLOCK, DN_WIDTH), row),
        out_shape=jax.ShapeDtypeStruct((batch * seq, DN_WIDTH), bf16),
        scratch_shapes=[
            pltpu.VMEM((DN_HEADS, DN_HEAD_DIM, DN_HEAD_DIM), f32),
            pltpu.VMEM((SUBLANES + DN_BLOCK, 3 * DN_WIDTH), f32),
        ],
        compiler_params=_params(("arbitrary", "arbitrary")),
        name="deltanet",
    )(qkv, ba, z, conv_w, alog_row, dtb_row, norm_w)


def _dilated_kernel(q_ref, k_ref, v_ref, o_ref, qs_ref, ks_ref, vs_ref, ob_ref, lse_ref, s_ref, p_ref, st_ref, *, seq):
    blk, res = DA_BLOCK, DA_RESIDUES
    per = seq // res
    head0 = lax.broadcasted_iota(jnp.int32, (blk, LANES), 1) < DA_HEAD_DIM
    ai = lax.broadcasted_iota(jnp.int32, (2 * blk, blk), 0) & (blk - 1)
    bi = lax.broadcasted_iota(jnp.int32, (2 * blk, blk), 1)
    ones_blk = jnp.ones((blk, LANES), bf16)
    lane_head0 = lax.broadcasted_iota(jnp.int32, (LANES, 2 * LANES), 0) < DA_HEAD_DIM
    col_head0 = lax.broadcasted_iota(jnp.int32, (LANES, 2 * LANES), 1) < LANES
    head_sum = jnp.where(lane_head0 == col_head0, 1.0, 0.0).astype(bf16)

    q_scale = DA_HEAD_DIM ** -0.5 * LOG2_E
    for r in range(res):
        rows = pl.ds(r, per, stride=res)
        qs_ref[r * per:(r + 1) * per, :] = q_ref[0, rows, :] * q_scale
        ks_ref[r * per:(r + 1) * per, :] = k_ref[0, rows, :]
        vs_ref[r * per:(r + 1) * per, :] = v_ref[0, rows, :]

    for br, (window, dil) in enumerate(DA_BRANCHES):
        assert window // dil == blk and res % dil == 0
        runs = res // dil
        run_len = blk // runs
        nblk = seq // (dil * blk)
        shift = run_len.bit_length() - 1

        def pos(a, runs=runs, run_len=run_len, shift=shift):
            return runs * (a & (run_len - 1)) + (a >> shift)

        from_cur = pos(bi) <= pos(ai)

        n_iter = dil * nblk // DA_UNROLL
        assert n_iter * DA_UNROLL == dil * nblk

        def tile(ref, offs, run_len=run_len, runs=runs):
            parts = [ref[pl.ds(o, run_len), :] for o in offs]
            return parts[0] if runs == 1 else jnp.concatenate(parts, axis=0)

        def blocks_of(it, dil=dil, runs=runs, run_len=run_len, nblk=nblk):
            out = []
            for u in range(DA_UNROLL):
                idx = it * DA_UNROLL + u
                cls = idx // nblk
                n = idx - cls * nblk
                n_prev = jnp.maximum(n - 1, 0)
                cur = [pl.multiple_of((dil * m + cls) * per + run_len * n, SUBLANES) for m in range(runs)]
                prev = [pl.multiple_of((dil * m + cls) * per + run_len * n_prev, SUBLANES) for m in range(runs)]
                out.append((cur, prev, n > 0))
            return out

        def masked_scores(blocks, tile=tile, from_cur=from_cur):
            out = []
            for cur, prev, has_prev in blocks:
                q = tile(qs_ref, cur).astype(bf16)
                zero = jnp.zeros_like(q)
                q2 = jnp.concatenate([jnp.where(head0, q, zero), jnp.where(head0, zero, q)], axis=0)
                s_c = _dot_nt(q2, tile(ks_ref, cur))
                s_p = _dot_nt(q2, tile(ks_ref, prev))
                out.append(jnp.where(from_cur, s_c, jnp.where(has_prev, s_p, NEG_BIG)))
            return out

        def softmax_stage(blocks, scores, tile=tile):
            out = []
            for (cur, prev, has_prev), s in zip(blocks, scores):
                sd = jnp.dot((tile(qs_ref, cur) * tile(ks_ref, prev)).astype(bf16), head_sum, preferred_element_type=f32)
                row_max = jnp.max(s, -1, keepdims=True)
                m0 = jnp.maximum(row_max[:blk], sd[:, :LANES])
                m1 = jnp.maximum(row_max[blk:], sd[:, LANES:])
                p = jnp.exp2(s - jnp.concatenate([m0, m1], axis=0))
                top = jnp.where(head0, m0, m1)
                p_d = jnp.where(has_prev, jnp.exp2(jnp.where(head0, sd[:, :LANES], sd[:, LANES:]) - top), 0.0)
                out.append((p.astype(bf16), p_d, top))
            return out

        def output_stage(blocks, probs, br=br, run_len=run_len, from_cur=from_cur, tile=tile):
            for (cur, prev, has_prev), (pb, p_d, top) in zip(blocks, probs):
                zb = jnp.zeros_like(pb)
                acc = (jnp.dot(jnp.where(from_cur, pb, zb), jnp.concatenate([tile(vs_ref, cur).astype(bf16), ones_blk], axis=1),
                               preferred_element_type=f32)
                       + jnp.dot(jnp.where(from_cur, zb, pb), jnp.concatenate([tile(vs_ref, prev).astype(bf16), ones_blk], axis=1),
                                 preferred_element_type=f32))
                den = jnp.where(head0, acc[:blk, LANES:], acc[blk:, LANES:]) + p_d
                o_blk = (jnp.where(head0, acc[:blk, :LANES], acc[blk:, :LANES]) + p_d * tile(vs_ref, prev)) / den
                lse_blk = top + jnp.log2(den)
                for m, o in enumerate(cur):
                    ob_ref[br, pl.ds(o, run_len), :] = o_blk[m * run_len:(m + 1) * run_len]
                    lse_ref[br, pl.ds(o, run_len), :] = lse_blk[m * run_len:(m + 1) * run_len]

        def put_scores(scores):
            for u, sc in enumerate(scores):
                s_ref[u] = sc

        def put_probs(probs):
            for u, (pb, p_d, top) in enumerate(probs):
                p_ref[u] = pb
                st_ref[u, 0] = p_d
                st_ref[u, 1] = top

        put_probs(softmax_stage(blocks_of(0), masked_scores(blocks_of(0))))
        put_scores(masked_scores(blocks_of(min(1, n_iter - 1))))

        def body(it, carry, n_iter=n_iter, blocks_of=blocks_of, masked_scores=masked_scores,
                 softmax_stage=softmax_stage, output_stage=output_stage, put_scores=put_scores, put_probs=put_probs):
            scores = [s_ref[u] for u in range(DA_UNROLL)]
            probs = [(p_ref[u], st_ref[u, 0], st_ref[u, 1]) for u in range(DA_UNROLL)]
            output_stage(blocks_of(it), probs)
            next_probs = softmax_stage(blocks_of(jnp.minimum(it + 1, n_iter - 1)), scores)
            next_scores = masked_scores(blocks_of(jnp.minimum(it + 2, n_iter - 1)))
            put_scores(next_scores)
            put_probs(next_probs)
            return carry

        lax.fori_loop(0, n_iter, body, 0)

    for r in range(res):
        rows = slice(r * per, (r + 1) * per)
        lses = [lse_ref[br, rows, :] for br in range(len(DA_BRANCHES))]
        top = functools.reduce(jnp.maximum, lses)
        wts = [jnp.exp2(l - top) for l in lses]
        num = sum(w * ob_ref[br, rows, :] for br, w in enumerate(wts))
        o_ref[0, pl.ds(r, per, stride=res), :] = num / sum(wts)


def _dilated_attention(qkv3, batch, seq):
    pairs = DA_WIDTH // LANES
    n_br = len(DA_BRANCHES)
    spec = lambda off: pl.BlockSpec((1, seq, LANES), lambda b, p, off=off: (b, 0, off + p))
    return pl.pallas_call(
        functools.partial(_dilated_kernel, seq=seq),
        grid=(batch, pairs),
        in_specs=[spec(0), spec(pairs), spec(2 * pairs)],
        out_specs=pl.BlockSpec((1, seq, LANES), lambda b, p: (b, 0, p)),
        out_shape=jax.ShapeDtypeStruct((batch, seq, DA_WIDTH), f32),
        scratch_shapes=[pltpu.VMEM((seq, LANES), f32)] * 3
        + [pltpu.VMEM((n_br, seq, LANES), f32)] * 2
        + [pltpu.VMEM((DA_UNROLL, 2 * DA_BLOCK, DA_BLOCK), f32),
           pltpu.VMEM((DA_UNROLL, 2 * DA_BLOCK, DA_BLOCK), bf16),
           pltpu.VMEM((DA_UNROLL, 2, DA_BLOCK, LANES), f32)],
        compiler_params=_params(("parallel", "parallel")),
        name="dilated_attention",
    )(qkv3, qkv3, qkv3)


def _mix_out_kernel(oa_ref, ob_ref, h_ref, w_ref, g_ref, b_ref, o_ref):
    w = w_ref[...]
    y = jnp.dot(oa_ref[...], w[:DN_WIDTH], preferred_element_type=f32)
    y = y + jnp.dot(ob_ref[...].astype(bf16), w[DN_WIDTH:], preferred_element_type=f32)
    o_ref[...] = _layer_norm(DEEPNORM_ALPHA * h_ref[...] + y, g_ref[...], b_ref[...])


def _mix_out(o_a, o_b, h, w_out, g, b):
    t = h.shape[0]
    tm = TOKEN_TILE
    fixed = lambda i: (0, 0)
    return pl.pallas_call(
        _mix_out_kernel,
        grid=(t // tm,),
        in_specs=[
            pl.BlockSpec((tm, DN_WIDTH), lambda i: (i, 0)),
            pl.BlockSpec((tm, DA_WIDTH), lambda i: (i, 0)),
            pl.BlockSpec((tm, D_MODEL), lambda i: (i, 0)),
            pl.BlockSpec(w_out.shape, fixed),
            pl.BlockSpec((1, D_MODEL), fixed),
            pl.BlockSpec((1, D_MODEL), fixed),
        ],
        out_specs=pl.BlockSpec((tm, D_MODEL), lambda i: (i, 0)),
        out_shape=jax.ShapeDtypeStruct((t, D_MODEL), f32),
        compiler_params=_params(("parallel",)),
        name="mix_out",
    )(o_a, o_b, h, w_out, g, b)


def _shortconv_kernel(h_ref, win_ref, cw_ref, wout_ref, g_ref, b_ref, o_ref, tail_ref):
    @pl.when(pl.program_id(1) == 0)
    def _():
        tail_ref[...] = jnp.zeros_like(tail_ref)

    tm = h_ref.shape[0]
    d = D_MODEL
    h = h_ref[...]
    hb = h.astype(bf16)
    gate_b = jnp.dot(hb, win_ref[:, :d], preferred_element_type=f32)
    gate_c = jnp.dot(hb, win_ref[:, d:2 * d], preferred_element_type=f32)
    hid = jnp.dot(hb, win_ref[:, 2 * d:], preferred_element_type=f32)
    u = gate_c * hid
    uu = jnp.concatenate([tail_ref[...], u], axis=0)
    cw = cw_ref[...]
    conv = u * cw[SC_CONV - 1:SC_CONV, :]
    for j in range(SC_CONV - 1):
        back = SC_CONV - 1 - j
        conv = conv + uu[SUBLANES - back:SUBLANES - back + tm, :] * cw[j:j + 1, :]
    tail_ref[...] = u[tm - SUBLANES:, :]
    y = jnp.dot((gate_b * conv).astype(bf16), wout_ref[...], preferred_element_type=f32)
    o_ref[...] = _layer_norm(DEEPNORM_ALPHA * h + y, g_ref[...], b_ref[...])


def _shortconv(h, w_in, conv_w, w_out, g, b, batch, seq):
    tm = TOKEN_TILE
    ns = seq // tm
    row = lambda bb, s: (bb * ns + s, 0)
    fixed = lambda bb, s: (0, 0)
    return pl.pallas_call(
        _shortconv_kernel,
        grid=(batch, ns),
        in_specs=[
            pl.BlockSpec((tm, D_MODEL), row),
            pl.BlockSpec(w_in.shape, fixed),
            pl.BlockSpec(conv_w.shape, fixed),
            pl.BlockSpec(w_out.shape, fixed),
            pl.BlockSpec((1, D_MODEL), fixed),
            pl.BlockSpec((1, D_MODEL), fixed),
        ],
        out_specs=pl.BlockSpec((tm, D_MODEL), row),
        out_shape=jax.ShapeDtypeStruct((batch * seq, D_MODEL), f32),
        scratch_shapes=[pltpu.VMEM((SUBLANES, D_MODEL), f32)],
        compiler_params=_params(("arbitrary", "arbitrary")),
        name="shortconv",
    )(h, w_in, conv_w, w_out, g, b)


def _route(h2, rwt_ref, rb_ref, cnt_ref):
    tm = h2.shape[0]
    w_hi, w_lo = _split(rwt_ref[...], 2)
    x_hi, x_lo = _split(h2, 2)
    nt = (((1,), (1,)), ((), ()))
    logits = (lax.dot_general(w_hi, x_hi, nt, preferred_element_type=f32)
              + lax.dot_general(w_hi, x_lo, nt, preferred_element_type=f32)
              + lax.dot_general(w_lo, x_hi, nt, preferred_element_type=f32))
    mx = jnp.max(logits, axis=0, keepdims=True)
    ex = jnp.exp(logits - mx)
    scores = ex / jnp.sum(ex, axis=0, keepdims=True)
    sel = scores + rb_ref[...]
    best = jnp.full((1, tm), -jnp.inf, f32)
    bucket = jnp.zeros((1, tm), jnp.int32)
    s_a = jnp.zeros((1, tm), f32)
    s_b = jnp.zeros((1, tm), f32)
    for grp in range(N_GROUPS):
        for p, (i, j) in enumerate(_PAIRS):
            a, b = grp * EXPERTS_PER_GROUP + i, grp * EXPERTS_PER_GROUP + j
            ps = sel[a:a + 1, :] + sel[b:b + 1, :]
            upd = ps > best
            best = jnp.where(upd, ps, best)
            bucket = jnp.where(upd, grp * len(_PAIRS) + p, bucket)
            s_a = jnp.where(upd, scores[a:a + 1, :], s_a)
            s_b = jnp.where(upd, scores[b:b + 1, :], s_b)
    denom = s_a + s_b
    gate_rows = jnp.concatenate([s_a / denom, s_b / denom, jnp.zeros((GATE_LANES - 2, tm), f32)], axis=0)
    gates = jnp.transpose(gate_rows)
    onehot = jnp.where(lax.broadcasted_iota(jnp.int32, (BUCKET_ROWS, tm), 0) == bucket, 1.0, 0.0)
    before = lax.broadcasted_iota(jnp.int32, (tm, tm), 0) < lax.broadcasted_iota(jnp.int32, (tm, tm), 1)
    prefix = jnp.dot(onehot.astype(bf16), jnp.where(before, 1.0, 0.0).astype(bf16), preferred_element_type=f32)
    cnt = cnt_ref[...]
    rank = jnp.sum(onehot * (prefix + cnt[:, 0:1]), axis=0, keepdims=True)
    cnt_ref[...] = cnt + jnp.sum(onehot, axis=1, keepdims=True)
    return gates, bucket, rank.astype(jnp.int32)


def _xattn_kernel(h_ref, kv_ref, wq_ref, wo_ref, g_ref, b_ref, rwt_ref, rb_ref,
                  hext_ref, route_ref, cnt_out_ref, cnt_ref):
    @pl.when(jnp.logical_and(pl.program_id(0) == 0, pl.program_id(1) == 0))
    def _():
        cnt_ref[...] = jnp.zeros_like(cnt_ref)

    h = h_ref[...]
    tm = h.shape[0]
    q = jnp.dot(h.astype(bf16), wq_ref[...], preferred_element_type=f32).astype(bf16)
    outs = []
    for hd in range(XA_HEADS):
        lo, hi = hd * XA_HEAD_DIM, (hd + 1) * XA_HEAD_DIM
        s = _dot_nt(q[:, lo:hi], kv_ref[:, lo:hi]) * (XA_HEAD_DIM ** -0.5)
        p = jnp.exp(s - jnp.max(s, -1, keepdims=True))
        o = _dot(p, kv_ref[:, D_MODEL + lo:D_MODEL + hi]) / jnp.sum(p, -1, keepdims=True)
        outs.append(o.astype(bf16))
    y = jnp.dot(jnp.concatenate(outs, axis=1), wo_ref[...], preferred_element_type=f32)
    h2 = _layer_norm(DEEPNORM_ALPHA * h + y, g_ref[...], b_ref[...])
    gates, bucket, rank = _route(h2, rwt_ref, rb_ref, cnt_ref)
    hext_ref[:, :D_MODEL] = h2
    hext_ref[:, D_MODEL:] = gates
    route_ref[...] = jnp.concatenate([bucket, rank, jnp.zeros((SUBLANES - 2, tm), jnp.int32)], axis=0)
    cnt_out_ref[...] = cnt_ref[...]


def _xattn_route(h, kv, w_q, w_o, g, b, rwt, rb, batch, seq, n_mem):
    tm = TOKEN_TILE
    ns = seq // tm
    t = batch * seq
    row = lambda bb, s: (bb * ns + s, 0)
    fixed = lambda bb, s: (0, 0)
    return pl.pallas_call(
        _xattn_kernel,
        grid=(batch, ns),
        in_specs=[
            pl.BlockSpec((tm, D_MODEL), row),
            pl.BlockSpec((n_mem, 2 * D_MODEL), lambda bb, s: (bb, 0)),
            pl.BlockSpec(w_q.shape, fixed),
            pl.BlockSpec(w_o.shape, fixed),
            pl.BlockSpec((1, D_MODEL), fixed),
            pl.BlockSpec((1, D_MODEL), fixed),
            pl.BlockSpec(rwt.shape, fixed),
            pl.BlockSpec(rb.shape, fixed),
        ],
        out_specs=[
            pl.BlockSpec((tm, ROW_EXT), row),
            pl.BlockSpec((SUBLANES, tm), lambda bb, s: (0, bb * ns + s)),
            pl.BlockSpec((BUCKET_ROWS, LANES), fixed),
        ],
        out_shape=[
            jax.ShapeDtypeStruct((t, ROW_EXT), f32),
            jax.ShapeDtypeStruct((SUBLANES, t), jnp.int32),
            jax.ShapeDtypeStruct((BUCKET_ROWS, LANES), f32),
        ],
        scratch_shapes=[pltpu.VMEM((BUCKET_ROWS, LANES), f32)],
        compiler_params=_params(("arbitrary", "arbitrary")),
        name="xattn_route",
    )(h, kv, w_q, w_o, g, b, rwt, rb)


def _dispatch_kernel(dest_ref, fill_ref, hext_ref, xs_ref, inv_ref, zeros_ref, sem, zsem):
    base = pl.program_id(0) * DISPATCH_CHUNK
    n_tiles = xs_ref.shape[0] // MOE_TILE

    @pl.when(pl.program_id(0) == 0)
    def _():
        zeros_ref[...] = jnp.zeros_like(zeros_ref)

        def fill(tile):
            return pltpu.make_async_copy(zeros_ref, xs_ref.at[pl.ds(tile * MOE_TILE, MOE_TILE)], zsem)

        for tile in range(n_tiles):
            @pl.when(fill_ref[tile] > 0)
            def _(tile=tile):
                fill(tile).start()

        def clear(j8, carry):
            for u in range(8):
                inv_ref[j8 * 8 + u] = 0
            return carry

        lax.fori_loop(0, inv_ref.shape[0] // 8, clear, 0)
        for tile in range(n_tiles):
            @pl.when(fill_ref[tile] > 0)
            def _(tile=tile):
                fill(tile).wait()

    def issue(j8, carry):
        for u in range(8):
            j = j8 * 8 + u
            d = dest_ref[base + j]
            inv_ref[d] = base + j
            pltpu.make_async_copy(hext_ref.at[pl.ds(j, 1)], xs_ref.at[pl.ds(d, 1)], sem).start()
        return carry

    lax.fori_loop(0, DISPATCH_CHUNK // 8, issue, 0)
    pltpu.make_async_copy(hext_ref, xs_ref.at[pl.ds(0, DISPATCH_CHUNK)], sem).wait()


def _dispatch(dest, fill, hext, n_rows):
    t = hext.shape[0]
    return pl.pallas_call(
        _dispatch_kernel,
        grid_spec=pltpu.PrefetchScalarGridSpec(
            num_scalar_prefetch=2,
            grid=(t // DISPATCH_CHUNK,),
            in_specs=[pl.BlockSpec((DISPATCH_CHUNK, ROW_EXT), lambda i, *_: (i, 0))],
            out_specs=[pl.BlockSpec(memory_space=pl.ANY), pl.BlockSpec(memory_space=pltpu.SMEM)],
            scratch_shapes=[pltpu.VMEM((MOE_TILE, ROW_EXT), f32), pltpu.SemaphoreType.DMA(()),
                            pltpu.SemaphoreType.DMA(())],
        ),
        out_shape=[jax.ShapeDtypeStruct((n_rows, ROW_EXT), f32), jax.ShapeDtypeStruct((n_rows,), jnp.int32)],
        compiler_params=_params(("arbitrary",)),
        name="moe_dispatch",
    )(dest, fill, hext)


def _experts_kernel(ea_ref, eb_ref, nvalid_ref, xs_idx_ref, inv_ref, xs_ref, wga_ref, wua_ref, wda_ref,
                    wgb_ref, wub_ref, wdb_ref, g_ref, b_ref, out_ref, obuf_ref, sem_ref):
    i = pl.program_id(0)
    tm = MOE_TILE
    prev = jnp.maximum(i - 1, 0)
    n_prev = jnp.where(i > 0, nvalid_ref[prev], 0)
    slot_prev = prev % 2

    def send(j):
        return pltpu.make_async_copy(obuf_ref.at[slot_prev, pl.ds(j, 1)],
                                     out_ref.at[pl.ds(inv_ref[prev * tm + j], 1)], sem_ref.at[slot_prev])

    def send_rows(lo, hi):
        for j in range(lo, hi):
            @pl.when(j < n_prev)
            def _(j=j):
                send(j).start()

    quarter = tm // 4
    n_cur = jnp.where(i < pl.num_programs(0) - 1, nvalid_ref[jnp.minimum(i, pl.num_programs(0) - 2)], 0)

    @pl.when(n_cur > 0)
    def _():
        x32 = xs_ref[:, :D_MODEL]
        x = x32.astype(bf16)
        gates = xs_ref[:, D_MODEL:]

        def hidden(wg_ref, wu_ref, gate):
            hg = jnp.dot(x, wg_ref[0, 0].astype(bf16), preferred_element_type=f32)
            hu = jnp.dot(x, wu_ref[0, 0].astype(bf16), preferred_element_type=f32)
            return (_silu(hg) * hu * gate).astype(bf16)

        send_rows(0, quarter)
        h_a = hidden(wga_ref, wua_ref, gates[:, 0:1])
        send_rows(quarter, 2 * quarter)
        h_b = hidden(wgb_ref, wub_ref, gates[:, 1:2])
        send_rows(2 * quarter, 3 * quarter)
        y = (jnp.dot(h_a, wda_ref[0, 0].astype(bf16), preferred_element_type=f32)
             + jnp.dot(h_b, wdb_ref[0, 0].astype(bf16), preferred_element_type=f32))
        send_rows(3 * quarter, tm)
        obuf_ref[i % 2] = _layer_norm(DEEPNORM_ALPHA * x32 + y, g_ref[...], b_ref[...])

    @pl.when(n_cur == 0)
    def _():
        send_rows(0, tm)

    for bit in reversed(range(tm.bit_length())):
        size = 1 << bit

        @pl.when((n_prev & size) != 0)
        def _(size=size):
            pltpu.make_async_copy(obuf_ref.at[slot_prev, pl.ds(0, size)], out_ref.at[pl.ds(0, size)],
                                  sem_ref.at[slot_prev]).wait()


def _experts(layer, ea, eb, nvalid, xs_idx, inv, xs, w_gate, w_up, w_down, g, b, t):
    n_tiles = xs.shape[0] // MOE_TILE
    up_a = pl.BlockSpec((1, 1, D_MODEL, D_EXPERT), lambda i, ea, *_: (layer, ea[i], 0, 0))
    up_b = pl.BlockSpec((1, 1, D_MODEL, D_EXPERT), lambda i, ea, eb, *_: (layer, eb[i], 0, 0))
    dn_a = pl.BlockSpec((1, 1, D_EXPERT, D_MODEL), lambda i, ea, *_: (layer, ea[i], 0, 0))
    dn_b = pl.BlockSpec((1, 1, D_EXPERT, D_MODEL), lambda i, ea, eb, *_: (layer, eb[i], 0, 0))
    fixed = lambda i, *_: (0, 0)
    return pl.pallas_call(
        _experts_kernel,
        grid_spec=pltpu.PrefetchScalarGridSpec(
            num_scalar_prefetch=5,
            grid=(n_tiles + 1,),
            in_specs=[pl.BlockSpec((MOE_TILE, ROW_EXT), lambda i, ea, eb, nv, xs_idx, inv: (xs_idx[i], 0)),
                      up_a, up_a, dn_a, up_b, up_b, dn_b,
                      pl.BlockSpec((1, D_MODEL), fixed), pl.BlockSpec((1, D_MODEL), fixed)],
            out_specs=pl.BlockSpec(memory_space=pl.ANY),
            scratch_shapes=[pltpu.VMEM((2, MOE_TILE, D_MODEL), f32), pltpu.SemaphoreType.DMA((2,))],
        ),
        out_shape=jax.ShapeDtypeStruct((t, D_MODEL), f32),
        compiler_params=_params(("arbitrary",)),
        name="moe_experts",
    )(ea, eb, nvalid, xs_idx, inv, xs, w_gate, w_up, w_down, w_gate, w_up, w_down, g, b)


def _moe_plan(route, counts):
    t = route.shape[1]
    n_rows = t + N_BUCKETS * MOE_TILE
    n_tiles = n_rows // MOE_TILE
    cnt = counts[:N_BUCKETS, 0].astype(jnp.int32)
    padded = (cnt + MOE_TILE - 1) // MOE_TILE * MOE_TILE
    ends = jnp.cumsum(padded)
    starts = ends - padded
    dest = starts[route[0]] + route[1]
    tile_start = jnp.arange(n_tiles + 1, dtype=jnp.int32) * MOE_TILE
    tile_bucket = jnp.sum(tile_start[:, None] >= ends[None, :], axis=1).astype(jnp.int32)
    used = tile_bucket < N_BUCKETS
    tile_bucket = jnp.minimum(tile_bucket, N_BUCKETS - 1)
    nvalid = jnp.where(used, jnp.clip(starts[tile_bucket] + cnt[tile_bucket] - tile_start, 0, MOE_TILE), 0)
    xs_idx = jnp.minimum(jnp.arange(n_tiles + 1, dtype=jnp.int32), jnp.maximum(ends[-1] // MOE_TILE - 1, 0))
    last_bucket = tile_bucket[jnp.maximum(ends[-1] // MOE_TILE - 1, 0)]
    tile_bucket = jnp.where(used, tile_bucket, last_bucket)
    pair_a = jnp.array([g * EXPERTS_PER_GROUP + i for g in range(N_GROUPS) for i, _ in _PAIRS], jnp.int32)
    pair_b = jnp.array([g * EXPERTS_PER_GROUP + j for g in range(N_GROUPS) for _, j in _PAIRS], jnp.int32)
    tiles = jnp.arange(n_tiles, dtype=jnp.int32)
    is_last = jnp.any(jnp.logical_and((tiles[:, None] + 1) * MOE_TILE == ends[None, :], padded[None, :] > 0), axis=1)
    fill = jnp.logical_or(is_last, tiles * MOE_TILE >= ends[-1]).astype(jnp.int32)
    return dict(dest=dest, ea=pair_a[tile_bucket], eb=pair_b[tile_bucket], nvalid=nvalid.astype(jnp.int32),
                xs_idx=xs_idx, fill=fill, n_rows=n_rows)


def _moe(layer, hext, route, counts, w_gate, w_up, w_down, g, b):
    plan = _moe_plan(route, counts)
    xs, inv = _dispatch(plan["dest"], plan["fill"], hext, plan["n_rows"])
    return _experts(layer, plan["ea"], plan["eb"], plan["nvalid"], plan["xs_idx"], inv, xs,
                    w_gate, w_up, w_down, g, b, hext.shape[0])


def kernel(x, mem, ab_w_in, ab_conv_w, ab_a_log, ab_dt_bias, ab_norm_w, ab_w_out, sc_w_in, sc_conv_w, sc_w_out, xa_w_q, xa_w_kv, xa_w_o, router_w, router_b, moe_w_gate, moe_w_up, moe_w_down, ln_g, ln_b):
    batch, seq, d = x.shape
    n_mem = mem.shape[1]
    t = batch * seq
    h = x.reshape(t, d)
    mem2 = mem.reshape(batch * n_mem, d)
    rwt = jnp.transpose(router_w)
    rb = router_b.reshape(N_EXPERTS, 1)
    row = lambda v: v.reshape(1, -1)

    for layer in range(DEPTH):
        i = layer // 2
        if layer % 2 == 0:
            w_in = ab_w_in[i]
            c0, c1, c2 = 3 * DN_WIDTH, 4 * DN_WIDTH, 4 * DN_WIDTH + 2 * DN_HEADS
            w_ba = jnp.pad(w_in[:, c1:c2], ((0, 0), (0, LANES - 2 * DN_HEADS)))
            ws = [w_in[:, :c0].astype(bf16), w_in[:, c0:c1].astype(bf16), w_ba.astype(bf16), w_in[:, c2:].astype(bf16)]
            qkv, z, ba, qkv_d = _project(h, ws, [f32, f32, f32, f32], TOKEN_TILE)
            lane_pad = lambda v: jnp.pad(v.reshape(1, DN_HEADS), ((0, 0), (DN_HEADS, LANES - 2 * DN_HEADS)))
            o_a = _deltanet(qkv, ba, z, ab_conv_w[i], lane_pad(ab_a_log[i]), lane_pad(ab_dt_bias[i]),
                            row(ab_norm_w[i]), batch, seq)
            o_b = _dilated_attention(qkv_d.reshape(batch, seq, 3 * DA_WIDTH), batch, seq).reshape(t, DA_WIDTH)
            h = _mix_out(o_a, o_b, h, ab_w_out[i].astype(bf16), row(ln_g[layer, 0]), row(ln_b[layer, 0]))
        else:
            h = _shortconv(h, sc_w_in[i].astype(bf16), sc_conv_w[i], sc_w_out[i].astype(bf16),
                           row(ln_g[layer, 0]), row(ln_b[layer, 0]), batch, seq)
        (kv,) = _project(mem2, [xa_w_kv[layer].astype(bf16)], [bf16], n_mem)
        hext, route, counts = _xattn_route(h, kv, xa_w_q[layer].astype(bf16), xa_w_o[layer].astype(bf16),
                                           row(ln_g[layer, 1]), row(ln_b[layer, 1]), rwt, rb, batch, seq, n_mem)
        h = _moe(layer, hext, route, counts, moe_w_gate, moe_w_up, moe_w_down,
                 row(ln_g[layer, 2]), row(ln_b[layer, 2]))
    return h.reshape(batch, seq, d)
```

```python
import functools

import jax
import jax.numpy as jnp
from jax import lax
from jax.experimental import pallas as pl
from jax.experimental.pallas import tpu as pltpu

D_MODEL = 1024
DEPTH = 2
DN_HEADS = 4
DN_HEAD_DIM = 128
DN_WIDTH = DN_HEADS * DN_HEAD_DIM
DN_CONV = 4
DN_CHUNK = 64
DA_HEADS = 8
DA_HEAD_DIM = 64
DA_WIDTH = DA_HEADS * DA_HEAD_DIM
DA_BRANCHES = ((128, 1), (512, 4), (2048, 16))
DA_BLOCK = 128
DA_RESIDUES = max(d for _, d in DA_BRANCHES)
SC_CONV = 3
XA_HEADS = 4
XA_HEAD_DIM = D_MODEL // XA_HEADS
N_EXPERTS = 16
N_GROUPS = 4
EXPERTS_PER_GROUP = N_EXPERTS // N_GROUPS
D_EXPERT = D_MODEL // 2
DEEPNORM_ALPHA = (2 * DEPTH) ** 0.25
LN_EPS = 1e-5
RMS_EPS = 1e-6

LANES = 128
SUBLANES = 8
VMEM_LIMIT_BYTES = 48 * 1024 * 1024

TOKEN_TILE = 512
DN_BLOCK = 256
MOE_TILE = 256
DISPATCH_CHUNK = 512
DA_UNROLL = 2
GATE_LANES = LANES
ROW_EXT = D_MODEL + GATE_LANES

_PAIRS = tuple((i, j) for i in range(EXPERTS_PER_GROUP) for j in range(i + 1, EXPERTS_PER_GROUP))
N_BUCKETS = N_GROUPS * len(_PAIRS)
BUCKET_ROWS = 32

NEG_BIG = -1e30
LOG2_E = 1.4426950408889634

bf16 = jnp.bfloat16
f32 = jnp.float32


def _params(semantics):
    return pltpu.CompilerParams(dimension_semantics=semantics, vmem_limit_bytes=VMEM_LIMIT_BYTES)


def _dot(a, b):
    return jnp.dot(a.astype(bf16), b.astype(bf16), preferred_element_type=f32)


def _dot_nt(a, b):
    return lax.dot_general(a.astype(bf16), b.astype(bf16), (((1,), (1,)), ((), ())), preferred_element_type=f32)


def _dot_tn(a, b):
    return lax.dot_general(a.astype(bf16), b.astype(bf16), (((0,), (0,)), ((), ())), preferred_element_type=f32)


def _split(a, parts):
    out = []
    rem = a
    for _ in range(parts):
        p = rem.astype(bf16)
        out.append(p)
        rem = rem - p.astype(f32)
    return out


def _layer_norm(v, g, b):
    mu = jnp.mean(v, -1, keepdims=True)
    c = v - mu
    var = jnp.mean(c * c, -1, keepdims=True)
    return c * lax.rsqrt(var + LN_EPS) * g + b


def _silu(v):
    return v * jax.nn.sigmoid(v)


def _softplus(v):
    return jnp.maximum(v, 0.0) + jnp.log(1.0 + jnp.exp(-jnp.abs(v)))


def _proj_kernel(x_ref, *refs, n_out):
    w_refs, o_refs = refs[:n_out], refs[n_out:]
    xb = x_ref[...].astype(bf16)
    for w_ref, o_ref in zip(w_refs, o_refs):
        n = w_ref.shape[1]
        for c0 in range(0, n, 512):
            c1 = min(c0 + 512, n)
            o_ref[:, c0:c1] = jnp.dot(xb, w_ref[:, c0:c1], preferred_element_type=f32).astype(o_ref.dtype)


def _project(x, ws, out_dtypes, tm):
    t, k = x.shape
    n_out = len(ws)
    return pl.pallas_call(
        functools.partial(_proj_kernel, n_out=n_out),
        grid=(t // tm,),
        in_specs=[pl.BlockSpec((tm, k), lambda i: (i, 0))]
        + [pl.BlockSpec(w.shape, lambda i: (0, 0)) for w in ws],
        out_specs=[pl.BlockSpec((tm, w.shape[1]), lambda i: (i, 0)) for w in ws],
        out_shape=[jax.ShapeDtypeStruct((t, w.shape[1]), dt) for w, dt in zip(ws, out_dtypes)],
        compiler_params=_params(("parallel",)),
        name="project",
    )(x, *ws)


def _deltanet_kernel(qkv_ref, ba_ref, z_ref, cw_ref, alog_ref, dtb_ref, nw_ref, o_ref, state_ref, xx_ref):
    @pl.when(pl.program_id(1) == 0)
    def _():
        state_ref[...] = jnp.zeros_like(state_ref)
        xx_ref[0:SUBLANES, :] = jnp.zeros((SUBLANES, xx_ref.shape[1]), f32)

    rows, ch, hd, width = DN_BLOCK, DN_CHUNK, DN_HEAD_DIM, DN_WIDTH
    sup = 2 * ch
    n_sup = rows // sup

    x = qkv_ref[...]
    xx_ref[SUBLANES:, :] = x
    cw = cw_ref[...]
    conv = x * cw[DN_CONV - 1:DN_CONV, :]
    for j in range(DN_CONV - 1):
        back = DN_CONV - 1 - j
        conv = conv + xx_ref[pl.ds(SUBLANES - back, rows), :] * cw[j:j + 1, :]
    xx_ref[0:SUBLANES, :] = x[rows - SUBLANES:, :]
    act = _silu(conv)

    ba = ba_ref[...]
    beta_all = jax.nn.sigmoid(ba)
    g_all = -jnp.exp(alog_ref[...]) * _softplus(ba + dtb_ref[...])

    ri = lax.broadcasted_iota(jnp.int32, (sup, sup), 0)
    ci = lax.broadcasted_iota(jnp.int32, (sup, sup), 1)
    same = (ri >= ch) == (ci >= ch)
    incl = jnp.logical_and(ri >= ci, same)
    strict = jnp.logical_and(ri > ci, same)
    lower_ones = jnp.where(incl, 1.0, 0.0).astype(bf16)
    eye = jnp.where(ri == ci, 1.0, 0.0)
    first_rows = lax.broadcasted_iota(jnp.int32, (sup, hd), 0) < ch
    z = z_ref[...]
    nw = nw_ref[...]

    chains = [(s, h) for s in range(n_sup) for h in range(DN_HEADS)]
    g_cum_all, g_cum_t = [], []
    for s in range(n_sup):
        g_blk = g_all[s * sup:(s + 1) * sup, :]
        gc = sum(jnp.dot(lower_ones, p, preferred_element_type=f32) for p in _split(g_blk, 2))
        g_cum_all.append(gc)
        g_cum_t.append(jnp.transpose(gc))

    st = {}
    for s, h in chains:
        r0 = s * sup
        q = act[r0:r0 + sup, h * hd:(h + 1) * hd]
        k = act[r0:r0 + sup, width + h * hd:width + (h + 1) * hd]
        v = act[r0:r0 + sup, 2 * width + h * hd:2 * width + (h + 1) * hd]
        q = q * (lax.rsqrt(jnp.sum(q * q, -1, keepdims=True) + RMS_EPS) * (hd ** -0.5))
        k = k * lax.rsqrt(jnp.sum(k * k, -1, keepdims=True) + RMS_EPS)
        beta_b = jnp.broadcast_to(beta_all[r0:r0 + sup, h:h + 1], (sup, hd))
        g_i = jnp.broadcast_to(g_cum_all[s][:, DN_HEADS + h:DN_HEADS + h + 1], (sup, hd))
        g_j = jnp.broadcast_to(g_cum_t[s][DN_HEADS + h:DN_HEADS + h + 1, :], (sup, sup))
        decay = jnp.where(incl, jnp.exp(jnp.minimum(g_i - g_j, 0.0)), 0.0)
        e_g = jnp.exp(g_i)
        g_last = jnp.where(first_rows, g_i[ch - 1:ch, :], g_i[sup - 1:sup, :])
        st[s, h] = dict(q=q, k=k, kb=k * beta_b, vb=v * beta_b, decay=decay, e_g=e_g,
                        k_tail=k * jnp.exp(g_last - g_i),
                        gl=(jnp.exp(g_i[ch - 1:ch, :]), jnp.exp(g_i[sup - 1:sup, :])))

    for c in chains:
        d = st[c]
        d["a"] = jnp.where(strict, _dot_nt(d["kb"], d["k"]) * d["decay"], 0.0)
        d["attn"] = jnp.where(incl, _dot_nt(d["q"], d["k"]) * d["decay"], 0.0)
    for c in chains:
        st[c]["t_inv"] = eye - st[c]["a"]
        st[c]["pw"] = st[c]["a"]
    for _ in range(5):
        for c in chains:
            st[c]["pw"] = _dot(st[c]["pw"], st[c]["pw"])
        for c in chains:
            st[c]["t_inv"] = st[c]["t_inv"] + _dot(st[c]["t_inv"], st[c]["pw"])
    for c in chains:
        d = st[c]
        sol = _dot(d["t_inv"], jnp.concatenate([d["vb"], d["kb"] * d["e_g"]], axis=1))
        d["u"], d["w"] = sol[:, :hd], sol[:, hd:]
    for c in chains:
        d = st[c]
        uw = jnp.concatenate([d["u"], d["w"]], axis=1).astype(bf16)
        mix = jnp.dot(d["attn"].astype(bf16), uw, preferred_element_type=f32)
        d["au"] = mix[:, :hd]
        d["qe"] = d["q"] * d["e_g"] - mix[:, hd:]
        kt = d["k_tail"].astype(bf16)
        zero = jnp.zeros_like(kt)
        d["nu_pw"] = [_dot_tn(jnp.where(first_rows == first, kt, zero), uw)
                      for first in (True, False)]

    for s in range(n_sup):
        for half in range(2):
            for h in range(DN_HEADS):
                d = st[s, h]
                state = state_ref[h]
                lo = half * ch
                o = _dot(d["qe"][lo:lo + ch], state) + d["au"][lo:lo + ch]
                nu_pw = d["nu_pw"][half]
                state_ref[h] = state * d["gl"][half] - _dot(nu_pw[:, hd:], state) + nu_pw[:, :hd]
                o = o * lax.rsqrt(jnp.mean(o * o, -1, keepdims=True) + RMS_EPS) * nw
                r0 = s * sup + lo
                o_ref[r0:r0 + ch, h * hd:(h + 1) * hd] = (o * _silu(z[r0:r0 + ch, h * hd:(h + 1) * hd])).astype(o_ref.dtype)


def _deltanet(qkv, ba, z, conv_w, alog_row, dtb_row, norm_w, batch, seq):
    nb = seq // DN_BLOCK
    row = lambda b, c: (b * nb + c, 0)
    fixed = lambda b, c: (0, 0)
    return pl.pallas_call(
        _deltanet_kernel,
        grid=(batch, nb),
        in_specs=[
            pl.BlockSpec((DN_BLOCK, 3 * DN_WIDTH), row),
            pl.BlockSpec((DN_BLOCK, LANES), row),
            pl.BlockSpec((DN_BLOCK, DN_WIDTH), row),
            pl.BlockSpec(conv_w.shape, fixed),
            pl.BlockSpec((1, LANES), fixed),
            pl.BlockSpec((1, LANES), fixed),
            pl.BlockSpec((1, DN_HEAD_DIM), fixed),
        ],
        out_specs=pl.BlockSpec((DN_BLOCK, DN_WIDTH), row),
        out_shape=jax.ShapeDtypeStruct((batch * seq, DN_WIDTH), bf16),
        scratch_shapes=[
            pltpu.VMEM((DN_HEADS, DN_HEAD_DIM, DN_HEAD_DIM), f32),
            pltpu.VMEM((SUBLANES + DN_BLOCK, 3 * DN_WIDTH), f32),
        ],
        compiler_params=_params(("arbitrary", "arbitrary")),
        name="deltanet",
    )(qkv, ba, z, conv_w, alog_row, dtb_row, norm_w)


def _dilated_kernel(q_ref, k_ref, v_ref, o_ref, qs_ref, ks_ref, vs_ref, ob_ref, lse_ref, s_ref, p_ref, st_ref, *, seq):
    blk, res = DA_BLOCK, DA_RESIDUES
    per = seq // res
    head0 = lax.broadcasted_iota(jnp.int32, (blk, LANES), 1) < DA_HEAD_DIM
    ai = lax.broadcasted_iota(jnp.int32, (2 * blk, blk), 0) & (blk - 1)
    bi = lax.broadcasted_iota(jnp.int32, (2 * blk, blk), 1)
    ones_blk = jnp.ones((blk, LANES), bf16)
    lane_head0 = lax.broadcasted_iota(jnp.int32, (LANES, 2 * LANES), 0) < DA_HEAD_DIM
    col_head0 = lax.broadcasted_iota(jnp.int32, (LANES, 2 * LANES), 1) < LANES
    head_sum = jnp.where(lane_head0 == col_head0, 1.0, 0.0).astype(bf16)

    q_scale = DA_HEAD_DIM ** -0.5 * LOG2_E
    for r in range(res):
        rows = pl.ds(r, per, stride=res)
        qs_ref[r * per:(r + 1) * per, :] = q_ref[0, rows, :] * q_scale
        ks_ref[r * per:(r + 1) * per, :] = k_ref[0, rows, :]
        vs_ref[r * per:(r + 1) * per, :] = v_ref[0, rows, :]

    for br, (window, dil) in enumerate(DA_BRANCHES):
        assert window // dil == blk and res % dil == 0
        runs = res // dil
        run_len = blk // runs
        nblk = seq // (dil * blk)
        shift = run_len.bit_length() - 1

        def pos(a, runs=runs, run_len=run_len, shift=shift):
            return runs * (a & (run_len - 1)) + (a >> shift)

        from_cur = pos(bi) <= pos(ai)

        n_iter = dil * nblk // DA_UNROLL
        assert n_iter * DA_UNROLL == dil * nblk

        def tile(ref, offs, run_len=run_len, runs=runs):
            parts = [ref[pl.ds(o, run_len), :] for o in offs]
            return parts[0] if runs == 1 else jnp.concatenate(parts, axis=0)

        def blocks_of(it, dil=dil, runs=runs, run_len=run_len, nblk=nblk):
            out = []
            for u in range(DA_UNROLL):
                idx = it * DA_UNROLL + u
                cls = idx // nblk
                n = idx - cls * nblk
                n_prev = jnp.maximum(n - 1, 0)
                cur = [pl.multiple_of((dil * m + cls) * per + run_len * n, SUBLANES) for m in range(runs)]
                prev = [pl.multiple_of((dil * m + cls) * per + run_len * n_prev, SUBLANES) for m in range(runs)]
                out.append((cur, prev, n > 0))
            return out

        def masked_scores(blocks, tile=tile, from_cur=from_cur):
            out = []
            for cur, prev, has_prev in blocks:
                q = tile(qs_ref, cur).astype(bf16)
                zero = jnp.zeros_like(q)
                q2 = jnp.concatenate([jnp.where(head0, q, zero), jnp.where(head0, zero, q)], axis=0)
                s_c = _dot_nt(q2, tile(ks_ref, cur))
                s_p = _dot_nt(q2, tile(ks_ref, prev))
                out.append(jnp.where(from_cur, s_c, jnp.where(has_prev, s_p, NEG_BIG)))
            return out

        def softmax_stage(blocks, scores, tile=tile):
            out = []
            for (cur, prev, has_prev), s in zip(blocks, scores):
                sd = jnp.dot((tile(qs_ref, cur) * tile(ks_ref, prev)).astype(bf16), head_sum, preferred_element_type=f32)
                row_max = jnp.max(s, -1, keepdims=True)
                m0 = jnp.maximum(row_max[:blk], sd[:, :LANES])
                m1 = jnp.maximum(row_max[blk:], sd[:, LANES:])
                p = jnp.exp2(s - jnp.concatenate([m0, m1], axis=0))
                top = jnp.where(head0, m0, m1)
                p_d = jnp.where(has_prev, jnp.exp2(jnp.where(head0, sd[:, :LANES], sd[:, LANES:]) - top), 0.0)
                out.append((p.astype(bf16), p_d, top))
            return out

        def output_stage(blocks, probs, br=br, run_len=run_len, from_cur=from_cur, tile=tile):
            for (cur, prev, has_prev), (pb, p_d, top) in zip(blocks, probs):
                zb = jnp.zeros_like(pb)
                acc = (jnp.dot(jnp.where(from_cur, pb, zb), jnp.concatenate([tile(vs_ref, cur).astype(bf16), ones_blk], axis=1),
                               preferred_element_type=f32)
                       + jnp.dot(jnp.where(from_cur, zb, pb), jnp.concatenate([tile(vs_ref, prev).astype(bf16), ones_blk], axis=1),
                                 preferred_element_type=f32))
                den = jnp.where(head0, acc[:blk, LANES:], acc[blk:, LANES:]) + p_d
                o_blk = (jnp.where(head0, acc[:blk, :LANES], acc[blk:, :LANES]) + p_d * tile(vs_ref, prev)) / den
                lse_blk = top + jnp.log2(den)
                for m, o in enumerate(cur):
                    ob_ref[br, pl.ds(o, run_len), :] = o_blk[m * run_len:(m + 1) * run_len]
                    lse_ref[br, pl.ds(o, run_len), :] = lse_blk[m * run_len:(m + 1) * run_len]

        def put_scores(scores):
            for u, sc in enumerate(scores):
                s_ref[u] = sc

        def put_probs(probs):
            for u, (pb, p_d, top) in enumerate(probs):
                p_ref[u] = pb
                st_ref[u, 0] = p_d
                st_ref[u, 1] = top

        put_probs(softmax_stage(blocks_of(0), masked_scores(blocks_of(0))))
        put_scores(masked_scores(blocks_of(min(1, n_iter - 1))))

        def body(it, carry, n_iter=n_iter, blocks_of=blocks_of, masked_scores=masked_scores,
                 softmax_stage=softmax_stage, output_stage=output_stage, put_scores=put_scores, put_probs=put_probs):
            scores = [s_ref[u] for u in range(DA_UNROLL)]
            probs = [(p_ref[u], st_ref[u, 0], st_ref[u, 1]) for u in range(DA_UNROLL)]
            output_stage(blocks_of(it), probs)
            next_probs = softmax_stage(blocks_of(jnp.minimum(it + 1, n_iter - 1)), scores)
            next_scores = masked_scores(blocks_of(jnp.minimum(it + 2, n_iter - 1)))
            put_scores(next_scores)
            put_probs(next_probs)
            return carry

        lax.fori_loop(0, n_iter, body, 0)

    for r in range(res):
        rows = slice(r * per, (r + 1) * per)
        lses = [lse_ref[br, rows, :] for br in range(len(DA_BRANCHES))]
        top = functools.reduce(jnp.maximum, lses)
        wts = [jnp.exp2(l - top) for l in lses]
        num = sum(w * ob_ref[br, rows, :] for br, w in enumerate(wts))
        o_ref[0, pl.ds(r, per, stride=res), :] = num / sum(wts)


def _dilated_attention(qkv3, batch, seq):
    pairs = DA_WIDTH // LANES
    n_br = len(DA_BRANCHES)
    spec = lambda off: pl.BlockSpec((1, seq, LANES), lambda b, p, off=off: (b, 0, off + p))
    return pl.pallas_call(
        functools.partial(_dilated_kernel, seq=seq),
        grid=(batch, pairs),
        in_specs=[spec(0), spec(pairs), spec(2 * pairs)],
        out_specs=pl.BlockSpec((1, seq, LANES), lambda b, p: (b, 0, p)),
        out_shape=jax.ShapeDtypeStruct((batch, seq, DA_WIDTH), f32),
        scratch_shapes=[pltpu.VMEM((seq, LANES), f32)] * 3
        + [pltpu.VMEM((n_br, seq, LANES), f32)] * 2
        + [pltpu.VMEM((DA_UNROLL, 2 * DA_BLOCK, DA_BLOCK), f32),
           pltpu.VMEM((DA_UNROLL, 2 * DA_BLOCK, DA_BLOCK), bf16),
           pltpu.VMEM((DA_UNROLL, 2, DA_BLOCK, LANES), f32)],
        compiler_params=_params(("parallel", "parallel")),
        name="dilated_attention",
    )(qkv3, qkv3, qkv3)


def _mix_out_kernel(oa_ref, ob_ref, h_ref, w_ref, g_ref, b_ref, o_ref):
    w = w_ref[...]
    y = jnp.dot(oa_ref[...], w[:DN_WIDTH], preferred_element_type=f32)
    y = y + jnp.dot(ob_ref[...].astype(bf16), w[DN_WIDTH:], preferred_element_type=f32)
    o_ref[...] = _layer_norm(DEEPNORM_ALPHA * h_ref[...] + y, g_ref[...], b_ref[...])


def _mix_out(o_a, o_b, h, w_out, g, b):
    t = h.shape[0]
    tm = TOKEN_TILE
    fixed = lambda i: (0, 0)
    return pl.pallas_call(
        _mix_out_kernel,
        grid=(t // tm,),
        in_specs=[
            pl.BlockSpec((tm, DN_WIDTH), lambda i: (i, 0)),
            pl.BlockSpec((tm, DA_WIDTH), lambda i: (i, 0)),
            pl.BlockSpec((tm, D_MODEL), lambda i: (i, 0)),
            pl.BlockSpec(w_out.shape, fixed),
            pl.BlockSpec((1, D_MODEL), fixed),
            pl.BlockSpec((1, D_MODEL), fixed),
        ],
        out_specs=pl.BlockSpec((tm, D_MODEL), lambda i: (i, 0)),
        out_shape=jax.ShapeDtypeStruct((t, D_MODEL), f32),
        compiler_params=_params(("parallel",)),
        name="mix_out",
    )(o_a, o_b, h, w_out, g, b)


def _shortconv_kernel(h_ref, win_ref, cw_ref, wout_ref, g_ref, b_ref, o_ref, tail_ref):
    @pl.when(pl.program_id(1) == 0)
    def _():
        tail_ref[...] = jnp.zeros_like(tail_ref)

    tm = h_ref.shape[0]
    d = D_MODEL
    h = h_ref[...]
    hb = h.astype(bf16)
    gate_b = jnp.dot(hb, win_ref[:, :d], preferred_element_type=f32)
    gate_c = jnp.dot(hb, win_ref[:, d:2 * d], preferred_element_type=f32)
    hid = jnp.dot(hb, win_ref[:, 2 * d:], preferred_element_type=f32)
    u = gate_c * hid
    uu = jnp.concatenate([tail_ref[...], u], axis=0)
    cw = cw_ref[...]
    conv = u * cw[SC_CONV - 1:SC_CONV, :]
    for j in range(SC_CONV - 1):
        back = SC_CONV - 1 - j
        conv = conv + uu[SUBLANES - back:SUBLANES - back + tm, :] * cw[j:j + 1, :]
    tail_ref[...] = u[tm - SUBLANES:, :]
    y = jnp.dot((gate_b * conv).astype(bf16), wout_ref[...], preferred_element_type=f32)
    o_ref[...] = _layer_norm(DEEPNORM_ALPHA * h + y, g_ref[...], b_ref[...])


def _shortconv(h, w_in, conv_w, w_out, g, b, batch, seq):
    tm = TOKEN_TILE
    ns = seq // tm
    row = lambda bb, s: (bb * ns + s, 0)
    fixed = lambda bb, s: (0, 0)
    return pl.pallas_call(
        _shortconv_kernel,
        grid=(batch, ns),
        in_specs=[
            pl.BlockSpec((tm, D_MODEL), row),
            pl.BlockSpec(w_in.shape, fixed),
            pl.BlockSpec(conv_w.shape, fixed),
            pl.BlockSpec(w_out.shape, fixed),
            pl.BlockSpec((1, D_MODEL), fixed),
            pl.BlockSpec((1, D_MODEL), fixed),
        ],
        out_specs=pl.BlockSpec((tm, D_MODEL), row),
        out_shape=jax.ShapeDtypeStruct((batch * seq, D_MODEL), f32),
        scratch_shapes=[pltpu.VMEM((SUBLANES, D_MODEL), f32)],
        compiler_params=_params(("arbitrary", "arbitrary")),
        name="shortconv",
    )(h, w_in, conv_w, w_out, g, b)


def _route(h2, rwt_ref, rb_ref, cnt_ref):
    tm = h2.shape[0]
    w_hi, w_lo = _split(rwt_ref[...], 2)
    x_hi, x_lo = _split(h2, 2)
    nt = (((1,), (1,)), ((), ()))
    logits = (lax.dot_general(w_hi, x_hi, nt, preferred_element_type=f32)
              + lax.dot_general(w_hi, x_lo, nt, preferred_element_type=f32)
              + lax.dot_general(w_lo, x_hi, nt, preferred_element_type=f32))
    mx = jnp.max(logits, axis=0, keepdims=True)
    ex = jnp.exp(logits - mx)
    scores = ex / jnp.sum(ex, axis=0, keepdims=True)
    sel = scores + rb_ref[...]
    best = jnp.full((1, tm), -jnp.inf, f32)
    bucket = jnp.zeros((1, tm), jnp.int32)
    s_a = jnp.zeros((1, tm), f32)
    s_b = jnp.zeros((1, tm), f32)
    for grp in range(N_GROUPS):
        for p, (i, j) in enumerate(_PAIRS):
            a, b = grp * EXPERTS_PER_GROUP + i, grp * EXPERTS_PER_GROUP + j
            ps = sel[a:a + 1, :] + sel[b:b + 1, :]
            upd = ps > best
            best = jnp.where(upd, ps, best)
            bucket = jnp.where(upd, grp * len(_PAIRS) + p, bucket)
            s_a = jnp.where(upd, scores[a:a + 1, :], s_a)
            s_b = jnp.where(upd, scores[b:b + 1, :], s_b)
    denom = s_a + s_b
    gate_rows = jnp.concatenate([s_a / denom, s_b / denom, jnp.zeros((GATE_LANES - 2, tm), f32)], axis=0)
    gates = jnp.transpose(gate_rows)
    onehot = jnp.where(lax.broadcasted_iota(jnp.int32, (BUCKET_ROWS, tm), 0) == bucket, 1.0, 0.0)
    before = lax.broadcasted_iota(jnp.int32, (tm, tm), 0) < lax.broadcasted_iota(jnp.int32, (tm, tm), 1)
    prefix = jnp.dot(onehot.astype(bf16), jnp.where(before, 1.0, 0.0).astype(bf16), preferred_element_type=f32)
    cnt = cnt_ref[...]
    rank = jnp.sum(onehot * (prefix + cnt[:, 0:1]), axis=0, keepdims=True)
    cnt_ref[...] = cnt + jnp.sum(onehot, axis=1, keepdims=True)
    return gates, bucket, rank.astype(jnp.int32)


def _xattn_kernel(h_ref, kv_ref, wq_ref, wo_ref, g_ref, b_ref, rwt_ref, rb_ref,
                  hext_ref, route_ref, cnt_out_ref, cnt_ref):
    @pl.when(jnp.logical_and(pl.program_id(0) == 0, pl.program_id(1) == 0))
    def _():
        cnt_ref[...] = jnp.zeros_like(cnt_ref)

    h = h_ref[...]
    tm = h.shape[0]
    q = jnp.dot(h.astype(bf16), wq_ref[...], preferred_element_type=f32).astype(bf16)
    outs = []
    for hd in range(XA_HEADS):
        lo, hi = hd * XA_HEAD_DIM, (hd + 1) * XA_HEAD_DIM
        s = _dot_nt(q[:, lo:hi], kv_ref[:, lo:hi]) * (XA_HEAD_DIM ** -0.5)
        p = jnp.exp(s - jnp.max(s, -1, keepdims=True))
        o = _dot(p, kv_ref[:, D_MODEL + lo:D_MODEL + hi]) / jnp.sum(p, -1, keepdims=True)
        outs.append(o.astype(bf16))
    y = jnp.dot(jnp.concatenate(outs, axis=1), wo_ref[...], preferred_element_type=f32)
    h2 = _layer_norm(DEEPNORM_ALPHA * h + y, g_ref[...], b_ref[...])
    gates, bucket, rank = _route(h2, rwt_ref, rb_ref, cnt_ref)
    hext_ref[:, :D_MODEL] = h2
    hext_ref[:, D_MODEL:] = gates
    route_ref[...] = jnp.concatenate([bucket, rank, jnp.zeros((SUBLANES - 2, tm), jnp.int32)], axis=0)
    cnt_out_ref[...] = cnt_ref[...]


def _xattn_route(h, kv, w_q, w_o, g, b, rwt, rb, batch, seq, n_mem):
    tm = TOKEN_TILE
    ns = seq // tm
    t = batch * seq
    row = lambda bb, s: (bb * ns + s, 0)
    fixed = lambda bb, s: (0, 0)
    return pl.pallas_call(
        _xattn_kernel,
        grid=(batch, ns),
        in_specs=[
            pl.BlockSpec((tm, D_MODEL), row),
            pl.BlockSpec((n_mem, 2 * D_MODEL), lambda bb, s: (bb, 0)),
            pl.BlockSpec(w_q.shape, fixed),
            pl.BlockSpec(w_o.shape, fixed),
            pl.BlockSpec((1, D_MODEL), fixed),
            pl.BlockSpec((1, D_MODEL), fixed),
            pl.BlockSpec(rwt.shape, fixed),
            pl.BlockSpec(rb.shape, fixed),
        ],
        out_specs=[
            pl.BlockSpec((tm, ROW_EXT), row),
            pl.BlockSpec((SUBLANES, tm), lambda bb, s: (0, bb * ns + s)),
            pl.BlockSpec((BUCKET_ROWS, LANES), fixed),
        ],
        out_shape=[
            jax.ShapeDtypeStruct((t, ROW_EXT), f32),
            jax.ShapeDtypeStruct((SUBLANES, t), jnp.int32),
            jax.ShapeDtypeStruct((BUCKET_ROWS, LANES), f32),
        ],
        scratch_shapes=[pltpu.VMEM((BUCKET_ROWS, LANES), f32)],
        compiler_params=_params(("arbitrary", "arbitrary")),
        name="xattn_route",
    )(h, kv, w_q, w_o, g, b, rwt, rb)


def _dispatch_kernel(dest_ref, fill_ref, hext_ref, xs_ref, inv_ref, zeros_ref, sem, zsem):
    base = pl.program_id(0) * DISPATCH_CHUNK
    n_tiles = xs_ref.shape[0] // MOE_TILE

    @pl.when(pl.program_id(0) == 0)
    def _():
        zeros_ref[...] = jnp.zeros_like(zeros_ref)

        def fill(tile):
            return pltpu.make_async_copy(zeros_ref, xs_ref.at[pl.ds(tile * MOE_TILE, MOE_TILE)], zsem)

        for tile in range(n_tiles):
            @pl.when(fill_ref[tile] > 0)
            def _(tile=tile):
                fill(tile).start()

        def clear(j8, carry):
            for u in range(8):
                inv_ref[j8 * 8 + u] = 0
            return carry

        lax.fori_loop(0, inv_ref.shape[0] // 8, clear, 0)
        for tile in range(n_tiles):
            @pl.when(fill_ref[tile] > 0)
            def _(tile=tile):
                fill(tile).wait()

    def issue(j8, carry):
        for u in range(8):
            j = j8 * 8 + u
            d = dest_ref[base + j]
            inv_ref[d] = base + j
            pltpu.make_async_copy(hext_ref.at[pl.ds(j, 1)], xs_ref.at[pl.ds(d, 1)], sem).start(priority=u % 2)
        return carry

    lax.fori_loop(0, DISPATCH_CHUNK // 8, issue, 0)
    pltpu.make_async_copy(hext_ref, xs_ref.at[pl.ds(0, DISPATCH_CHUNK)], sem).wait()


def _dispatch(dest, fill, hext, n_rows):
    t = hext.shape[0]
    return pl.pallas_call(
        _dispatch_kernel,
        grid_spec=pltpu.PrefetchScalarGridSpec(
            num_scalar_prefetch=2,
            grid=(t // DISPATCH_CHUNK,),
            in_specs=[pl.BlockSpec((DISPATCH_CHUNK, ROW_EXT), lambda i, *_: (i, 0))],
            out_specs=[pl.BlockSpec(memory_space=pl.ANY), pl.BlockSpec(memory_space=pltpu.SMEM)],
            scratch_shapes=[pltpu.VMEM((MOE_TILE, ROW_EXT), f32), pltpu.SemaphoreType.DMA(()),
                            pltpu.SemaphoreType.DMA(())],
        ),
        out_shape=[jax.ShapeDtypeStruct((n_rows, ROW_EXT), f32), jax.ShapeDtypeStruct((n_rows,), jnp.int32)],
        compiler_params=_params(("arbitrary",)),
        name="moe_dispatch",
    )(dest, fill, hext)


def _experts_kernel(ea_ref, eb_ref, nvalid_ref, xs_idx_ref, inv_ref, xs_ref, wga_ref, wua_ref, wda_ref,
                    wgb_ref, wub_ref, wdb_ref, g_ref, b_ref, out_ref, obuf_ref, sem_ref):
    i = pl.program_id(0)
    tm = MOE_TILE
    prev = jnp.maximum(i - 1, 0)
    n_prev = jnp.where(i > 0, nvalid_ref[prev], 0)
    slot_prev = prev % 2

    def send(j):
        return pltpu.make_async_copy(obuf_ref.at[slot_prev, pl.ds(j, 1)],
                                     out_ref.at[pl.ds(inv_ref[prev * tm + j], 1)], sem_ref.at[slot_prev])

    def send_rows(lo, hi):
        for j in range(lo, hi):
            @pl.when(j < n_prev)
            def _(j=j):
                send(j).start(priority=j % 2)

    n_cur = jnp.where(i < pl.num_programs(0) - 1, nvalid_ref[jnp.minimum(i, pl.num_programs(0) - 2)], 0)
    send_rows(0, tm)

    @pl.when(n_cur > 0)
    def _():
        x32 = xs_ref[:, :D_MODEL]
        x = x32.astype(bf16)
        gates = xs_ref[:, D_MODEL:]

        def hidden(wg_ref, wu_ref, gate):
            hg = jnp.dot(x, wg_ref[0, 0].astype(bf16), preferred_element_type=f32)
            hu = jnp.dot(x, wu_ref[0, 0].astype(bf16), preferred_element_type=f32)
            return (_silu(hg) * hu * gate).astype(bf16)

        h_a = hidden(wga_ref, wua_ref, gates[:, 0:1])
        h_b = hidden(wgb_ref, wub_ref, gates[:, 1:2])
        y = (jnp.dot(h_a, wda_ref[0, 0].astype(bf16), preferred_element_type=f32)
             + jnp.dot(h_b, wdb_ref[0, 0].astype(bf16), preferred_element_type=f32))
        obuf_ref[i % 2] = _layer_norm(DEEPNORM_ALPHA * x32 + y, g_ref[...], b_ref[...])

    for bit in reversed(range(tm.bit_length())):
        size = 1 << bit

        @pl.when((n_prev & size) != 0)
        def _(size=size):
            pltpu.make_async_copy(obuf_ref.at[slot_prev, pl.ds(0, size)], out_ref.at[pl.ds(0, size)],
                                  sem_ref.at[slot_prev]).wait()


def _experts(layer, ea, eb, nvalid, xs_idx, inv, xs, w_gate, w_up, w_down, g, b, t):
    n_tiles = xs.shape[0] // MOE_TILE
    up_a = pl.BlockSpec((1, 1, D_MODEL, D_EXPERT), lambda i, ea, *_: (layer, ea[i], 0, 0))
    up_b = pl.BlockSpec((1, 1, D_MODEL, D_EXPERT), lambda i, ea, eb, *_: (layer, eb[i], 0, 0))
    dn_a = pl.BlockSpec((1, 1, D_EXPERT, D_MODEL), lambda i, ea, *_: (layer, ea[i], 0, 0))
    dn_b = pl.BlockSpec((1, 1, D_EXPERT, D_MODEL), lambda i, ea, eb, *_: (layer, eb[i], 0, 0))
    fixed = lambda i, *_: (0, 0)
    return pl.pallas_call(
        _experts_kernel,
        grid_spec=pltpu.PrefetchScalarGridSpec(
            num_scalar_prefetch=5,
            grid=(n_tiles + 1,),
            in_specs=[pl.BlockSpec((MOE_TILE, ROW_EXT), lambda i, ea, eb, nv, xs_idx, inv: (xs_idx[i], 0)),
                      up_a, up_a, dn_a, up_b, up_b, dn_b,
                      pl.BlockSpec((1, D_MODEL), fixed), pl.BlockSpec((1, D_MODEL), fixed)],
            out_specs=pl.BlockSpec(memory_space=pl.ANY),
            scratch_shapes=[pltpu.VMEM((2, MOE_TILE, D_MODEL), f32), pltpu.SemaphoreType.DMA((2,))],
        ),
        out_shape=jax.ShapeDtypeStruct((t, D_MODEL), f32),
        compiler_params=_params(("arbitrary",)),
        name="moe_experts",
    )(ea, eb, nvalid, xs_idx, inv, xs, w_gate, w_up, w_down, w_gate, w_up, w_down, g, b)


def _moe_plan(route, counts):
    t = route.shape[1]
    n_rows = t + N_BUCKETS * MOE_TILE
    n_tiles = n_rows // MOE_TILE
    cnt = counts[:N_BUCKETS, 0].astype(jnp.int32)
    padded = (cnt + MOE_TILE - 1) // MOE_TILE * MOE_TILE
    ends = jnp.cumsum(padded)
    starts = ends - padded
    dest = starts[route[0]] + route[1]
    tile_start = jnp.arange(n_tiles + 1, dtype=jnp.int32) * MOE_TILE
    tile_bucket = jnp.sum(tile_start[:, None] >= ends[None, :], axis=1).astype(jnp.int32)
    used = tile_bucket < N_BUCKETS
    tile_bucket = jnp.minimum(tile_bucket, N_BUCKETS - 1)
    nvalid = jnp.where(used, jnp.clip(starts[tile_bucket] + cnt[tile_bucket] - tile_start, 0, MOE_TILE), 0)
    xs_idx = jnp.minimum(jnp.arange(n_tiles + 1, dtype=jnp.int32), jnp.maximum(ends[-1] // MOE_TILE - 1, 0))
    last_bucket = tile_bucket[jnp.maximum(ends[-1] // MOE_TILE - 1, 0)]
    tile_bucket = jnp.where(used, tile_bucket, last_bucket)
    pair_a = jnp.array([g * EXPERTS_PER_GROUP + i for g in range(N_GROUPS) for i, _ in _PAIRS], jnp.int32)
    pair_b = jnp.array([g * EXPERTS_PER_GROUP + j for g in range(N_GROUPS) for _, j in _PAIRS], jnp.int32)
    tiles = jnp.arange(n_tiles, dtype=jnp.int32)
    is_last = jnp.any(jnp.logical_and((tiles[:, None] + 1) * MOE_TILE == ends[None, :], padded[None, :] > 0), axis=1)
    fill = jnp.logical_or(is_last, tiles * MOE_TILE >= ends[-1]).astype(jnp.int32)
    return dict(dest=dest, ea=pair_a[tile_bucket], eb=pair_b[tile_bucket], nvalid=nvalid.astype(jnp.int32),
                xs_idx=xs_idx, fill=fill, n_rows=n_rows)


def _moe(layer, hext, route, counts, w_gate, w_up, w_down, g, b):
    plan = _moe_plan(route, counts)
    xs, inv = _dispatch(plan["dest"], plan["fill"], hext, plan["n_rows"])
    return _experts(layer, plan["ea"], plan["eb"], plan["nvalid"], plan["xs_idx"], inv, xs,
                    w_gate, w_up, w_down, g, b, hext.shape[0])


def kernel(x, mem, ab_w_in, ab_conv_w, ab_a_log, ab_dt_bias, ab_norm_w, ab_w_out, sc_w_in, sc_conv_w, sc_w_out, xa_w_q, xa_w_kv, xa_w_o, router_w, router_b, moe_w_gate, moe_w_up, moe_w_down, ln_g, ln_b):
    batch, seq, d = x.shape
    n_mem = mem.shape[1]
    t = batch * seq
    h = x.reshape(t, d)
    mem2 = mem.reshape(batch * n_mem, d)
    rwt = jnp.transpose(router_w)
    rb = router_b.reshape(N_EXPERTS, 1)
    row = lambda v: v.reshape(1, -1)

    for layer in range(DEPTH):
        i = layer // 2
        if layer % 2 == 0:
            w_in = ab_w_in[i]
            c0, c1, c2 = 3 * DN_WIDTH, 4 * DN_WIDTH, 4 * DN_WIDTH + 2 * DN_HEADS
            w_ba = jnp.pad(w_in[:, c1:c2], ((0, 0), (0, LANES - 2 * DN_HEADS)))
            ws = [w_in[:, :c0].astype(bf16), w_in[:, c0:c1].astype(bf16), w_ba.astype(bf16), w_in[:, c2:].astype(bf16)]
            qkv, z, ba, qkv_d = _project(h, ws, [f32, f32, f32, f32], TOKEN_TILE)
            lane_pad = lambda v: jnp.pad(v.reshape(1, DN_HEADS), ((0, 0), (DN_HEADS, LANES - 2 * DN_HEADS)))
            o_a = _deltanet(qkv, ba, z, ab_conv_w[i], lane_pad(ab_a_log[i]), lane_pad(ab_dt_bias[i]),
                            row(ab_norm_w[i]), batch, seq)
            o_b = _dilated_attention(qkv_d.reshape(batch, seq, 3 * DA_WIDTH), batch, seq).reshape(t, DA_WIDTH)
            h = _mix_out(o_a, o_b, h, ab_w_out[i].astype(bf16), row(ln_g[layer, 0]), row(ln_b[layer, 0]))
        else:
            h = _shortconv(h, sc_w_in[i].astype(bf16), sc_conv_w[i], sc_w_out[i].astype(bf16),
                           row(ln_g[layer, 0]), row(ln_b[layer, 0]), batch, seq)
        (kv,) = _project(mem2, [xa_w_kv[layer].astype(bf16)], [bf16], n_mem)
        hext, route, counts = _xattn_route(h, kv, xa_w_q[layer].astype(bf16), xa_w_o[layer].astype(bf16),
                                           row(ln_g[layer, 1]), row(ln_b[layer, 1]), rwt, rb, batch, seq, n_mem)
        h = _moe(layer, hext, route, counts, moe_w_gate, moe_w_up, moe_w_down,
                 row(ln_g[layer, 2]), row(ln_b[layer, 2]))
    return h.reshape(batch, seq, d)
```

```python
import functools

import jax
import jax.numpy as jnp
from jax import lax
from jax.experimental import pallas as pl
from jax.experimental.pallas import tpu as pltpu

D_MODEL = 1024
DEPTH = 2
DN_HEADS = 4
DN_HEAD_DIM = 128
DN_WIDTH = DN_HEADS * DN_HEAD_DIM
DN_CONV = 4
DN_CHUNK = 64
DA_HEADS = 8
DA_HEAD_DIM = 64
DA_WIDTH = DA_HEADS * DA_HEAD_DIM
DA_BRANCHES = ((128, 1), (512, 4), (2048, 16))
DA_BLOCK = 128
DA_RESIDUES = max(d for _, d in DA_BRANCHES)
SC_CONV = 3
XA_HEADS = 4
XA_HEAD_DIM = D_MODEL // XA_HEADS
N_EXPERTS = 16
N_GROUPS = 4
EXPERTS_PER_GROUP = N_EXPERTS // N_GROUPS
D_EXPERT = D_MODEL // 2
DEEPNORM_ALPHA = (2 * DEPTH) ** 0.25
LN_EPS = 1e-5
RMS_EPS = 1e-6

LANES = 128
SUBLANES = 8
VMEM_LIMIT_BYTES = 48 * 1024 * 1024

TOKEN_TILE = 512
DN_BLOCK = 256
MOE_TILE = 256
DISPATCH_CHUNK = 512
DA_UNROLL = 2
GATE_LANES = LANES
ROW_EXT = D_MODEL + GATE_LANES

_PAIRS = tuple((i, j) for i in range(EXPERTS_PER_GROUP) for j in range(i + 1, EXPERTS_PER_GROUP))
N_BUCKETS = N_GROUPS * len(_PAIRS)
BUCKET_ROWS = 32

NEG_BIG = -1e30
LOG2_E = 1.4426950408889634

bf16 = jnp.bfloat16
f32 = jnp.float32


def _params(semantics):
    return pltpu.CompilerParams(dimension_semantics=semantics, vmem_limit_bytes=VMEM_LIMIT_BYTES)


def _dot(a, b):
    return jnp.dot(a.astype(bf16), b.astype(bf16), preferred_element_type=f32)


def _dot_nt(a, b):
    return lax.dot_general(a.astype(bf16), b.astype(bf16), (((1,), (1,)), ((), ())), preferred_element_type=f32)


def _dot_tn(a, b):
    return lax.dot_general(a.astype(bf16), b.astype(bf16), (((0,), (0,)), ((), ())), preferred_element_type=f32)


def _split(a, parts):
    out = []
    rem = a
    for _ in range(parts):
        p = rem.astype(bf16)
        out.append(p)
        rem = rem - p.astype(f32)
    return out


def _layer_norm(v, g, b):
    mu = jnp.mean(v, -1, keepdims=True)
    c = v - mu
    var = jnp.mean(c * c, -1, keepdims=True)
    return c * lax.rsqrt(var + LN_EPS) * g + b


def _silu(v):
    return v * jax.nn.sigmoid(v)


def _softplus(v):
    return jnp.maximum(v, 0.0) + jnp.log(1.0 + jnp.exp(-jnp.abs(v)))


def _proj_kernel(x_ref, *refs, n_out):
    w_refs, o_refs = refs[:n_out], refs[n_out:]
    xb = x_ref[...].astype(bf16)
    for w_ref, o_ref in zip(w_refs, o_refs):
        n = w_ref.shape[1]
        for c0 in range(0, n, 512):
            c1 = min(c0 + 512, n)
            o_ref[:, c0:c1] = jnp.dot(xb, w_ref[:, c0:c1], preferred_element_type=f32).astype(o_ref.dtype)


def _project(x, ws, out_dtypes, tm):
    t, k = x.shape
    n_out = len(ws)
    return pl.pallas_call(
        functools.partial(_proj_kernel, n_out=n_out),
        grid=(t // tm,),
        in_specs=[pl.BlockSpec((tm, k), lambda i: (i, 0))]
        + [pl.BlockSpec(w.shape, lambda i: (0, 0)) for w in ws],
        out_specs=[pl.BlockSpec((tm, w.shape[1]), lambda i: (i, 0)) for w in ws],
        out_shape=[jax.ShapeDtypeStruct((t, w.shape[1]), dt) for w, dt in zip(ws, out_dtypes)],
        compiler_params=_params(("parallel",)),
        name="project",
    )(x, *ws)


def _deltanet_kernel(qkv_ref, ba_ref, z_ref, cw_ref, alog_ref, dtb_ref, nw_ref, o_ref, state_ref, xx_ref):
    @pl.when(pl.program_id(1) == 0)
    def _():
        state_ref[...] = jnp.zeros_like(state_ref)
        xx_ref[0:SUBLANES, :] = jnp.zeros((SUBLANES, xx_ref.shape[1]), f32)

    rows, ch, hd, width = DN_BLOCK, DN_CHUNK, DN_HEAD_DIM, DN_WIDTH
    sup = 2 * ch
    n_sup = rows // sup

    x = qkv_ref[...]
    xx_ref[SUBLANES:, :] = x
    cw = cw_ref[...]
    conv = x * cw[DN_CONV - 1:DN_CONV, :]
    for j in range(DN_CONV - 1):
        back = DN_CONV - 1 - j
        conv = conv + xx_ref[pl.ds(SUBLANES - back, rows), :] * cw[j:j + 1, :]
    xx_ref[0:SUBLANES, :] = x[rows - SUBLANES:, :]
    act = _silu(conv)

    ba = ba_ref[...]
    beta_all = jax.nn.sigmoid(ba)
    g_all = -jnp.exp(alog_ref[...]) * _softplus(ba + dtb_ref[...])

    ri = lax.broadcasted_iota(jnp.int32, (sup, sup), 0)
    ci = lax.broadcasted_iota(jnp.int32, (sup, sup), 1)
    same = (ri >= ch) == (ci >= ch)
    incl = jnp.logical_and(ri >= ci, same)
    strict = jnp.logical_and(ri > ci, same)
    lower_ones = jnp.where(incl, 1.0, 0.0).astype(bf16)
    eye = jnp.where(ri == ci, 1.0, 0.0)
    first_rows = lax.broadcasted_iota(jnp.int32, (sup, hd), 0) < ch
    z = z_ref[...]
    nw = nw_ref[...]

    chains = [(s, h) for s in range(n_sup) for h in range(DN_HEADS)]
    g_cum_all, g_cum_t = [], []
    for s in range(n_sup):
        g_blk = g_all[s * sup:(s + 1) * sup, :]
        gc = sum(jnp.dot(lower_ones, p, preferred_element_type=f32) for p in _split(g_blk, 2))
        g_cum_all.append(gc)
        g_cum_t.append(jnp.transpose(gc))

    st = {}
    for s, h in chains:
        r0 = s * sup
        q = act[r0:r0 + sup, h * hd:(h + 1) * hd]
        k = act[r0:r0 + sup, width + h * hd:width + (h + 1) * hd]
        v = act[r0:r0 + sup, 2 * width + h * hd:2 * width + (h + 1) * hd]
        q = q * (lax.rsqrt(jnp.sum(q * q, -1, keepdims=True) + RMS_EPS) * (hd ** -0.5))
        k = k * lax.rsqrt(jnp.sum(k * k, -1, keepdims=True) + RMS_EPS)
        beta_b = jnp.broadcast_to(beta_all[r0:r0 + sup, h:h + 1], (sup, hd))
        g_i = jnp.broadcast_to(g_cum_all[s][:, DN_HEADS + h:DN_HEADS + h + 1], (sup, hd))
        g_j = jnp.broadcast_to(g_cum_t[s][DN_HEADS + h:DN_HEADS + h + 1, :], (sup, sup))
        decay = jnp.where(incl, jnp.exp(jnp.minimum(g_i - g_j, 0.0)), 0.0)
        e_g = jnp.exp(g_i)
        g_last = jnp.where(first_rows, g_i[ch - 1:ch, :], g_i[sup - 1:sup, :])
        st[s, h] = dict(q=q, k=k, kb=k * beta_b, vb=v * beta_b, decay=decay, e_g=e_g,
                        k_tail=k * jnp.exp(g_last - g_i),
                        gl=(jnp.exp(g_i[ch - 1:ch, :]), jnp.exp(g_i[sup - 1:sup, :])))

    for c in chains:
        d = st[c]
        d["a"] = jnp.where(strict, _dot_nt(d["kb"], d["k"]) * d["decay"], 0.0)
        d["attn"] = jnp.where(incl, _dot_nt(d["q"], d["k"]) * d["decay"], 0.0)
    for c in chains:
        st[c]["t_inv"] = eye - st[c]["a"]
        st[c]["pw"] = st[c]["a"]
    for _ in range(5):
        for c in chains:
            st[c]["pw"] = _dot(st[c]["pw"], st[c]["pw"])
        for c in chains:
            st[c]["t_inv"] = st[c]["t_inv"] + _dot(st[c]["t_inv"], st[c]["pw"])
    for c in chains:
        d = st[c]
        sol = _dot(d["t_inv"], jnp.concatenate([d["vb"], d["kb"] * d["e_g"]], axis=1))
        d["u"], d["w"] = sol[:, :hd], sol[:, hd:]
    for c in chains:
        d = st[c]
        uw = jnp.concatenate([d["u"], d["w"]], axis=1).astype(bf16)
        mix = jnp.dot(d["attn"].astype(bf16), uw, preferred_element_type=f32)
        d["au"] = mix[:, :hd]
        d["qe"] = d["q"] * d["e_g"] - mix[:, hd:]
        kt = d["k_tail"].astype(bf16)
        zero = jnp.zeros_like(kt)
        d["nu_pw"] = [_dot_tn(jnp.where(first_rows == first, kt, zero), uw)
                      for first in (True, False)]

    for s in range(n_sup):
        for half in range(2):
            for h in range(DN_HEADS):
                d = st[s, h]
                state = state_ref[h]
                lo = half * ch
                o = _dot(d["qe"][lo:lo + ch], state) + d["au"][lo:lo + ch]
                nu_pw = d["nu_pw"][half]
                state_ref[h] = state * d["gl"][half] - _dot(nu_pw[:, hd:], state) + nu_pw[:, :hd]
                o = o * lax.rsqrt(jnp.mean(o * o, -1, keepdims=True) + RMS_EPS) * nw
                r0 = s * sup + lo
                o_ref[r0:r0 + ch, h * hd:(h + 1) * hd] = (o * _silu(z[r0:r0 + ch, h * hd:(h + 1) * hd])).astype(o_ref.dtype)


def _deltanet(qkv, ba, z, conv_w, alog_row, dtb_row, norm_w, batch, seq):
    nb = seq // DN_BLOCK
    row = lambda b, c: (b * nb + c, 0)
    fixed = lambda b, c: (0, 0)
    return pl.pallas_call(
        _deltanet_kernel,
        grid=(batch, nb),
        in_specs=[
            pl.BlockSpec((DN_BLOCK, 3 * DN_WIDTH), row),
            pl.BlockSpec((DN_BLOCK, LANES), row),
            pl.BlockSpec((DN_BLOCK, DN_WIDTH), row),
            pl.BlockSpec(conv_w.shape, fixed),
            pl.BlockSpec((1, LANES), fixed),
            pl.BlockSpec((1, LANES), fixed),
            pl.BlockSpec((1, DN_HEAD_DIM), fixed),
        ],
        out_specs=pl.BlockSpec((DN_BLOCK, DN_WIDTH), row),
        out_shape=jax.ShapeDtypeStruct((batch * seq, DN_WIDTH), bf16),
        scratch_shapes=[
            pltpu.VMEM((DN_HEADS, DN_HEAD_DIM, DN_HEAD_DIM), f32),
            pltpu.VMEM((SUBLANES + DN_BLOCK, 3 * DN_WIDTH), f32),
        ],
        compiler_params=_params(("arbitrary", "arbitrary")),
        name="deltanet",
    )(qkv, ba, z, conv_w, alog_row, dtb_row, norm_w)


def _dilated_kernel(q_ref, k_ref, v_ref, o_ref, qs_ref, ks_ref, vs_ref, ob_ref, lse_ref, s_ref, p_ref, st_ref, *, seq):
    blk, res = DA_BLOCK, DA_RESIDUES
    per = seq // res
    head0 = lax.broadcasted_iota(jnp.int32, (blk, LANES), 1) < DA_HEAD_DIM
    ai = lax.broadcasted_iota(jnp.int32, (2 * blk, blk), 0) & (blk - 1)
    bi = lax.broadcasted_iota(jnp.int32, (2 * blk, blk), 1)
    ones_blk = jnp.ones((blk, LANES), bf16)
    lane_head0 = lax.broadcasted_iota(jnp.int32, (LANES, 2 * LANES), 0) < DA_HEAD_DIM
    col_head0 = lax.broadcasted_iota(jnp.int32, (LANES, 2 * LANES), 1) < LANES
    head_sum = jnp.where(lane_head0 == col_head0, 1.0, 0.0).astype(bf16)

    q_scale = DA_HEAD_DIM ** -0.5 * LOG2_E
    for r in range(res):
        rows = pl.ds(r, per, stride=res)
        qs_ref[r * per:(r + 1) * per, :] = q_ref[0, rows, :] * q_scale
        ks_ref[r * per:(r + 1) * per, :] = k_ref[0, rows, :]
        vs_ref[r * per:(r + 1) * per, :] = v_ref[0, rows, :]

    for br, (window, dil) in enumerate(DA_BRANCHES):
        assert window // dil == blk and res % dil == 0
        runs = res // dil
        run_len = blk // runs
        nblk = seq // (dil * blk)
        shift = run_len.bit_length() - 1

        def pos(a, runs=runs, run_len=run_len, shift=shift):
            return runs * (a & (run_len - 1)) + (a >> shift)

        from_cur = pos(bi) <= pos(ai)

        n_iter = dil * nblk // DA_UNROLL
        assert n_iter * DA_UNROLL == dil * nblk

        def tile(ref, offs, run_len=run_len, runs=runs):
            parts = [ref[pl.ds(o, run_len), :] for o in offs]
            return parts[0] if runs == 1 else jnp.concatenate(parts, axis=0)

        def blocks_of(it, dil=dil, runs=runs, run_len=run_len, nblk=nblk):
            out = []
            for u in range(DA_UNROLL):
                idx = it * DA_UNROLL + u
                cls = idx // nblk
                n = idx - cls * nblk
                n_prev = jnp.maximum(n - 1, 0)
                cur = [pl.multiple_of((dil * m + cls) * per + run_len * n, SUBLANES) for m in range(runs)]
                prev = [pl.multiple_of((dil * m + cls) * per + run_len * n_prev, SUBLANES) for m in range(runs)]
                out.append((cur, prev, n > 0))
            return out

        def masked_scores(blocks, tile=tile, from_cur=from_cur):
            out = []
            for cur, prev, has_prev in blocks:
                q = tile(qs_ref, cur).astype(bf16)
                zero = jnp.zeros_like(q)
                q2 = jnp.concatenate([jnp.where(head0, q, zero), jnp.where(head0, zero, q)], axis=0)
                s_c = _dot_nt(q2, tile(ks_ref, cur))
                s_p = _dot_nt(q2, tile(ks_ref, prev))
                out.append(jnp.where(from_cur, s_c, jnp.where(has_prev, s_p, NEG_BIG)))
            return out

        def softmax_stage(blocks, scores, tile=tile):
            out = []
            for (cur, prev, has_prev), s in zip(blocks, scores):
                sd = jnp.dot((tile(qs_ref, cur) * tile(ks_ref, prev)).astype(bf16), head_sum, preferred_element_type=f32)
                row_max = jnp.max(s, -1, keepdims=True)
                m0 = jnp.maximum(row_max[:blk], sd[:, :LANES])
                m1 = jnp.maximum(row_max[blk:], sd[:, LANES:])
                p = jnp.exp2(s - jnp.concatenate([m0, m1], axis=0))
                top = jnp.where(head0, m0, m1)
                p_d = jnp.where(has_prev, jnp.exp2(jnp.where(head0, sd[:, :LANES], sd[:, LANES:]) - top), 0.0)
                out.append((p.astype(bf16), p_d, top))
            return out

        def output_stage(blocks, probs, br=br, run_len=run_len, from_cur=from_cur, tile=tile):
            for (cur, prev, has_prev), (pb, p_d, top) in zip(blocks, probs):
                zb = jnp.zeros_like(pb)
                acc = (jnp.dot(jnp.where(from_cur, pb, zb), jnp.concatenate([tile(vs_ref, cur).astype(bf16), ones_blk], axis=1),
                               preferred_element_type=f32)
                       + jnp.dot(jnp.where(from_cur, zb, pb), jnp.concatenate([tile(vs_ref, prev).astype(bf16), ones_blk], axis=1),
                                 preferred_element_type=f32))
                den = jnp.where(head0, acc[:blk, LANES:], acc[blk:, LANES:]) + p_d
                o_blk = (jnp.where(head0, acc[:blk, :LANES], acc[blk:, :LANES]) + p_d * tile(vs_ref, prev)) / den
                lse_blk = top + jnp.log2(den)
                for m, o in enumerate(cur):
                    ob_ref[br, pl.ds(o, run_len), :] = o_blk[m * run_len:(m + 1) * run_len]
                    lse_ref[br, pl.ds(o, run_len), :] = lse_blk[m * run_len:(m + 1) * run_len]

        def put_scores(scores):
            for u, sc in enumerate(scores):
                s_ref[u] = sc

        def put_probs(probs):
            for u, (pb, p_d, top) in enumerate(probs):
                p_ref[u] = pb
                st_ref[u, 0] = p_d
                st_ref[u, 1] = top

        put_probs(softmax_stage(blocks_of(0), masked_scores(blocks_of(0))))
        put_scores(masked_scores(blocks_of(min(1, n_iter - 1))))

        def body(it, carry, n_iter=n_iter, blocks_of=blocks_of, masked_scores=masked_scores,
                 softmax_stage=softmax_stage, output_stage=output_stage, put_scores=put_scores, put_probs=put_probs):
            scores = [s_ref[u] for u in range(DA_UNROLL)]
            probs = [(p_ref[u], st_ref[u, 0], st_ref[u, 1]) for u in range(DA_UNROLL)]
            output_stage(blocks_of(it), probs)
            next_probs = softmax_stage(blocks_of(jnp.minimum(it + 1, n_iter - 1)), scores)
            next_scores = masked_scores(blocks_of(jnp.minimum(it + 2, n_iter - 1)))
            put_scores(next_scores)
            put_probs(next_probs)
            return carry

        lax.fori_loop(0, n_iter, body, 0)

    for r in range(res):
        rows = slice(r * per, (r + 1) * per)
        lses = [lse_ref[br, rows, :] for br in range(len(DA_BRANCHES))]
        top = functools.reduce(jnp.maximum, lses)
        wts = [jnp.exp2(l - top) for l in lses]
        num = sum(w * ob_ref[br, rows, :] for br, w in enumerate(wts))
        o_ref[0, pl.ds(r, per, stride=res), :] = num / sum(wts)


def _dilated_attention(qkv3, batch, seq):
    pairs = DA_WIDTH // LANES
    n_br = len(DA_BRANCHES)
    spec = lambda off: pl.BlockSpec((1, seq, LANES), lambda b, p, off=off: (b, 0, off + p))
    return pl.pallas_call(
        functools.partial(_dilated_kernel, seq=seq),
        grid=(batch, pairs),
        in_specs=[spec(0), spec(pairs), spec(2 * pairs)],
        out_specs=pl.BlockSpec((1, seq, LANES), lambda b, p: (b, 0, p)),
        out_shape=jax.ShapeDtypeStruct((batch, seq, DA_WIDTH), f32),
        scratch_shapes=[pltpu.VMEM((seq, LANES), f32)] * 3
        + [pltpu.VMEM((n_br, seq, LANES), f32)] * 2
        + [pltpu.VMEM((DA_UNROLL, 2 * DA_BLOCK, DA_BLOCK), f32),
           pltpu.VMEM((DA_UNROLL, 2 * DA_BLOCK, DA_BLOCK), bf16),
           pltpu.VMEM((DA_UNROLL, 2, DA_BLOCK, LANES), f32)],
        compiler_params=_params(("parallel", "parallel")),
        name="dilated_attention",
    )(qkv3, qkv3, qkv3)


def _mix_out_kernel(oa_ref, ob_ref, h_ref, w_ref, g_ref, b_ref, o_ref):
    w = w_ref[...]
    y = jnp.dot(oa_ref[...], w[:DN_WIDTH], preferred_element_type=f32)
    y = y + jnp.dot(ob_ref[...].astype(bf16), w[DN_WIDTH:], preferred_element_type=f32)
    o_ref[...] = _layer_norm(DEEPNORM_ALPHA * h_ref[...] + y, g_ref[...], b_ref[...])


def _mix_out(o_a, o_b, h, w_out, g, b):
    t = h.shape[0]
    tm = TOKEN_TILE
    fixed = lambda i: (0, 0)
    return pl.pallas_call(
        _mix_out_kernel,
        grid=(t // tm,),
        in_specs=[
            pl.BlockSpec((tm, DN_WIDTH), lambda i: (i, 0)),
            pl.BlockSpec((tm, DA_WIDTH), lambda i: (i, 0)),
            pl.BlockSpec((tm, D_MODEL), lambda i: (i, 0)),
            pl.BlockSpec(w_out.shape, fixed),
            pl.BlockSpec((1, D_MODEL), fixed),
            pl.BlockSpec((1, D_MODEL), fixed),
        ],
        out_specs=pl.BlockSpec((tm, D_MODEL), lambda i: (i, 0)),
        out_shape=jax.ShapeDtypeStruct((t, D_MODEL), f32),
        compiler_params=_params(("parallel",)),
        name="mix_out",
    )(o_a, o_b, h, w_out, g, b)


def _shortconv_kernel(h_ref, win_ref, cw_ref, wout_ref, g_ref, b_ref, o_ref, tail_ref):
    @pl.when(pl.program_id(1) == 0)
    def _():
        tail_ref[...] = jnp.zeros_like(tail_ref)

    tm = h_ref.shape[0]
    d = D_MODEL
    h = h_ref[...]
    hb = h.astype(bf16)
    gate_b = jnp.dot(hb, win_ref[:, :d], preferred_element_type=f32)
    gate_c = jnp.dot(hb, win_ref[:, d:2 * d], preferred_element_type=f32)
    hid = jnp.dot(hb, win_ref[:, 2 * d:], preferred_element_type=f32)
    u = gate_c * hid
    uu = jnp.concatenate([tail_ref[...], u], axis=0)
    cw = cw_ref[...]
    conv = u * cw[SC_CONV - 1:SC_CONV, :]
    for j in range(SC_CONV - 1):
        back = SC_CONV - 1 - j
        conv = conv + uu[SUBLANES - back:SUBLANES - back + tm, :] * cw[j:j + 1, :]
    tail_ref[...] = u[tm - SUBLANES:, :]
    y = jnp.dot((gate_b * conv).astype(bf16), wout_ref[...], preferred_element_type=f32)
    o_ref[...] = _layer_norm(DEEPNORM_ALPHA * h + y, g_ref[...], b_ref[...])


def _shortconv(h, w_in, conv_w, w_out, g, b, batch, seq):
    tm = TOKEN_TILE
    ns = seq // tm
    row = lambda bb, s: (bb * ns + s, 0)
    fixed = lambda bb, s: (0, 0)
    return pl.pallas_call(
        _shortconv_kernel,
        grid=(batch, ns),
        in_specs=[
            pl.BlockSpec((tm, D_MODEL), row),
            pl.BlockSpec(w_in.shape, fixed),
            pl.BlockSpec(conv_w.shape, fixed),
            pl.BlockSpec(w_out.shape, fixed),
            pl.BlockSpec((1, D_MODEL), fixed),
            pl.BlockSpec((1, D_MODEL), fixed),
        ],
        out_specs=pl.BlockSpec((tm, D_MODEL), row),
        out_shape=jax.ShapeDtypeStruct((batch * seq, D_MODEL), f32),
        scratch_shapes=[pltpu.VMEM((SUBLANES, D_MODEL), f32)],
        compiler_params=_params(("arbitrary", "arbitrary")),
        name="shortconv",
    )(h, w_in, conv_w, w_out, g, b)


def _route(h2, rwt_ref, rb_ref, cnt_ref):
    tm = h2.shape[0]
    w_hi, w_lo = _split(rwt_ref[...], 2)
    x_hi, x_lo = _split(h2, 2)
    nt = (((1,), (1,)), ((), ()))
    logits = (lax.dot_general(w_hi, x_hi, nt, preferred_element_type=f32)
              + lax.dot_general(w_hi, x_lo, nt, preferred_element_type=f32)
              + lax.dot_general(w_lo, x_hi, nt, preferred_element_type=f32))
    mx = jnp.max(logits, axis=0, keepdims=True)
    ex = jnp.exp(logits - mx)
    scores = ex / jnp.sum(ex, axis=0, keepdims=True)
    sel = scores + rb_ref[...]
    best = jnp.full((1, tm), -jnp.inf, f32)
    bucket = jnp.zeros((1, tm), jnp.int32)
    s_a = jnp.zeros((1, tm), f32)
    s_b = jnp.zeros((1, tm), f32)
    for grp in range(N_GROUPS):
        for p, (i, j) in enumerate(_PAIRS):
            a, b = grp * EXPERTS_PER_GROUP + i, grp * EXPERTS_PER_GROUP + j
            ps = sel[a:a + 1, :] + sel[b:b + 1, :]
            upd = ps > best
            best = jnp.where(upd, ps, best)
            bucket = jnp.where(upd, grp * len(_PAIRS) + p, bucket)
            s_a = jnp.where(upd, scores[a:a + 1, :], s_a)
            s_b = jnp.where(upd, scores[b:b + 1, :], s_b)
    denom = s_a + s_b
    gate_rows = jnp.concatenate([s_a / denom, s_b / denom, jnp.zeros((GATE_LANES - 2, tm), f32)], axis=0)
    gates = jnp.transpose(gate_rows)
    onehot = jnp.where(lax.broadcasted_iota(jnp.int32, (BUCKET_ROWS, tm), 0) == bucket, 1.0, 0.0)
    before = lax.broadcasted_iota(jnp.int32, (tm, tm), 0) < lax.broadcasted_iota(jnp.int32, (tm, tm), 1)
    prefix = jnp.dot(onehot.astype(bf16), jnp.where(before, 1.0, 0.0).astype(bf16), preferred_element_type=f32)
    cnt = cnt_ref[...]
    rank = jnp.sum(onehot * (prefix + cnt[:, 0:1]), axis=0, keepdims=True)
    cnt_ref[...] = cnt + jnp.sum(onehot, axis=1, keepdims=True)
    return gates, bucket, rank.astype(jnp.int32)


def _xattn_kernel(h_ref, kv_ref, wq_ref, wo_ref, g_ref, b_ref, rwt_ref, rb_ref,
                  hext_ref, route_ref, cnt_out_ref, cnt_ref):
    @pl.when(jnp.logical_and(pl.program_id(0) == 0, pl.program_id(1) == 0))
    def _():
        cnt_ref[...] = jnp.zeros_like(cnt_ref)

    h = h_ref[...]
    tm = h.shape[0]
    q = jnp.dot(h.astype(bf16), wq_ref[...], preferred_element_type=f32).astype(bf16)
    outs = []
    for hd in range(XA_HEADS):
        lo, hi = hd * XA_HEAD_DIM, (hd + 1) * XA_HEAD_DIM
        s = _dot_nt(q[:, lo:hi], kv_ref[:, lo:hi]) * (XA_HEAD_DIM ** -0.5)
        p = jnp.exp(s - jnp.max(s, -1, keepdims=True))
        o = _dot(p, kv_ref[:, D_MODEL + lo:D_MODEL + hi]) / jnp.sum(p, -1, keepdims=True)
        outs.append(o.astype(bf16))
    y = jnp.dot(jnp.concatenate(outs, axis=1), wo_ref[...], preferred_element_type=f32)
    h2 = _layer_norm(DEEPNORM_ALPHA * h + y, g_ref[...], b_ref[...])
    gates, bucket, rank = _route(h2, rwt_ref, rb_ref, cnt_ref)
    hext_ref[:, :D_MODEL] = h2
    hext_ref[:, D_MODEL:] = gates
    route_ref[...] = jnp.concatenate([bucket, rank, jnp.zeros((SUBLANES - 2, tm), jnp.int32)], axis=0)
    cnt_out_ref[...] = cnt_ref[...]


def _xattn_route(h, kv, w_q, w_o, g, b, rwt, rb, batch, seq, n_mem):
    tm = TOKEN_TILE
    ns = seq // tm
    t = batch * seq
    row = lambda bb, s: (bb * ns + s, 0)
    fixed = lambda bb, s: (0, 0)
    return pl.pallas_call(
        _xattn_kernel,
        grid=(batch, ns),
        in_specs=[
            pl.BlockSpec((tm, D_MODEL), row),
            pl.BlockSpec((n_mem, 2 * D_MODEL), lambda bb, s: (bb, 0)),
            pl.BlockSpec(w_q.shape, fixed),
            pl.BlockSpec(w_o.shape, fixed),
            pl.BlockSpec((1, D_MODEL), fixed),
            pl.BlockSpec((1, D_MODEL), fixed),
            pl.BlockSpec(rwt.shape, fixed),
            pl.BlockSpec(rb.shape, fixed),
        ],
        out_specs=[
            pl.BlockSpec((tm, ROW_EXT), row),
            pl.BlockSpec((SUBLANES, tm), lambda bb, s: (0, bb * ns + s)),
            pl.BlockSpec((BUCKET_ROWS, LANES), fixed),
        ],
        out_shape=[
            jax.ShapeDtypeStruct((t, ROW_EXT), f32),
            jax.ShapeDtypeStruct((SUBLANES, t), jnp.int32),
            jax.ShapeDtypeStruct((BUCKET_ROWS, LANES), f32),
        ],
        scratch_shapes=[pltpu.VMEM((BUCKET_ROWS, LANES), f32)],
        compiler_params=_params(("arbitrary", "arbitrary")),
        name="xattn_route",
    )(h, kv, w_q, w_o, g, b, rwt, rb)


def _dispatch_kernel(dest_ref, fill_ref, hext_ref, xs_ref, inv_ref, zeros_ref, sem, zsem):
    base = pl.program_id(0) * DISPATCH_CHUNK
    n_tiles = xs_ref.shape[0] // MOE_TILE

    @pl.when(pl.program_id(0) == 0)
    def _():
        zeros_ref[...] = jnp.zeros_like(zeros_ref)

        def fill(tile):
            return pltpu.make_async_copy(zeros_ref, xs_ref.at[pl.ds(tile * MOE_TILE, MOE_TILE)], zsem)

        for tile in range(n_tiles):
            @pl.when(fill_ref[tile] > 0)
            def _(tile=tile):
                fill(tile).start()

        def clear(j8, carry):
            for u in range(8):
                inv_ref[j8 * 8 + u] = 0
            return carry

        lax.fori_loop(0, inv_ref.shape[0] // 8, clear, 0)
        for tile in range(n_tiles):
            @pl.when(fill_ref[tile] > 0)
            def _(tile=tile):
                fill(tile).wait()

    def issue(j8, carry):
        for u in range(8):
            j = j8 * 8 + u
            d = dest_ref[base + j]
            inv_ref[d] = base + j
            pltpu.make_async_copy(hext_ref.at[pl.ds(j, 1)], xs_ref.at[pl.ds(d, 1)], sem).start(priority=u % 2)
        return carry

    lax.fori_loop(0, DISPATCH_CHUNK // 8, issue, 0)
    pltpu.make_async_copy(hext_ref, xs_ref.at[pl.ds(0, DISPATCH_CHUNK)], sem).wait()


def _dispatch(dest, fill, hext, n_rows):
    t = hext.shape[0]
    return pl.pallas_call(
        _dispatch_kernel,
        grid_spec=pltpu.PrefetchScalarGridSpec(
            num_scalar_prefetch=2,
            grid=(t // DISPATCH_CHUNK,),
            in_specs=[pl.BlockSpec((DISPATCH_CHUNK, ROW_EXT), lambda i, *_: (i, 0))],
            out_specs=[pl.BlockSpec(memory_space=pl.ANY), pl.BlockSpec(memory_space=pltpu.SMEM)],
            scratch_shapes=[pltpu.VMEM((MOE_TILE, ROW_EXT), f32), pltpu.SemaphoreType.DMA(()),
                            pltpu.SemaphoreType.DMA(())],
        ),
        out_shape=[jax.ShapeDtypeStruct((n_rows, ROW_EXT), f32), jax.ShapeDtypeStruct((n_rows,), jnp.int32)],
        compiler_params=_params(("arbitrary",)),
        name="moe_dispatch",
    )(dest, fill, hext)


def _experts_kernel(grp_ref, ea_ref, eb_ref, nvalid_ref, xs_idx_ref, inv_ref, xs_ref, wg_ref, wu_ref, wd_ref,
                    g_ref, b_ref, out_ref, obuf_ref, sem_ref):
    i = pl.program_id(0)
    tm = MOE_TILE
    prev = jnp.maximum(i - 1, 0)
    n_prev = jnp.where(i > 0, nvalid_ref[prev], 0)
    slot_prev = prev % 2

    def send(j):
        return pltpu.make_async_copy(obuf_ref.at[slot_prev, pl.ds(j, 1)],
                                     out_ref.at[pl.ds(inv_ref[prev * tm + j], 1)], sem_ref.at[slot_prev])

    def send_rows(lo, hi):
        for j in range(lo, hi):
            @pl.when(j < n_prev)
            def _(j=j):
                send(j).start(priority=j % 2)

    n_cur = jnp.where(i < pl.num_programs(0) - 1, nvalid_ref[jnp.minimum(i, pl.num_programs(0) - 2)], 0)
    send_rows(0, tm)

    @pl.when(n_cur > 0)
    def _():
        x32 = xs_ref[:, :D_MODEL]
        x = x32.astype(bf16)
        gates = xs_ref[:, D_MODEL:]

        e_a, e_b = ea_ref[i], eb_ref[i]

        def hidden(e, gate):
            hg = jnp.dot(x, wg_ref[0, e].astype(bf16), preferred_element_type=f32)
            hu = jnp.dot(x, wu_ref[0, e].astype(bf16), preferred_element_type=f32)
            return (_silu(hg) * hu * gate).astype(bf16)

        h_a = hidden(e_a, gates[:, 0:1])
        h_b = hidden(e_b, gates[:, 1:2])
        y = (jnp.dot(h_a, wd_ref[0, e_a].astype(bf16), preferred_element_type=f32)
             + jnp.dot(h_b, wd_ref[0, e_b].astype(bf16), preferred_element_type=f32))
        obuf_ref[i % 2] = _layer_norm(DEEPNORM_ALPHA * x32 + y, g_ref[...], b_ref[...])

    for bit in reversed(range(tm.bit_length())):
        size = 1 << bit

        @pl.when((n_prev & size) != 0)
        def _(size=size):
            pltpu.make_async_copy(obuf_ref.at[slot_prev, pl.ds(0, size)], out_ref.at[pl.ds(0, size)],
                                  sem_ref.at[slot_prev]).wait()


def _experts(layer, grp, ea, eb, nvalid, xs_idx, inv, xs, w_gate, w_up, w_down, g, b, t):
    n_tiles = xs.shape[0] // MOE_TILE
    once = dict(pipeline_mode=pl.Buffered(1))
    up = pl.BlockSpec((1, EXPERTS_PER_GROUP, D_MODEL, D_EXPERT), lambda i, grp, *_: (layer, grp[i], 0, 0), **once)
    dn = pl.BlockSpec((1, EXPERTS_PER_GROUP, D_EXPERT, D_MODEL), lambda i, grp, *_: (layer, grp[i], 0, 0), **once)
    fixed = lambda i, *_: (0, 0)
    return pl.pallas_call(
        _experts_kernel,
        grid_spec=pltpu.PrefetchScalarGridSpec(
            num_scalar_prefetch=6,
            grid=(n_tiles + 1,),
            in_specs=[pl.BlockSpec((MOE_TILE, ROW_EXT), lambda i, grp, ea, eb, nv, xs_idx, inv: (xs_idx[i], 0)),
                      up, up, dn,
                      pl.BlockSpec((1, D_MODEL), fixed), pl.BlockSpec((1, D_MODEL), fixed)],
            out_specs=pl.BlockSpec(memory_space=pl.ANY),
            scratch_shapes=[pltpu.VMEM((2, MOE_TILE, D_MODEL), f32), pltpu.SemaphoreType.DMA((2,))],
        ),
        out_shape=jax.ShapeDtypeStruct((t, D_MODEL), f32),
        compiler_params=_params(("arbitrary",)),
        name="moe_experts",
    )(grp, ea, eb, nvalid, xs_idx, inv, xs, w_gate, w_up, w_down, g, b)


def _moe_plan(route, counts):
    t = route.shape[1]
    n_rows = t + N_BUCKETS * MOE_TILE
    n_tiles = n_rows // MOE_TILE
    cnt = counts[:N_BUCKETS, 0].astype(jnp.int32)
    padded = (cnt + MOE_TILE - 1) // MOE_TILE * MOE_TILE
    ends = jnp.cumsum(padded)
    starts = ends - padded
    dest = starts[route[0]] + route[1]
    tile_start = jnp.arange(n_tiles + 1, dtype=jnp.int32) * MOE_TILE
    tile_bucket = jnp.sum(tile_start[:, None] >= ends[None, :], axis=1).astype(jnp.int32)
    used = tile_bucket < N_BUCKETS
    tile_bucket = jnp.minimum(tile_bucket, N_BUCKETS - 1)
    nvalid = jnp.where(used, jnp.clip(starts[tile_bucket] + cnt[tile_bucket] - tile_start, 0, MOE_TILE), 0)
    xs_idx = jnp.minimum(jnp.arange(n_tiles + 1, dtype=jnp.int32), jnp.maximum(ends[-1] // MOE_TILE - 1, 0))
    last_bucket = tile_bucket[jnp.maximum(ends[-1] // MOE_TILE - 1, 0)]
    tile_bucket = jnp.where(used, tile_bucket, last_bucket)
    pair_a = jnp.array([i for _ in range(N_GROUPS) for i, _ in _PAIRS], jnp.int32)
    pair_b = jnp.array([j for _ in range(N_GROUPS) for _, j in _PAIRS], jnp.int32)
    tiles = jnp.arange(n_tiles, dtype=jnp.int32)
    is_last = jnp.any(jnp.logical_and((tiles[:, None] + 1) * MOE_TILE == ends[None, :], padded[None, :] > 0), axis=1)
    fill = jnp.logical_or(is_last, tiles * MOE_TILE >= ends[-1]).astype(jnp.int32)
    return dict(dest=dest, grp=tile_bucket // len(_PAIRS), ea=pair_a[tile_bucket], eb=pair_b[tile_bucket],
                nvalid=nvalid.astype(jnp.int32),
                xs_idx=xs_idx, fill=fill, n_rows=n_rows)


def _moe(layer, hext, route, counts, w_gate, w_up, w_down, g, b):
    plan = _moe_plan(route, counts)
    xs, inv = _dispatch(plan["dest"], plan["fill"], hext, plan["n_rows"])
    return _experts(layer, plan["grp"], plan["ea"], plan["eb"], plan["nvalid"], plan["xs_idx"], inv, xs,
                    w_gate, w_up, w_down, g, b, hext.shape[0])


def kernel(x, mem, ab_w_in, ab_conv_w, ab_a_log, ab_dt_bias, ab_norm_w, ab_w_out, sc_w_in, sc_conv_w, sc_w_out, xa_w_q, xa_w_kv, xa_w_o, router_w, router_b, moe_w_gate, moe_w_up, moe_w_down, ln_g, ln_b):
    batch, seq, d = x.shape
    n_mem = mem.shape[1]
    t = batch * seq
    h = x.reshape(t, d)
    mem2 = mem.reshape(batch * n_mem, d)
    rwt = jnp.transpose(router_w)
    rb = router_b.reshape(N_EXPERTS, 1)
    row = lambda v: v.reshape(1, -1)

    for layer in range(DEPTH):
        i = layer // 2
        if layer % 2 == 0:
            w_in = ab_w_in[i]
            c0, c1, c2 = 3 * DN_WIDTH, 4 * DN_WIDTH, 4 * DN_WIDTH + 2 * DN_HEADS
            w_ba = jnp.pad(w_in[:, c1:c2], ((0, 0), (0, LANES - 2 * DN_HEADS)))
            ws = [w_in[:, :c0].astype(bf16), w_in[:, c0:c1].astype(bf16), w_ba.astype(bf16), w_in[:, c2:].astype(bf16)]
            qkv, z, ba, qkv_d = _project(h, ws, [f32, f32, f32, f32], TOKEN_TILE)
            lane_pad = lambda v: jnp.pad(v.reshape(1, DN_HEADS), ((0, 0), (DN_HEADS, LANES - 2 * DN_HEADS)))
            o_a = _deltanet(qkv, ba, z, ab_conv_w[i], lane_pad(ab_a_log[i]), lane_pad(ab_dt_bias[i]),
                            row(ab_norm_w[i]), batch, seq)
            o_b = _dilated_attention(qkv_d.reshape(batch, seq, 3 * DA_WIDTH), batch, seq).reshape(t, DA_WIDTH)
            h = _mix_out(o_a, o_b, h, ab_w_out[i].astype(bf16), row(ln_g[layer, 0]), row(ln_b[layer, 0]))
        else:
            h = _shortconv(h, sc_w_in[i].astype(bf16), sc_conv_w[i], sc_w_out[i].astype(bf16),
                           row(ln_g[layer, 0]), row(ln_b[layer, 0]), batch, seq)
        (kv,) = _project(mem2, [xa_w_kv[layer].astype(bf16)], [bf16], n_mem)
        hext, route, counts = _xattn_route(h, kv, xa_w_q[layer].astype(bf16), xa_w_o[layer].astype(bf16),
                                           row(ln_g[layer, 1]), row(ln_b[layer, 1]), rwt, rb, batch, seq, n_mem)
        h = _moe(layer, hext, route, counts, moe_w_gate, moe_w_up, moe_w_down,
                 row(ln_g[layer, 2]), row(ln_b[layer, 2]))
    return h.reshape(batch, seq, d)
```

```python
import functools

import jax
import jax.numpy as jnp
from jax import lax
from jax.experimental import pallas as pl
from jax.experimental.pallas import tpu as pltpu

D_MODEL = 1024
DEPTH = 2
DN_HEADS = 4
DN_HEAD_DIM = 128
DN_WIDTH = DN_HEADS * DN_HEAD_DIM
DN_CONV = 4
DN_CHUNK = 64
DA_HEADS = 8
DA_HEAD_DIM = 64
DA_WIDTH = DA_HEADS * DA_HEAD_DIM
DA_BRANCHES = ((128, 1), (512, 4), (2048, 16))
DA_BLOCK = 128
DA_RESIDUES = max(d for _, d in DA_BRANCHES)
SC_CONV = 3
XA_HEADS = 4
XA_HEAD_DIM = D_MODEL // XA_HEADS
N_EXPERTS = 16
N_GROUPS = 4
EXPERTS_PER_GROUP = N_EXPERTS // N_GROUPS
D_EXPERT = D_MODEL // 2
DEEPNORM_ALPHA = (2 * DEPTH) ** 0.25
LN_EPS = 1e-5
RMS_EPS = 1e-6

LANES = 128
SUBLANES = 8
VMEM_LIMIT_BYTES = 48 * 1024 * 1024

TOKEN_TILE = 512
DN_BLOCK = 256
MOE_TILE = 256
DISPATCH_CHUNK = 512
DA_UNROLL = 2
GATE_LANES = LANES
ROW_EXT = D_MODEL + GATE_LANES

_PAIRS = tuple((i, j) for i in range(EXPERTS_PER_GROUP) for j in range(i + 1, EXPERTS_PER_GROUP))
N_BUCKETS = N_GROUPS * len(_PAIRS)
BUCKET_ROWS = 32

NEG_BIG = -1e30
LOG2_E = 1.4426950408889634

bf16 = jnp.bfloat16
f32 = jnp.float32


def _params(semantics):
    return pltpu.CompilerParams(dimension_semantics=semantics, vmem_limit_bytes=VMEM_LIMIT_BYTES)


def _dot(a, b):
    return jnp.dot(a.astype(bf16), b.astype(bf16), preferred_element_type=f32)


def _dot_nt(a, b):
    return lax.dot_general(a.astype(bf16), b.astype(bf16), (((1,), (1,)), ((), ())), preferred_element_type=f32)


def _dot_tn(a, b):
    return lax.dot_general(a.astype(bf16), b.astype(bf16), (((0,), (0,)), ((), ())), preferred_element_type=f32)


def _split(a, parts):
    out = []
    rem = a
    for _ in range(parts):
        p = rem.astype(bf16)
        out.append(p)
        rem = rem - p.astype(f32)
    return out


def _layer_norm(v, g, b):
    mu = jnp.mean(v, -1, keepdims=True)
    c = v - mu
    var = jnp.mean(c * c, -1, keepdims=True)
    return c * lax.rsqrt(var + LN_EPS) * g + b


def _silu(v):
    return v * jax.nn.sigmoid(v)


def _softplus(v):
    return jnp.maximum(v, 0.0) + jnp.log(1.0 + jnp.exp(-jnp.abs(v)))


def _proj_kernel(x_ref, *refs, n_out):
    w_refs, o_refs = refs[:n_out], refs[n_out:]
    xb = x_ref[...].astype(bf16)
    for w_ref, o_ref in zip(w_refs, o_refs):
        n = w_ref.shape[1]
        for c0 in range(0, n, 512):
            c1 = min(c0 + 512, n)
            o_ref[:, c0:c1] = jnp.dot(xb, w_ref[:, c0:c1], preferred_element_type=f32).astype(o_ref.dtype)


def _project(x, ws, out_dtypes, tm):
    t, k = x.shape
    n_out = len(ws)
    return pl.pallas_call(
        functools.partial(_proj_kernel, n_out=n_out),
        grid=(t // tm,),
        in_specs=[pl.BlockSpec((tm, k), lambda i: (i, 0))]
        + [pl.BlockSpec(w.shape, lambda i: (0, 0)) for w in ws],
        out_specs=[pl.BlockSpec((tm, w.shape[1]), lambda i: (i, 0)) for w in ws],
        out_shape=[jax.ShapeDtypeStruct((t, w.shape[1]), dt) for w, dt in zip(ws, out_dtypes)],
        compiler_params=_params(("parallel",)),
        name="project",
    )(x, *ws)


def _deltanet_kernel(qkv_ref, ba_ref, z_ref, cw_ref, alog_ref, dtb_ref, nw_ref, o_ref, state_ref, xx_ref):
    @pl.when(pl.program_id(1) == 0)
    def _():
        state_ref[...] = jnp.zeros_like(state_ref)
        xx_ref[0:SUBLANES, :] = jnp.zeros((SUBLANES, xx_ref.shape[1]), f32)

    rows, ch, hd, width = DN_BLOCK, DN_CHUNK, DN_HEAD_DIM, DN_WIDTH
    sup = 2 * ch
    n_sup = rows // sup

    x = qkv_ref[...]
    xx_ref[SUBLANES:, :] = x
    cw = cw_ref[...]
    conv = x * cw[DN_CONV - 1:DN_CONV, :]
    for j in range(DN_CONV - 1):
        back = DN_CONV - 1 - j
        conv = conv + xx_ref[pl.ds(SUBLANES - back, rows), :] * cw[j:j + 1, :]
    xx_ref[0:SUBLANES, :] = x[rows - SUBLANES:, :]
    act = _silu(conv)

    ba = ba_ref[...]
    beta_all = jax.nn.sigmoid(ba)
    g_all = -jnp.exp(alog_ref[...]) * _softplus(ba + dtb_ref[...])

    ri = lax.broadcasted_iota(jnp.int32, (sup, sup), 0)
    ci = lax.broadcasted_iota(jnp.int32, (sup, sup), 1)
    same = (ri >= ch) == (ci >= ch)
    incl = jnp.logical_and(ri >= ci, same)
    strict = jnp.logical_and(ri > ci, same)
    lower_ones = jnp.where(incl, 1.0, 0.0).astype(bf16)
    eye = jnp.where(ri == ci, 1.0, 0.0)
    first_rows = lax.broadcasted_iota(jnp.int32, (sup, hd), 0) < ch
    z = z_ref[...]
    nw = nw_ref[...]

    chains = [(s, h) for s in range(n_sup) for h in range(DN_HEADS)]
    g_cum_all, g_cum_t = [], []
    for s in range(n_sup):
        g_blk = g_all[s * sup:(s + 1) * sup, :]
        gc = sum(jnp.dot(lower_ones, p, preferred_element_type=f32) for p in _split(g_blk, 2))
        g_cum_all.append(gc)
        g_cum_t.append(jnp.transpose(gc))

    st = {}
    for s, h in chains:
        r0 = s * sup
        q = act[r0:r0 + sup, h * hd:(h + 1) * hd]
        k = act[r0:r0 + sup, width + h * hd:width + (h + 1) * hd]
        v = act[r0:r0 + sup, 2 * width + h * hd:2 * width + (h + 1) * hd]
        q = q * (lax.rsqrt(jnp.sum(q * q, -1, keepdims=True) + RMS_EPS) * (hd ** -0.5))
        k = k * lax.rsqrt(jnp.sum(k * k, -1, keepdims=True) + RMS_EPS)
        beta_b = jnp.broadcast_to(beta_all[r0:r0 + sup, h:h + 1], (sup, hd))
        g_i = jnp.broadcast_to(g_cum_all[s][:, DN_HEADS + h:DN_HEADS + h + 1], (sup, hd))
        g_j = jnp.broadcast_to(g_cum_t[s][DN_HEADS + h:DN_HEADS + h + 1, :], (sup, sup))
        decay = jnp.where(incl, jnp.exp(jnp.minimum(g_i - g_j, 0.0)), 0.0)
        e_g = jnp.exp(g_i)
        g_last = jnp.where(first_rows, g_i[ch - 1:ch, :], g_i[sup - 1:sup, :])
        st[s, h] = dict(q=q, k=k, kb=k * beta_b, vb=v * beta_b, decay=decay, e_g=e_g,
                        k_tail=k * jnp.exp(g_last - g_i),
                        gl=(jnp.exp(g_i[ch - 1:ch, :]), jnp.exp(g_i[sup - 1:sup, :])))

    for c in chains:
        d = st[c]
        d["a"] = jnp.where(strict, _dot_nt(d["kb"], d["k"]) * d["decay"], 0.0)
        d["attn"] = jnp.where(incl, _dot_nt(d["q"], d["k"]) * d["decay"], 0.0)
    for c in chains:
        st[c]["t_inv"] = eye - st[c]["a"]
        st[c]["pw"] = st[c]["a"]
    for _ in range(5):
        for c in chains:
            st[c]["pw"] = _dot(st[c]["pw"], st[c]["pw"])
        for c in chains:
            st[c]["t_inv"] = st[c]["t_inv"] + _dot(st[c]["t_inv"], st[c]["pw"])
    for c in chains:
        d = st[c]
        sol = _dot(d["t_inv"], jnp.concatenate([d["vb"], d["kb"] * d["e_g"]], axis=1))
        d["u"], d["w"] = sol[:, :hd], sol[:, hd:]
    for c in chains:
        d = st[c]
        uw = jnp.concatenate([d["u"], d["w"]], axis=1).astype(bf16)
        mix = jnp.dot(d["attn"].astype(bf16), uw, preferred_element_type=f32)
        d["au"] = mix[:, :hd]
        d["qe"] = d["q"] * d["e_g"] - mix[:, hd:]
        kt = d["k_tail"].astype(bf16)
        zero = jnp.zeros_like(kt)
        d["nu_pw"] = [_dot_tn(jnp.where(first_rows == first, kt, zero), uw)
                      for first in (True, False)]

    for s in range(n_sup):
        for half in range(2):
            for h in range(DN_HEADS):
                d = st[s, h]
                state = state_ref[h]
                lo = half * ch
                o = _dot(d["qe"][lo:lo + ch], state) + d["au"][lo:lo + ch]
                nu_pw = d["nu_pw"][half]
                state_ref[h] = state * d["gl"][half] - _dot(nu_pw[:, hd:], state) + nu_pw[:, :hd]
                o = o * lax.rsqrt(jnp.mean(o * o, -1, keepdims=True) + RMS_EPS) * nw
                r0 = s * sup + lo
                o_ref[r0:r0 + ch, h * hd:(h + 1) * hd] = (o * _silu(z[r0:r0 + ch, h * hd:(h + 1) * hd])).astype(o_ref.dtype)


def _deltanet(qkv, ba, z, conv_w, alog_row, dtb_row, norm_w, batch, seq):
    nb = seq // DN_BLOCK
    row = lambda b, c: (b * nb + c, 0)
    fixed = lambda b, c: (0, 0)
    return pl.pallas_call(
        _deltanet_kernel,
        grid=(batch, nb),
        in_specs=[
            pl.BlockSpec((DN_BLOCK, 3 * DN_WIDTH), row),
            pl.BlockSpec((DN_BLOCK, LANES), row),
            pl.BlockSpec((DN_BLOCK, DN_WIDTH), row),
            pl.BlockSpec(conv_w.shape, fixed),
            pl.BlockSpec((1, LANES), fixed),
            pl.BlockSpec((1, LANES), fixed),
            pl.BlockSpec((1, DN_HEAD_DIM), fixed),
        ],
        out_specs=pl.BlockSpec((DN_BLOCK, DN_WIDTH), row),
        out_shape=jax.ShapeDtypeStruct((batch * seq, DN_WIDTH), bf16),
        scratch_shapes=[
            pltpu.VMEM((DN_HEADS, DN_HEAD_DIM, DN_HEAD_DIM), f32),
            pltpu.VMEM((SUBLANES + DN_BLOCK, 3 * DN_WIDTH), f32),
        ],
        compiler_params=_params(("arbitrary", "arbitrary")),
        name="deltanet",
    )(qkv, ba, z, conv_w, alog_row, dtb_row, norm_w)


def _dilated_kernel(q_ref, k_ref, v_ref, o_ref, qs_ref, ks_ref, vs_ref, ob_ref, lse_ref, s_ref, p_ref, st_ref, *, seq):
    blk, res = DA_BLOCK, DA_RESIDUES
    per = seq // res
    head0 = lax.broadcasted_iota(jnp.int32, (blk, LANES), 1) < DA_HEAD_DIM
    ai = lax.broadcasted_iota(jnp.int32, (2 * blk, blk), 0) & (blk - 1)
    bi = lax.broadcasted_iota(jnp.int32, (2 * blk, blk), 1)
    ones_blk = jnp.ones((blk, LANES), bf16)
    lane_head0 = lax.broadcasted_iota(jnp.int32, (LANES, 2 * LANES), 0) < DA_HEAD_DIM
    col_head0 = lax.broadcasted_iota(jnp.int32, (LANES, 2 * LANES), 1) < LANES
    head_sum = jnp.where(lane_head0 == col_head0, 1.0, 0.0).astype(bf16)

    q_scale = DA_HEAD_DIM ** -0.5 * LOG2_E
    for r in range(res):
        rows = pl.ds(r, per, stride=res)
        qs_ref[r * per:(r + 1) * per, :] = q_ref[0, rows, :] * q_scale
        ks_ref[r * per:(r + 1) * per, :] = k_ref[0, rows, :]
        vs_ref[r * per:(r + 1) * per, :] = v_ref[0, rows, :]

    for br, (window, dil) in enumerate(DA_BRANCHES):
        assert window // dil == blk and res % dil == 0
        runs = res // dil
        run_len = blk // runs
        nblk = seq // (dil * blk)
        shift = run_len.bit_length() - 1

        def pos(a, runs=runs, run_len=run_len, shift=shift):
            return runs * (a & (run_len - 1)) + (a >> shift)

        from_cur = pos(bi) <= pos(ai)

        n_iter = dil * nblk // DA_UNROLL
        assert n_iter * DA_UNROLL == dil * nblk

        def tile(ref, offs, run_len=run_len, runs=runs):
            parts = [ref[pl.ds(o, run_len), :] for o in offs]
            return parts[0] if runs == 1 else jnp.concatenate(parts, axis=0)

        def blocks_of(it, dil=dil, runs=runs, run_len=run_len, nblk=nblk):
            out = []
            for u in range(DA_UNROLL):
                idx = it * DA_UNROLL + u
                cls = idx // nblk
                n = idx - cls * nblk
                n_prev = jnp.maximum(n - 1, 0)
                cur = [pl.multiple_of((dil * m + cls) * per + run_len * n, SUBLANES) for m in range(runs)]
                prev = [pl.multiple_of((dil * m + cls) * per + run_len * n_prev, SUBLANES) for m in range(runs)]
                out.append((cur, prev, n > 0))
            return out

        def masked_scores(blocks, tile=tile, from_cur=from_cur):
            out = []
            for cur, prev, has_prev in blocks:
                q = tile(qs_ref, cur).astype(bf16)
                zero = jnp.zeros_like(q)
                q2 = jnp.concatenate([jnp.where(head0, q, zero), jnp.where(head0, zero, q)], axis=0)
                s_c = _dot_nt(q2, tile(ks_ref, cur))
                s_p = _dot_nt(q2, tile(ks_ref, prev))
                out.append(jnp.where(from_cur, s_c, jnp.where(has_prev, s_p, NEG_BIG)))
            return out

        def softmax_stage(blocks, scores, tile=tile):
            out = []
            for (cur, prev, has_prev), s in zip(blocks, scores):
                sd = jnp.dot((tile(qs_ref, cur) * tile(ks_ref, prev)).astype(bf16), head_sum, preferred_element_type=f32)
                row_max = jnp.max(s, -1, keepdims=True)
                m0 = jnp.maximum(row_max[:blk], sd[:, :LANES])
                m1 = jnp.maximum(row_max[blk:], sd[:, LANES:])
                p = jnp.exp2(s - jnp.concatenate([m0, m1], axis=0))
                top = jnp.where(head0, m0, m1)
                p_d = jnp.where(has_prev, jnp.exp2(jnp.where(head0, sd[:, :LANES], sd[:, LANES:]) - top), 0.0)
                out.append((p.astype(bf16), p_d, top))
            return out

        def output_stage(blocks, probs, br=br, run_len=run_len, from_cur=from_cur, tile=tile):
            for (cur, prev, has_prev), (pb, p_d, top) in zip(blocks, probs):
                zb = jnp.zeros_like(pb)
                acc = (jnp.dot(jnp.where(from_cur, pb, zb), jnp.concatenate([tile(vs_ref, cur).astype(bf16), ones_blk], axis=1),
                               preferred_element_type=f32)
                       + jnp.dot(jnp.where(from_cur, zb, pb), jnp.concatenate([tile(vs_ref, prev).astype(bf16), ones_blk], axis=1),
                                 preferred_element_type=f32))
                den = jnp.where(head0, acc[:blk, LANES:], acc[blk:, LANES:]) + p_d
                o_blk = (jnp.where(head0, acc[:blk, :LANES], acc[blk:, :LANES]) + p_d * tile(vs_ref, prev)) / den
                lse_blk = top + jnp.log2(den)
                for m, o in enumerate(cur):
                    ob_ref[br, pl.ds(o, run_len), :] = o_blk[m * run_len:(m + 1) * run_len]
                    lse_ref[br, pl.ds(o, run_len), :] = lse_blk[m * run_len:(m + 1) * run_len]

        def put_scores(scores):
            for u, sc in enumerate(scores):
                s_ref[u] = sc

        def put_probs(probs):
            for u, (pb, p_d, top) in enumerate(probs):
                p_ref[u] = pb
                st_ref[u, 0] = p_d
                st_ref[u, 1] = top

        put_probs(softmax_stage(blocks_of(0), masked_scores(blocks_of(0))))
        put_scores(masked_scores(blocks_of(min(1, n_iter - 1))))

        def body(it, carry, n_iter=n_iter, blocks_of=blocks_of, masked_scores=masked_scores,
                 softmax_stage=softmax_stage, output_stage=output_stage, put_scores=put_scores, put_probs=put_probs):
            scores = [s_ref[u] for u in range(DA_UNROLL)]
            probs = [(p_ref[u], st_ref[u, 0], st_ref[u, 1]) for u in range(DA_UNROLL)]
            output_stage(blocks_of(it), probs)
            next_probs = softmax_stage(blocks_of(jnp.minimum(it + 1, n_iter - 1)), scores)
            next_scores = masked_scores(blocks_of(jnp.minimum(it + 2, n_iter - 1)))
            put_scores(next_scores)
            put_probs(next_probs)
            return carry

        lax.fori_loop(0, n_iter, body, 0)

    for r in range(res):
        rows = slice(r * per, (r + 1) * per)
        lses = [lse_ref[br, rows, :] for br in range(len(DA_BRANCHES))]
        top = functools.reduce(jnp.maximum, lses)
        wts = [jnp.exp2(l - top) for l in lses]
        num = sum(w * ob_ref[br, rows, :] for br, w in enumerate(wts))
        o_ref[0, pl.ds(r, per, stride=res), :] = num / sum(wts)


def _dilated_attention(qkv3, batch, seq):
    pairs = DA_WIDTH // LANES
    n_br = len(DA_BRANCHES)
    spec = lambda off: pl.BlockSpec((1, seq, LANES), lambda b, p, off=off: (b, 0, off + p))
    return pl.pallas_call(
        functools.partial(_dilated_kernel, seq=seq),
        grid=(batch, pairs),
        in_specs=[spec(0), spec(pairs), spec(2 * pairs)],
        out_specs=pl.BlockSpec((1, seq, LANES), lambda b, p: (b, 0, p)),
        out_shape=jax.ShapeDtypeStruct((batch, seq, DA_WIDTH), f32),
        scratch_shapes=[pltpu.VMEM((seq, LANES), f32)] * 3
        + [pltpu.VMEM((n_br, seq, LANES), f32)] * 2
        + [pltpu.VMEM((DA_UNROLL, 2 * DA_BLOCK, DA_BLOCK), f32),
           pltpu.VMEM((DA_UNROLL, 2 * DA_BLOCK, DA_BLOCK), bf16),
           pltpu.VMEM((DA_UNROLL, 2, DA_BLOCK, LANES), f32)],
        compiler_params=_params(("parallel", "parallel")),
        name="dilated_attention",
    )(qkv3, qkv3, qkv3)


def _mix_out_kernel(oa_ref, ob_ref, h_ref, w_ref, g_ref, b_ref, o_ref):
    w = w_ref[...]
    y = jnp.dot(oa_ref[...], w[:DN_WIDTH], preferred_element_type=f32)
    y = y + jnp.dot(ob_ref[...].astype(bf16), w[DN_WIDTH:], preferred_element_type=f32)
    o_ref[...] = _layer_norm(DEEPNORM_ALPHA * h_ref[...] + y, g_ref[...], b_ref[...])


def _mix_out(o_a, o_b, h, w_out, g, b):
    t = h.shape[0]
    tm = TOKEN_TILE
    fixed = lambda i: (0, 0)
    return pl.pallas_call(
        _mix_out_kernel,
        grid=(t // tm,),
        in_specs=[
            pl.BlockSpec((tm, DN_WIDTH), lambda i: (i, 0)),
            pl.BlockSpec((tm, DA_WIDTH), lambda i: (i, 0)),
            pl.BlockSpec((tm, D_MODEL), lambda i: (i, 0)),
            pl.BlockSpec(w_out.shape, fixed),
            pl.BlockSpec((1, D_MODEL), fixed),
            pl.BlockSpec((1, D_MODEL), fixed),
        ],
        out_specs=pl.BlockSpec((tm, D_MODEL), lambda i: (i, 0)),
        out_shape=jax.ShapeDtypeStruct((t, D_MODEL), f32),
        compiler_params=_params(("parallel",)),
        name="mix_out",
    )(o_a, o_b, h, w_out, g, b)


def _shortconv_kernel(h_ref, win_ref, cw_ref, wout_ref, g_ref, b_ref, o_ref, tail_ref):
    @pl.when(pl.program_id(1) == 0)
    def _():
        tail_ref[...] = jnp.zeros_like(tail_ref)

    tm = h_ref.shape[0]
    d = D_MODEL
    h = h_ref[...]
    hb = h.astype(bf16)
    gate_b = jnp.dot(hb, win_ref[:, :d], preferred_element_type=f32)
    gate_c = jnp.dot(hb, win_ref[:, d:2 * d], preferred_element_type=f32)
    hid = jnp.dot(hb, win_ref[:, 2 * d:], preferred_element_type=f32)
    u = gate_c * hid
    uu = jnp.concatenate([tail_ref[...], u], axis=0)
    cw = cw_ref[...]
    conv = u * cw[SC_CONV - 1:SC_CONV, :]
    for j in range(SC_CONV - 1):
        back = SC_CONV - 1 - j
        conv = conv + uu[SUBLANES - back:SUBLANES - back + tm, :] * cw[j:j + 1, :]
    tail_ref[...] = u[tm - SUBLANES:, :]
    y = jnp.dot((gate_b * conv).astype(bf16), wout_ref[...], preferred_element_type=f32)
    o_ref[...] = _layer_norm(DEEPNORM_ALPHA * h + y, g_ref[...], b_ref[...])


def _shortconv(h, w_in, conv_w, w_out, g, b, batch, seq):
    tm = TOKEN_TILE
    ns = seq // tm
    row = lambda bb, s: (bb * ns + s, 0)
    fixed = lambda bb, s: (0, 0)
    return pl.pallas_call(
        _shortconv_kernel,
        grid=(batch, ns),
        in_specs=[
            pl.BlockSpec((tm, D_MODEL), row),
            pl.BlockSpec(w_in.shape, fixed),
            pl.BlockSpec(conv_w.shape, fixed),
            pl.BlockSpec(w_out.shape, fixed),
            pl.BlockSpec((1, D_MODEL), fixed),
            pl.BlockSpec((1, D_MODEL), fixed),
        ],
        out_specs=pl.BlockSpec((tm, D_MODEL), row),
        out_shape=jax.ShapeDtypeStruct((batch * seq, D_MODEL), f32),
        scratch_shapes=[pltpu.VMEM((SUBLANES, D_MODEL), f32)],
        compiler_params=_params(("arbitrary", "arbitrary")),
        name="shortconv",
    )(h, w_in, conv_w, w_out, g, b)


def _route(h2, rwt_ref, rb_ref, cnt_ref):
    tm = h2.shape[0]
    w_hi, w_lo = _split(rwt_ref[...], 2)
    x_hi, x_lo = _split(h2, 2)
    nt = (((1,), (1,)), ((), ()))
    logits = (lax.dot_general(w_hi, x_hi, nt, preferred_element_type=f32)
              + lax.dot_general(w_hi, x_lo, nt, preferred_element_type=f32)
              + lax.dot_general(w_lo, x_hi, nt, preferred_element_type=f32))
    mx = jnp.max(logits, axis=0, keepdims=True)
    ex = jnp.exp(logits - mx)
    scores = ex / jnp.sum(ex, axis=0, keepdims=True)
    sel = scores + rb_ref[...]
    best = jnp.full((1, tm), -jnp.inf, f32)
    bucket = jnp.zeros((1, tm), jnp.int32)
    s_a = jnp.zeros((1, tm), f32)
    s_b = jnp.zeros((1, tm), f32)
    for grp in range(N_GROUPS):
        for p, (i, j) in enumerate(_PAIRS):
            a, b = grp * EXPERTS_PER_GROUP + i, grp * EXPERTS_PER_GROUP + j
            ps = sel[a:a + 1, :] + sel[b:b + 1, :]
            upd = ps > best
            best = jnp.where(upd, ps, best)
            bucket = jnp.where(upd, grp * len(_PAIRS) + p, bucket)
            s_a = jnp.where(upd, scores[a:a + 1, :], s_a)
            s_b = jnp.where(upd, scores[b:b + 1, :], s_b)
    denom = s_a + s_b
    gate_rows = jnp.concatenate([s_a / denom, s_b / denom, jnp.zeros((GATE_LANES - 2, tm), f32)], axis=0)
    gates = jnp.transpose(gate_rows)
    onehot = jnp.where(lax.broadcasted_iota(jnp.int32, (BUCKET_ROWS, tm), 0) == bucket, 1.0, 0.0)
    before = lax.broadcasted_iota(jnp.int32, (tm, tm), 0) < lax.broadcasted_iota(jnp.int32, (tm, tm), 1)
    prefix = jnp.dot(onehot.astype(bf16), jnp.where(before, 1.0, 0.0).astype(bf16), preferred_element_type=f32)
    cnt = cnt_ref[...]
    rank = jnp.sum(onehot * (prefix + cnt[:, 0:1]), axis=0, keepdims=True)
    cnt_ref[...] = cnt + jnp.sum(onehot, axis=1, keepdims=True)
    return gates, bucket, rank.astype(jnp.int32)


def _xattn_kernel(h_ref, kv_ref, wq_ref, wo_ref, g_ref, b_ref, rwt_ref, rb_ref,
                  hext_ref, route_ref, cnt_out_ref, cnt_ref):
    @pl.when(jnp.logical_and(pl.program_id(0) == 0, pl.program_id(1) == 0))
    def _():
        cnt_ref[...] = jnp.zeros_like(cnt_ref)

    h = h_ref[...]
    tm = h.shape[0]
    q = jnp.dot(h.astype(bf16), wq_ref[...], preferred_element_type=f32).astype(bf16)
    outs = []
    for hd in range(XA_HEADS):
        lo, hi = hd * XA_HEAD_DIM, (hd + 1) * XA_HEAD_DIM
        s = _dot_nt(q[:, lo:hi], kv_ref[:, lo:hi]) * (XA_HEAD_DIM ** -0.5)
        p = jnp.exp(s - jnp.max(s, -1, keepdims=True))
        o = _dot(p, kv_ref[:, D_MODEL + lo:D_MODEL + hi]) / jnp.sum(p, -1, keepdims=True)
        outs.append(o.astype(bf16))
    y = jnp.dot(jnp.concatenate(outs, axis=1), wo_ref[...], preferred_element_type=f32)
    h2 = _layer_norm(DEEPNORM_ALPHA * h + y, g_ref[...], b_ref[...])
    gates, bucket, rank = _route(h2, rwt_ref, rb_ref, cnt_ref)
    hext_ref[:, :D_MODEL] = h2
    hext_ref[:, D_MODEL:] = gates
    route_ref[...] = jnp.concatenate([bucket, rank, jnp.zeros((SUBLANES - 2, tm), jnp.int32)], axis=0)
    cnt_out_ref[...] = cnt_ref[...]


def _xattn_route(h, kv, w_q, w_o, g, b, rwt, rb, batch, seq, n_mem):
    tm = TOKEN_TILE
    ns = seq // tm
    t = batch * seq
    row = lambda bb, s: (bb * ns + s, 0)
    fixed = lambda bb, s: (0, 0)
    return pl.pallas_call(
        _xattn_kernel,
        grid=(batch, ns),
        in_specs=[
            pl.BlockSpec((tm, D_MODEL), row),
            pl.BlockSpec((n_mem, 2 * D_MODEL), lambda bb, s: (bb, 0)),
            pl.BlockSpec(w_q.shape, fixed),
            pl.BlockSpec(w_o.shape, fixed),
            pl.BlockSpec((1, D_MODEL), fixed),
            pl.BlockSpec((1, D_MODEL), fixed),
            pl.BlockSpec(rwt.shape, fixed),
            pl.BlockSpec(rb.shape, fixed),
        ],
        out_specs=[
            pl.BlockSpec((tm, ROW_EXT), row),
            pl.BlockSpec((SUBLANES, tm), lambda bb, s: (0, bb * ns + s)),
            pl.BlockSpec((BUCKET_ROWS, LANES), fixed),
        ],
        out_shape=[
            jax.ShapeDtypeStruct((t, ROW_EXT), f32),
            jax.ShapeDtypeStruct((SUBLANES, t), jnp.int32),
            jax.ShapeDtypeStruct((BUCKET_ROWS, LANES), f32),
        ],
        scratch_shapes=[pltpu.VMEM((BUCKET_ROWS, LANES), f32)],
        compiler_params=_params(("arbitrary", "arbitrary")),
        name="xattn_route",
    )(h, kv, w_q, w_o, g, b, rwt, rb)


def _dispatch_kernel(dest_ref, fill_ref, hext_ref, xs_ref, inv_ref, zeros_ref, sem, zsem):
    base = pl.program_id(0) * DISPATCH_CHUNK
    n_tiles = xs_ref.shape[0] // MOE_TILE

    @pl.when(pl.program_id(0) == 0)
    def _():
        zeros_ref[...] = jnp.zeros_like(zeros_ref)

        def fill(tile):
            return pltpu.make_async_copy(zeros_ref, xs_ref.at[pl.ds(tile * MOE_TILE, MOE_TILE)], zsem)

        for tile in range(n_tiles):
            @pl.when(fill_ref[tile] > 0)
            def _(tile=tile):
                fill(tile).start()

        def clear(j8, carry):
            for u in range(8):
                inv_ref[j8 * 8 + u] = 0
            return carry

        lax.fori_loop(0, inv_ref.shape[0] // 8, clear, 0)
        for tile in range(n_tiles):
            @pl.when(fill_ref[tile] > 0)
            def _(tile=tile):
                fill(tile).wait()

    for j in range(DISPATCH_CHUNK):
        d = dest_ref[base + j]
        inv_ref[d] = base + j
        pltpu.make_async_copy(hext_ref.at[pl.ds(j, 1)], xs_ref.at[pl.ds(d, 1)], sem).start(priority=j % 2)
    pltpu.make_async_copy(hext_ref, xs_ref.at[pl.ds(0, DISPATCH_CHUNK)], sem).wait()


def _dispatch(dest, fill, hext, n_rows):
    t = hext.shape[0]
    return pl.pallas_call(
        _dispatch_kernel,
        grid_spec=pltpu.PrefetchScalarGridSpec(
            num_scalar_prefetch=2,
            grid=(t // DISPATCH_CHUNK,),
            in_specs=[pl.BlockSpec((DISPATCH_CHUNK, ROW_EXT), lambda i, *_: (i, 0))],
            out_specs=[pl.BlockSpec(memory_space=pl.ANY), pl.BlockSpec(memory_space=pltpu.SMEM)],
            scratch_shapes=[pltpu.VMEM((MOE_TILE, ROW_EXT), f32), pltpu.SemaphoreType.DMA(()),
                            pltpu.SemaphoreType.DMA(())],
        ),
        out_shape=[jax.ShapeDtypeStruct((n_rows, ROW_EXT), f32), jax.ShapeDtypeStruct((n_rows,), jnp.int32)],
        compiler_params=_params(("arbitrary",)),
        name="moe_dispatch",
    )(dest, fill, hext)


def _experts_kernel(ea_ref, eb_ref, nvalid_ref, xs_idx_ref, inv_ref, xs_ref, wga_ref, wua_ref, wda_ref,
                    wgb_ref, wub_ref, wdb_ref, g_ref, b_ref, out_ref, obuf_ref, sem_ref):
    i = pl.program_id(0)
    tm = MOE_TILE
    prev = jnp.maximum(i - 1, 0)
    n_prev = jnp.where(i > 0, nvalid_ref[prev], 0)
    slot_prev = prev % 2

    def send(j):
        return pltpu.make_async_copy(obuf_ref.at[slot_prev, pl.ds(j, 1)],
                                     out_ref.at[pl.ds(inv_ref[prev * tm + j], 1)], sem_ref.at[slot_prev])

    n_cur = jnp.where(i < pl.num_programs(0) - 1, nvalid_ref[jnp.minimum(i, pl.num_programs(0) - 2)], 0)

    @pl.when(n_prev == tm)
    def _():
        for j in range(tm):
            send(j).start(priority=j % 2)

    @pl.when(jnp.logical_and(n_prev > 0, n_prev < tm))
    def _():
        for j in range(tm):
            @pl.when(j < n_prev)
            def _(j=j):
                send(j).start(priority=j % 2)

    @pl.when(n_cur > 0)
    def _():
        x32 = xs_ref[:, :D_MODEL]
        x = x32.astype(bf16)
        gates = xs_ref[:, D_MODEL:]

        def hidden(wg_ref, wu_ref, gate):
            hg = jnp.dot(x, wg_ref[0, 0].astype(bf16), preferred_element_type=f32)
            hu = jnp.dot(x, wu_ref[0, 0].astype(bf16), preferred_element_type=f32)
            return (_silu(hg) * hu * gate).astype(bf16)

        h_a = hidden(wga_ref, wua_ref, gates[:, 0:1])
        h_b = hidden(wgb_ref, wub_ref, gates[:, 1:2])
        y = (jnp.dot(h_a, wda_ref[0, 0].astype(bf16), preferred_element_type=f32)
             + jnp.dot(h_b, wdb_ref[0, 0].astype(bf16), preferred_element_type=f32))
        obuf_ref[i % 2] = _layer_norm(DEEPNORM_ALPHA * x32 + y, g_ref[...], b_ref[...])

    for bit in reversed(range(tm.bit_length())):
        size = 1 << bit

        @pl.when((n_prev & size) != 0)
        def _(size=size):
            pltpu.make_async_copy(obuf_ref.at[slot_prev, pl.ds(0, size)], out_ref.at[pl.ds(0, size)],
                                  sem_ref.at[slot_prev]).wait()


def _experts(layer, ea, eb, nvalid, xs_idx, inv, xs, w_gate, w_up, w_down, g, b, t):
    n_tiles = xs.shape[0] // MOE_TILE
    up_a = pl.BlockSpec((1, 1, D_MODEL, D_EXPERT), lambda i, ea, *_: (layer, ea[i], 0, 0))
    up_b = pl.BlockSpec((1, 1, D_MODEL, D_EXPERT), lambda i, ea, eb, *_: (layer, eb[i], 0, 0))
    dn_a = pl.BlockSpec((1, 1, D_EXPERT, D_MODEL), lambda i, ea, *_: (layer, ea[i], 0, 0))
    dn_b = pl.BlockSpec((1, 1, D_EXPERT, D_MODEL), lambda i, ea, eb, *_: (layer, eb[i], 0, 0))
    fixed = lambda i, *_: (0, 0)
    return pl.pallas_call(
        _experts_kernel,
        grid_spec=pltpu.PrefetchScalarGridSpec(
            num_scalar_prefetch=5,
            grid=(n_tiles + 1,),
            in_specs=[pl.BlockSpec((MOE_TILE, ROW_EXT), lambda i, ea, eb, nv, xs_idx, inv: (xs_idx[i], 0)),
                      up_a, up_a, dn_a, up_b, up_b, dn_b,
                      pl.BlockSpec((1, D_MODEL), fixed), pl.BlockSpec((1, D_MODEL), fixed)],
            out_specs=pl.BlockSpec(memory_space=pl.ANY),
            scratch_shapes=[pltpu.VMEM((2, MOE_TILE, D_MODEL), f32), pltpu.SemaphoreType.DMA((2,))],
        ),
        out_shape=jax.ShapeDtypeStruct((t, D_MODEL), f32),
        compiler_params=_params(("arbitrary",)),
        name="moe_experts",
    )(ea, eb, nvalid, xs_idx, inv, xs, w_gate, w_up, w_down, w_gate, w_up, w_down, g, b)


def _moe_plan(route, counts):
    t = route.shape[1]
    n_rows = t + N_BUCKETS * MOE_TILE
    n_tiles = n_rows // MOE_TILE
    cnt = counts[:N_BUCKETS, 0].astype(jnp.int32)
    padded = (cnt + MOE_TILE - 1) // MOE_TILE * MOE_TILE
    ends = jnp.cumsum(padded)
    starts = ends - padded
    dest = starts[route[0]] + route[1]
    tile_start = jnp.arange(n_tiles + 1, dtype=jnp.int32) * MOE_TILE
    tile_bucket = jnp.sum(tile_start[:, None] >= ends[None, :], axis=1).astype(jnp.int32)
    used = tile_bucket < N_BUCKETS
    tile_bucket = jnp.minimum(tile_bucket, N_BUCKETS - 1)
    nvalid = jnp.where(used, jnp.clip(starts[tile_bucket] + cnt[tile_bucket] - tile_start, 0, MOE_TILE), 0)
    xs_idx = jnp.minimum(jnp.arange(n_tiles + 1, dtype=jnp.int32), jnp.maximum(ends[-1] // MOE_TILE - 1, 0))
    last_bucket = tile_bucket[jnp.maximum(ends[-1] // MOE_TILE - 1, 0)]
    tile_bucket = jnp.where(used, tile_bucket, last_bucket)
    pair_a = jnp.array([g * EXPERTS_PER_GROUP + i for g in range(N_GROUPS) for i, _ in _PAIRS], jnp.int32)
    pair_b = jnp.array([g * EXPERTS_PER_GROUP + j for g in range(N_GROUPS) for _, j in _PAIRS], jnp.int32)
    tiles = jnp.arange(n_tiles, dtype=jnp.int32)
    is_last = jnp.any(jnp.logical_and((tiles[:, None] + 1) * MOE_TILE == ends[None, :], padded[None, :] > 0), axis=1)
    fill = jnp.logical_or(is_last, tiles * MOE_TILE >= ends[-1]).astype(jnp.int32)
    return dict(dest=dest, ea=pair_a[tile_bucket], eb=pair_b[tile_bucket], nvalid=nvalid.astype(jnp.int32),
                xs_idx=xs_idx, fill=fill, n_rows=n_rows)


def _moe(layer, hext, route, counts, w_gate, w_up, w_down, g, b):
    plan = _moe_plan(route, counts)
    xs, inv = _dispatch(plan["dest"], plan["fill"], hext, plan["n_rows"])
    return _experts(layer, plan["ea"], plan["eb"], plan["nvalid"], plan["xs_idx"], inv, xs,
                    w_gate, w_up, w_down, g, b, hext.shape[0])


def kernel(x, mem, ab_w_in, ab_conv_w, ab_a_log, ab_dt_bias, ab_norm_w, ab_w_out, sc_w_in, sc_conv_w, sc_w_out, xa_w_q, xa_w_kv, xa_w_o, router_w, router_b, moe_w_gate, moe_w_up, moe_w_down, ln_g, ln_b):
    batch, seq, d = x.shape
    n_mem = mem.shape[1]
    t = batch * seq
    h = x.reshape(t, d)
    mem2 = mem.reshape(batch * n_mem, d)
    rwt = jnp.transpose(router_w)
    rb = router_b.reshape(N_EXPERTS, 1)
    row = lambda v: v.reshape(1, -1)

    for layer in range(DEPTH):
        i = layer // 2
        if layer % 2 == 0:
            w_in = ab_w_in[i]
            c0, c1, c2 = 3 * DN_WIDTH, 4 * DN_WIDTH, 4 * DN_WIDTH + 2 * DN_HEADS
            w_ba = jnp.pad(w_in[:, c1:c2], ((0, 0), (0, LANES - 2 * DN_HEADS)))
            ws = [w_in[:, :c0].astype(bf16), w_in[:, c0:c1].astype(bf16), w_ba.astype(bf16), w_in[:, c2:].astype(bf16)]
            qkv, z, ba, qkv_d = _project(h, ws, [f32, f32, f32, f32], TOKEN_TILE)
            lane_pad = lambda v: jnp.pad(v.reshape(1, DN_HEADS), ((0, 0), (DN_HEADS, LANES - 2 * DN_HEADS)))
            o_a = _deltanet(qkv, ba, z, ab_conv_w[i], lane_pad(ab_a_log[i]), lane_pad(ab_dt_bias[i]),
                            row(ab_norm_w[i]), batch, seq)
            o_b = _dilated_attention(qkv_d.reshape(batch, seq, 3 * DA_WIDTH), batch, seq).reshape(t, DA_WIDTH)
            h = _mix_out(o_a, o_b, h, ab_w_out[i].astype(bf16), row(ln_g[layer, 0]), row(ln_b[layer, 0]))
        else:
            h = _shortconv(h, sc_w_in[i].astype(bf16), sc_conv_w[i], sc_w_out[i].astype(bf16),
                           row(ln_g[layer, 0]), row(ln_b[layer, 0]), batch, seq)
        (kv,) = _project(mem2, [xa_w_kv[layer].astype(bf16)], [bf16], n_mem)
        hext, route, counts = _xattn_route(h, kv, xa_w_q[layer].astype(bf16), xa_w_o[layer].astype(bf16),
                                           row(ln_g[layer, 1]), row(ln_b[layer, 1]), rwt, rb, batch, seq, n_mem)
        h = _moe(layer, hext, route, counts, moe_w_gate, moe_w_up, moe_w_down,
                 row(ln_g[layer, 2]), row(ln_b[layer, 2]))
    return h.reshape(batch, seq, d)
```

```python
import functools

import jax
import jax.numpy as jnp
from jax import lax
from jax.experimental import pallas as pl
from jax.experimental.pallas import tpu as pltpu

D_MODEL = 1024
DEPTH = 2
DN_HEADS = 4
DN_HEAD_DIM = 128
DN_WIDTH = DN_HEADS * DN_HEAD_DIM
DN_CONV = 4
DN_CHUNK = 64
DA_HEADS = 8
DA_HEAD_DIM = 64
DA_WIDTH = DA_HEADS * DA_HEAD_DIM
DA_BRANCHES = ((128, 1), (512, 4), (2048, 16))
DA_BLOCK = 128
DA_RESIDUES = max(d for _, d in DA_BRANCHES)
SC_CONV = 3
XA_HEADS = 4
XA_HEAD_DIM = D_MODEL // XA_HEADS
N_EXPERTS = 16
N_GROUPS = 4
EXPERTS_PER_GROUP = N_EXPERTS // N_GROUPS
D_EXPERT = D_MODEL // 2
DEEPNORM_ALPHA = (2 * DEPTH) ** 0.25
LN_EPS = 1e-5
RMS_EPS = 1e-6

LANES = 128
SUBLANES = 8
VMEM_LIMIT_BYTES = 48 * 1024 * 1024

TOKEN_TILE = 512
DN_BLOCK = 256
MOE_TILE = 256
DISPATCH_CHUNK = 512
DA_UNROLL = 2
GATE_LANES = LANES
ROW_EXT = D_MODEL + GATE_LANES

_PAIRS = tuple((i, j) for i in range(EXPERTS_PER_GROUP) for j in range(i + 1, EXPERTS_PER_GROUP))
N_BUCKETS = N_GROUPS * len(_PAIRS)
BUCKET_ROWS = 32

NEG_BIG = -1e30
LOG2_E = 1.4426950408889634

bf16 = jnp.bfloat16
f32 = jnp.float32


def _params(semantics):
    return pltpu.CompilerParams(dimension_semantics=semantics, vmem_limit_bytes=VMEM_LIMIT_BYTES)


def _dot(a, b):
    return jnp.dot(a.astype(bf16), b.astype(bf16), preferred_element_type=f32)


def _dot_nt(a, b):
    return lax.dot_general(a.astype(bf16), b.astype(bf16), (((1,), (1,)), ((), ())), preferred_element_type=f32)


def _dot_tn(a, b):
    return lax.dot_general(a.astype(bf16), b.astype(bf16), (((0,), (0,)), ((), ())), preferred_element_type=f32)


def _split(a, parts):
    out = []
    rem = a
    for _ in range(parts):
        p = rem.astype(bf16)
        out.append(p)
        rem = rem - p.astype(f32)
    return out


def _layer_norm(v, g, b):
    mu = jnp.mean(v, -1, keepdims=True)
    c = v - mu
    var = jnp.mean(c * c, -1, keepdims=True)
    return c * lax.rsqrt(var + LN_EPS) * g + b


def _silu(v):
    return v * jax.nn.sigmoid(v)


def _softplus(v):
    return jnp.maximum(v, 0.0) + jnp.log(1.0 + jnp.exp(-jnp.abs(v)))


def _proj_kernel(x_ref, *refs, n_out):
    w_refs, o_refs = refs[:n_out], refs[n_out:]
    xb = x_ref[...].astype(bf16)
    for w_ref, o_ref in zip(w_refs, o_refs):
        n = w_ref.shape[1]
        for c0 in range(0, n, 512):
            c1 = min(c0 + 512, n)
            o_ref[:, c0:c1] = jnp.dot(xb, w_ref[:, c0:c1], preferred_element_type=f32).astype(o_ref.dtype)


def _project(x, ws, out_dtypes, tm):
    t, k = x.shape
    n_out = len(ws)
    return pl.pallas_call(
        functools.partial(_proj_kernel, n_out=n_out),
        grid=(t // tm,),
        in_specs=[pl.BlockSpec((tm, k), lambda i: (i, 0))]
        + [pl.BlockSpec(w.shape, lambda i: (0, 0)) for w in ws],
        out_specs=[pl.BlockSpec((tm, w.shape[1]), lambda i: (i, 0)) for w in ws],
        out_shape=[jax.ShapeDtypeStruct((t, w.shape[1]), dt) for w, dt in zip(ws, out_dtypes)],
        compiler_params=_params(("parallel",)),
        name="project",
    )(x, *ws)


def _deltanet_kernel(qkv_ref, ba_ref, z_ref, cw_ref, alog_ref, dtb_ref, nw_ref, o_ref, state_ref, xx_ref):
    @pl.when(pl.program_id(1) == 0)
    def _():
        state_ref[...] = jnp.zeros_like(state_ref)
        xx_ref[0:SUBLANES, :] = jnp.zeros((SUBLANES, xx_ref.shape[1]), f32)

    rows, ch, hd, width = DN_BLOCK, DN_CHUNK, DN_HEAD_DIM, DN_WIDTH
    sup = 2 * ch
    n_sup = rows // sup

    x = qkv_ref[...]
    xx_ref[SUBLANES:, :] = x
    cw = cw_ref[...]
    conv = x * cw[DN_CONV - 1:DN_CONV, :]
    for j in range(DN_CONV - 1):
        back = DN_CONV - 1 - j
        conv = conv + xx_ref[pl.ds(SUBLANES - back, rows), :] * cw[j:j + 1, :]
    xx_ref[0:SUBLANES, :] = x[rows - SUBLANES:, :]
    act = _silu(conv)

    ba = ba_ref[...]
    beta_all = jax.nn.sigmoid(ba)
    g_all = -jnp.exp(alog_ref[...]) * _softplus(ba + dtb_ref[...])

    ri = lax.broadcasted_iota(jnp.int32, (sup, sup), 0)
    ci = lax.broadcasted_iota(jnp.int32, (sup, sup), 1)
    same = (ri >= ch) == (ci >= ch)
    incl = jnp.logical_and(ri >= ci, same)
    strict = jnp.logical_and(ri > ci, same)
    lower_ones = jnp.where(incl, 1.0, 0.0).astype(bf16)
    eye = jnp.where(ri == ci, 1.0, 0.0)
    first_rows = lax.broadcasted_iota(jnp.int32, (sup, hd), 0) < ch
    z = z_ref[...]
    nw = nw_ref[...]

    chains = [(s, h) for s in range(n_sup) for h in range(DN_HEADS)]
    g_cum_all, g_cum_t = [], []
    for s in range(n_sup):
        g_blk = g_all[s * sup:(s + 1) * sup, :]
        gc = sum(jnp.dot(lower_ones, p, preferred_element_type=f32) for p in _split(g_blk, 2))
        g_cum_all.append(gc)
        g_cum_t.append(jnp.transpose(gc))

    st = {}
    for s, h in chains:
        r0 = s * sup
        q = act[r0:r0 + sup, h * hd:(h + 1) * hd]
        k = act[r0:r0 + sup, width + h * hd:width + (h + 1) * hd]
        v = act[r0:r0 + sup, 2 * width + h * hd:2 * width + (h + 1) * hd]
        q = q * (lax.rsqrt(jnp.sum(q * q, -1, keepdims=True) + RMS_EPS) * (hd ** -0.5))
        k = k * lax.rsqrt(jnp.sum(k * k, -1, keepdims=True) + RMS_EPS)
        beta_b = jnp.broadcast_to(beta_all[r0:r0 + sup, h:h + 1], (sup, hd))
        g_i = jnp.broadcast_to(g_cum_all[s][:, DN_HEADS + h:DN_HEADS + h + 1], (sup, hd))
        g_j = jnp.broadcast_to(g_cum_t[s][DN_HEADS + h:DN_HEADS + h + 1, :], (sup, sup))
        decay = jnp.where(incl, jnp.exp(jnp.minimum(g_i - g_j, 0.0)), 0.0)
        e_g = jnp.exp(g_i)
        g_last = jnp.where(first_rows, g_i[ch - 1:ch, :], g_i[sup - 1:sup, :])
        st[s, h] = dict(q=q, k=k, kb=k * beta_b, vb=v * beta_b, decay=decay, e_g=e_g,
                        k_tail=k * jnp.exp(g_last - g_i),
                        gl=(jnp.exp(g_i[ch - 1:ch, :]), jnp.exp(g_i[sup - 1:sup, :])))

    for c in chains:
        d = st[c]
        d["a"] = jnp.where(strict, _dot_nt(d["kb"], d["k"]) * d["decay"], 0.0)
        d["attn"] = jnp.where(incl, _dot_nt(d["q"], d["k"]) * d["decay"], 0.0)
    for c in chains:
        st[c]["t_inv"] = eye - st[c]["a"]
        st[c]["pw"] = st[c]["a"]
    for _ in range(5):
        for c in chains:
            st[c]["pw"] = _dot(st[c]["pw"], st[c]["pw"])
        for c in chains:
            st[c]["t_inv"] = st[c]["t_inv"] + _dot(st[c]["t_inv"], st[c]["pw"])
    for c in chains:
        d = st[c]
        sol = _dot(d["t_inv"], jnp.concatenate([d["vb"], d["kb"] * d["e_g"]], axis=1))
        d["u"], d["w"] = sol[:, :hd], sol[:, hd:]
    for c in chains:
        d = st[c]
        uw = jnp.concatenate([d["u"], d["w"]], axis=1).astype(bf16)
        mix = jnp.dot(d["attn"].astype(bf16), uw, preferred_element_type=f32)
        d["au"] = mix[:, :hd]
        d["qe"] = d["q"] * d["e_g"] - mix[:, hd:]
        kt = d["k_tail"].astype(bf16)
        zero = jnp.zeros_like(kt)
        d["nu_pw"] = [_dot_tn(jnp.where(first_rows == first, kt, zero), uw)
                      for first in (True, False)]

    for s in range(n_sup):
        for half in range(2):
            for h in range(DN_HEADS):
                d = st[s, h]
                state = state_ref[h]
                lo = half * ch
                o = _dot(d["qe"][lo:lo + ch], state) + d["au"][lo:lo + ch]
                nu_pw = d["nu_pw"][half]
                state_ref[h] = state * d["gl"][half] - _dot(nu_pw[:, hd:], state) + nu_pw[:, :hd]
                o = o * lax.rsqrt(jnp.mean(o * o, -1, keepdims=True) + RMS_EPS) * nw
                r0 = s * sup + lo
                o_ref[r0:r0 + ch, h * hd:(h + 1) * hd] = (o * _silu(z[r0:r0 + ch, h * hd:(h + 1) * hd])).astype(o_ref.dtype)


def _deltanet(qkv, ba, z, conv_w, alog_row, dtb_row, norm_w, batch, seq):
    nb = seq // DN_BLOCK
    row = lambda b, c: (b * nb + c, 0)
    fixed = lambda b, c: (0, 0)
    return pl.pallas_call(
        _deltanet_kernel,
        grid=(batch, nb),
        in_specs=[
            pl.BlockSpec((DN_BLOCK, 3 * DN_WIDTH), row),
            pl.BlockSpec((DN_BLOCK, LANES), row),
            pl.BlockSpec((DN_BLOCK, DN_WIDTH), row),
            pl.BlockSpec(conv_w.shape, fixed),
            pl.BlockSpec((1, LANES), fixed),
            pl.BlockSpec((1, LANES), fixed),
            pl.BlockSpec((1, DN_HEAD_DIM), fixed),
        ],
        out_specs=pl.BlockSpec((DN_BLOCK, DN_WIDTH), row),
        out_shape=jax.ShapeDtypeStruct((batch * seq, DN_WIDTH), bf16),
        scratch_shapes=[
            pltpu.VMEM((DN_HEADS, DN_HEAD_DIM, DN_HEAD_DIM), f32),
            pltpu.VMEM((SUBLANES + DN_BLOCK, 3 * DN_WIDTH), f32),
        ],
        compiler_params=_params(("arbitrary", "arbitrary")),
        name="deltanet",
    )(qkv, ba, z, conv_w, alog_row, dtb_row, norm_w)


def _dilated_kernel(q_ref, k_ref, v_ref, o_ref, qs_ref, ks_ref, vs_ref, ob_ref, lse_ref, s_ref, p_ref, st_ref, *, seq):
    blk, res = DA_BLOCK, DA_RESIDUES
    per = seq // res
    head0 = lax.broadcasted_iota(jnp.int32, (blk, LANES), 1) < DA_HEAD_DIM
    ai = lax.broadcasted_iota(jnp.int32, (2 * blk, blk), 0) & (blk - 1)
    bi = lax.broadcasted_iota(jnp.int32, (2 * blk, blk), 1)
    ones_blk = jnp.ones((blk, LANES), bf16)
    lane_head0 = lax.broadcasted_iota(jnp.int32, (LANES, 2 * LANES), 0) < DA_HEAD_DIM
    col_head0 = lax.broadcasted_iota(jnp.int32, (LANES, 2 * LANES), 1) < LANES
    head_sum = jnp.where(lane_head0 == col_head0, 1.0, 0.0).astype(bf16)

    q_scale = DA_HEAD_DIM ** -0.5 * LOG2_E
    for r in range(res):
        rows = pl.ds(r, per, stride=res)
        qs_ref[r * per:(r + 1) * per, :] = q_ref[0, rows, :] * q_scale
        ks_ref[r * per:(r + 1) * per, :] = k_ref[0, rows, :]
        vs_ref[r * per:(r + 1) * per, :] = v_ref[0, rows, :]

    for br, (window, dil) in enumerate(DA_BRANCHES):
        assert window // dil == blk and res % dil == 0
        runs = res // dil
        run_len = blk // runs
        nblk = seq // (dil * blk)
        shift = run_len.bit_length() - 1

        def pos(a, runs=runs, run_len=run_len, shift=shift):
            return runs * (a & (run_len - 1)) + (a >> shift)

        from_cur = pos(bi) <= pos(ai)

        n_iter = dil * nblk // DA_UNROLL
        assert n_iter * DA_UNROLL == dil * nblk

        def tile(ref, offs, run_len=run_len, runs=runs):
            parts = [ref[pl.ds(o, run_len), :] for o in offs]
            return parts[0] if runs == 1 else jnp.concatenate(parts, axis=0)

        def blocks_of(it, dil=dil, runs=runs, run_len=run_len, nblk=nblk):
            out = []
            for u in range(DA_UNROLL):
                idx = it * DA_UNROLL + u
                cls = idx // nblk
                n = idx - cls * nblk
                n_prev = jnp.maximum(n - 1, 0)
                cur = [pl.multiple_of((dil * m + cls) * per + run_len * n, SUBLANES) for m in range(runs)]
                prev = [pl.multiple_of((dil * m + cls) * per + run_len * n_prev, SUBLANES) for m in range(runs)]
                out.append((cur, prev, n > 0))
            return out

        def masked_scores(blocks, tile=tile, from_cur=from_cur):
            out = []
            for cur, prev, has_prev in blocks:
                q = tile(qs_ref, cur).astype(bf16)
                zero = jnp.zeros_like(q)
                q2 = jnp.concatenate([jnp.where(head0, q, zero), jnp.where(head0, zero, q)], axis=0)
                s_c = _dot_nt(q2, tile(ks_ref, cur))
                s_p = _dot_nt(q2, tile(ks_ref, prev))
                out.append(jnp.where(from_cur, s_c, jnp.where(has_prev, s_p, NEG_BIG)))
            return out

        def softmax_stage(blocks, scores, tile=tile):
            out = []
            for (cur, prev, has_prev), s in zip(blocks, scores):
                sd = jnp.dot((tile(qs_ref, cur) * tile(ks_ref, prev)).astype(bf16), head_sum, preferred_element_type=f32)
                row_max = jnp.max(s, -1, keepdims=True)
                m0 = jnp.maximum(row_max[:blk], sd[:, :LANES])
                m1 = jnp.maximum(row_max[blk:], sd[:, LANES:])
                p = jnp.exp2(s - jnp.concatenate([m0, m1], axis=0))
                top = jnp.where(head0, m0, m1)
                p_d = jnp.where(has_prev, jnp.exp2(jnp.where(head0, sd[:, :LANES], sd[:, LANES:]) - top), 0.0)
                out.append((p.astype(bf16), p_d, top))
            return out

        def output_stage(blocks, probs, br=br, run_len=run_len, from_cur=from_cur, tile=tile):
            for (cur, prev, has_prev), (pb, p_d, top) in zip(blocks, probs):
                zb = jnp.zeros_like(pb)
                acc = (jnp.dot(jnp.where(from_cur, pb, zb), jnp.concatenate([tile(vs_ref, cur).astype(bf16), ones_blk], axis=1),
                               preferred_element_type=f32)
                       + jnp.dot(jnp.where(from_cur, zb, pb), jnp.concatenate([tile(vs_ref, prev).astype(bf16), ones_blk], axis=1),
                                 preferred_element_type=f32))
                den = jnp.where(head0, acc[:blk, LANES:], acc[blk:, LANES:]) + p_d
                o_blk = (jnp.where(head0, acc[:blk, :LANES], acc[blk:, :LANES]) + p_d * tile(vs_ref, prev)) / den
                lse_blk = top + jnp.log2(den)
                for m, o in enumerate(cur):
                    ob_ref[br, pl.ds(o, run_len), :] = o_blk[m * run_len:(m + 1) * run_len]
                    lse_ref[br, pl.ds(o, run_len), :] = lse_blk[m * run_len:(m + 1) * run_len]

        def put_scores(scores):
            for u, sc in enumerate(scores):
                s_ref[u] = sc

        def put_probs(probs):
            for u, (pb, p_d, top) in enumerate(probs):
                p_ref[u] = pb
                st_ref[u, 0] = p_d
                st_ref[u, 1] = top

        put_probs(softmax_stage(blocks_of(0), masked_scores(blocks_of(0))))
        put_scores(masked_scores(blocks_of(min(1, n_iter - 1))))

        def body(it, carry, n_iter=n_iter, blocks_of=blocks_of, masked_scores=masked_scores,
                 softmax_stage=softmax_stage, output_stage=output_stage, put_scores=put_scores, put_probs=put_probs):
            scores = [s_ref[u] for u in range(DA_UNROLL)]
            probs = [(p_ref[u], st_ref[u, 0], st_ref[u, 1]) for u in range(DA_UNROLL)]
            output_stage(blocks_of(it), probs)
            next_probs = softmax_stage(blocks_of(jnp.minimum(it + 1, n_iter - 1)), scores)
            next_scores = masked_scores(blocks_of(jnp.minimum(it + 2, n_iter - 1)))
            put_scores(next_scores)
            put_probs(next_probs)
            return carry

        lax.fori_loop(0, n_iter, body, 0)

    for r in range(res):
        rows = slice(r * per, (r + 1) * per)
        lses = [lse_ref[br, rows, :] for br in range(len(DA_BRANCHES))]
        top = functools.reduce(jnp.maximum, lses)
        wts = [jnp.exp2(l - top) for l in lses]
        num = sum(w * ob_ref[br, rows, :] for br, w in enumerate(wts))
        o_ref[0, pl.ds(r, per, stride=res), :] = num / sum(wts)


def _dilated_attention(qkv3, batch, seq):
    pairs = DA_WIDTH // LANES
    n_br = len(DA_BRANCHES)
    spec = lambda off: pl.BlockSpec((1, seq, LANES), lambda b, p, off=off: (b, 0, off + p))
    return pl.pallas_call(
        functools.partial(_dilated_kernel, seq=seq),
        grid=(batch, pairs),
        in_specs=[spec(0), spec(pairs), spec(2 * pairs)],
        out_specs=pl.BlockSpec((1, seq, LANES), lambda b, p: (b, 0, p)),
        out_shape=jax.ShapeDtypeStruct((batch, seq, DA_WIDTH), f32),
        scratch_shapes=[pltpu.VMEM((seq, LANES), f32)] * 3
        + [pltpu.VMEM((n_br, seq, LANES), f32)] * 2
        + [pltpu.VMEM((DA_UNROLL, 2 * DA_BLOCK, DA_BLOCK), f32),
           pltpu.VMEM((DA_UNROLL, 2 * DA_BLOCK, DA_BLOCK), bf16),
           pltpu.VMEM((DA_UNROLL, 2, DA_BLOCK, LANES), f32)],
        compiler_params=_params(("parallel", "parallel")),
        name="dilated_attention",
    )(qkv3, qkv3, qkv3)


def _mix_out_kernel(oa_ref, ob_ref, h_ref, w_ref, g_ref, b_ref, o_ref):
    w = w_ref[...]
    y = jnp.dot(oa_ref[...], w[:DN_WIDTH], preferred_element_type=f32)
    y = y + jnp.dot(ob_ref[...].astype(bf16), w[DN_WIDTH:], preferred_element_type=f32)
    o_ref[...] = _layer_norm(DEEPNORM_ALPHA * h_ref[...] + y, g_ref[...], b_ref[...])


def _mix_out(o_a, o_b, h, w_out, g, b):
    t = h.shape[0]
    tm = TOKEN_TILE
    fixed = lambda i: (0, 0)
    return pl.pallas_call(
        _mix_out_kernel,
        grid=(t // tm,),
        in_specs=[
            pl.BlockSpec((tm, DN_WIDTH), lambda i: (i, 0)),
            pl.BlockSpec((tm, DA_WIDTH), lambda i: (i, 0)),
            pl.BlockSpec((tm, D_MODEL), lambda i: (i, 0)),
            pl.BlockSpec(w_out.shape, fixed),
            pl.BlockSpec((1, D_MODEL), fixed),
            pl.BlockSpec((1, D_MODEL), fixed),
        ],
        out_specs=pl.BlockSpec((tm, D_MODEL), lambda i: (i, 0)),
        out_shape=jax.ShapeDtypeStruct((t, D_MODEL), f32),
        compiler_params=_params(("parallel",)),
        name="mix_out",
    )(o_a, o_b, h, w_out, g, b)


def _shortconv_kernel(h_ref, win_ref, cw_ref, wout_ref, g_ref, b_ref, o_ref, tail_ref):
    @pl.when(pl.program_id(1) == 0)
    def _():
        tail_ref[...] = jnp.zeros_like(tail_ref)

    tm = h_ref.shape[0]
    d = D_MODEL
    h = h_ref[...]
    hb = h.astype(bf16)
    gate_b = jnp.dot(hb, win_ref[:, :d], preferred_element_type=f32)
    gate_c = jnp.dot(hb, win_ref[:, d:2 * d], preferred_element_type=f32)
    hid = jnp.dot(hb, win_ref[:, 2 * d:], preferred_element_type=f32)
    u = gate_c * hid
    uu = jnp.concatenate([tail_ref[...], u], axis=0)
    cw = cw_ref[...]
    conv = u * cw[SC_CONV - 1:SC_CONV, :]
    for j in range(SC_CONV - 1):
        back = SC_CONV - 1 - j
        conv = conv + uu[SUBLANES - back:SUBLANES - back + tm, :] * cw[j:j + 1, :]
    tail_ref[...] = u[tm - SUBLANES:, :]
    y = jnp.dot((gate_b * conv).astype(bf16), wout_ref[...], preferred_element_type=f32)
    o_ref[...] = _layer_norm(DEEPNORM_ALPHA * h + y, g_ref[...], b_ref[...])


def _shortconv(h, w_in, conv_w, w_out, g, b, batch, seq):
    tm = TOKEN_TILE
    ns = seq // tm
    row = lambda bb, s: (bb * ns + s, 0)
    fixed = lambda bb, s: (0, 0)
    return pl.pallas_call(
        _shortconv_kernel,
        grid=(batch, ns),
        in_specs=[
            pl.BlockSpec((tm, D_MODEL), row),
            pl.BlockSpec(w_in.shape, fixed),
            pl.BlockSpec(conv_w.shape, fixed),
            pl.BlockSpec(w_out.shape, fixed),
            pl.BlockSpec((1, D_MODEL), fixed),
            pl.BlockSpec((1, D_MODEL), fixed),
        ],
        out_specs=pl.BlockSpec((tm, D_MODEL), row),
        out_shape=jax.ShapeDtypeStruct((batch * seq, D_MODEL), f32),
        scratch_shapes=[pltpu.VMEM((SUBLANES, D_MODEL), f32)],
        compiler_params=_params(("arbitrary", "arbitrary")),
        name="shortconv",
    )(h, w_in, conv_w, w_out, g, b)


def _route(h2, rwt_ref, rb_ref, cnt_ref):
    tm = h2.shape[0]
    w_hi, w_lo = _split(rwt_ref[...], 2)
    x_hi, x_lo = _split(h2, 2)
    nt = (((1,), (1,)), ((), ()))
    logits = (lax.dot_general(w_hi, x_hi, nt, preferred_element_type=f32)
              + lax.dot_general(w_hi, x_lo, nt, preferred_element_type=f32)
              + lax.dot_general(w_lo, x_hi, nt, preferred_element_type=f32))
    mx = jnp.max(logits, axis=0, keepdims=True)
    ex = jnp.exp(logits - mx)
    scores = ex / jnp.sum(ex, axis=0, keepdims=True)
    sel = scores + rb_ref[...]
    best = jnp.full((1, tm), -jnp.inf, f32)
    bucket = jnp.zeros((1, tm), jnp.int32)
    s_a = jnp.zeros((1, tm), f32)
    s_b = jnp.zeros((1, tm), f32)
    for grp in range(N_GROUPS):
        for p, (i, j) in enumerate(_PAIRS):
            a, b = grp * EXPERTS_PER_GROUP + i, grp * EXPERTS_PER_GROUP + j
            ps = sel[a:a + 1, :] + sel[b:b + 1, :]
            upd = ps > best
            best = jnp.where(upd, ps, best)
            bucket = jnp.where(upd, grp * len(_PAIRS) + p, bucket)
            s_a = jnp.where(upd, scores[a:a + 1, :], s_a)
            s_b = jnp.where(upd, scores[b:b + 1, :], s_b)
    denom = s_a + s_b
    gate_rows = jnp.concatenate([s_a / denom, s_b / denom, jnp.zeros((GATE_LANES - 2, tm), f32)], axis=0)
    gates = jnp.transpose(gate_rows)
    onehot = jnp.where(lax.broadcasted_iota(jnp.int32, (BUCKET_ROWS, tm), 0) == bucket, 1.0, 0.0)
    before = lax.broadcasted_iota(jnp.int32, (tm, tm), 0) < lax.broadcasted_iota(jnp.int32, (tm, tm), 1)
    prefix = jnp.dot(onehot.astype(bf16), jnp.where(before, 1.0, 0.0).astype(bf16), preferred_element_type=f32)
    cnt = cnt_ref[...]
    rank = jnp.sum(onehot * (prefix + cnt[:, 0:1]), axis=0, keepdims=True)
    cnt_ref[...] = cnt + jnp.sum(onehot, axis=1, keepdims=True)
    return gates, bucket, rank.astype(jnp.int32)


def _xattn_kernel(h_ref, kv_ref, wq_ref, wo_ref, g_ref, b_ref, rwt_ref, rb_ref,
                  hext_ref, route_ref, cnt_out_ref, cnt_ref):
    @pl.when(jnp.logical_and(pl.program_id(0) == 0, pl.program_id(1) == 0))
    def _():
        cnt_ref[...] = jnp.zeros_like(cnt_ref)

    h = h_ref[...]
    tm = h.shape[0]
    q = jnp.dot(h.astype(bf16), wq_ref[...], preferred_element_type=f32).astype(bf16)
    outs = []
    for hd in range(XA_HEADS):
        lo, hi = hd * XA_HEAD_DIM, (hd + 1) * XA_HEAD_DIM
        s = _dot_nt(q[:, lo:hi], kv_ref[:, lo:hi]) * (XA_HEAD_DIM ** -0.5)
        p = jnp.exp(s - jnp.max(s, -1, keepdims=True))
        o = _dot(p, kv_ref[:, D_MODEL + lo:D_MODEL + hi]) / jnp.sum(p, -1, keepdims=True)
        outs.append(o.astype(bf16))
    y = jnp.dot(jnp.concatenate(outs, axis=1), wo_ref[...], preferred_element_type=f32)
    h2 = _layer_norm(DEEPNORM_ALPHA * h + y, g_ref[...], b_ref[...])
    gates, bucket, rank = _route(h2, rwt_ref, rb_ref, cnt_ref)
    hext_ref[:, :D_MODEL] = h2
    hext_ref[:, D_MODEL:] = gates
    route_ref[...] = jnp.concatenate([bucket, rank, jnp.zeros((SUBLANES - 2, tm), jnp.int32)], axis=0)
    cnt_out_ref[...] = cnt_ref[...]


def _xattn_route(h, kv, w_q, w_o, g, b, rwt, rb, batch, seq, n_mem):
    tm = TOKEN_TILE
    ns = seq // tm
    t = batch * seq
    row = lambda bb, s: (bb * ns + s, 0)
    fixed = lambda bb, s: (0, 0)
    return pl.pallas_call(
        _xattn_kernel,
        grid=(batch, ns),
        in_specs=[
            pl.BlockSpec((tm, D_MODEL), row),
            pl.BlockSpec((n_mem, 2 * D_MODEL), lambda bb, s: (bb, 0)),
            pl.BlockSpec(w_q.shape, fixed),
            pl.BlockSpec(w_o.shape, fixed),
            pl.BlockSpec((1, D_MODEL), fixed),
            pl.BlockSpec((1, D_MODEL), fixed),
            pl.BlockSpec(rwt.shape, fixed),
            pl.BlockSpec(rb.shape, fixed),
        ],
        out_specs=[
            pl.BlockSpec((tm, ROW_EXT), row),
            pl.BlockSpec((SUBLANES, tm), lambda bb, s: (0, bb * ns + s)),
            pl.BlockSpec((BUCKET_ROWS, LANES), fixed),
        ],
        out_shape=[
            jax.ShapeDtypeStruct((t, ROW_EXT), f32),
            jax.ShapeDtypeStruct((SUBLANES, t), jnp.int32),
            jax.ShapeDtypeStruct((BUCKET_ROWS, LANES), f32),
        ],
        scratch_shapes=[pltpu.VMEM((BUCKET_ROWS, LANES), f32)],
        compiler_params=_params(("arbitrary", "arbitrary")),
        name="xattn_route",
    )(h, kv, w_q, w_o, g, b, rwt, rb)


def _dispatch_kernel(dest_ref, fill_ref, hext_ref, xs_ref, inv_ref, zeros_ref, sem, zsem):
    base = pl.program_id(0) * DISPATCH_CHUNK
    n_tiles = xs_ref.shape[0] // MOE_TILE

    @pl.when(pl.program_id(0) == 0)
    def _():
        zeros_ref[...] = jnp.zeros_like(zeros_ref)

        def fill(tile):
            return pltpu.make_async_copy(zeros_ref, xs_ref.at[pl.ds(tile * MOE_TILE, MOE_TILE)], zsem)

        for tile in range(n_tiles):
            @pl.when(fill_ref[tile] > 0)
            def _(tile=tile):
                fill(tile).start()

        def clear(j8, carry):
            for u in range(8):
                inv_ref[j8 * 8 + u] = 0
            return carry

        lax.fori_loop(0, inv_ref.shape[0] // 8, clear, 0)
        for tile in range(n_tiles):
            @pl.when(fill_ref[tile] > 0)
            def _(tile=tile):
                fill(tile).wait()

    for j in range(DISPATCH_CHUNK):
        d = dest_ref[base + j]
        inv_ref[d] = base + j
        pltpu.make_async_copy(hext_ref.at[pl.ds(j, 1)], xs_ref.at[pl.ds(d, 1)], sem).start(priority=j % 2)
    pltpu.make_async_copy(hext_ref, xs_ref.at[pl.ds(0, DISPATCH_CHUNK)], sem).wait()


def _dispatch(dest, fill, hext, n_rows):
    t = hext.shape[0]
    return pl.pallas_call(
        _dispatch_kernel,
        grid_spec=pltpu.PrefetchScalarGridSpec(
            num_scalar_prefetch=2,
            grid=(t // DISPATCH_CHUNK,),
            in_specs=[pl.BlockSpec((DISPATCH_CHUNK, ROW_EXT), lambda i, *_: (i, 0))],
            out_specs=[pl.BlockSpec(memory_space=pl.ANY), pl.BlockSpec(memory_space=pltpu.SMEM)],
            scratch_shapes=[pltpu.VMEM((MOE_TILE, ROW_EXT), f32), pltpu.SemaphoreType.DMA(()),
                            pltpu.SemaphoreType.DMA(())],
        ),
        out_shape=[jax.ShapeDtypeStruct((n_rows, ROW_EXT), f32), jax.ShapeDtypeStruct((n_rows,), jnp.int32)],
        compiler_params=_params(("arbitrary",)),
        name="moe_dispatch",
    )(dest, fill, hext)


def _experts_kernel(ea_ref, eb_ref, nvalid_ref, xs_idx_ref, inv_ref, xs_ref, wga_ref, wua_ref, wda_ref,
                    wgb_ref, wub_ref, wdb_ref, g_ref, b_ref, out_ref, obuf_ref, sem_ref):
    i = pl.program_id(0)
    tm = MOE_TILE
    prev = jnp.maximum(i - 1, 0)
    n_prev = jnp.where(i > 0, nvalid_ref[prev], 0)
    slot_prev = prev % 2

    def send(j):
        return pltpu.make_async_copy(obuf_ref.at[slot_prev, pl.ds(j, 1)],
                                     out_ref.at[pl.ds(inv_ref[prev * tm + j], 1)], sem_ref.at[slot_prev])

    n_cur = jnp.where(i < pl.num_programs(0) - 1, nvalid_ref[jnp.minimum(i, pl.num_programs(0) - 2)], 0)

    def send_full(lo, hi):
        for j in range(lo, hi):
            send(j).start(priority=j % 2)

    def compute(between):
        x32 = xs_ref[:, :D_MODEL]
        x = x32.astype(bf16)
        gates = xs_ref[:, D_MODEL:]

        def hidden(wg_ref, wu_ref, gate):
            hg = jnp.dot(x, wg_ref[0, 0].astype(bf16), preferred_element_type=f32)
            hu = jnp.dot(x, wu_ref[0, 0].astype(bf16), preferred_element_type=f32)
            return (_silu(hg) * hu * gate).astype(bf16)

        between(0)
        h_a = hidden(wga_ref, wua_ref, gates[:, 0:1])
        between(1)
        h_b = hidden(wgb_ref, wub_ref, gates[:, 1:2])
        between(2)
        y = (jnp.dot(h_a, wda_ref[0, 0].astype(bf16), preferred_element_type=f32)
             + jnp.dot(h_b, wdb_ref[0, 0].astype(bf16), preferred_element_type=f32))
        between(3)
        obuf_ref[i % 2] = _layer_norm(DEEPNORM_ALPHA * x32 + y, g_ref[...], b_ref[...])

    usual = jnp.logical_and(n_prev == tm, n_cur > 0)
    quarter = tm // 4

    @pl.when(usual)
    def _():
        compute(lambda k: send_full(k * quarter, (k + 1) * quarter))

    @pl.when(jnp.logical_not(usual))
    def _():
        @pl.when(n_prev == tm)
        def _():
            send_full(0, tm)

        @pl.when(jnp.logical_and(n_prev > 0, n_prev < tm))
        def _():
            for j in range(tm):
                @pl.when(j < n_prev)
                def _(j=j):
                    send(j).start(priority=j % 2)

        @pl.when(n_cur > 0)
        def _():
            compute(lambda k: None)

    for bit in reversed(range(tm.bit_length())):
        size = 1 << bit

        @pl.when((n_prev & size) != 0)
        def _(size=size):
            pltpu.make_async_copy(obuf_ref.at[slot_prev, pl.ds(0, size)], out_ref.at[pl.ds(0, size)],
                                  sem_ref.at[slot_prev]).wait()


def _experts(layer, ea, eb, nvalid, xs_idx, inv, xs, w_gate, w_up, w_down, g, b, t):
    n_tiles = xs.shape[0] // MOE_TILE
    up_a = pl.BlockSpec((1, 1, D_MODEL, D_EXPERT), lambda i, ea, *_: (layer, ea[i], 0, 0))
    up_b = pl.BlockSpec((1, 1, D_MODEL, D_EXPERT), lambda i, ea, eb, *_: (layer, eb[i], 0, 0))
    dn_a = pl.BlockSpec((1, 1, D_EXPERT, D_MODEL), lambda i, ea, *_: (layer, ea[i], 0, 0))
    dn_b = pl.BlockSpec((1, 1, D_EXPERT, D_MODEL), lambda i, ea, eb, *_: (layer, eb[i], 0, 0))
    fixed = lambda i, *_: (0, 0)
    return pl.pallas_call(
        _experts_kernel,
        grid_spec=pltpu.PrefetchScalarGridSpec(
            num_scalar_prefetch=5,
            grid=(n_tiles + 1,),
            in_specs=[pl.BlockSpec((MOE_TILE, ROW_EXT), lambda i, ea, eb, nv, xs_idx, inv: (xs_idx[i], 0)),
                      up_a, up_a, dn_a, up_b, up_b, dn_b,
                      pl.BlockSpec((1, D_MODEL), fixed), pl.BlockSpec((1, D_MODEL), fixed)],
            out_specs=pl.BlockSpec(memory_space=pl.ANY),
            scratch_shapes=[pltpu.VMEM((2, MOE_TILE, D_MODEL), f32), pltpu.SemaphoreType.DMA((2,))],
        ),
        out_shape=jax.ShapeDtypeStruct((t, D_MODEL), f32),
        compiler_params=_params(("arbitrary",)),
        name="moe_experts",
    )(ea, eb, nvalid, xs_idx, inv, xs, w_gate, w_up, w_down, w_gate, w_up, w_down, g, b)


def _moe_plan(route, counts):
    t = route.shape[1]
    n_rows = t + N_BUCKETS * MOE_TILE
    n_tiles = n_rows // MOE_TILE
    cnt = counts[:N_BUCKETS, 0].astype(jnp.int32)
    padded = (cnt + MOE_TILE - 1) // MOE_TILE * MOE_TILE
    ends = jnp.cumsum(padded)
    starts = ends - padded
    dest = starts[route[0]] + route[1]
    tile_start = jnp.arange(n_tiles + 1, dtype=jnp.int32) * MOE_TILE
    tile_bucket = jnp.sum(tile_start[:, None] >= ends[None, :], axis=1).astype(jnp.int32)
    used = tile_bucket < N_BUCKETS
    tile_bucket = jnp.minimum(tile_bucket, N_BUCKETS - 1)
    nvalid = jnp.where(used, jnp.clip(starts[tile_bucket] + cnt[tile_bucket] - tile_start, 0, MOE_TILE), 0)
    xs_idx = jnp.minimum(jnp.arange(n_tiles + 1, dtype=jnp.int32), jnp.maximum(ends[-1] // MOE_TILE - 1, 0))
    last_bucket = tile_bucket[jnp.maximum(ends[-1] // MOE_TILE - 1, 0)]
    tile_bucket = jnp.where(used, tile_bucket, last_bucket)
    pair_a = jnp.array([g * EXPERTS_PER_GROUP + i for g in range(N_GROUPS) for i, _ in _PAIRS], jnp.int32)
    pair_b = jnp.array([g * EXPERTS_PER_GROUP + j for g in range(N_GROUPS) for _, j in _PAIRS], jnp.int32)
    tiles = jnp.arange(n_tiles, dtype=jnp.int32)
    is_last = jnp.any(jnp.logical_and((tiles[:, None] + 1) * MOE_TILE == ends[None, :], padded[None, :] > 0), axis=1)
    fill = jnp.logical_or(is_last, tiles * MOE_TILE >= ends[-1]).astype(jnp.int32)
    return dict(dest=dest, ea=pair_a[tile_bucket], eb=pair_b[tile_bucket], nvalid=nvalid.astype(jnp.int32),
                xs_idx=xs_idx, fill=fill, n_rows=n_rows)


def _moe(layer, hext, route, counts, w_gate, w_up, w_down, g, b):
    plan = _moe_plan(route, counts)
    xs, inv = _dispatch(plan["dest"], plan["fill"], hext, plan["n_rows"])
    return _experts(layer, plan["ea"], plan["eb"], plan["nvalid"], plan["xs_idx"], inv, xs,
                    w_gate, w_up, w_down, g, b, hext.shape[0])


def kernel(x, mem, ab_w_in, ab_conv_w, ab_a_log, ab_dt_bias, ab_norm_w, ab_w_out, sc_w_in, sc_conv_w, sc_w_out, xa_w_q, xa_w_kv, xa_w_o, router_w, router_b, moe_w_gate, moe_w_up, moe_w_down, ln_g, ln_b):
    batch, seq, d = x.shape
    n_mem = mem.shape[1]
    t = batch * seq
    h = x.reshape(t, d)
    mem2 = mem.reshape(batch * n_mem, d)
    rwt = jnp.transpose(router_w)
    rb = router_b.reshape(N_EXPERTS, 1)
    row = lambda v: v.reshape(1, -1)

    for layer in range(DEPTH):
        i = layer // 2
        if layer % 2 == 0:
            w_in = ab_w_in[i]
            c0, c1, c2 = 3 * DN_WIDTH, 4 * DN_WIDTH, 4 * DN_WIDTH + 2 * DN_HEADS
            w_ba = jnp.pad(w_in[:, c1:c2], ((0, 0), (0, LANES - 2 * DN_HEADS)))
            ws = [w_in[:, :c0].astype(bf16), w_in[:, c0:c1].astype(bf16), w_ba.astype(bf16), w_in[:, c2:].astype(bf16)]
            qkv, z, ba, qkv_d = _project(h, ws, [f32, f32, f32, f32], TOKEN_TILE)
            lane_pad = lambda v: jnp.pad(v.reshape(1, DN_HEADS), ((0, 0), (DN_HEADS, LANES - 2 * DN_HEADS)))
            o_a = _deltanet(qkv, ba, z, ab_conv_w[i], lane_pad(ab_a_log[i]), lane_pad(ab_dt_bias[i]),
                            row(ab_norm_w[i]), batch, seq)
            o_b = _dilated_attention(qkv_d.reshape(batch, seq, 3 * DA_WIDTH), batch, seq).reshape(t, DA_WIDTH)
            h = _mix_out(o_a, o_b, h, ab_w_out[i].astype(bf16), row(ln_g[layer, 0]), row(ln_b[layer, 0]))
        else:
            h = _shortconv(h, sc_w_in[i].astype(bf16), sc_conv_w[i], sc_w_out[i].astype(bf16),
                           row(ln_g[layer, 0]), row(ln_b[layer, 0]), batch, seq)
        (kv,) = _project(mem2, [xa_w_kv[layer].astype(bf16)], [bf16], n_mem)
        hext, route, counts = _xattn_route(h, kv, xa_w_q[layer].astype(bf16), xa_w_o[layer].astype(bf16),
                                           row(ln_g[layer, 1]), row(ln_b[layer, 1]), rwt, rb, batch, seq, n_mem)
        h = _moe(layer, hext, route, counts, moe_w_gate, moe_w_up, moe_w_down,
                 row(ln_g[layer, 2]), row(ln_b[layer, 2]))
    return h.reshape(batch, seq, d)
```

```python
import functools

import jax
import jax.numpy as jnp
from jax import lax
from jax.experimental import pallas as pl
from jax.experimental.pallas import tpu as pltpu

D_MODEL = 1024
DEPTH = 2
DN_HEADS = 4
DN_HEAD_DIM = 128
DN_WIDTH = DN_HEADS * DN_HEAD_DIM
DN_CONV = 4
DN_CHUNK = 64
DA_HEADS = 8
DA_HEAD_DIM = 64
DA_WIDTH = DA_HEADS * DA_HEAD_DIM
DA_BRANCHES = ((128, 1), (512, 4), (2048, 16))
DA_BLOCK = 128
DA_RESIDUES = max(d for _, d in DA_BRANCHES)
SC_CONV = 3
XA_HEADS = 4
XA_HEAD_DIM = D_MODEL // XA_HEADS
N_EXPERTS = 16
N_GROUPS = 4
EXPERTS_PER_GROUP = N_EXPERTS // N_GROUPS
D_EXPERT = D_MODEL // 2
DEEPNORM_ALPHA = (2 * DEPTH) ** 0.25
LN_EPS = 1e-5
RMS_EPS = 1e-6

LANES = 128
SUBLANES = 8
VMEM_LIMIT_BYTES = 48 * 1024 * 1024

TOKEN_TILE = 512
XA_TILE = 1024
DN_BLOCK = 256
MOE_TILE = 256
DISPATCH_CHUNK = 512
DA_UNROLL = 2
GATE_LANES = LANES
ROW_EXT = D_MODEL + GATE_LANES

_PAIRS = tuple((i, j) for i in range(EXPERTS_PER_GROUP) for j in range(i + 1, EXPERTS_PER_GROUP))
N_BUCKETS = N_GROUPS * len(_PAIRS)
BUCKET_ROWS = 32

NEG_BIG = -1e30
LOG2_E = 1.4426950408889634

bf16 = jnp.bfloat16
f32 = jnp.float32


def _params(semantics):
    return pltpu.CompilerParams(dimension_semantics=semantics, vmem_limit_bytes=VMEM_LIMIT_BYTES)


def _dot(a, b):
    return jnp.dot(a.astype(bf16), b.astype(bf16), preferred_element_type=f32)


def _dot_nt(a, b):
    return lax.dot_general(a.astype(bf16), b.astype(bf16), (((1,), (1,)), ((), ())), preferred_element_type=f32)


def _dot_tn(a, b):
    return lax.dot_general(a.astype(bf16), b.astype(bf16), (((0,), (0,)), ((), ())), preferred_element_type=f32)


def _split(a, parts):
    out = []
    rem = a
    for _ in range(parts):
        p = rem.astype(bf16)
        out.append(p)
        rem = rem - p.astype(f32)
    return out


def _layer_norm(v, g, b):
    mu = jnp.mean(v, -1, keepdims=True)
    c = v - mu
    var = jnp.mean(c * c, -1, keepdims=True)
    return c * lax.rsqrt(var + LN_EPS) * g + b


def _silu(v):
    return v * jax.nn.sigmoid(v)


def _softplus(v):
    return jnp.maximum(v, 0.0) + jnp.log(1.0 + jnp.exp(-jnp.abs(v)))


def _proj_kernel(x_ref, *refs, n_out):
    w_refs, o_refs = refs[:n_out], refs[n_out:]
    xb = x_ref[...].astype(bf16)
    for w_ref, o_ref in zip(w_refs, o_refs):
        n = w_ref.shape[1]
        for c0 in range(0, n, 512):
            c1 = min(c0 + 512, n)
            o_ref[:, c0:c1] = jnp.dot(xb, w_ref[:, c0:c1], preferred_element_type=f32).astype(o_ref.dtype)


def _project(x, ws, out_dtypes, tm):
    t, k = x.shape
    n_out = len(ws)
    return pl.pallas_call(
        functools.partial(_proj_kernel, n_out=n_out),
        grid=(t // tm,),
        in_specs=[pl.BlockSpec((tm, k), lambda i: (i, 0))]
        + [pl.BlockSpec(w.shape, lambda i: (0, 0)) for w in ws],
        out_specs=[pl.BlockSpec((tm, w.shape[1]), lambda i: (i, 0)) for w in ws],
        out_shape=[jax.ShapeDtypeStruct((t, w.shape[1]), dt) for w, dt in zip(ws, out_dtypes)],
        compiler_params=_params(("parallel",)),
        name="project",
    )(x, *ws)


def _proj_ab_kernel(x_ref, wqkv_ref, wz_ref, wba_ref, wd_ref, cw_ref, alog_ref, dtb_ref,
                    act_ref, z_ref, bg_ref, qkvd_ref, xx_ref, *, tiles_per_seq):
    @pl.when(pl.program_id(0) % tiles_per_seq == 0)
    def _():
        xx_ref[0:SUBLANES, :] = jnp.zeros((SUBLANES, xx_ref.shape[1]), f32)

    tm, hd = x_ref.shape[0], DN_HEAD_DIM
    xb = x_ref[...].astype(bf16)
    for c0 in range(0, 3 * DN_WIDTH, 512):
        xx_ref[SUBLANES:, c0:c0 + 512] = jnp.dot(xb, wqkv_ref[:, c0:c0 + 512], preferred_element_type=f32)
    z_ref[...] = jnp.dot(xb, wz_ref[...], preferred_element_type=f32)
    for c0 in range(0, 3 * DA_WIDTH, 512):
        qkvd_ref[:, c0:c0 + 512] = jnp.dot(xb, wd_ref[:, c0:c0 + 512], preferred_element_type=f32)
    ba = jnp.dot(xb, wba_ref[...], preferred_element_type=f32)
    is_beta = lax.broadcasted_iota(jnp.int32, ba.shape, 1) < DN_HEADS
    bg_ref[...] = jnp.where(is_beta, jax.nn.sigmoid(ba), -jnp.exp(alog_ref[...]) * _softplus(ba + dtb_ref[...]))

    cw = cw_ref[...]
    for c in range(3 * DN_HEADS):
        cols = slice(c * hd, (c + 1) * hd)
        conv = xx_ref[SUBLANES:, cols] * cw[DN_CONV - 1:DN_CONV, cols]
        for j in range(DN_CONV - 1):
            back = DN_CONV - 1 - j
            conv = conv + xx_ref[pl.ds(SUBLANES - back, tm), cols] * cw[j:j + 1, cols]
        a = _silu(conv)
        if c < 2 * DN_HEADS:
            scale = hd ** -0.5 if c < DN_HEADS else 1.0
            a = a * (lax.rsqrt(jnp.sum(a * a, -1, keepdims=True) + RMS_EPS) * scale)
        act_ref[:, cols] = a
    xx_ref[0:SUBLANES, :] = xx_ref[tm:tm + SUBLANES, :]


def _project_ab(x, w_qkv, w_z, w_ba, w_d, conv_w, alog_row, dtb_row, seq):
    t, k = x.shape
    tm = TOKEN_TILE
    fixed = lambda i: (0, 0)
    row = lambda i: (i, 0)
    widths = (3 * DN_WIDTH, DN_WIDTH, LANES, 3 * DA_WIDTH)
    return pl.pallas_call(
        functools.partial(_proj_ab_kernel, tiles_per_seq=seq // tm),
        grid=(t // tm,),
        in_specs=[pl.BlockSpec((tm, k), row)]
        + [pl.BlockSpec(w.shape, fixed) for w in (w_qkv, w_z, w_ba, w_d, conv_w, alog_row, dtb_row)],
        out_specs=[pl.BlockSpec((tm, n), row) for n in widths],
        out_shape=[jax.ShapeDtypeStruct((t, n), f32) for n in widths],
        scratch_shapes=[pltpu.VMEM((SUBLANES + tm, 3 * DN_WIDTH), f32)],
        compiler_params=_params(("arbitrary",)),
        name="project_ab",
    )(x, w_qkv, w_z, w_ba, w_d, conv_w, alog_row, dtb_row)


def _deltanet_kernel(act_ref, bg_ref, z_ref, nw_ref, o_ref, state_ref):
    @pl.when(pl.program_id(1) == 0)
    def _():
        state_ref[...] = jnp.zeros_like(state_ref)

    rows, ch, hd, width = DN_BLOCK, DN_CHUNK, DN_HEAD_DIM, DN_WIDTH
    sup = 2 * ch
    n_sup = rows // sup

    act = act_ref[...]
    beta_all = bg_ref[...]
    g_all = beta_all

    ri = lax.broadcasted_iota(jnp.int32, (sup, sup), 0)
    ci = lax.broadcasted_iota(jnp.int32, (sup, sup), 1)
    same = (ri >= ch) == (ci >= ch)
    incl = jnp.logical_and(ri >= ci, same)
    strict = jnp.logical_and(ri > ci, same)
    lower_ones = jnp.where(incl, 1.0, 0.0).astype(bf16)
    eye = jnp.where(ri == ci, 1.0, 0.0)
    first_rows = lax.broadcasted_iota(jnp.int32, (sup, hd), 0) < ch
    z = z_ref[...]
    nw = nw_ref[...]

    chains = [(s, h) for s in range(n_sup) for h in range(DN_HEADS)]
    g_cum_all, g_cum_t = [], []
    for s in range(n_sup):
        g_blk = g_all[s * sup:(s + 1) * sup, :]
        gc = sum(jnp.dot(lower_ones, p, preferred_element_type=f32) for p in _split(g_blk, 2))
        g_cum_all.append(gc)
        g_cum_t.append(jnp.transpose(gc))

    st = {}
    for s, h in chains:
        r0 = s * sup
        q = act[r0:r0 + sup, h * hd:(h + 1) * hd]
        k = act[r0:r0 + sup, width + h * hd:width + (h + 1) * hd]
        v = act[r0:r0 + sup, 2 * width + h * hd:2 * width + (h + 1) * hd]
        beta_b = jnp.broadcast_to(beta_all[r0:r0 + sup, h:h + 1], (sup, hd))
        g_i = jnp.broadcast_to(g_cum_all[s][:, DN_HEADS + h:DN_HEADS + h + 1], (sup, hd))
        g_j = jnp.broadcast_to(g_cum_t[s][DN_HEADS + h:DN_HEADS + h + 1, :], (sup, sup))
        decay = jnp.where(incl, jnp.exp(jnp.minimum(g_i - g_j, 0.0)), 0.0)
        e_g = jnp.exp(g_i)
        g_last = jnp.where(first_rows, g_i[ch - 1:ch, :], g_i[sup - 1:sup, :])
        st[s, h] = dict(q=q, k=k, kb=k * beta_b, vb=v * beta_b, decay=decay, e_g=e_g,
                        k_tail=k * jnp.exp(g_last - g_i),
                        gl=(jnp.exp(g_i[ch - 1:ch, :]), jnp.exp(g_i[sup - 1:sup, :])))

    for c in chains:
        d = st[c]
        d["a"] = jnp.where(strict, _dot_nt(d["kb"], d["k"]) * d["decay"], 0.0)
        d["attn"] = jnp.where(incl, _dot_nt(d["q"], d["k"]) * d["decay"], 0.0)
    for c in chains:
        st[c]["t_inv"] = eye - st[c]["a"]
        st[c]["pw"] = st[c]["a"]
    for _ in range(5):
        for c in chains:
            st[c]["pw"] = _dot(st[c]["pw"], st[c]["pw"])
        for c in chains:
            st[c]["t_inv"] = st[c]["t_inv"] + _dot(st[c]["t_inv"], st[c]["pw"])
    for c in chains:
        d = st[c]
        sol = _dot(d["t_inv"], jnp.concatenate([d["vb"], d["kb"] * d["e_g"]], axis=1))
        d["u"], d["w"] = sol[:, :hd], sol[:, hd:]
    for c in chains:
        d = st[c]
        uw = jnp.concatenate([d["u"], d["w"]], axis=1).astype(bf16)
        mix = jnp.dot(d["attn"].astype(bf16), uw, preferred_element_type=f32)
        d["au"] = mix[:, :hd]
        d["qe"] = d["q"] * d["e_g"] - mix[:, hd:]
        kt = d["k_tail"].astype(bf16)
        zero = jnp.zeros_like(kt)
        d["nu_pw"] = [_dot_tn(jnp.where(first_rows == first, kt, zero), uw)
                      for first in (True, False)]

    for s in range(n_sup):
        for half in range(2):
            for h in range(DN_HEADS):
                d = st[s, h]
                state = state_ref[h]
                lo = half * ch
                o = _dot(d["qe"][lo:lo + ch], state) + d["au"][lo:lo + ch]
                nu_pw = d["nu_pw"][half]
                state_ref[h] = state * d["gl"][half] - _dot(nu_pw[:, hd:], state) + nu_pw[:, :hd]
                o = o * lax.rsqrt(jnp.mean(o * o, -1, keepdims=True) + RMS_EPS) * nw
                r0 = s * sup + lo
                o_ref[r0:r0 + ch, h * hd:(h + 1) * hd] = (o * _silu(z[r0:r0 + ch, h * hd:(h + 1) * hd])).astype(o_ref.dtype)


def _deltanet(act, bg, z, norm_w, batch, seq):
    nb = seq // DN_BLOCK
    row = lambda b, c: (b * nb + c, 0)
    fixed = lambda b, c: (0, 0)
    return pl.pallas_call(
        _deltanet_kernel,
        grid=(batch, nb),
        in_specs=[
            pl.BlockSpec((DN_BLOCK, 3 * DN_WIDTH), row),
            pl.BlockSpec((DN_BLOCK, LANES), row),
            pl.BlockSpec((DN_BLOCK, DN_WIDTH), row),
            pl.BlockSpec((1, DN_HEAD_DIM), fixed),
        ],
        out_specs=pl.BlockSpec((DN_BLOCK, DN_WIDTH), row),
        out_shape=jax.ShapeDtypeStruct((batch * seq, DN_WIDTH), bf16),
        scratch_shapes=[pltpu.VMEM((DN_HEADS, DN_HEAD_DIM, DN_HEAD_DIM), f32)],
        compiler_params=_params(("arbitrary", "arbitrary")),
        name="deltanet",
    )(act, bg, z, norm_w)


def _dilated_kernel(q_ref, k_ref, v_ref, o_ref, qs_ref, ks_ref, vs_ref, ob_ref, lse_ref, s_ref, p_ref, st_ref, *, seq):
    blk, res = DA_BLOCK, DA_RESIDUES
    per = seq // res
    head0 = lax.broadcasted_iota(jnp.int32, (blk, LANES), 1) < DA_HEAD_DIM
    ai = lax.broadcasted_iota(jnp.int32, (2 * blk, blk), 0) & (blk - 1)
    bi = lax.broadcasted_iota(jnp.int32, (2 * blk, blk), 1)
    ones_blk = jnp.ones((blk, LANES), bf16)
    lane_head0 = lax.broadcasted_iota(jnp.int32, (LANES, 2 * LANES), 0) < DA_HEAD_DIM
    col_head0 = lax.broadcasted_iota(jnp.int32, (LANES, 2 * LANES), 1) < LANES
    head_sum = jnp.where(lane_head0 == col_head0, 1.0, 0.0).astype(bf16)

    q_scale = DA_HEAD_DIM ** -0.5 * LOG2_E
    for r in range(res):
        rows = pl.ds(r, per, stride=res)
        qs_ref[r * per:(r + 1) * per, :] = q_ref[0, rows, :] * q_scale
        ks_ref[r * per:(r + 1) * per, :] = k_ref[0, rows, :]
        vs_ref[r * per:(r + 1) * per, :] = v_ref[0, rows, :]

    for br, (window, dil) in enumerate(DA_BRANCHES):
        assert window // dil == blk and res % dil == 0
        runs = res // dil
        run_len = blk // runs
        nblk = seq // (dil * blk)
        shift = run_len.bit_length() - 1

        def pos(a, runs=runs, run_len=run_len, shift=shift):
            return runs * (a & (run_len - 1)) + (a >> shift)

        from_cur = pos(bi) <= pos(ai)

        n_iter = dil * nblk // DA_UNROLL
        assert n_iter * DA_UNROLL == dil * nblk

        def tile(ref, offs, run_len=run_len, runs=runs):
            parts = [ref[pl.ds(o, run_len), :] for o in offs]
            return parts[0] if runs == 1 else jnp.concatenate(parts, axis=0)

        def blocks_of(it, dil=dil, runs=runs, run_len=run_len, nblk=nblk):
            out = []
            for u in range(DA_UNROLL):
                idx = it * DA_UNROLL + u
                cls = idx // nblk
                n = idx - cls * nblk
                n_prev = jnp.maximum(n - 1, 0)
                cur = [pl.multiple_of((dil * m + cls) * per + run_len * n, SUBLANES) for m in range(runs)]
                prev = [pl.multiple_of((dil * m + cls) * per + run_len * n_prev, SUBLANES) for m in range(runs)]
                out.append((cur, prev, n > 0))
            return out

        def masked_scores(blocks, tile=tile, from_cur=from_cur):
            out = []
            for cur, prev, has_prev in blocks:
                q = tile(qs_ref, cur).astype(bf16)
                zero = jnp.zeros_like(q)
                q2 = jnp.concatenate([jnp.where(head0, q, zero), jnp.where(head0, zero, q)], axis=0)
                s_c = _dot_nt(q2, tile(ks_ref, cur))
                s_p = _dot_nt(q2, tile(ks_ref, prev))
                out.append(jnp.where(from_cur, s_c, jnp.where(has_prev, s_p, NEG_BIG)))
            return out

        def softmax_stage(blocks, scores, tile=tile):
            out = []
            for (cur, prev, has_prev), s in zip(blocks, scores):
                qk = tile(qs_ref, cur) * tile(ks_ref, prev)
                sd0 = jnp.sum(jnp.where(head0, qk, 0.0), -1, keepdims=True)
                sd1 = jnp.sum(jnp.where(head0, 0.0, qk), -1, keepdims=True)
                row_max = jnp.max(s, -1, keepdims=True)
                m0 = jnp.maximum(row_max[:blk], sd0)
                m1 = jnp.maximum(row_max[blk:], sd1)
                p = jnp.exp2(s - jnp.concatenate([m0, m1], axis=0))
                top = jnp.where(head0, m0, m1)
                p_d = jnp.where(has_prev, jnp.exp2(jnp.where(head0, sd0, sd1) - top), 0.0)
                out.append((p.astype(bf16), p_d, top))
            return out

        def output_stage(blocks, probs, br=br, run_len=run_len, from_cur=from_cur, tile=tile):
            for (cur, prev, has_prev), (pb, p_d, top) in zip(blocks, probs):
                zb = jnp.zeros_like(pb)
                acc = (jnp.dot(jnp.where(from_cur, pb, zb), jnp.concatenate([tile(vs_ref, cur).astype(bf16), ones_blk], axis=1),
                               preferred_element_type=f32)
                       + jnp.dot(jnp.where(from_cur, zb, pb), jnp.concatenate([tile(vs_ref, prev).astype(bf16), ones_blk], axis=1),
                                 preferred_element_type=f32))
                den = jnp.where(head0, acc[:blk, LANES:], acc[blk:, LANES:]) + p_d
                o_blk = (jnp.where(head0, acc[:blk, :LANES], acc[blk:, :LANES]) + p_d * tile(vs_ref, prev)) / den
                lse_blk = top + jnp.log2(den)
                for m, o in enumerate(cur):
                    ob_ref[br, pl.ds(o, run_len), :] = o_blk[m * run_len:(m + 1) * run_len]
                    lse_ref[br, pl.ds(o, run_len), :] = lse_blk[m * run_len:(m + 1) * run_len]

        def put_scores(scores):
            for u, sc in enumerate(scores):
                s_ref[u] = sc

        def put_probs(probs):
            for u, (pb, p_d, top) in enumerate(probs):
                p_ref[u] = pb
                st_ref[u, 0] = p_d
                st_ref[u, 1] = top

        put_probs(softmax_stage(blocks_of(0), masked_scores(blocks_of(0))))
        put_scores(masked_scores(blocks_of(min(1, n_iter - 1))))

        def body(it, carry, n_iter=n_iter, blocks_of=blocks_of, masked_scores=masked_scores,
                 softmax_stage=softmax_stage, output_stage=output_stage, put_scores=put_scores, put_probs=put_probs):
            scores = [s_ref[u] for u in range(DA_UNROLL)]
            probs = [(p_ref[u], st_ref[u, 0], st_ref[u, 1]) for u in range(DA_UNROLL)]
            output_stage(blocks_of(it), probs)
            next_probs = softmax_stage(blocks_of(jnp.minimum(it + 1, n_iter - 1)), scores)
            next_scores = masked_scores(blocks_of(jnp.minimum(it + 2, n_iter - 1)))
            put_scores(next_scores)
            put_probs(next_probs)
            return carry

        lax.fori_loop(0, n_iter, body, 0)

    for r in range(res):
        rows = slice(r * per, (r + 1) * per)
        lses = [lse_ref[br, rows, :] for br in range(len(DA_BRANCHES))]
        top = functools.reduce(jnp.maximum, lses)
        wts = [jnp.exp2(l - top) for l in lses]
        num = sum(w * ob_ref[br, rows, :] for br, w in enumerate(wts))
        o_ref[0, pl.ds(r, per, stride=res), :] = num / sum(wts)


def _dilated_attention(qkv3, batch, seq):
    pairs = DA_WIDTH // LANES
    n_br = len(DA_BRANCHES)
    spec = lambda off: pl.BlockSpec((1, seq, LANES), lambda b, p, off=off: (b, 0, off + p))
    return pl.pallas_call(
        functools.partial(_dilated_kernel, seq=seq),
        grid=(batch, pairs),
        in_specs=[spec(0), spec(pairs), spec(2 * pairs)],
        out_specs=pl.BlockSpec((1, seq, LANES), lambda b, p: (b, 0, p)),
        out_shape=jax.ShapeDtypeStruct((batch, seq, DA_WIDTH), f32),
        scratch_shapes=[pltpu.VMEM((seq, LANES), f32)] * 3
        + [pltpu.VMEM((n_br, seq, LANES), f32)] * 2
        + [pltpu.VMEM((DA_UNROLL, 2 * DA_BLOCK, DA_BLOCK), f32),
           pltpu.VMEM((DA_UNROLL, 2 * DA_BLOCK, DA_BLOCK), bf16),
           pltpu.VMEM((DA_UNROLL, 2, DA_BLOCK, LANES), f32)],
        compiler_params=_params(("parallel", "parallel")),
        name="dilated_attention",
    )(qkv3, qkv3, qkv3)


def _mix_out_kernel(oa_ref, ob_ref, h_ref, w_ref, g_ref, b_ref, o_ref):
    w = w_ref[...]
    y = jnp.dot(oa_ref[...], w[:DN_WIDTH], preferred_element_type=f32)
    y = y + jnp.dot(ob_ref[...].astype(bf16), w[DN_WIDTH:], preferred_element_type=f32)
    o_ref[...] = _layer_norm(DEEPNORM_ALPHA * h_ref[...] + y, g_ref[...], b_ref[...])


def _mix_out(o_a, o_b, h, w_out, g, b):
    t = h.shape[0]
    tm = TOKEN_TILE
    fixed = lambda i: (0, 0)
    return pl.pallas_call(
        _mix_out_kernel,
        grid=(t // tm,),
        in_specs=[
            pl.BlockSpec((tm, DN_WIDTH), lambda i: (i, 0)),
            pl.BlockSpec((tm, DA_WIDTH), lambda i: (i, 0)),
            pl.BlockSpec((tm, D_MODEL), lambda i: (i, 0)),
            pl.BlockSpec(w_out.shape, fixed),
            pl.BlockSpec((1, D_MODEL), fixed),
            pl.BlockSpec((1, D_MODEL), fixed),
        ],
        out_specs=pl.BlockSpec((tm, D_MODEL), lambda i: (i, 0)),
        out_shape=jax.ShapeDtypeStruct((t, D_MODEL), f32),
        compiler_params=_params(("parallel",)),
        name="mix_out",
    )(o_a, o_b, h, w_out, g, b)


def _shortconv_kernel(h_ref, win_ref, cw_ref, wout_ref, g_ref, b_ref, o_ref, tail_ref):
    @pl.when(pl.program_id(1) == 0)
    def _():
        tail_ref[...] = jnp.zeros_like(tail_ref)

    tm = h_ref.shape[0]
    d = D_MODEL
    h = h_ref[...]
    hb = h.astype(bf16)
    gate_b = jnp.dot(hb, win_ref[:, :d], preferred_element_type=f32)
    gate_c = jnp.dot(hb, win_ref[:, d:2 * d], preferred_element_type=f32)
    hid = jnp.dot(hb, win_ref[:, 2 * d:], preferred_element_type=f32)
    u = gate_c * hid
    uu = jnp.concatenate([tail_ref[...], u], axis=0)
    cw = cw_ref[...]
    conv = u * cw[SC_CONV - 1:SC_CONV, :]
    for j in range(SC_CONV - 1):
        back = SC_CONV - 1 - j
        conv = conv + uu[SUBLANES - back:SUBLANES - back + tm, :] * cw[j:j + 1, :]
    tail_ref[...] = u[tm - SUBLANES:, :]
    y = jnp.dot((gate_b * conv).astype(bf16), wout_ref[...], preferred_element_type=f32)
    o_ref[...] = _layer_norm(DEEPNORM_ALPHA * h + y, g_ref[...], b_ref[...])


def _shortconv(h, w_in, conv_w, w_out, g, b, batch, seq):
    tm = TOKEN_TILE
    ns = seq // tm
    row = lambda bb, s: (bb * ns + s, 0)
    fixed = lambda bb, s: (0, 0)
    return pl.pallas_call(
        _shortconv_kernel,
        grid=(batch, ns),
        in_specs=[
            pl.BlockSpec((tm, D_MODEL), row),
            pl.BlockSpec(w_in.shape, fixed),
            pl.BlockSpec(conv_w.shape, fixed),
            pl.BlockSpec(w_out.shape, fixed),
            pl.BlockSpec((1, D_MODEL), fixed),
            pl.BlockSpec((1, D_MODEL), fixed),
        ],
        out_specs=pl.BlockSpec((tm, D_MODEL), row),
        out_shape=jax.ShapeDtypeStruct((batch * seq, D_MODEL), f32),
        scratch_shapes=[pltpu.VMEM((SUBLANES, D_MODEL), f32)],
        compiler_params=_params(("arbitrary", "arbitrary")),
        name="shortconv",
    )(h, w_in, conv_w, w_out, g, b)


def _route(h2, rwt_ref, rb_ref, cnt_ref):
    tm = h2.shape[0]
    w_hi, w_lo = _split(rwt_ref[...], 2)
    x_hi, x_lo = _split(h2, 2)
    nt = (((1,), (1,)), ((), ()))
    logits = (lax.dot_general(w_hi, x_hi, nt, preferred_element_type=f32)
              + lax.dot_general(w_hi, x_lo, nt, preferred_element_type=f32)
              + lax.dot_general(w_lo, x_hi, nt, preferred_element_type=f32))
    mx = jnp.max(logits, axis=0, keepdims=True)
    ex = jnp.exp(logits - mx)
    scores = ex / jnp.sum(ex, axis=0, keepdims=True)
    sel = scores + rb_ref[...]
    best = jnp.full((1, tm), -jnp.inf, f32)
    bucket = jnp.zeros((1, tm), jnp.int32)
    s_a = jnp.zeros((1, tm), f32)
    s_b = jnp.zeros((1, tm), f32)
    for grp in range(N_GROUPS):
        for p, (i, j) in enumerate(_PAIRS):
            a, b = grp * EXPERTS_PER_GROUP + i, grp * EXPERTS_PER_GROUP + j
            ps = sel[a:a + 1, :] + sel[b:b + 1, :]
            upd = ps > best
            best = jnp.where(upd, ps, best)
            bucket = jnp.where(upd, grp * len(_PAIRS) + p, bucket)
            s_a = jnp.where(upd, scores[a:a + 1, :], s_a)
            s_b = jnp.where(upd, scores[b:b + 1, :], s_b)
    denom = s_a + s_b
    gate_rows = jnp.concatenate([s_a / denom, s_b / denom, jnp.zeros((GATE_LANES - 2, tm), f32)], axis=0)
    gates = jnp.transpose(gate_rows)
    onehot = jnp.where(lax.broadcasted_iota(jnp.int32, (BUCKET_ROWS, tm), 0) == bucket, 1.0, 0.0)
    before = lax.broadcasted_iota(jnp.int32, (tm, tm), 0) < lax.broadcasted_iota(jnp.int32, (tm, tm), 1)
    prefix = jnp.dot(onehot.astype(bf16), jnp.where(before, 1.0, 0.0).astype(bf16), preferred_element_type=f32)
    cnt = cnt_ref[...]
    rank = jnp.sum(onehot * (prefix + cnt[:, 0:1]), axis=0, keepdims=True)
    cnt_ref[...] = cnt + jnp.sum(onehot, axis=1, keepdims=True)
    return gates, bucket, rank.astype(jnp.int32)


def _xattn_kernel(h_ref, kv_ref, wq_ref, wo_ref, g_ref, b_ref, rwt_ref, rb_ref,
                  hext_ref, route_ref, cnt_out_ref, cnt_ref):
    @pl.when(jnp.logical_and(pl.program_id(0) == 0, pl.program_id(1) == 0))
    def _():
        cnt_ref[...] = jnp.zeros_like(cnt_ref)

    h = h_ref[...]
    tm = h.shape[0]
    q = jnp.dot(h.astype(bf16), wq_ref[...], preferred_element_type=f32).astype(bf16)
    heads = [(hd * XA_HEAD_DIM, (hd + 1) * XA_HEAD_DIM) for hd in range(XA_HEADS)]
    scores = [_dot_nt(q[:, lo:hi], kv_ref[:, lo:hi]) for lo, hi in heads]
    probs = [jnp.exp2(s - jnp.max(s, -1, keepdims=True)) for s in scores]
    outs = [(_dot(p, kv_ref[:, D_MODEL + lo:D_MODEL + hi]) / jnp.sum(p, -1, keepdims=True)).astype(bf16)
            for p, (lo, hi) in zip(probs, heads)]
    y = jnp.dot(jnp.concatenate(outs, axis=1), wo_ref[...], preferred_element_type=f32)
    h2 = _layer_norm(DEEPNORM_ALPHA * h + y, g_ref[...], b_ref[...])
    gates, bucket, rank = _route(h2, rwt_ref, rb_ref, cnt_ref)
    hext_ref[:, :D_MODEL] = h2
    hext_ref[:, D_MODEL:] = gates
    route_ref[...] = jnp.concatenate([bucket, rank, jnp.zeros((SUBLANES - 2, tm), jnp.int32)], axis=0)
    cnt_out_ref[...] = cnt_ref[...]


def _xattn_route(h, kv, w_q, w_o, g, b, rwt, rb, batch, seq, n_mem):
    tm = XA_TILE
    ns = seq // tm
    t = batch * seq
    row = lambda bb, s: (bb * ns + s, 0)
    fixed = lambda bb, s: (0, 0)
    return pl.pallas_call(
        _xattn_kernel,
        grid=(batch, ns),
        in_specs=[
            pl.BlockSpec((tm, D_MODEL), row),
            pl.BlockSpec((n_mem, 2 * D_MODEL), lambda bb, s: (bb, 0)),
            pl.BlockSpec(w_q.shape, fixed),
            pl.BlockSpec(w_o.shape, fixed),
            pl.BlockSpec((1, D_MODEL), fixed),
            pl.BlockSpec((1, D_MODEL), fixed),
            pl.BlockSpec(rwt.shape, fixed),
            pl.BlockSpec(rb.shape, fixed),
        ],
        out_specs=[
            pl.BlockSpec((tm, ROW_EXT), row),
            pl.BlockSpec((SUBLANES, tm), lambda bb, s: (0, bb * ns + s)),
            pl.BlockSpec((BUCKET_ROWS, LANES), fixed),
        ],
        out_shape=[
            jax.ShapeDtypeStruct((t, ROW_EXT), f32),
            jax.ShapeDtypeStruct((SUBLANES, t), jnp.int32),
            jax.ShapeDtypeStruct((BUCKET_ROWS, LANES), f32),
        ],
        scratch_shapes=[pltpu.VMEM((BUCKET_ROWS, LANES), f32)],
        compiler_params=_params(("arbitrary", "arbitrary")),
        name="xattn_route",
    )(h, kv, w_q, w_o, g, b, rwt, rb)


def _dispatch_kernel(dest_ref, fill_ref, hext_ref, xs_ref, inv_ref, zeros_ref, sem, zsem):
    base = pl.program_id(0) * DISPATCH_CHUNK
    n_tiles = xs_ref.shape[0] // MOE_TILE

    @pl.when(pl.program_id(0) == 0)
    def _():
        zeros_ref[...] = jnp.zeros_like(zeros_ref)

        def fill(tile):
            return pltpu.make_async_copy(zeros_ref, xs_ref.at[pl.ds(tile * MOE_TILE, MOE_TILE)], zsem)

        for tile in range(n_tiles):
            @pl.when(fill_ref[tile] > 0)
            def _(tile=tile):
                fill(tile).start()

        def clear(j8, carry):
            for u in range(8):
                inv_ref[j8 * 8 + u] = 0
            return carry

        lax.fori_loop(0, inv_ref.shape[0] // 8, clear, 0)
        for tile in range(n_tiles):
            @pl.when(fill_ref[tile] > 0)
            def _(tile=tile):
                fill(tile).wait()

    for j in range(DISPATCH_CHUNK):
        d = dest_ref[base + j]
        inv_ref[d] = base + j
        pltpu.make_async_copy(hext_ref.at[pl.ds(j, 1)], xs_ref.at[pl.ds(d, 1)], sem).start(priority=j % 2)
    pltpu.make_async_copy(hext_ref, xs_ref.at[pl.ds(0, DISPATCH_CHUNK)], sem).wait()


def _dispatch(dest, fill, hext, n_rows):
    t = hext.shape[0]
    return pl.pallas_call(
        _dispatch_kernel,
        grid_spec=pltpu.PrefetchScalarGridSpec(
            num_scalar_prefetch=2,
            grid=(t // DISPATCH_CHUNK,),
            in_specs=[pl.BlockSpec((DISPATCH_CHUNK, ROW_EXT), lambda i, *_: (i, 0))],
            out_specs=[pl.BlockSpec(memory_space=pl.ANY), pl.BlockSpec(memory_space=pltpu.SMEM)],
            scratch_shapes=[pltpu.VMEM((MOE_TILE, ROW_EXT), f32), pltpu.SemaphoreType.DMA(()),
                            pltpu.SemaphoreType.DMA(())],
        ),
        out_shape=[jax.ShapeDtypeStruct((n_rows, ROW_EXT), f32), jax.ShapeDtypeStruct((n_rows,), jnp.int32)],
        compiler_params=_params(("arbitrary",)),
        name="moe_dispatch",
    )(dest, fill, hext)


def _experts_kernel(ea_ref, eb_ref, nvalid_ref, xs_idx_ref, inv_ref, xs_ref, wga_ref, wua_ref, wda_ref,
                    wgb_ref, wub_ref, wdb_ref, g_ref, b_ref, out_ref, obuf_ref, sem_ref):
    i = pl.program_id(0)
    tm = MOE_TILE
    prev = jnp.maximum(i - 1, 0)
    n_prev = jnp.where(i > 0, nvalid_ref[prev], 0)
    slot_prev = prev % 2

    def send(j):
        return pltpu.make_async_copy(obuf_ref.at[slot_prev, pl.ds(j, 1)],
                                     out_ref.at[pl.ds(inv_ref[prev * tm + j], 1)], sem_ref.at[slot_prev])

    n_cur = jnp.where(i < pl.num_programs(0) - 1, nvalid_ref[jnp.minimum(i, pl.num_programs(0) - 2)], 0)

    def send_full(lo, hi):
        for j in range(lo, hi):
            send(j).start(priority=j % 2)

    def compute(between):
        x32 = xs_ref[:, :D_MODEL]
        x = x32.astype(bf16)
        gates = xs_ref[:, D_MODEL:]

        def hidden(wg_ref, wu_ref, gate):
            hg = jnp.dot(x, wg_ref[0, 0].astype(bf16), preferred_element_type=f32)
            hu = jnp.dot(x, wu_ref[0, 0].astype(bf16), preferred_element_type=f32)
            return (_silu(hg) * hu * gate).astype(bf16)

        between(0)
        h_a = hidden(wga_ref, wua_ref, gates[:, 0:1])
        between(1)
        h_b = hidden(wgb_ref, wub_ref, gates[:, 1:2])
        between(2)
        y = (jnp.dot(h_a, wda_ref[0, 0].astype(bf16), preferred_element_type=f32)
             + jnp.dot(h_b, wdb_ref[0, 0].astype(bf16), preferred_element_type=f32))
        between(3)
        obuf_ref[i % 2] = _layer_norm(DEEPNORM_ALPHA * x32 + y, g_ref[...], b_ref[...])

    usual = jnp.logical_and(n_prev == tm, n_cur > 0)
    quarter = tm // 4

    @pl.when(usual)
    def _():
        compute(lambda k: send_full(k * quarter, (k + 1) * quarter))

    @pl.when(jnp.logical_not(usual))
    def _():
        @pl.when(n_prev == tm)
        def _():
            send_full(0, tm)

        @pl.when(jnp.logical_and(n_prev > 0, n_prev < tm))
        def _():
            for j in range(tm):
                @pl.when(j < n_prev)
                def _(j=j):
                    send(j).start(priority=j % 2)

        @pl.when(n_cur > 0)
        def _():
            compute(lambda k: None)

    for bit in reversed(range(tm.bit_length())):
        size = 1 << bit

        @pl.when((n_prev & size) != 0)
        def _(size=size):
            pltpu.make_async_copy(obuf_ref.at[slot_prev, pl.ds(0, size)], out_ref.at[pl.ds(0, size)],
                                  sem_ref.at[slot_prev]).wait()


def _experts(layer, ea, eb, nvalid, xs_idx, inv, xs, w_gate, w_up, w_down, g, b, t):
    n_tiles = xs.shape[0] // MOE_TILE
    up_a = pl.BlockSpec((1, 1, D_MODEL, D_EXPERT), lambda i, ea, *_: (layer, ea[i], 0, 0))
    up_b = pl.BlockSpec((1, 1, D_MODEL, D_EXPERT), lambda i, ea, eb, *_: (layer, eb[i], 0, 0))
    dn_a = pl.BlockSpec((1, 1, D_EXPERT, D_MODEL), lambda i, ea, *_: (layer, ea[i], 0, 0))
    dn_b = pl.BlockSpec((1, 1, D_EXPERT, D_MODEL), lambda i, ea, eb, *_: (layer, eb[i], 0, 0))
    fixed = lambda i, *_: (0, 0)
    return pl.pallas_call(
        _experts_kernel,
        grid_spec=pltpu.PrefetchScalarGridSpec(
            num_scalar_prefetch=5,
            grid=(n_tiles + 1,),
            in_specs=[pl.BlockSpec((MOE_TILE, ROW_EXT), lambda i, ea, eb, nv, xs_idx, inv: (xs_idx[i], 0)),
                      up_a, up_a, dn_a, up_b, up_b, dn_b,
                      pl.BlockSpec((1, D_MODEL), fixed), pl.BlockSpec((1, D_MODEL), fixed)],
            out_specs=pl.BlockSpec(memory_space=pl.ANY),
            scratch_shapes=[pltpu.VMEM((2, MOE_TILE, D_MODEL), f32), pltpu.SemaphoreType.DMA((2,))],
        ),
        out_shape=jax.ShapeDtypeStruct((t, D_MODEL), f32),
        compiler_params=_params(("arbitrary",)),
        name="moe_experts",
    )(ea, eb, nvalid, xs_idx, inv, xs, w_gate, w_up, w_down, w_gate, w_up, w_down, g, b)


def _moe_plan(route, counts):
    t = route.shape[1]
    n_rows = t + N_BUCKETS * MOE_TILE
    n_tiles = n_rows // MOE_TILE
    cnt = counts[:N_BUCKETS, 0].astype(jnp.int32)
    padded = (cnt + MOE_TILE - 1) // MOE_TILE * MOE_TILE
    ends = jnp.cumsum(padded)
    starts = ends - padded
    buckets = jnp.arange(N_BUCKETS, dtype=jnp.int32)
    dest = jnp.sum(jnp.where(route[0][None, :] == buckets[:, None], starts[:, None], 0), axis=0) + route[1]
    tile_start = jnp.arange(n_tiles + 1, dtype=jnp.int32) * MOE_TILE
    tile_bucket = jnp.sum(tile_start[:, None] >= ends[None, :], axis=1).astype(jnp.int32)
    used = tile_bucket < N_BUCKETS
    tile_bucket = jnp.minimum(tile_bucket, N_BUCKETS - 1)
    nvalid = jnp.where(used, jnp.clip(starts[tile_bucket] + cnt[tile_bucket] - tile_start, 0, MOE_TILE), 0)
    xs_idx = jnp.minimum(jnp.arange(n_tiles + 1, dtype=jnp.int32), jnp.maximum(ends[-1] // MOE_TILE - 1, 0))
    last_bucket = tile_bucket[jnp.maximum(ends[-1] // MOE_TILE - 1, 0)]
    tile_bucket = jnp.where(used, tile_bucket, last_bucket)
    pair_a = jnp.array([g * EXPERTS_PER_GROUP + i for g in range(N_GROUPS) for i, _ in _PAIRS], jnp.int32)
    pair_b = jnp.array([g * EXPERTS_PER_GROUP + j for g in range(N_GROUPS) for _, j in _PAIRS], jnp.int32)
    tiles = jnp.arange(n_tiles, dtype=jnp.int32)
    is_last = jnp.any(jnp.logical_and((tiles[:, None] + 1) * MOE_TILE == ends[None, :], padded[None, :] > 0), axis=1)
    fill = jnp.logical_or(is_last, tiles * MOE_TILE >= ends[-1]).astype(jnp.int32)
    return dict(dest=dest, ea=pair_a[tile_bucket], eb=pair_b[tile_bucket], nvalid=nvalid.astype(jnp.int32),
                xs_idx=xs_idx, fill=fill, n_rows=n_rows)


def _moe(layer, hext, route, counts, w_gate, w_up, w_down, g, b):
    plan = _moe_plan(route, counts)
    xs, inv = _dispatch(plan["dest"], plan["fill"], hext, plan["n_rows"])
    return _experts(layer, plan["ea"], plan["eb"], plan["nvalid"], plan["xs_idx"], inv, xs,
                    w_gate, w_up, w_down, g, b, hext.shape[0])


def kernel(x, mem, ab_w_in, ab_conv_w, ab_a_log, ab_dt_bias, ab_norm_w, ab_w_out, sc_w_in, sc_conv_w, sc_w_out, xa_w_q, xa_w_kv, xa_w_o, router_w, router_b, moe_w_gate, moe_w_up, moe_w_down, ln_g, ln_b):
    batch, seq, d = x.shape
    n_mem = mem.shape[1]
    t = batch * seq
    h = x.reshape(t, d)
    mem2 = mem.reshape(batch * n_mem, d)
    rwt = jnp.transpose(router_w)
    rb = router_b.reshape(N_EXPERTS, 1)
    row = lambda v: v.reshape(1, -1)

    for layer in range(DEPTH):
        i = layer // 2
        if layer % 2 == 0:
            w_in = ab_w_in[i]
            c0, c1, c2 = 3 * DN_WIDTH, 4 * DN_WIDTH, 4 * DN_WIDTH + 2 * DN_HEADS
            w_ba = jnp.pad(w_in[:, c1:c2], ((0, 0), (0, LANES - 2 * DN_HEADS)))
            lane_pad = lambda v: jnp.pad(v.reshape(1, DN_HEADS), ((0, 0), (DN_HEADS, LANES - 2 * DN_HEADS)))
            act, z, bg, qkv_d = _project_ab(h, w_in[:, :c0].astype(bf16), w_in[:, c0:c1].astype(bf16), w_ba.astype(bf16),
                                            w_in[:, c2:].astype(bf16), ab_conv_w[i], lane_pad(ab_a_log[i]),
                                            lane_pad(ab_dt_bias[i]), seq)
            o_a = _deltanet(act, bg, z, row(ab_norm_w[i]), batch, seq)
            o_b = _dilated_attention(qkv_d.reshape(batch, seq, 3 * DA_WIDTH), batch, seq).reshape(t, DA_WIDTH)
            h = _mix_out(o_a, o_b, h, ab_w_out[i].astype(bf16), row(ln_g[layer, 0]), row(ln_b[layer, 0]))
        else:
            h = _shortconv(h, sc_w_in[i].astype(bf16), sc_conv_w[i], sc_w_out[i].astype(bf16),
                           row(ln_g[layer, 0]), row(ln_b[layer, 0]), batch, seq)
        (kv,) = _project(mem2, [xa_w_kv[layer].astype(bf16)], [bf16], n_mem)
        w_q = (xa_w_q[layer] * (XA_HEAD_DIM ** -0.5 * LOG2_E)).astype(bf16)
        hext, route, counts = _xattn_route(h, kv, w_q, xa_w_o[layer].astype(bf16),
                                           row(ln_g[layer, 1]), row(ln_b[layer, 1]), rwt, rb, batch, seq, n_mem)
        h = _moe(layer, hext, route, counts, moe_w_gate, moe_w_up, moe_w_down,
                 row(ln_g[layer, 2]), row(ln_b[layer, 2]))
    return h.reshape(batch, seq, d)
```

```python
import functools

import jax
import jax.numpy as jnp
from jax import lax
from jax.experimental import pallas as pl
from jax.experimental.pallas import tpu as pltpu

D_MODEL = 1024
DEPTH = 2
DN_HEADS = 4
DN_HEAD_DIM = 128
DN_WIDTH = DN_HEADS * DN_HEAD_DIM
DN_CONV = 4
DN_CHUNK = 64
DA_HEADS = 8
DA_HEAD_DIM = 64
DA_WIDTH = DA_HEADS * DA_HEAD_DIM
DA_BRANCHES = ((128, 1), (512, 4), (2048, 16))
DA_BLOCK = 128
DA_RESIDUES = max(d for _, d in DA_BRANCHES)
SC_CONV = 3
XA_HEADS = 4
XA_HEAD_DIM = D_MODEL // XA_HEADS
N_EXPERTS = 16
N_GROUPS = 4
EXPERTS_PER_GROUP = N_EXPERTS // N_GROUPS
D_EXPERT = D_MODEL // 2
DEEPNORM_ALPHA = (2 * DEPTH) ** 0.25
LN_EPS = 1e-5
RMS_EPS = 1e-6

LANES = 128
SUBLANES = 8
VMEM_LIMIT_BYTES = 48 * 1024 * 1024

TOKEN_TILE = 512
XA_TILE = 1024
DN_BLOCK = 256
MOE_TILE = 256
DISPATCH_CHUNK = 512
DA_UNROLL = 2
GATE_LANES = LANES
ROW_EXT = D_MODEL + GATE_LANES

_PAIRS = tuple((i, j) for i in range(EXPERTS_PER_GROUP) for j in range(i + 1, EXPERTS_PER_GROUP))
N_BUCKETS = N_GROUPS * len(_PAIRS)
BUCKET_ROWS = 32

NEG_BIG = -1e30
LOG2_E = 1.4426950408889634

bf16 = jnp.bfloat16
f32 = jnp.float32


def _params(semantics):
    return pltpu.CompilerParams(dimension_semantics=semantics, vmem_limit_bytes=VMEM_LIMIT_BYTES)


def _dot(a, b):
    return jnp.dot(a.astype(bf16), b.astype(bf16), preferred_element_type=f32)


def _dot_nt(a, b):
    return lax.dot_general(a.astype(bf16), b.astype(bf16), (((1,), (1,)), ((), ())), preferred_element_type=f32)


def _dot_tn(a, b):
    return lax.dot_general(a.astype(bf16), b.astype(bf16), (((0,), (0,)), ((), ())), preferred_element_type=f32)


def _split(a, parts):
    out = []
    rem = a
    for _ in range(parts):
        p = rem.astype(bf16)
        out.append(p)
        rem = rem - p.astype(f32)
    return out


def _layer_norm(v, g, b):
    mu = jnp.mean(v, -1, keepdims=True)
    c = v - mu
    var = jnp.mean(c * c, -1, keepdims=True)
    return c * lax.rsqrt(var + LN_EPS) * g + b


def _silu(v):
    return v * jax.nn.sigmoid(v)


def _softplus(v):
    return jnp.maximum(v, 0.0) + jnp.log(1.0 + jnp.exp(-jnp.abs(v)))


def _proj_kernel(x_ref, *refs, n_out):
    w_refs, o_refs = refs[:n_out], refs[n_out:]
    xb = x_ref[...].astype(bf16)
    for w_ref, o_ref in zip(w_refs, o_refs):
        n = w_ref.shape[1]
        for c0 in range(0, n, 512):
            c1 = min(c0 + 512, n)
            o_ref[:, c0:c1] = jnp.dot(xb, w_ref[:, c0:c1], preferred_element_type=f32).astype(o_ref.dtype)


def _project(x, ws, out_dtypes, tm):
    t, k = x.shape
    n_out = len(ws)
    return pl.pallas_call(
        functools.partial(_proj_kernel, n_out=n_out),
        grid=(t // tm,),
        in_specs=[pl.BlockSpec((tm, k), lambda i: (i, 0))]
        + [pl.BlockSpec(w.shape, lambda i: (0, 0)) for w in ws],
        out_specs=[pl.BlockSpec((tm, w.shape[1]), lambda i: (i, 0)) for w in ws],
        out_shape=[jax.ShapeDtypeStruct((t, w.shape[1]), dt) for w, dt in zip(ws, out_dtypes)],
        compiler_params=_params(("parallel",)),
        name="project",
    )(x, *ws)


def _proj_ab_kernel(x_ref, wqkv_ref, wz_ref, wba_ref, wd_ref, cw_ref, alog_ref, dtb_ref,
                    act_ref, z_ref, bg_ref, qkvd_ref, xx_ref, *, tiles_per_seq):
    @pl.when(pl.program_id(0) % tiles_per_seq == 0)
    def _():
        xx_ref[0:SUBLANES, :] = jnp.zeros((SUBLANES, xx_ref.shape[1]), f32)

    tm, hd = x_ref.shape[0], DN_HEAD_DIM
    xb = x_ref[...].astype(bf16)
    for c0 in range(0, 3 * DN_WIDTH, 512):
        xx_ref[SUBLANES:, c0:c0 + 512] = jnp.dot(xb, wqkv_ref[:, c0:c0 + 512], preferred_element_type=f32)
    z_ref[...] = jnp.dot(xb, wz_ref[...], preferred_element_type=f32)
    for c0 in range(0, 3 * DA_WIDTH, 512):
        qkvd_ref[:, c0:c0 + 512] = jnp.dot(xb, wd_ref[:, c0:c0 + 512], preferred_element_type=f32)
    ba = jnp.dot(xb, wba_ref[...], preferred_element_type=f32)
    is_beta = lax.broadcasted_iota(jnp.int32, ba.shape, 1) < DN_HEADS
    bg_ref[...] = jnp.where(is_beta, jax.nn.sigmoid(ba), -jnp.exp(alog_ref[...]) * _softplus(ba + dtb_ref[...]))

    cw = cw_ref[...]
    for c in range(3 * DN_HEADS):
        cols = slice(c * hd, (c + 1) * hd)
        conv = xx_ref[SUBLANES:, cols] * cw[DN_CONV - 1:DN_CONV, cols]
        for j in range(DN_CONV - 1):
            back = DN_CONV - 1 - j
            conv = conv + xx_ref[pl.ds(SUBLANES - back, tm), cols] * cw[j:j + 1, cols]
        a = _silu(conv)
        if c < 2 * DN_HEADS:
            scale = hd ** -0.5 if c < DN_HEADS else 1.0
            a = a * (lax.rsqrt(jnp.sum(a * a, -1, keepdims=True) + RMS_EPS) * scale)
        act_ref[:, cols] = a
    xx_ref[0:SUBLANES, :] = xx_ref[tm:tm + SUBLANES, :]


def _project_ab(x, w_qkv, w_z, w_ba, w_d, conv_w, alog_row, dtb_row, seq):
    t, k = x.shape
    tm = TOKEN_TILE
    fixed = lambda i: (0, 0)
    row = lambda i: (i, 0)
    widths = (3 * DN_WIDTH, DN_WIDTH, LANES, 3 * DA_WIDTH)
    return pl.pallas_call(
        functools.partial(_proj_ab_kernel, tiles_per_seq=seq // tm),
        grid=(t // tm,),
        in_specs=[pl.BlockSpec((tm, k), row)]
        + [pl.BlockSpec(w.shape, fixed) for w in (w_qkv, w_z, w_ba, w_d, conv_w, alog_row, dtb_row)],
        out_specs=[pl.BlockSpec((tm, n), row) for n in widths],
        out_shape=[jax.ShapeDtypeStruct((t, n), f32) for n in widths],
        scratch_shapes=[pltpu.VMEM((SUBLANES + tm, 3 * DN_WIDTH), f32)],
        compiler_params=_params(("arbitrary",)),
        name="project_ab",
    )(x, w_qkv, w_z, w_ba, w_d, conv_w, alog_row, dtb_row)


def _deltanet_kernel(act_ref, bg_ref, z_ref, nw_ref, o_ref, state_ref):
    @pl.when(pl.program_id(1) == 0)
    def _():
        state_ref[...] = jnp.zeros_like(state_ref)

    rows, ch, hd, width = DN_BLOCK, DN_CHUNK, DN_HEAD_DIM, DN_WIDTH
    sup = 2 * ch
    n_sup = rows // sup

    act = act_ref[...]
    beta_all = bg_ref[...]
    g_all = beta_all

    ri = lax.broadcasted_iota(jnp.int32, (sup, sup), 0)
    ci = lax.broadcasted_iota(jnp.int32, (sup, sup), 1)
    same = (ri >= ch) == (ci >= ch)
    incl = jnp.logical_and(ri >= ci, same)
    strict = jnp.logical_and(ri > ci, same)
    lower_ones = jnp.where(incl, 1.0, 0.0).astype(bf16)
    eye = jnp.where(ri == ci, 1.0, 0.0)
    first_rows = lax.broadcasted_iota(jnp.int32, (sup, hd), 0) < ch
    z = z_ref[...]
    nw = nw_ref[...]

    chains = [(s, h) for s in range(n_sup) for h in range(DN_HEADS)]
    g_cum_all, g_cum_t = [], []
    for s in range(n_sup):
        g_blk = g_all[s * sup:(s + 1) * sup, :]
        gc = sum(jnp.dot(lower_ones, p, preferred_element_type=f32) for p in _split(g_blk, 2))
        g_cum_all.append(gc)
        g_cum_t.append(jnp.transpose(gc))

    st = {}
    for s, h in chains:
        r0 = s * sup
        q = act[r0:r0 + sup, h * hd:(h + 1) * hd]
        k = act[r0:r0 + sup, width + h * hd:width + (h + 1) * hd]
        v = act[r0:r0 + sup, 2 * width + h * hd:2 * width + (h + 1) * hd]
        beta_b = jnp.broadcast_to(beta_all[r0:r0 + sup, h:h + 1], (sup, hd))
        g_i = jnp.broadcast_to(g_cum_all[s][:, DN_HEADS + h:DN_HEADS + h + 1], (sup, hd))
        g_j = jnp.broadcast_to(g_cum_t[s][DN_HEADS + h:DN_HEADS + h + 1, :], (sup, sup))
        decay = jnp.where(incl, jnp.exp(jnp.minimum(g_i - g_j, 0.0)), 0.0)
        e_g = jnp.exp(g_i)
        g_last = jnp.where(first_rows, g_i[ch - 1:ch, :], g_i[sup - 1:sup, :])
        st[s, h] = dict(q=q, k=k, kb=k * beta_b, vb=v * beta_b, decay=decay, e_g=e_g,
                        k_tail=k * jnp.exp(g_last - g_i),
                        gl=(jnp.exp(g_i[ch - 1:ch, :]), jnp.exp(g_i[sup - 1:sup, :])))

    for c in chains:
        d = st[c]
        d["a"] = jnp.where(strict, _dot_nt(d["kb"], d["k"]) * d["decay"], 0.0)
        d["attn"] = jnp.where(incl, _dot_nt(d["q"], d["k"]) * d["decay"], 0.0)
    for c in chains:
        st[c]["t_inv"] = eye - st[c]["a"]
        st[c]["pw"] = st[c]["a"]
    for _ in range(5):
        for c in chains:
            st[c]["pw"] = _dot(st[c]["pw"], st[c]["pw"])
        for c in chains:
            st[c]["t_inv"] = st[c]["t_inv"] + _dot(st[c]["t_inv"], st[c]["pw"])
    for c in chains:
        d = st[c]
        sol = _dot(d["t_inv"], jnp.concatenate([d["vb"], d["kb"] * d["e_g"]], axis=1))
        d["u"], d["w"] = sol[:, :hd], sol[:, hd:]
    for c in chains:
        d = st[c]
        uw = jnp.concatenate([d["u"], d["w"]], axis=1).astype(bf16)
        mix = jnp.dot(d["attn"].astype(bf16), uw, preferred_element_type=f32)
        d["au"] = mix[:, :hd]
        d["qe"] = d["q"] * d["e_g"] - mix[:, hd:]
        kt = d["k_tail"].astype(bf16)
        zero = jnp.zeros_like(kt)
        d["nu_pw"] = [_dot_tn(jnp.where(first_rows == first, kt, zero), uw)
                      for first in (True, False)]

    for s in range(n_sup):
        for half in range(2):
            for h in range(DN_HEADS):
                d = st[s, h]
                state = state_ref[h]
                lo = half * ch
                o = _dot(d["qe"][lo:lo + ch], state) + d["au"][lo:lo + ch]
                nu_pw = d["nu_pw"][half]
                state_ref[h] = state * d["gl"][half] - _dot(nu_pw[:, hd:], state) + nu_pw[:, :hd]
                o = o * lax.rsqrt(jnp.mean(o * o, -1, keepdims=True) + RMS_EPS) * nw
                r0 = s * sup + lo
                o_ref[r0:r0 + ch, h * hd:(h + 1) * hd] = (o * _silu(z[r0:r0 + ch, h * hd:(h + 1) * hd])).astype(o_ref.dtype)


def _deltanet(act, bg, z, norm_w, batch, seq):
    nb = seq // DN_BLOCK
    row = lambda b, c: (b * nb + c, 0)
    fixed = lambda b, c: (0, 0)
    return pl.pallas_call(
        _deltanet_kernel,
        grid=(batch, nb),
        in_specs=[
            pl.BlockSpec((DN_BLOCK, 3 * DN_WIDTH), row),
            pl.BlockSpec((DN_BLOCK, LANES), row),
            pl.BlockSpec((DN_BLOCK, DN_WIDTH), row),
            pl.BlockSpec((1, DN_HEAD_DIM), fixed),
        ],
        out_specs=pl.BlockSpec((DN_BLOCK, DN_WIDTH), row),
        out_shape=jax.ShapeDtypeStruct((batch * seq, DN_WIDTH), bf16),
        scratch_shapes=[pltpu.VMEM((DN_HEADS, DN_HEAD_DIM, DN_HEAD_DIM), f32)],
        compiler_params=_params(("arbitrary", "arbitrary")),
        name="deltanet",
    )(act, bg, z, norm_w)


def _dilated_kernel(q_ref, k_ref, v_ref, o_ref, qs_ref, ks_ref, vs_ref, ob_ref, lse_ref, s_ref, p_ref, st_ref, *, seq):
    blk, res = DA_BLOCK, DA_RESIDUES
    per = seq // res
    head0 = lax.broadcasted_iota(jnp.int32, (blk, LANES), 1) < DA_HEAD_DIM
    ai = lax.broadcasted_iota(jnp.int32, (2 * blk, blk), 0) & (blk - 1)
    bi = lax.broadcasted_iota(jnp.int32, (2 * blk, blk), 1)
    ones_blk = jnp.ones((2 * blk, LANES), bf16)

    q_scale = DA_HEAD_DIM ** -0.5 * LOG2_E
    for r in range(res):
        rows = pl.ds(r, per, stride=res)
        qs_ref[r * per:(r + 1) * per, :] = q_ref[0, rows, :] * q_scale
        ks_ref[r * per:(r + 1) * per, :] = k_ref[0, rows, :]
        vs_ref[r * per:(r + 1) * per, :] = v_ref[0, rows, :]

    for br, (window, dil) in enumerate(DA_BRANCHES):
        assert window // dil == blk and res % dil == 0
        runs = res // dil
        run_len = blk // runs
        nblk = seq // (dil * blk)
        shift = run_len.bit_length() - 1

        def pos(a, runs=runs, run_len=run_len, shift=shift):
            return runs * (a & (run_len - 1)) + (a >> shift)

        in_window = jnp.concatenate([pos(bi) >= pos(ai), pos(bi) <= pos(ai)], axis=1)

        n_iter = dil * nblk // DA_UNROLL
        assert n_iter * DA_UNROLL == dil * nblk

        def tile(ref, offs, run_len=run_len, runs=runs):
            parts = [ref[pl.ds(o, run_len), :] for o in offs]
            return parts[0] if runs == 1 else jnp.concatenate(parts, axis=0)

        def blocks_of(it, dil=dil, runs=runs, run_len=run_len, nblk=nblk):
            out = []
            for u in range(DA_UNROLL):
                idx = it * DA_UNROLL + u
                cls = idx // nblk
                n = idx - cls * nblk
                n_prev = jnp.maximum(n - 1, 0)
                cur = [pl.multiple_of((dil * m + cls) * per + run_len * n, SUBLANES) for m in range(runs)]
                prev = [pl.multiple_of((dil * m + cls) * per + run_len * n_prev, SUBLANES) for m in range(runs)]
                out.append((cur, prev, n > 0))
            return out

        def masked_scores(blocks, tile=tile, in_window=in_window):
            out = []
            for cur, prev, has_prev in blocks:
                q = tile(qs_ref, cur).astype(bf16)
                zero = jnp.zeros_like(q)
                q2 = jnp.concatenate([jnp.where(head0, q, zero), jnp.where(head0, zero, q)], axis=0)
                keys = jnp.concatenate([tile(ks_ref, prev), tile(ks_ref, cur)], axis=0)
                s = jnp.where(in_window, _dot_nt(q2, keys), NEG_BIG)
                out.append(jnp.concatenate([jnp.where(has_prev, s[:, :blk], NEG_BIG), s[:, blk:]], axis=1))
            return out

        def softmax_stage(blocks, scores):
            out = []
            for s in scores:
                top = jnp.max(s, -1, keepdims=True)
                out.append((jnp.exp2(s - top).astype(bf16), jnp.where(head0, top[:blk], top[blk:])))
            return out

        def output_stage(blocks, probs, br=br, run_len=run_len, tile=tile):
            for (cur, prev, has_prev), (pb, top) in zip(blocks, probs):
                vals = jnp.concatenate([tile(vs_ref, prev), tile(vs_ref, cur)], axis=0).astype(bf16)
                acc = jnp.dot(pb, jnp.concatenate([vals, ones_blk], axis=1), preferred_element_type=f32)
                den = jnp.where(head0, acc[:blk, LANES:], acc[blk:, LANES:])
                o_blk = jnp.where(head0, acc[:blk, :LANES], acc[blk:, :LANES]) / den
                lse_blk = top + jnp.log2(den)
                for m, o in enumerate(cur):
                    ob_ref[br, pl.ds(o, run_len), :] = o_blk[m * run_len:(m + 1) * run_len]
                    lse_ref[br, pl.ds(o, run_len), :] = lse_blk[m * run_len:(m + 1) * run_len]

        def put_scores(scores):
            for u, sc in enumerate(scores):
                s_ref[u] = sc

        def put_probs(probs):
            for u, (pb, top) in enumerate(probs):
                p_ref[u] = pb
                st_ref[u] = top

        put_probs(softmax_stage(blocks_of(0), masked_scores(blocks_of(0))))
        put_scores(masked_scores(blocks_of(min(1, n_iter - 1))))

        def body(it, carry, n_iter=n_iter, blocks_of=blocks_of, masked_scores=masked_scores,
                 softmax_stage=softmax_stage, output_stage=output_stage, put_scores=put_scores, put_probs=put_probs):
            scores = [s_ref[u] for u in range(DA_UNROLL)]
            probs = [(p_ref[u], st_ref[u]) for u in range(DA_UNROLL)]
            output_stage(blocks_of(it), probs)
            next_probs = softmax_stage(blocks_of(jnp.minimum(it + 1, n_iter - 1)), scores)
            next_scores = masked_scores(blocks_of(jnp.minimum(it + 2, n_iter - 1)))
            put_scores(next_scores)
            put_probs(next_probs)
            return carry

        lax.fori_loop(0, n_iter, body, 0)

    for r in range(res):
        rows = slice(r * per, (r + 1) * per)
        lses = [lse_ref[br, rows, :] for br in range(len(DA_BRANCHES))]
        top = functools.reduce(jnp.maximum, lses)
        wts = [jnp.exp2(l - top) for l in lses]
        num = sum(w * ob_ref[br, rows, :] for br, w in enumerate(wts))
        o_ref[0, pl.ds(r, per, stride=res), :] = num / sum(wts)


def _dilated_attention(qkv3, batch, seq):
    pairs = DA_WIDTH // LANES
    n_br = len(DA_BRANCHES)
    spec = lambda off: pl.BlockSpec((1, seq, LANES), lambda b, p, off=off: (b, 0, off + p))
    return pl.pallas_call(
        functools.partial(_dilated_kernel, seq=seq),
        grid=(batch, pairs),
        in_specs=[spec(0), spec(pairs), spec(2 * pairs)],
        out_specs=pl.BlockSpec((1, seq, LANES), lambda b, p: (b, 0, p)),
        out_shape=jax.ShapeDtypeStruct((batch, seq, DA_WIDTH), f32),
        scratch_shapes=[pltpu.VMEM((seq, LANES), f32)] * 3
        + [pltpu.VMEM((n_br, seq, LANES), f32)] * 2
        + [pltpu.VMEM((DA_UNROLL, 2 * DA_BLOCK, 2 * DA_BLOCK), f32),
           pltpu.VMEM((DA_UNROLL, 2 * DA_BLOCK, 2 * DA_BLOCK), bf16),
           pltpu.VMEM((DA_UNROLL, DA_BLOCK, LANES), f32)],
        compiler_params=_params(("parallel", "parallel")),
        name="dilated_attention",
    )(qkv3, qkv3, qkv3)


def _mix_out_kernel(oa_ref, ob_ref, h_ref, w_ref, g_ref, b_ref, o_ref):
    w = w_ref[...]
    y = jnp.dot(oa_ref[...], w[:DN_WIDTH], preferred_element_type=f32)
    y = y + jnp.dot(ob_ref[...].astype(bf16), w[DN_WIDTH:], preferred_element_type=f32)
    o_ref[...] = _layer_norm(DEEPNORM_ALPHA * h_ref[...] + y, g_ref[...], b_ref[...])


def _mix_out(o_a, o_b, h, w_out, g, b):
    t = h.shape[0]
    tm = TOKEN_TILE
    fixed = lambda i: (0, 0)
    return pl.pallas_call(
        _mix_out_kernel,
        grid=(t // tm,),
        in_specs=[
            pl.BlockSpec((tm, DN_WIDTH), lambda i: (i, 0)),
            pl.BlockSpec((tm, DA_WIDTH), lambda i: (i, 0)),
            pl.BlockSpec((tm, D_MODEL), lambda i: (i, 0)),
            pl.BlockSpec(w_out.shape, fixed),
            pl.BlockSpec((1, D_MODEL), fixed),
            pl.BlockSpec((1, D_MODEL), fixed),
        ],
        out_specs=pl.BlockSpec((tm, D_MODEL), lambda i: (i, 0)),
        out_shape=jax.ShapeDtypeStruct((t, D_MODEL), f32),
        compiler_params=_params(("parallel",)),
        name="mix_out",
    )(o_a, o_b, h, w_out, g, b)


def _shortconv_kernel(h_ref, win_ref, cw_ref, wout_ref, g_ref, b_ref, o_ref, tail_ref):
    @pl.when(pl.program_id(1) == 0)
    def _():
        tail_ref[...] = jnp.zeros_like(tail_ref)

    tm = h_ref.shape[0]
    d = D_MODEL
    h = h_ref[...]
    hb = h.astype(bf16)
    gate_b = jnp.dot(hb, win_ref[:, :d], preferred_element_type=f32)
    gate_c = jnp.dot(hb, win_ref[:, d:2 * d], preferred_element_type=f32)
    hid = jnp.dot(hb, win_ref[:, 2 * d:], preferred_element_type=f32)
    u = gate_c * hid
    uu = jnp.concatenate([tail_ref[...], u], axis=0)
    cw = cw_ref[...]
    conv = u * cw[SC_CONV - 1:SC_CONV, :]
    for j in range(SC_CONV - 1):
        back = SC_CONV - 1 - j
        conv = conv + uu[SUBLANES - back:SUBLANES - back + tm, :] * cw[j:j + 1, :]
    tail_ref[...] = u[tm - SUBLANES:, :]
    y = jnp.dot((gate_b * conv).astype(bf16), wout_ref[...], preferred_element_type=f32)
    o_ref[...] = _layer_norm(DEEPNORM_ALPHA * h + y, g_ref[...], b_ref[...])


def _shortconv(h, w_in, conv_w, w_out, g, b, batch, seq):
    tm = TOKEN_TILE
    ns = seq // tm
    row = lambda bb, s: (bb * ns + s, 0)
    fixed = lambda bb, s: (0, 0)
    return pl.pallas_call(
        _shortconv_kernel,
        grid=(batch, ns),
        in_specs=[
            pl.BlockSpec((tm, D_MODEL), row),
            pl.BlockSpec(w_in.shape, fixed),
            pl.BlockSpec(conv_w.shape, fixed),
            pl.BlockSpec(w_out.shape, fixed),
            pl.BlockSpec((1, D_MODEL), fixed),
            pl.BlockSpec((1, D_MODEL), fixed),
        ],
        out_specs=pl.BlockSpec((tm, D_MODEL), row),
        out_shape=jax.ShapeDtypeStruct((batch * seq, D_MODEL), f32),
        scratch_shapes=[pltpu.VMEM((SUBLANES, D_MODEL), f32)],
        compiler_params=_params(("arbitrary", "arbitrary")),
        name="shortconv",
    )(h, w_in, conv_w, w_out, g, b)


def _route(h2, rwt_ref, rb_ref, cnt_ref):
    tm = h2.shape[0]
    w_hi, w_lo = _split(rwt_ref[...], 2)
    x_hi, x_lo = _split(h2, 2)
    nt = (((1,), (1,)), ((), ()))
    logits = (lax.dot_general(w_hi, x_hi, nt, preferred_element_type=f32)
              + lax.dot_general(w_hi, x_lo, nt, preferred_element_type=f32)
              + lax.dot_general(w_lo, x_hi, nt, preferred_element_type=f32))
    mx = jnp.max(logits, axis=0, keepdims=True)
    ex = jnp.exp(logits - mx)
    scores = ex / jnp.sum(ex, axis=0, keepdims=True)
    sel = scores + rb_ref[...]
    best = jnp.full((1, tm), -jnp.inf, f32)
    bucket = jnp.zeros((1, tm), jnp.int32)
    s_a = jnp.zeros((1, tm), f32)
    s_b = jnp.zeros((1, tm), f32)
    for grp in range(N_GROUPS):
        for p, (i, j) in enumerate(_PAIRS):
            a, b = grp * EXPERTS_PER_GROUP + i, grp * EXPERTS_PER_GROUP + j
            ps = sel[a:a + 1, :] + sel[b:b + 1, :]
            upd = ps > best
            best = jnp.where(upd, ps, best)
            bucket = jnp.where(upd, grp * len(_PAIRS) + p, bucket)
            s_a = jnp.where(upd, scores[a:a + 1, :], s_a)
            s_b = jnp.where(upd, scores[b:b + 1, :], s_b)
    denom = s_a + s_b
    gate_rows = jnp.concatenate([s_a / denom, s_b / denom, jnp.zeros((GATE_LANES - 2, tm), f32)], axis=0)
    gates = jnp.transpose(gate_rows)
    onehot = jnp.where(lax.broadcasted_iota(jnp.int32, (BUCKET_ROWS, tm), 0) == bucket, 1.0, 0.0)
    before = lax.broadcasted_iota(jnp.int32, (tm, tm), 0) < lax.broadcasted_iota(jnp.int32, (tm, tm), 1)
    prefix = jnp.dot(onehot.astype(bf16), jnp.where(before, 1.0, 0.0).astype(bf16), preferred_element_type=f32)
    cnt = cnt_ref[...]
    rank = jnp.sum(onehot * (prefix + cnt[:, 0:1]), axis=0, keepdims=True)
    cnt_ref[...] = cnt + jnp.sum(onehot, axis=1, keepdims=True)
    return gates, bucket, rank.astype(jnp.int32)


def _xattn_kernel(h_ref, kv_ref, wq_ref, wo_ref, g_ref, b_ref, rwt_ref, rb_ref,
                  hext_ref, route_ref, cnt_out_ref, cnt_ref):
    @pl.when(jnp.logical_and(pl.program_id(0) == 0, pl.program_id(1) == 0))
    def _():
        cnt_ref[...] = jnp.zeros_like(cnt_ref)

    h = h_ref[...]
    tm = h.shape[0]
    q = jnp.dot(h.astype(bf16), wq_ref[...], preferred_element_type=f32).astype(bf16)
    heads = [(hd * XA_HEAD_DIM, (hd + 1) * XA_HEAD_DIM) for hd in range(XA_HEADS)]
    scores = [_dot_nt(q[:, lo:hi], kv_ref[:, lo:hi]) for lo, hi in heads]
    probs = [jnp.exp2(s - jnp.max(s, -1, keepdims=True)) for s in scores]
    outs = [(_dot(p, kv_ref[:, D_MODEL + lo:D_MODEL + hi]) / jnp.sum(p, -1, keepdims=True)).astype(bf16)
            for p, (lo, hi) in zip(probs, heads)]
    y = jnp.dot(jnp.concatenate(outs, axis=1), wo_ref[...], preferred_element_type=f32)
    h2 = _layer_norm(DEEPNORM_ALPHA * h + y, g_ref[...], b_ref[...])
    gates, bucket, rank = _route(h2, rwt_ref, rb_ref, cnt_ref)
    hext_ref[:, :D_MODEL] = h2
    hext_ref[:, D_MODEL:] = gates
    route_ref[...] = jnp.concatenate([bucket, rank, jnp.zeros((SUBLANES - 2, tm), jnp.int32)], axis=0)
    cnt_out_ref[...] = cnt_ref[...]


def _xattn_route(h, kv, w_q, w_o, g, b, rwt, rb, batch, seq, n_mem):
    tm = XA_TILE
    ns = seq // tm
    t = batch * seq
    row = lambda bb, s: (bb * ns + s, 0)
    fixed = lambda bb, s: (0, 0)
    return pl.pallas_call(
        _xattn_kernel,
        grid=(batch, ns),
        in_specs=[
            pl.BlockSpec((tm, D_MODEL), row),
            pl.BlockSpec((n_mem, 2 * D_MODEL), lambda bb, s: (bb, 0)),
            pl.BlockSpec(w_q.shape, fixed),
            pl.BlockSpec(w_o.shape, fixed),
            pl.BlockSpec((1, D_MODEL), fixed),
            pl.BlockSpec((1, D_MODEL), fixed),
            pl.BlockSpec(rwt.shape, fixed),
            pl.BlockSpec(rb.shape, fixed),
        ],
        out_specs=[
            pl.BlockSpec((tm, ROW_EXT), row),
            pl.BlockSpec((SUBLANES, tm), lambda bb, s: (0, bb * ns + s)),
            pl.BlockSpec((BUCKET_ROWS, LANES), fixed),
        ],
        out_shape=[
            jax.ShapeDtypeStruct((t, ROW_EXT), f32),
            jax.ShapeDtypeStruct((SUBLANES, t), jnp.int32),
            jax.ShapeDtypeStruct((BUCKET_ROWS, LANES), f32),
        ],
        scratch_shapes=[pltpu.VMEM((BUCKET_ROWS, LANES), f32)],
        compiler_params=_params(("arbitrary", "arbitrary")),
        name="xattn_route",
    )(h, kv, w_q, w_o, g, b, rwt, rb)


def _dispatch_kernel(dest_ref, fill_ref, hext_ref, xs_ref, inv_ref, zeros_ref, sem, zsem):
    base = pl.program_id(0) * DISPATCH_CHUNK
    n_tiles = xs_ref.shape[0] // MOE_TILE

    @pl.when(pl.program_id(0) == 0)
    def _():
        zeros_ref[...] = jnp.zeros_like(zeros_ref)

        def fill(tile):
            return pltpu.make_async_copy(zeros_ref, xs_ref.at[pl.ds(tile * MOE_TILE, MOE_TILE)], zsem)

        for tile in range(n_tiles):
            @pl.when(fill_ref[tile] > 0)
            def _(tile=tile):
                fill(tile).start()

        def clear(j8, carry):
            for u in range(8):
                inv_ref[j8 * 8 + u] = 0
            return carry

        lax.fori_loop(0, inv_ref.shape[0] // 8, clear, 0)
        for tile in range(n_tiles):
            @pl.when(fill_ref[tile] > 0)
            def _(tile=tile):
                fill(tile).wait()

    for j in range(DISPATCH_CHUNK):
        d = dest_ref[base + j]
        inv_ref[d] = base + j
        pltpu.make_async_copy(hext_ref.at[pl.ds(j, 1)], xs_ref.at[pl.ds(d, 1)], sem).start(priority=j % 2)
    pltpu.make_async_copy(hext_ref, xs_ref.at[pl.ds(0, DISPATCH_CHUNK)], sem).wait()


def _dispatch(dest, fill, hext, n_rows):
    t = hext.shape[0]
    return pl.pallas_call(
        _dispatch_kernel,
        grid_spec=pltpu.PrefetchScalarGridSpec(
            num_scalar_prefetch=2,
            grid=(t // DISPATCH_CHUNK,),
            in_specs=[pl.BlockSpec((DISPATCH_CHUNK, ROW_EXT), lambda i, *_: (i, 0))],
            out_specs=[pl.BlockSpec(memory_space=pl.ANY), pl.BlockSpec(memory_space=pltpu.SMEM)],
            scratch_shapes=[pltpu.VMEM((MOE_TILE, ROW_EXT), f32), pltpu.SemaphoreType.DMA(()),
                            pltpu.SemaphoreType.DMA(())],
        ),
        out_shape=[jax.ShapeDtypeStruct((n_rows, ROW_EXT), f32), jax.ShapeDtypeStruct((n_rows,), jnp.int32)],
        compiler_params=_params(("arbitrary",)),
        name="moe_dispatch",
    )(dest, fill, hext)


def _experts_kernel(ea_ref, eb_ref, nvalid_ref, xs_idx_ref, inv_ref, xs_ref, wga_ref, wua_ref, wda_ref,
                    wgb_ref, wub_ref, wdb_ref, g_ref, b_ref, out_ref, obuf_ref, sem_ref):
    i = pl.program_id(0)
    tm = MOE_TILE
    prev = jnp.maximum(i - 1, 0)
    n_prev = jnp.where(i > 0, nvalid_ref[prev], 0)
    slot_prev = prev % 2

    def send(j):
        return pltpu.make_async_copy(obuf_ref.at[slot_prev, pl.ds(j, 1)],
                                     out_ref.at[pl.ds(inv_ref[prev * tm + j], 1)], sem_ref.at[slot_prev])

    n_cur = jnp.where(i < pl.num_programs(0) - 1, nvalid_ref[jnp.minimum(i, pl.num_programs(0) - 2)], 0)

    def send_full(lo, hi):
        for j in range(lo, hi):
            send(j).start(priority=j % 2)

    def compute(between):
        x32 = xs_ref[:, :D_MODEL]
        x = x32.astype(bf16)
        gates = xs_ref[:, D_MODEL:]

        def hidden(wg_ref, wu_ref, gate):
            hg = jnp.dot(x, wg_ref[0, 0].astype(bf16), preferred_element_type=f32)
            hu = jnp.dot(x, wu_ref[0, 0].astype(bf16), preferred_element_type=f32)
            return (_silu(hg) * hu * gate).astype(bf16)

        between(0)
        h_a = hidden(wga_ref, wua_ref, gates[:, 0:1])
        between(1)
        h_b = hidden(wgb_ref, wub_ref, gates[:, 1:2])
        between(2)
        y = (jnp.dot(h_a, wda_ref[0, 0].astype(bf16), preferred_element_type=f32)
             + jnp.dot(h_b, wdb_ref[0, 0].astype(bf16), preferred_element_type=f32))
        between(3)
        obuf_ref[i % 2] = _layer_norm(DEEPNORM_ALPHA * x32 + y, g_ref[...], b_ref[...])

    usual = jnp.logical_and(n_prev == tm, n_cur > 0)
    quarter = tm // 4

    @pl.when(usual)
    def _():
        compute(lambda k: send_full(k * quarter, (k + 1) * quarter))

    @pl.when(jnp.logical_not(usual))
    def _():
        @pl.when(n_prev == tm)
        def _():
            send_full(0, tm)

        @pl.when(jnp.logical_and(n_prev > 0, n_prev < tm))
        def _():
            for j in range(tm):
                @pl.when(j < n_prev)
                def _(j=j):
                    send(j).start(priority=j % 2)

        @pl.when(n_cur > 0)
        def _():
            compute(lambda k: None)

    for bit in reversed(range(tm.bit_length())):
        size = 1 << bit

        @pl.when((n_prev & size) != 0)
        def _(size=size):
            pltpu.make_async_copy(obuf_ref.at[slot_prev, pl.ds(0, size)], out_ref.at[pl.ds(0, size)],
                                  sem_ref.at[slot_prev]).wait()


def _experts(layer, ea, eb, nvalid, xs_idx, inv, xs, w_gate, w_up, w_down, g, b, t):
    n_tiles = xs.shape[0] // MOE_TILE
    up_a = pl.BlockSpec((1, 1, D_MODEL, D_EXPERT), lambda i, ea, *_: (layer, ea[i], 0, 0))
    up_b = pl.BlockSpec((1, 1, D_MODEL, D_EXPERT), lambda i, ea, eb, *_: (layer, eb[i], 0, 0))
    dn_a = pl.BlockSpec((1, 1, D_EXPERT, D_MODEL), lambda i, ea, *_: (layer, ea[i], 0, 0))
    dn_b = pl.BlockSpec((1, 1, D_EXPERT, D_MODEL), lambda i, ea, eb, *_: (layer, eb[i], 0, 0))
    fixed = lambda i, *_: (0, 0)
    return pl.pallas_call(
        _experts_kernel,
        grid_spec=pltpu.PrefetchScalarGridSpec(
            num_scalar_prefetch=5,
            grid=(n_tiles + 1,),
            in_specs=[pl.BlockSpec((MOE_TILE, ROW_EXT), lambda i, ea, eb, nv, xs_idx, inv: (xs_idx[i], 0)),
                      up_a, up_a, dn_a, up_b, up_b, dn_b,
                      pl.BlockSpec((1, D_MODEL), fixed), pl.BlockSpec((1, D_MODEL), fixed)],
            out_specs=pl.BlockSpec(memory_space=pl.ANY),
            scratch_shapes=[pltpu.VMEM((2, MOE_TILE, D_MODEL), f32), pltpu.SemaphoreType.DMA((2,))],
        ),
        out_shape=jax.ShapeDtypeStruct((t, D_MODEL), f32),
        compiler_params=_params(("arbitrary",)),
        name="moe_experts",
    )(ea, eb, nvalid, xs_idx, inv, xs, w_gate, w_up, w_down, w_gate, w_up, w_down, g, b)


def _moe_plan(route, counts):
    t = route.shape[1]
    n_rows = t + N_BUCKETS * MOE_TILE
    n_tiles = n_rows // MOE_TILE
    cnt = counts[:N_BUCKETS, 0].astype(jnp.int32)
    padded = (cnt + MOE_TILE - 1) // MOE_TILE * MOE_TILE
    ends = jnp.cumsum(padded)
    starts = ends - padded
    buckets = jnp.arange(N_BUCKETS, dtype=jnp.int32)
    dest = jnp.sum(jnp.where(route[0][None, :] == buckets[:, None], starts[:, None], 0), axis=0) + route[1]
    tile_start = jnp.arange(n_tiles + 1, dtype=jnp.int32) * MOE_TILE
    tile_bucket = jnp.sum(tile_start[:, None] >= ends[None, :], axis=1).astype(jnp.int32)
    used = tile_bucket < N_BUCKETS
    tile_bucket = jnp.minimum(tile_bucket, N_BUCKETS - 1)
    nvalid = jnp.where(used, jnp.clip(starts[tile_bucket] + cnt[tile_bucket] - tile_start, 0, MOE_TILE), 0)
    xs_idx = jnp.minimum(jnp.arange(n_tiles + 1, dtype=jnp.int32), jnp.maximum(ends[-1] // MOE_TILE - 1, 0))
    last_bucket = tile_bucket[jnp.maximum(ends[-1] // MOE_TILE - 1, 0)]
    tile_bucket = jnp.where(used, tile_bucket, last_bucket)
    pair_a = jnp.array([g * EXPERTS_PER_GROUP + i for g in range(N_GROUPS) for i, _ in _PAIRS], jnp.int32)
    pair_b = jnp.array([g * EXPERTS_PER_GROUP + j for g in range(N_GROUPS) for _, j in _PAIRS], jnp.int32)
    tiles = jnp.arange(n_tiles, dtype=jnp.int32)
    is_last = jnp.any(jnp.logical_and((tiles[:, None] + 1) * MOE_TILE == ends[None, :], padded[None, :] > 0), axis=1)
    fill = jnp.logical_or(is_last, tiles * MOE_TILE >= ends[-1]).astype(jnp.int32)
    return dict(dest=dest, ea=pair_a[tile_bucket], eb=pair_b[tile_bucket], nvalid=nvalid.astype(jnp.int32),
                xs_idx=xs_idx, fill=fill, n_rows=n_rows)


def _moe(layer, hext, route, counts, w_gate, w_up, w_down, g, b):
    plan = _moe_plan(route, counts)
    xs, inv = _dispatch(plan["dest"], plan["fill"], hext, plan["n_rows"])
    return _experts(layer, plan["ea"], plan["eb"], plan["nvalid"], plan["xs_idx"], inv, xs,
                    w_gate, w_up, w_down, g, b, hext.shape[0])


def kernel(x, mem, ab_w_in, ab_conv_w, ab_a_log, ab_dt_bias, ab_norm_w, ab_w_out, sc_w_in, sc_conv_w, sc_w_out, xa_w_q, xa_w_kv, xa_w_o, router_w, router_b, moe_w_gate, moe_w_up, moe_w_down, ln_g, ln_b):
    batch, seq, d = x.shape
    n_mem = mem.shape[1]
    t = batch * seq
    h = x.reshape(t, d)
    mem2 = mem.reshape(batch * n_mem, d)
    rwt = jnp.transpose(router_w)
    rb = router_b.reshape(N_EXPERTS, 1)
    row = lambda v: v.reshape(1, -1)

    for layer in range(DEPTH):
        i = layer // 2
        if layer % 2 == 0:
            w_in = ab_w_in[i]
            c0, c1, c2 = 3 * DN_WIDTH, 4 * DN_WIDTH, 4 * DN_WIDTH + 2 * DN_HEADS
            w_ba = jnp.pad(w_in[:, c1:c2], ((0, 0), (0, LANES - 2 * DN_HEADS)))
            lane_pad = lambda v: jnp.pad(v.reshape(1, DN_HEADS), ((0, 0), (DN_HEADS, LANES - 2 * DN_HEADS)))
            act, z, bg, qkv_d = _project_ab(h, w_in[:, :c0].astype(bf16), w_in[:, c0:c1].astype(bf16), w_ba.astype(bf16),
                                            w_in[:, c2:].astype(bf16), ab_conv_w[i], lane_pad(ab_a_log[i]),
                                            lane_pad(ab_dt_bias[i]), seq)
            o_a = _deltanet(act, bg, z, row(ab_norm_w[i]), batch, seq)
            o_b = _dilated_attention(qkv_d.reshape(batch, seq, 3 * DA_WIDTH), batch, seq).reshape(t, DA_WIDTH)
            h = _mix_out(o_a, o_b, h, ab_w_out[i].astype(bf16), row(ln_g[layer, 0]), row(ln_b[layer, 0]))
        else:
            h = _shortconv(h, sc_w_in[i].astype(bf16), sc_conv_w[i], sc_w_out[i].astype(bf16),
                           row(ln_g[layer, 0]), row(ln_b[layer, 0]), batch, seq)
        (kv,) = _project(mem2, [xa_w_kv[layer].astype(bf16)], [bf16], n_mem)
        w_q = (xa_w_q[layer] * (XA_HEAD_DIM ** -0.5 * LOG2_E)).astype(bf16)
        hext, route, counts = _xattn_route(h, kv, w_q, xa_w_o[layer].astype(bf16),
                                           row(ln_g[layer, 1]), row(ln_b[layer, 1]), rwt, rb, batch, seq, n_mem)
        h = _moe(layer, hext, route, counts, moe_w_gate, moe_w_up, moe_w_down,
                 row(ln_g[layer, 2]), row(ln_b[layer, 2]))
    return h.reshape(batch, seq, d)
```

```python
import functools

import jax
import jax.numpy as jnp
from jax import lax
from jax.experimental import pallas as pl
from jax.experimental.pallas import tpu as pltpu

D_MODEL = 1024
DEPTH = 2
DN_HEADS = 4
DN_HEAD_DIM = 128
DN_WIDTH = DN_HEADS * DN_HEAD_DIM
DN_CONV = 4
DN_CHUNK = 64
DA_HEADS = 8
DA_HEAD_DIM = 64
DA_WIDTH = DA_HEADS * DA_HEAD_DIM
DA_BRANCHES = ((128, 1), (512, 4), (2048, 16))
DA_BLOCK = 128
DA_RESIDUES = max(d for _, d in DA_BRANCHES)
SC_CONV = 3
XA_HEADS = 4
XA_HEAD_DIM = D_MODEL // XA_HEADS
N_EXPERTS = 16
N_GROUPS = 4
EXPERTS_PER_GROUP = N_EXPERTS // N_GROUPS
D_EXPERT = D_MODEL // 2
DEEPNORM_ALPHA = (2 * DEPTH) ** 0.25
LN_EPS = 1e-5
RMS_EPS = 1e-6

LANES = 128
SUBLANES = 8
VMEM_LIMIT_BYTES = 48 * 1024 * 1024

TOKEN_TILE = 512
XA_TILE = 1024
DN_BLOCK = 256
MOE_TILE = 256
DISPATCH_CHUNK = 512
DA_UNROLL = 4
GATE_LANES = LANES
ROW_EXT = D_MODEL + GATE_LANES

_PAIRS = tuple((i, j) for i in range(EXPERTS_PER_GROUP) for j in range(i + 1, EXPERTS_PER_GROUP))
N_BUCKETS = N_GROUPS * len(_PAIRS)
BUCKET_ROWS = 32

NEG_BIG = -1e30
LOG2_E = 1.4426950408889634

bf16 = jnp.bfloat16
f32 = jnp.float32


def _params(semantics):
    return pltpu.CompilerParams(dimension_semantics=semantics, vmem_limit_bytes=VMEM_LIMIT_BYTES)


def _dot(a, b):
    return jnp.dot(a.astype(bf16), b.astype(bf16), preferred_element_type=f32)


def _dot_nt(a, b):
    return lax.dot_general(a.astype(bf16), b.astype(bf16), (((1,), (1,)), ((), ())), preferred_element_type=f32)


def _dot_tn(a, b):
    return lax.dot_general(a.astype(bf16), b.astype(bf16), (((0,), (0,)), ((), ())), preferred_element_type=f32)


def _split(a, parts):
    out = []
    rem = a
    for _ in range(parts):
        p = rem.astype(bf16)
        out.append(p)
        rem = rem - p.astype(f32)
    return out


def _layer_norm(v, g, b):
    mu = jnp.mean(v, -1, keepdims=True)
    c = v - mu
    var = jnp.mean(c * c, -1, keepdims=True)
    return c * lax.rsqrt(var + LN_EPS) * g + b


def _silu(v):
    return v * jax.nn.sigmoid(v)


def _softplus(v):
    return jnp.maximum(v, 0.0) + jnp.log(1.0 + jnp.exp(-jnp.abs(v)))


def _proj_kernel(x_ref, *refs, n_out):
    w_refs, o_refs = refs[:n_out], refs[n_out:]
    xb = x_ref[...].astype(bf16)
    for w_ref, o_ref in zip(w_refs, o_refs):
        n = w_ref.shape[1]
        for c0 in range(0, n, 512):
            c1 = min(c0 + 512, n)
            o_ref[:, c0:c1] = jnp.dot(xb, w_ref[:, c0:c1], preferred_element_type=f32).astype(o_ref.dtype)


def _project(x, ws, out_dtypes, tm):
    t, k = x.shape
    n_out = len(ws)
    return pl.pallas_call(
        functools.partial(_proj_kernel, n_out=n_out),
        grid=(t // tm,),
        in_specs=[pl.BlockSpec((tm, k), lambda i: (i, 0))]
        + [pl.BlockSpec(w.shape, lambda i: (0, 0)) for w in ws],
        out_specs=[pl.BlockSpec((tm, w.shape[1]), lambda i: (i, 0)) for w in ws],
        out_shape=[jax.ShapeDtypeStruct((t, w.shape[1]), dt) for w, dt in zip(ws, out_dtypes)],
        compiler_params=_params(("parallel",)),
        name="project",
    )(x, *ws)


def _proj_ab_kernel(x_ref, wqkv_ref, wz_ref, wba_ref, wd_ref, cw_ref, alog_ref, dtb_ref,
                    act_ref, z_ref, bg_ref, qkvd_ref, xx_ref, *, tiles_per_seq):
    @pl.when(pl.program_id(0) % tiles_per_seq == 0)
    def _():
        xx_ref[0:SUBLANES, :] = jnp.zeros((SUBLANES, xx_ref.shape[1]), f32)

    tm, hd = x_ref.shape[0], DN_HEAD_DIM
    xb = x_ref[...].astype(bf16)
    cw = cw_ref[...]

    def conv_act(c):
        cols = slice(c * hd, (c + 1) * hd)
        conv = xx_ref[SUBLANES:, cols] * cw[DN_CONV - 1:DN_CONV, cols]
        for j in range(DN_CONV - 1):
            back = DN_CONV - 1 - j
            conv = conv + xx_ref[pl.ds(SUBLANES - back, tm), cols] * cw[j:j + 1, cols]
        a = _silu(conv)
        if c < 2 * DN_HEADS:
            scale = hd ** -0.5 if c < DN_HEADS else 1.0
            a = a * (lax.rsqrt(jnp.sum(a * a, -1, keepdims=True) + RMS_EPS) * scale)
        act_ref[:, cols] = a

    for c0 in range(0, 3 * DN_WIDTH, 512):
        xx_ref[SUBLANES:, c0:c0 + 512] = jnp.dot(xb, wqkv_ref[:, c0:c0 + 512], preferred_element_type=f32)
        for c in range(c0 // hd, (c0 + 512) // hd):
            conv_act(c)
    z_ref[...] = jnp.dot(xb, wz_ref[...], preferred_element_type=f32)
    for c0 in range(0, 3 * DA_WIDTH, 512):
        qkvd_ref[:, c0:c0 + 512] = jnp.dot(xb, wd_ref[:, c0:c0 + 512], preferred_element_type=f32)
    ba = jnp.dot(xb, wba_ref[...], preferred_element_type=f32)
    is_beta = lax.broadcasted_iota(jnp.int32, ba.shape, 1) < DN_HEADS
    bg_ref[...] = jnp.where(is_beta, jax.nn.sigmoid(ba), -jnp.exp(alog_ref[...]) * _softplus(ba + dtb_ref[...]))
    xx_ref[0:SUBLANES, :] = xx_ref[tm:tm + SUBLANES, :]


def _project_ab(x, w_qkv, w_z, w_ba, w_d, conv_w, alog_row, dtb_row, seq):
    t, k = x.shape
    tm = TOKEN_TILE
    fixed = lambda i: (0, 0)
    row = lambda i: (i, 0)
    widths = (3 * DN_WIDTH, DN_WIDTH, LANES, 3 * DA_WIDTH)
    return pl.pallas_call(
        functools.partial(_proj_ab_kernel, tiles_per_seq=seq // tm),
        grid=(t // tm,),
        in_specs=[pl.BlockSpec((tm, k), row)]
        + [pl.BlockSpec(w.shape, fixed) for w in (w_qkv, w_z, w_ba, w_d, conv_w, alog_row, dtb_row)],
        out_specs=[pl.BlockSpec((tm, n), row) for n in widths],
        out_shape=[jax.ShapeDtypeStruct((t, n), f32) for n in widths],
        scratch_shapes=[pltpu.VMEM((SUBLANES + tm, 3 * DN_WIDTH), f32)],
        compiler_params=_params(("arbitrary",)),
        name="project_ab",
    )(x, w_qkv, w_z, w_ba, w_d, conv_w, alog_row, dtb_row)


def _deltanet_kernel(act_ref, bg_ref, z_ref, nw_ref, o_ref, state_ref):
    @pl.when(pl.program_id(1) == 0)
    def _():
        state_ref[...] = jnp.zeros_like(state_ref)

    rows, ch, hd, width = DN_BLOCK, DN_CHUNK, DN_HEAD_DIM, DN_WIDTH
    sup = 2 * ch
    n_sup = rows // sup

    act = act_ref[...]
    beta_all = bg_ref[...]
    g_all = beta_all

    ri = lax.broadcasted_iota(jnp.int32, (sup, sup), 0)
    ci = lax.broadcasted_iota(jnp.int32, (sup, sup), 1)
    same = (ri >= ch) == (ci >= ch)
    incl = jnp.logical_and(ri >= ci, same)
    strict = jnp.logical_and(ri > ci, same)
    lower_ones = jnp.where(incl, 1.0, 0.0).astype(bf16)
    eye = jnp.where(ri == ci, 1.0, 0.0)
    first_rows = lax.broadcasted_iota(jnp.int32, (sup, hd), 0) < ch
    z = z_ref[...]
    nw = nw_ref[...]

    chains = [(s, h) for s in range(n_sup) for h in range(DN_HEADS)]
    g_cum_all, g_cum_t = [], []
    for s in range(n_sup):
        g_blk = g_all[s * sup:(s + 1) * sup, :]
        gc = sum(jnp.dot(lower_ones, p, preferred_element_type=f32) for p in _split(g_blk, 2))
        g_cum_all.append(gc)
        g_cum_t.append(jnp.transpose(gc))

    st = {}
    for s, h in chains:
        r0 = s * sup
        q = act[r0:r0 + sup, h * hd:(h + 1) * hd]
        k = act[r0:r0 + sup, width + h * hd:width + (h + 1) * hd]
        v = act[r0:r0 + sup, 2 * width + h * hd:2 * width + (h + 1) * hd]
        beta_b = jnp.broadcast_to(beta_all[r0:r0 + sup, h:h + 1], (sup, hd))
        g_i = jnp.broadcast_to(g_cum_all[s][:, DN_HEADS + h:DN_HEADS + h + 1], (sup, hd))
        g_j = jnp.broadcast_to(g_cum_t[s][DN_HEADS + h:DN_HEADS + h + 1, :], (sup, sup))
        decay = jnp.where(incl, jnp.exp(jnp.minimum(g_i - g_j, 0.0)), 0.0)
        e_g = jnp.exp(g_i)
        g_last = jnp.where(first_rows, g_i[ch - 1:ch, :], g_i[sup - 1:sup, :])
        st[s, h] = dict(q=q, k=k, kb=k * beta_b, vb=v * beta_b, decay=decay, e_g=e_g,
                        k_tail=k * jnp.exp(g_last - g_i),
                        gl=(jnp.exp(g_i[ch - 1:ch, :]), jnp.exp(g_i[sup - 1:sup, :])))

    for c in chains:
        d = st[c]
        d["a"] = jnp.where(strict, _dot_nt(d["kb"], d["k"]) * d["decay"], 0.0)
        d["attn"] = jnp.where(incl, _dot_nt(d["q"], d["k"]) * d["decay"], 0.0)
    for c in chains:
        st[c]["t_inv"] = eye - st[c]["a"]
        st[c]["pw"] = st[c]["a"]
    for _ in range(5):
        for c in chains:
            st[c]["pw"] = _dot(st[c]["pw"], st[c]["pw"])
        for c in chains:
            st[c]["t_inv"] = st[c]["t_inv"] + _dot(st[c]["t_inv"], st[c]["pw"])
    for c in chains:
        d = st[c]
        sol = _dot(d["t_inv"], jnp.concatenate([d["vb"], d["kb"] * d["e_g"]], axis=1))
        d["u"], d["w"] = sol[:, :hd], sol[:, hd:]
    for c in chains:
        d = st[c]
        uw = jnp.concatenate([d["u"], d["w"]], axis=1).astype(bf16)
        mix = jnp.dot(d["attn"].astype(bf16), uw, preferred_element_type=f32)
        d["au"] = mix[:, :hd]
        d["qe"] = d["q"] * d["e_g"] - mix[:, hd:]
        kt = d["k_tail"].astype(bf16)
        zero = jnp.zeros_like(kt)
        d["nu_pw"] = [_dot_tn(jnp.where(first_rows == first, kt, zero), uw)
                      for first in (True, False)]

    for s in range(n_sup):
        for half in range(2):
            for h in range(DN_HEADS):
                d = st[s, h]
                state = state_ref[h]
                lo = half * ch
                o = _dot(d["qe"][lo:lo + ch], state) + d["au"][lo:lo + ch]
                nu_pw = d["nu_pw"][half]
                state_ref[h] = state * d["gl"][half] - _dot(nu_pw[:, hd:], state) + nu_pw[:, :hd]
                o = o * lax.rsqrt(jnp.mean(o * o, -1, keepdims=True) + RMS_EPS) * nw
                r0 = s * sup + lo
                o_ref[r0:r0 + ch, h * hd:(h + 1) * hd] = (o * _silu(z[r0:r0 + ch, h * hd:(h + 1) * hd])).astype(o_ref.dtype)


def _deltanet(act, bg, z, norm_w, batch, seq):
    nb = seq // DN_BLOCK
    row = lambda b, c: (b * nb + c, 0)
    fixed = lambda b, c: (0, 0)
    return pl.pallas_call(
        _deltanet_kernel,
        grid=(batch, nb),
        in_specs=[
            pl.BlockSpec((DN_BLOCK, 3 * DN_WIDTH), row),
            pl.BlockSpec((DN_BLOCK, LANES), row),
            pl.BlockSpec((DN_BLOCK, DN_WIDTH), row),
            pl.BlockSpec((1, DN_HEAD_DIM), fixed),
        ],
        out_specs=pl.BlockSpec((DN_BLOCK, DN_WIDTH), row),
        out_shape=jax.ShapeDtypeStruct((batch * seq, DN_WIDTH), bf16),
        scratch_shapes=[pltpu.VMEM((DN_HEADS, DN_HEAD_DIM, DN_HEAD_DIM), f32)],
        compiler_params=_params(("arbitrary", "arbitrary")),
        name="deltanet",
    )(act, bg, z, norm_w)


def _dilated_kernel(q_ref, k_ref, v_ref, o_ref, qs_ref, ks_ref, vs_ref, ob_ref, lse_ref, s_ref, p_ref, st_ref, *, seq):
    blk, res = DA_BLOCK, DA_RESIDUES
    per = seq // res
    head0 = lax.broadcasted_iota(jnp.int32, (blk, LANES), 1) < DA_HEAD_DIM
    ai = lax.broadcasted_iota(jnp.int32, (2 * blk, blk), 0) & (blk - 1)
    bi = lax.broadcasted_iota(jnp.int32, (2 * blk, blk), 1)
    ones_blk = jnp.ones((2 * blk, LANES), bf16)

    q_scale = DA_HEAD_DIM ** -0.5 * LOG2_E
    for r in range(res):
        rows = pl.ds(r, per, stride=res)
        qs_ref[r * per:(r + 1) * per, :] = q_ref[0, rows, :] * q_scale
        ks_ref[r * per:(r + 1) * per, :] = k_ref[0, rows, :]
        vs_ref[r * per:(r + 1) * per, :] = v_ref[0, rows, :]

    for br, (window, dil) in enumerate(DA_BRANCHES):
        assert window // dil == blk and res % dil == 0
        runs = res // dil
        run_len = blk // runs
        nblk = seq // (dil * blk)
        shift = run_len.bit_length() - 1

        def pos(a, runs=runs, run_len=run_len, shift=shift):
            return runs * (a & (run_len - 1)) + (a >> shift)

        in_window = jnp.concatenate([pos(bi) >= pos(ai), pos(bi) <= pos(ai)], axis=1)

        n_iter = dil * nblk // DA_UNROLL
        assert n_iter * DA_UNROLL == dil * nblk

        def tile(ref, offs, run_len=run_len, runs=runs):
            parts = [ref[pl.ds(o, run_len), :] for o in offs]
            return parts[0] if runs == 1 else jnp.concatenate(parts, axis=0)

        def blocks_of(it, dil=dil, runs=runs, run_len=run_len, nblk=nblk):
            out = []
            for u in range(DA_UNROLL):
                idx = it * DA_UNROLL + u
                cls = idx // nblk
                n = idx - cls * nblk
                n_prev = jnp.maximum(n - 1, 0)
                cur = [pl.multiple_of((dil * m + cls) * per + run_len * n, SUBLANES) for m in range(runs)]
                prev = [pl.multiple_of((dil * m + cls) * per + run_len * n_prev, SUBLANES) for m in range(runs)]
                out.append((cur, prev, n > 0))
            return out

        def masked_scores(blocks, tile=tile, in_window=in_window):
            out = []
            for cur, prev, has_prev in blocks:
                q = tile(qs_ref, cur).astype(bf16)
                zero = jnp.zeros_like(q)
                q2 = jnp.concatenate([jnp.where(head0, q, zero), jnp.where(head0, zero, q)], axis=0)
                keys = jnp.concatenate([tile(ks_ref, prev), tile(ks_ref, cur)], axis=0)
                s = jnp.where(in_window, _dot_nt(q2, keys), NEG_BIG)
                out.append(jnp.concatenate([jnp.where(has_prev, s[:, :blk], NEG_BIG), s[:, blk:]], axis=1))
            return out

        def softmax_stage(blocks, scores):
            out = []
            for s in scores:
                top = jnp.max(s, -1, keepdims=True)
                out.append((jnp.exp2(s - top).astype(bf16), jnp.where(head0, top[:blk], top[blk:])))
            return out

        def output_stage(blocks, probs, br=br, run_len=run_len, tile=tile):
            for (cur, prev, has_prev), (pb, top) in zip(blocks, probs):
                vals = jnp.concatenate([tile(vs_ref, prev), tile(vs_ref, cur)], axis=0).astype(bf16)
                acc = jnp.dot(pb, jnp.concatenate([vals, ones_blk], axis=1), preferred_element_type=f32)
                den = jnp.where(head0, acc[:blk, LANES:], acc[blk:, LANES:])
                o_blk = jnp.where(head0, acc[:blk, :LANES], acc[blk:, :LANES]) / den
                lse_blk = top + jnp.log2(den)
                for m, o in enumerate(cur):
                    ob_ref[br, pl.ds(o, run_len), :] = o_blk[m * run_len:(m + 1) * run_len]
                    lse_ref[br, pl.ds(o, run_len), :] = lse_blk[m * run_len:(m + 1) * run_len]

        def put_scores(scores):
            for u, sc in enumerate(scores):
                s_ref[u] = sc

        def put_probs(probs):
            for u, (pb, top) in enumerate(probs):
                p_ref[u] = pb
                st_ref[u] = top

        put_probs(softmax_stage(blocks_of(0), masked_scores(blocks_of(0))))
        put_scores(masked_scores(blocks_of(min(1, n_iter - 1))))

        def body(it, carry, n_iter=n_iter, blocks_of=blocks_of, masked_scores=masked_scores,
                 softmax_stage=softmax_stage, output_stage=output_stage, put_scores=put_scores, put_probs=put_probs):
            scores = [s_ref[u] for u in range(DA_UNROLL)]
            probs = [(p_ref[u], st_ref[u]) for u in range(DA_UNROLL)]
            output_stage(blocks_of(it), probs)
            next_probs = softmax_stage(blocks_of(jnp.minimum(it + 1, n_iter - 1)), scores)
            next_scores = masked_scores(blocks_of(jnp.minimum(it + 2, n_iter - 1)))
            put_scores(next_scores)
            put_probs(next_probs)
            return carry

        lax.fori_loop(0, n_iter, body, 0)

    for r in range(res):
        rows = slice(r * per, (r + 1) * per)
        lses = [lse_ref[br, rows, :] for br in range(len(DA_BRANCHES))]
        top = functools.reduce(jnp.maximum, lses)
        wts = [jnp.exp2(l - top) for l in lses]
        num = sum(w * ob_ref[br, rows, :] for br, w in enumerate(wts))
        o_ref[0, pl.ds(r, per, stride=res), :] = num / sum(wts)


def _dilated_attention(qkv3, batch, seq):
    pairs = DA_WIDTH // LANES
    n_br = len(DA_BRANCHES)
    spec = lambda off: pl.BlockSpec((1, seq, LANES), lambda b, p, off=off: (b, 0, off + p))
    return pl.pallas_call(
        functools.partial(_dilated_kernel, seq=seq),
        grid=(batch, pairs),
        in_specs=[spec(0), spec(pairs), spec(2 * pairs)],
        out_specs=pl.BlockSpec((1, seq, LANES), lambda b, p: (b, 0, p)),
        out_shape=jax.ShapeDtypeStruct((batch, seq, DA_WIDTH), f32),
        scratch_shapes=[pltpu.VMEM((seq, LANES), f32)] * 3
        + [pltpu.VMEM((n_br, seq, LANES), f32)] * 2
        + [pltpu.VMEM((DA_UNROLL, 2 * DA_BLOCK, 2 * DA_BLOCK), f32),
           pltpu.VMEM((DA_UNROLL, 2 * DA_BLOCK, 2 * DA_BLOCK), bf16),
           pltpu.VMEM((DA_UNROLL, DA_BLOCK, LANES), f32)],
        compiler_params=_params(("parallel", "parallel")),
        name="dilated_attention",
    )(qkv3, qkv3, qkv3)


def _mix_out_kernel(oa_ref, ob_ref, h_ref, w_ref, g_ref, b_ref, o_ref):
    w = w_ref[...]
    y = jnp.dot(oa_ref[...], w[:DN_WIDTH], preferred_element_type=f32)
    y = y + jnp.dot(ob_ref[...].astype(bf16), w[DN_WIDTH:], preferred_element_type=f32)
    o_ref[...] = _layer_norm(DEEPNORM_ALPHA * h_ref[...] + y, g_ref[...], b_ref[...])


def _mix_out(o_a, o_b, h, w_out, g, b):
    t = h.shape[0]
    tm = TOKEN_TILE
    fixed = lambda i: (0, 0)
    return pl.pallas_call(
        _mix_out_kernel,
        grid=(t // tm,),
        in_specs=[
            pl.BlockSpec((tm, DN_WIDTH), lambda i: (i, 0)),
            pl.BlockSpec((tm, DA_WIDTH), lambda i: (i, 0)),
            pl.BlockSpec((tm, D_MODEL), lambda i: (i, 0)),
            pl.BlockSpec(w_out.shape, fixed),
            pl.BlockSpec((1, D_MODEL), fixed),
            pl.BlockSpec((1, D_MODEL), fixed),
        ],
        out_specs=pl.BlockSpec((tm, D_MODEL), lambda i: (i, 0)),
        out_shape=jax.ShapeDtypeStruct((t, D_MODEL), f32),
        compiler_params=_params(("parallel",)),
        name="mix_out",
    )(o_a, o_b, h, w_out, g, b)


def _shortconv_kernel(h_ref, win_ref, cw_ref, wout_ref, g_ref, b_ref, o_ref, tail_ref):
    @pl.when(pl.program_id(1) == 0)
    def _():
        tail_ref[...] = jnp.zeros_like(tail_ref)

    tm = h_ref.shape[0]
    d = D_MODEL
    h = h_ref[...]
    hb = h.astype(bf16)
    gate_b = jnp.dot(hb, win_ref[:, :d], preferred_element_type=f32)
    gate_c = jnp.dot(hb, win_ref[:, d:2 * d], preferred_element_type=f32)
    hid = jnp.dot(hb, win_ref[:, 2 * d:], preferred_element_type=f32)
    u = gate_c * hid
    uu = jnp.concatenate([tail_ref[...], u], axis=0)
    cw = cw_ref[...]
    conv = u * cw[SC_CONV - 1:SC_CONV, :]
    for j in range(SC_CONV - 1):
        back = SC_CONV - 1 - j
        conv = conv + uu[SUBLANES - back:SUBLANES - back + tm, :] * cw[j:j + 1, :]
    tail_ref[...] = u[tm - SUBLANES:, :]
    y = jnp.dot((gate_b * conv).astype(bf16), wout_ref[...], preferred_element_type=f32)
    o_ref[...] = _layer_norm(DEEPNORM_ALPHA * h + y, g_ref[...], b_ref[...])


def _shortconv(h, w_in, conv_w, w_out, g, b, batch, seq):
    tm = TOKEN_TILE
    ns = seq // tm
    row = lambda bb, s: (bb * ns + s, 0)
    fixed = lambda bb, s: (0, 0)
    return pl.pallas_call(
        _shortconv_kernel,
        grid=(batch, ns),
        in_specs=[
            pl.BlockSpec((tm, D_MODEL), row),
            pl.BlockSpec(w_in.shape, fixed),
            pl.BlockSpec(conv_w.shape, fixed),
            pl.BlockSpec(w_out.shape, fixed),
            pl.BlockSpec((1, D_MODEL), fixed),
            pl.BlockSpec((1, D_MODEL), fixed),
        ],
        out_specs=pl.BlockSpec((tm, D_MODEL), row),
        out_shape=jax.ShapeDtypeStruct((batch * seq, D_MODEL), f32),
        scratch_shapes=[pltpu.VMEM((SUBLANES, D_MODEL), f32)],
        compiler_params=_params(("arbitrary", "arbitrary")),
        name="shortconv",
    )(h, w_in, conv_w, w_out, g, b)


def _route(h2, rwt_ref, rb_ref, cnt_ref):
    tm = h2.shape[0]
    w_hi, w_lo = _split(rwt_ref[...], 2)
    x_hi, x_lo = _split(h2, 2)
    nt = (((1,), (1,)), ((), ()))
    logits = (lax.dot_general(w_hi, x_hi, nt, preferred_element_type=f32)
              + lax.dot_general(w_hi, x_lo, nt, preferred_element_type=f32)
              + lax.dot_general(w_lo, x_hi, nt, preferred_element_type=f32))
    mx = jnp.max(logits, axis=0, keepdims=True)
    ex = jnp.exp(logits - mx)
    scores = ex / jnp.sum(ex, axis=0, keepdims=True)
    sel = scores + rb_ref[...]
    best = jnp.full((1, tm), -jnp.inf, f32)
    bucket = jnp.zeros((1, tm), jnp.int32)
    s_a = jnp.zeros((1, tm), f32)
    s_b = jnp.zeros((1, tm), f32)
    for grp in range(N_GROUPS):
        for p, (i, j) in enumerate(_PAIRS):
            a, b = grp * EXPERTS_PER_GROUP + i, grp * EXPERTS_PER_GROUP + j
            ps = sel[a:a + 1, :] + sel[b:b + 1, :]
            upd = ps > best
            best = jnp.where(upd, ps, best)
            bucket = jnp.where(upd, grp * len(_PAIRS) + p, bucket)
            s_a = jnp.where(upd, scores[a:a + 1, :], s_a)
            s_b = jnp.where(upd, scores[b:b + 1, :], s_b)
    denom = s_a + s_b
    gate_rows = jnp.concatenate([s_a / denom, s_b / denom, jnp.zeros((GATE_LANES - 2, tm), f32)], axis=0)
    gates = jnp.transpose(gate_rows)
    onehot = jnp.where(lax.broadcasted_iota(jnp.int32, (BUCKET_ROWS, tm), 0) == bucket, 1.0, 0.0)
    before = lax.broadcasted_iota(jnp.int32, (tm, tm), 0) < lax.broadcasted_iota(jnp.int32, (tm, tm), 1)
    prefix = jnp.dot(onehot.astype(bf16), jnp.where(before, 1.0, 0.0).astype(bf16), preferred_element_type=f32)
    cnt = cnt_ref[...]
    rank = jnp.sum(onehot * (prefix + cnt[:, 0:1]), axis=0, keepdims=True)
    cnt_ref[...] = cnt + jnp.sum(onehot, axis=1, keepdims=True)
    return gates, bucket, rank.astype(jnp.int32)


def _xattn_kernel(h_ref, kv_ref, wq_ref, wo_ref, g_ref, b_ref, rwt_ref, rb_ref,
                  hext_ref, route_ref, cnt_out_ref, cnt_ref):
    @pl.when(jnp.logical_and(pl.program_id(0) == 0, pl.program_id(1) == 0))
    def _():
        cnt_ref[...] = jnp.zeros_like(cnt_ref)

    h = h_ref[...]
    tm = h.shape[0]
    q = jnp.dot(h.astype(bf16), wq_ref[...], preferred_element_type=f32).astype(bf16)
    heads = [(hd * XA_HEAD_DIM, (hd + 1) * XA_HEAD_DIM) for hd in range(XA_HEADS)]
    scores = [_dot_nt(q[:, lo:hi], kv_ref[:, lo:hi]) for lo, hi in heads]
    probs = [jnp.exp2(s - jnp.max(s, -1, keepdims=True)) for s in scores]
    outs = [(_dot(p, kv_ref[:, D_MODEL + lo:D_MODEL + hi]) / jnp.sum(p, -1, keepdims=True)).astype(bf16)
            for p, (lo, hi) in zip(probs, heads)]
    y = jnp.dot(jnp.concatenate(outs, axis=1), wo_ref[...], preferred_element_type=f32)
    h2 = _layer_norm(DEEPNORM_ALPHA * h + y, g_ref[...], b_ref[...])
    gates, bucket, rank = _route(h2, rwt_ref, rb_ref, cnt_ref)
    hext_ref[:, :D_MODEL] = h2
    hext_ref[:, D_MODEL:] = gates
    route_ref[...] = jnp.concatenate([bucket, rank, jnp.zeros((SUBLANES - 2, tm), jnp.int32)], axis=0)
    cnt_out_ref[...] = cnt_ref[...]


def _xattn_route(h, kv, w_q, w_o, g, b, rwt, rb, batch, seq, n_mem):
    tm = XA_TILE
    ns = seq // tm
    t = batch * seq
    row = lambda bb, s: (bb * ns + s, 0)
    fixed = lambda bb, s: (0, 0)
    return pl.pallas_call(
        _xattn_kernel,
        grid=(batch, ns),
        in_specs=[
            pl.BlockSpec((tm, D_MODEL), row),
            pl.BlockSpec((n_mem, 2 * D_MODEL), lambda bb, s: (bb, 0)),
            pl.BlockSpec(w_q.shape, fixed),
            pl.BlockSpec(w_o.shape, fixed),
            pl.BlockSpec((1, D_MODEL), fixed),
            pl.BlockSpec((1, D_MODEL), fixed),
            pl.BlockSpec(rwt.shape, fixed),
            pl.BlockSpec(rb.shape, fixed),
        ],
        out_specs=[
            pl.BlockSpec((tm, ROW_EXT), row),
            pl.BlockSpec((SUBLANES, tm), lambda bb, s: (0, bb * ns + s)),
            pl.BlockSpec((BUCKET_ROWS, LANES), fixed),
        ],
        out_shape=[
            jax.ShapeDtypeStruct((t, ROW_EXT), f32),
            jax.ShapeDtypeStruct((SUBLANES, t), jnp.int32),
            jax.ShapeDtypeStruct((BUCKET_ROWS, LANES), f32),
        ],
        scratch_shapes=[pltpu.VMEM((BUCKET_ROWS, LANES), f32)],
        compiler_params=_params(("arbitrary", "arbitrary")),
        name="xattn_route",
    )(h, kv, w_q, w_o, g, b, rwt, rb)


def _dispatch_kernel(dest_ref, fill_ref, hext_ref, xs_ref, inv_ref, zeros_ref, sem, zsem):
    base = pl.program_id(0) * DISPATCH_CHUNK
    n_tiles = xs_ref.shape[0] // MOE_TILE

    @pl.when(pl.program_id(0) == 0)
    def _():
        zeros_ref[...] = jnp.zeros_like(zeros_ref)

        def fill(tile):
            return pltpu.make_async_copy(zeros_ref, xs_ref.at[pl.ds(tile * MOE_TILE, MOE_TILE)], zsem)

        for tile in range(n_tiles):
            @pl.when(fill_ref[tile] > 0)
            def _(tile=tile):
                fill(tile).start()

        def clear(j8, carry):
            for u in range(8):
                inv_ref[j8 * 8 + u] = 0
            return carry

        lax.fori_loop(0, inv_ref.shape[0] // 8, clear, 0)
        for tile in range(n_tiles):
            @pl.when(fill_ref[tile] > 0)
            def _(tile=tile):
                fill(tile).wait()

    for j in range(DISPATCH_CHUNK):
        d = dest_ref[base + j]
        inv_ref[d] = base + j
        pltpu.make_async_copy(hext_ref.at[pl.ds(j, 1)], xs_ref.at[pl.ds(d, 1)], sem).start(priority=j % 2)
    pltpu.make_async_copy(hext_ref, xs_ref.at[pl.ds(0, DISPATCH_CHUNK)], sem).wait()


def _dispatch(dest, fill, hext, n_rows):
    t = hext.shape[0]
    return pl.pallas_call(
        _dispatch_kernel,
        grid_spec=pltpu.PrefetchScalarGridSpec(
            num_scalar_prefetch=2,
            grid=(t // DISPATCH_CHUNK,),
            in_specs=[pl.BlockSpec((DISPATCH_CHUNK, ROW_EXT), lambda i, *_: (i, 0))],
            out_specs=[pl.BlockSpec(memory_space=pl.ANY), pl.BlockSpec(memory_space=pltpu.SMEM)],
            scratch_shapes=[pltpu.VMEM((MOE_TILE, ROW_EXT), f32), pltpu.SemaphoreType.DMA(()),
                            pltpu.SemaphoreType.DMA(())],
        ),
        out_shape=[jax.ShapeDtypeStruct((n_rows, ROW_EXT), f32), jax.ShapeDtypeStruct((n_rows,), jnp.int32)],
        compiler_params=_params(("arbitrary",)),
        name="moe_dispatch",
    )(dest, fill, hext)


def _experts_kernel(ea_ref, eb_ref, nvalid_ref, xs_idx_ref, inv_ref, xs_ref, wga_ref, wua_ref, wda_ref,
                    wgb_ref, wub_ref, wdb_ref, g_ref, b_ref, out_ref, obuf_ref, sem_ref):
    i = pl.program_id(0)
    tm = MOE_TILE
    prev = jnp.maximum(i - 1, 0)
    n_prev = jnp.where(i > 0, nvalid_ref[prev], 0)
    slot_prev = prev % 2

    def send(j):
        return pltpu.make_async_copy(obuf_ref.at[slot_prev, pl.ds(j, 1)],
                                     out_ref.at[pl.ds(inv_ref[prev * tm + j], 1)], sem_ref.at[slot_prev])

    n_cur = jnp.where(i < pl.num_programs(0) - 1, nvalid_ref[jnp.minimum(i, pl.num_programs(0) - 2)], 0)

    def send_full(lo, hi):
        for j in range(lo, hi):
            send(j).start(priority=j % 2)

    def compute(between):
        x32 = xs_ref[:, :D_MODEL]
        x = x32.astype(bf16)
        gates = xs_ref[:, D_MODEL:]

        def hidden(wg_ref, wu_ref, gate):
            hg = jnp.dot(x, wg_ref[0, 0].astype(bf16), preferred_element_type=f32)
            hu = jnp.dot(x, wu_ref[0, 0].astype(bf16), preferred_element_type=f32)
            return (_silu(hg) * hu * gate).astype(bf16)

        between(0)
        h_a = hidden(wga_ref, wua_ref, gates[:, 0:1])
        between(1)
        h_b = hidden(wgb_ref, wub_ref, gates[:, 1:2])
        between(2)
        y = (jnp.dot(h_a, wda_ref[0, 0].astype(bf16), preferred_element_type=f32)
             + jnp.dot(h_b, wdb_ref[0, 0].astype(bf16), preferred_element_type=f32))
        between(3)
        obuf_ref[i % 2] = _layer_norm(DEEPNORM_ALPHA * x32 + y, g_ref[...], b_ref[...])

    usual = jnp.logical_and(n_prev == tm, n_cur > 0)
    quarter = tm // 4

    @pl.when(usual)
    def _():
        compute(lambda k: send_full(k * quarter, (k + 1) * quarter))

    @pl.when(jnp.logical_not(usual))
    def _():
        @pl.when(n_prev == tm)
        def _():
            send_full(0, tm)

        @pl.when(jnp.logical_and(n_prev > 0, n_prev < tm))
        def _():
            for j in range(tm):
                @pl.when(j < n_prev)
                def _(j=j):
                    send(j).start(priority=j % 2)

        @pl.when(n_cur > 0)
        def _():
            compute(lambda k: None)

    for bit in reversed(range(tm.bit_length())):
        size = 1 << bit

        @pl.when((n_prev & size) != 0)
        def _(size=size):
            pltpu.make_async_copy(obuf_ref.at[slot_prev, pl.ds(0, size)], out_ref.at[pl.ds(0, size)],
                                  sem_ref.at[slot_prev]).wait()


def _experts(layer, ea, eb, nvalid, xs_idx, inv, xs, w_gate, w_up, w_down, g, b, t):
    n_tiles = xs.shape[0] // MOE_TILE
    up_a = pl.BlockSpec((1, 1, D_MODEL, D_EXPERT), lambda i, ea, *_: (layer, ea[i], 0, 0))
    up_b = pl.BlockSpec((1, 1, D_MODEL, D_EXPERT), lambda i, ea, eb, *_: (layer, eb[i], 0, 0))
    dn_a = pl.BlockSpec((1, 1, D_EXPERT, D_MODEL), lambda i, ea, *_: (layer, ea[i], 0, 0))
    dn_b = pl.BlockSpec((1, 1, D_EXPERT, D_MODEL), lambda i, ea, eb, *_: (layer, eb[i], 0, 0))
    fixed = lambda i, *_: (0, 0)
    return pl.pallas_call(
        _experts_kernel,
        grid_spec=pltpu.PrefetchScalarGridSpec(
            num_scalar_prefetch=5,
            grid=(n_tiles + 1,),
            in_specs=[pl.BlockSpec((MOE_TILE, ROW_EXT), lambda i, ea, eb, nv, xs_idx, inv: (xs_idx[i], 0)),
                      up_a, up_a, dn_a, up_b, up_b, dn_b,
                      pl.BlockSpec((1, D_MODEL), fixed), pl.BlockSpec((1, D_MODEL), fixed)],
            out_specs=pl.BlockSpec(memory_space=pl.ANY),
            scratch_shapes=[pltpu.VMEM((2, MOE_TILE, D_MODEL), f32), pltpu.SemaphoreType.DMA((2,))],
        ),
        out_shape=jax.ShapeDtypeStruct((t, D_MODEL), f32),
        compiler_params=_params(("arbitrary",)),
        name="moe_experts",
    )(ea, eb, nvalid, xs_idx, inv, xs, w_gate, w_up, w_down, w_gate, w_up, w_down, g, b)


def _moe_plan(route, counts):
    t = route.shape[1]
    n_rows = t + N_BUCKETS * MOE_TILE
    n_tiles = n_rows // MOE_TILE
    cnt = counts[:N_BUCKETS, 0].astype(jnp.int32)
    padded = (cnt + MOE_TILE - 1) // MOE_TILE * MOE_TILE
    ends = jnp.cumsum(padded)
    starts = ends - padded
    buckets = jnp.arange(N_BUCKETS, dtype=jnp.int32)
    dest = jnp.sum(jnp.where(route[0][None, :] == buckets[:, None], starts[:, None], 0), axis=0) + route[1]
    tile_start = jnp.arange(n_tiles + 1, dtype=jnp.int32) * MOE_TILE
    tile_bucket = jnp.sum(tile_start[:, None] >= ends[None, :], axis=1).astype(jnp.int32)
    used = tile_bucket < N_BUCKETS
    tile_bucket = jnp.minimum(tile_bucket, N_BUCKETS - 1)
    nvalid = jnp.where(used, jnp.clip(starts[tile_bucket] + cnt[tile_bucket] - tile_start, 0, MOE_TILE), 0)
    xs_idx = jnp.minimum(jnp.arange(n_tiles + 1, dtype=jnp.int32), jnp.maximum(ends[-1] // MOE_TILE - 1, 0))
    last_bucket = tile_bucket[jnp.maximum(ends[-1] // MOE_TILE - 1, 0)]
    tile_bucket = jnp.where(used, tile_bucket, last_bucket)
    pair_a = jnp.array([g * EXPERTS_PER_GROUP + i for g in range(N_GROUPS) for i, _ in _PAIRS], jnp.int32)
    pair_b = jnp.array([g * EXPERTS_PER_GROUP + j for g in range(N_GROUPS) for _, j in _PAIRS], jnp.int32)
    tiles = jnp.arange(n_tiles, dtype=jnp.int32)
    is_last = jnp.any(jnp.logical_and((tiles[:, None] + 1) * MOE_TILE == ends[None, :], padded[None, :] > 0), axis=1)
    fill = jnp.logical_or(is_last, tiles * MOE_TILE >= ends[-1]).astype(jnp.int32)
    return dict(dest=dest, ea=pair_a[tile_bucket], eb=pair_b[tile_bucket], nvalid=nvalid.astype(jnp.int32),
                xs_idx=xs_idx, fill=fill, n_rows=n_rows)


def _moe(layer, hext, route, counts, w_gate, w_up, w_down, g, b):
    plan = _moe_plan(route, counts)
    xs, inv = _dispatch(plan["dest"], plan["fill"], hext, plan["n_rows"])
    return _experts(layer, plan["ea"], plan["eb"], plan["nvalid"], plan["xs_idx"], inv, xs,
                    w_gate, w_up, w_down, g, b, hext.shape[0])


def kernel(x, mem, ab_w_in, ab_conv_w, ab_a_log, ab_dt_bias, ab_norm_w, ab_w_out, sc_w_in, sc_conv_w, sc_w_out, xa_w_q, xa_w_kv, xa_w_o, router_w, router_b, moe_w_gate, moe_w_up, moe_w_down, ln_g, ln_b):
    batch, seq, d = x.shape
    n_mem = mem.shape[1]
    t = batch * seq
    h = x.reshape(t, d)
    mem2 = mem.reshape(batch * n_mem, d)
    rwt = jnp.transpose(router_w)
    rb = router_b.reshape(N_EXPERTS, 1)
    row = lambda v: v.reshape(1, -1)

    for layer in range(DEPTH):
        i = layer // 2
        if layer % 2 == 0:
            w_in = ab_w_in[i]
            c0, c1, c2 = 3 * DN_WIDTH, 4 * DN_WIDTH, 4 * DN_WIDTH + 2 * DN_HEADS
            w_ba = jnp.pad(w_in[:, c1:c2], ((0, 0), (0, LANES - 2 * DN_HEADS)))
            lane_pad = lambda v: jnp.pad(v.reshape(1, DN_HEADS), ((0, 0), (DN_HEADS, LANES - 2 * DN_HEADS)))
            act, z, bg, qkv_d = _project_ab(h, w_in[:, :c0].astype(bf16), w_in[:, c0:c1].astype(bf16), w_ba.astype(bf16),
                                            w_in[:, c2:].astype(bf16), ab_conv_w[i], lane_pad(ab_a_log[i]),
                                            lane_pad(ab_dt_bias[i]), seq)
            o_a = _deltanet(act, bg, z, row(ab_norm_w[i]), batch, seq)
            o_b = _dilated_attention(qkv_d.reshape(batch, seq, 3 * DA_WIDTH), batch, seq).reshape(t, DA_WIDTH)
            h = _mix_out(o_a, o_b, h, ab_w_out[i].astype(bf16), row(ln_g[layer, 0]), row(ln_b[layer, 0]))
        else:
            h = _shortconv(h, sc_w_in[i].astype(bf16), sc_conv_w[i], sc_w_out[i].astype(bf16),
                           row(ln_g[layer, 0]), row(ln_b[layer, 0]), batch, seq)
        (kv,) = _project(mem2, [xa_w_kv[layer].astype(bf16)], [bf16], n_mem)
        w_q = (xa_w_q[layer] * (XA_HEAD_DIM ** -0.5 * LOG2_E)).astype(bf16)
        hext, route, counts = _xattn_route(h, kv, w_q, xa_w_o[layer].astype(bf16),
                                           row(ln_g[layer, 1]), row(ln_b[layer, 1]), rwt, rb, batch, seq, n_mem)
        h = _moe(layer, hext, route, counts, moe_w_gate, moe_w_up, moe_w_down,
                 row(ln_g[layer, 2]), row(ln_b[layer, 2]))
    return h.reshape(batch, seq, d)
```

```python
import functools

import jax
import jax.numpy as jnp
from jax import lax
from jax.experimental import pallas as pl
from jax.experimental.pallas import tpu as pltpu

D_MODEL = 1024
DEPTH = 2
DN_HEADS = 4
DN_HEAD_DIM = 128
DN_WIDTH = DN_HEADS * DN_HEAD_DIM
DN_CONV = 4
DN_CHUNK = 64
DA_HEADS = 8
DA_HEAD_DIM = 64
DA_WIDTH = DA_HEADS * DA_HEAD_DIM
DA_BRANCHES = ((128, 1), (512, 4), (2048, 16))
DA_BLOCK = 128
DA_RESIDUES = max(d for _, d in DA_BRANCHES)
SC_CONV = 3
XA_HEADS = 4
XA_HEAD_DIM = D_MODEL // XA_HEADS
N_EXPERTS = 16
N_GROUPS = 4
EXPERTS_PER_GROUP = N_EXPERTS // N_GROUPS
D_EXPERT = D_MODEL // 2
DEEPNORM_ALPHA = (2 * DEPTH) ** 0.25
LN_EPS = 1e-5
RMS_EPS = 1e-6

LANES = 128
SUBLANES = 8
VMEM_LIMIT_BYTES = 48 * 1024 * 1024

TOKEN_TILE = 512
XA_TILE = 1024
SC_TILE = 1024
DN_BLOCK = 256
MOE_TILE = 256
DISPATCH_CHUNK = 512
DA_UNROLL = 4
GATE_LANES = LANES
ROW_EXT = D_MODEL + GATE_LANES

_PAIRS = tuple((i, j) for i in range(EXPERTS_PER_GROUP) for j in range(i + 1, EXPERTS_PER_GROUP))
N_BUCKETS = N_GROUPS * len(_PAIRS)
BUCKET_ROWS = 32

NEG_BIG = -1e30
LOG2_E = 1.4426950408889634

bf16 = jnp.bfloat16
f32 = jnp.float32


def _params(semantics):
    return pltpu.CompilerParams(dimension_semantics=semantics, vmem_limit_bytes=VMEM_LIMIT_BYTES)


def _dot(a, b):
    return jnp.dot(a.astype(bf16), b.astype(bf16), preferred_element_type=f32)


def _dot_nt(a, b):
    return lax.dot_general(a.astype(bf16), b.astype(bf16), (((1,), (1,)), ((), ())), preferred_element_type=f32)


def _dot_tn(a, b):
    return lax.dot_general(a.astype(bf16), b.astype(bf16), (((0,), (0,)), ((), ())), preferred_element_type=f32)


def _split(a, parts):
    out = []
    rem = a
    for _ in range(parts):
        p = rem.astype(bf16)
        out.append(p)
        rem = rem - p.astype(f32)
    return out


def _layer_norm(v, g, b):
    mu = jnp.mean(v, -1, keepdims=True)
    c = v - mu
    var = jnp.mean(c * c, -1, keepdims=True)
    return c * lax.rsqrt(var + LN_EPS) * g + b


def _silu(v):
    return v * jax.nn.sigmoid(v)


def _softplus(v):
    return jnp.maximum(v, 0.0) + jnp.log(1.0 + jnp.exp(-jnp.abs(v)))


def _proj_kernel(x_ref, *refs, n_out):
    w_refs, o_refs = refs[:n_out], refs[n_out:]
    xb = x_ref[...].astype(bf16)
    for w_ref, o_ref in zip(w_refs, o_refs):
        n = w_ref.shape[1]
        for c0 in range(0, n, 512):
            c1 = min(c0 + 512, n)
            o_ref[:, c0:c1] = jnp.dot(xb, w_ref[:, c0:c1], preferred_element_type=f32).astype(o_ref.dtype)


def _project(x, ws, out_dtypes, tm):
    t, k = x.shape
    n_out = len(ws)
    return pl.pallas_call(
        functools.partial(_proj_kernel, n_out=n_out),
        grid=(t // tm,),
        in_specs=[pl.BlockSpec((tm, k), lambda i: (i, 0))]
        + [pl.BlockSpec(w.shape, lambda i: (0, 0)) for w in ws],
        out_specs=[pl.BlockSpec((tm, w.shape[1]), lambda i: (i, 0)) for w in ws],
        out_shape=[jax.ShapeDtypeStruct((t, w.shape[1]), dt) for w, dt in zip(ws, out_dtypes)],
        compiler_params=_params(("parallel",)),
        name="project",
    )(x, *ws)


def _proj_ab_kernel(x_ref, wqkv_ref, wz_ref, wba_ref, wd_ref, cw_ref, alog_ref, dtb_ref,
                    act_ref, z_ref, bg_ref, qkvd_ref, xx_ref, *, tiles_per_seq):
    @pl.when(pl.program_id(0) % tiles_per_seq == 0)
    def _():
        xx_ref[0:SUBLANES, :] = jnp.zeros((SUBLANES, xx_ref.shape[1]), f32)

    tm, hd = x_ref.shape[0], DN_HEAD_DIM
    xb = x_ref[...].astype(bf16)
    cw = cw_ref[...]

    def conv_act(c):
        cols = slice(c * hd, (c + 1) * hd)
        conv = xx_ref[SUBLANES:, cols] * cw[DN_CONV - 1:DN_CONV, cols]
        for j in range(DN_CONV - 1):
            back = DN_CONV - 1 - j
            conv = conv + xx_ref[pl.ds(SUBLANES - back, tm), cols] * cw[j:j + 1, cols]
        a = _silu(conv)
        if c < 2 * DN_HEADS:
            scale = hd ** -0.5 if c < DN_HEADS else 1.0
            a = a * (lax.rsqrt(jnp.sum(a * a, -1, keepdims=True) + RMS_EPS) * scale)
        act_ref[:, cols] = a

    for c0 in range(0, 3 * DN_WIDTH, 512):
        xx_ref[SUBLANES:, c0:c0 + 512] = jnp.dot(xb, wqkv_ref[:, c0:c0 + 512], preferred_element_type=f32)
        for c in range(c0 // hd, (c0 + 512) // hd):
            conv_act(c)
    z_ref[...] = jnp.dot(xb, wz_ref[...], preferred_element_type=f32)
    for c0 in range(0, 3 * DA_WIDTH, 512):
        qkvd_ref[:, c0:c0 + 512] = jnp.dot(xb, wd_ref[:, c0:c0 + 512], preferred_element_type=f32)
    ba = jnp.dot(xb, wba_ref[...], preferred_element_type=f32)
    is_beta = lax.broadcasted_iota(jnp.int32, ba.shape, 1) < DN_HEADS
    bg_ref[...] = jnp.where(is_beta, jax.nn.sigmoid(ba), -jnp.exp(alog_ref[...]) * _softplus(ba + dtb_ref[...]))
    xx_ref[0:SUBLANES, :] = xx_ref[tm:tm + SUBLANES, :]


def _project_ab(x, w_qkv, w_z, w_ba, w_d, conv_w, alog_row, dtb_row, seq):
    t, k = x.shape
    tm = TOKEN_TILE
    fixed = lambda i: (0, 0)
    row = lambda i: (i, 0)
    widths = (3 * DN_WIDTH, DN_WIDTH, LANES, 3 * DA_WIDTH)
    return pl.pallas_call(
        functools.partial(_proj_ab_kernel, tiles_per_seq=seq // tm),
        grid=(t // tm,),
        in_specs=[pl.BlockSpec((tm, k), row)]
        + [pl.BlockSpec(w.shape, fixed) for w in (w_qkv, w_z, w_ba, w_d, conv_w, alog_row, dtb_row)],
        out_specs=[pl.BlockSpec((tm, n), row) for n in widths],
        out_shape=[jax.ShapeDtypeStruct((t, n), f32) for n in widths],
        scratch_shapes=[pltpu.VMEM((SUBLANES + tm, 3 * DN_WIDTH), f32)],
        compiler_params=_params(("arbitrary",)),
        name="project_ab",
    )(x, w_qkv, w_z, w_ba, w_d, conv_w, alog_row, dtb_row)


def _deltanet_kernel(act_ref, bg_ref, z_ref, nw_ref, o_ref, state_ref):
    @pl.when(pl.program_id(1) == 0)
    def _():
        state_ref[...] = jnp.zeros_like(state_ref)

    rows, ch, hd, width = DN_BLOCK, DN_CHUNK, DN_HEAD_DIM, DN_WIDTH
    sup = 2 * ch
    n_sup = rows // sup

    act = act_ref[...]
    beta_all = bg_ref[...]
    g_all = beta_all

    ri = lax.broadcasted_iota(jnp.int32, (sup, sup), 0)
    ci = lax.broadcasted_iota(jnp.int32, (sup, sup), 1)
    same = (ri >= ch) == (ci >= ch)
    incl = jnp.logical_and(ri >= ci, same)
    strict = jnp.logical_and(ri > ci, same)
    lower_ones = jnp.where(incl, 1.0, 0.0).astype(bf16)
    eye = jnp.where(ri == ci, 1.0, 0.0)
    first_rows = lax.broadcasted_iota(jnp.int32, (sup, hd), 0) < ch
    z = z_ref[...]
    nw = nw_ref[...]

    chains = [(s, h) for s in range(n_sup) for h in range(DN_HEADS)]
    g_cum_all, g_cum_t = [], []
    for s in range(n_sup):
        g_blk = g_all[s * sup:(s + 1) * sup, :]
        gc = sum(jnp.dot(lower_ones, p, preferred_element_type=f32) for p in _split(g_blk, 2))
        g_cum_all.append(gc)
        g_cum_t.append(jnp.transpose(gc))

    st = {}
    for s, h in chains:
        r0 = s * sup
        q = act[r0:r0 + sup, h * hd:(h + 1) * hd]
        k = act[r0:r0 + sup, width + h * hd:width + (h + 1) * hd]
        v = act[r0:r0 + sup, 2 * width + h * hd:2 * width + (h + 1) * hd]
        beta_b = jnp.broadcast_to(beta_all[r0:r0 + sup, h:h + 1], (sup, hd))
        g_i = jnp.broadcast_to(g_cum_all[s][:, DN_HEADS + h:DN_HEADS + h + 1], (sup, hd))
        g_j = jnp.broadcast_to(g_cum_t[s][DN_HEADS + h:DN_HEADS + h + 1, :], (sup, sup))
        decay = jnp.where(incl, jnp.exp(jnp.minimum(g_i - g_j, 0.0)), 0.0)
        e_g = jnp.exp(g_i)
        g_last = jnp.where(first_rows, g_i[ch - 1:ch, :], g_i[sup - 1:sup, :])
        st[s, h] = dict(q=q, k=k, kb=k * beta_b, vb=v * beta_b, decay=decay, e_g=e_g,
                        k_tail=k * jnp.exp(g_last - g_i),
                        gl=(jnp.exp(g_i[ch - 1:ch, :]), jnp.exp(g_i[sup - 1:sup, :])))

    for c in chains:
        d = st[c]
        d["a"] = jnp.where(strict, _dot_nt(d["kb"], d["k"]) * d["decay"], 0.0)
        d["attn"] = jnp.where(incl, _dot_nt(d["q"], d["k"]) * d["decay"], 0.0)
    for c in chains:
        st[c]["t_inv"] = eye - st[c]["a"]
        st[c]["pw"] = st[c]["a"]
    for _ in range(5):
        for c in chains:
            st[c]["pw"] = _dot(st[c]["pw"], st[c]["pw"])
        for c in chains:
            st[c]["t_inv"] = st[c]["t_inv"] + _dot(st[c]["t_inv"], st[c]["pw"])
    for c in chains:
        d = st[c]
        sol = _dot(d["t_inv"], jnp.concatenate([d["vb"], d["kb"] * d["e_g"]], axis=1))
        d["u"], d["w"] = sol[:, :hd], sol[:, hd:]
    for c in chains:
        d = st[c]
        uw = jnp.concatenate([d["u"], d["w"]], axis=1).astype(bf16)
        mix = jnp.dot(d["attn"].astype(bf16), uw, preferred_element_type=f32)
        d["au"] = mix[:, :hd]
        d["qe"] = d["q"] * d["e_g"] - mix[:, hd:]
        kt = d["k_tail"].astype(bf16)
        zero = jnp.zeros_like(kt)
        d["nu_pw"] = [_dot_tn(jnp.where(first_rows == first, kt, zero), uw)
                      for first in (True, False)]

    for s in range(n_sup):
        for half in range(2):
            for h in range(DN_HEADS):
                d = st[s, h]
                state = state_ref[h]
                lo = half * ch
                o = _dot(d["qe"][lo:lo + ch], state) + d["au"][lo:lo + ch]
                nu_pw = d["nu_pw"][half]
                state_ref[h] = state * d["gl"][half] - _dot(nu_pw[:, hd:], state) + nu_pw[:, :hd]
                o = o * lax.rsqrt(jnp.mean(o * o, -1, keepdims=True) + RMS_EPS) * nw
                r0 = s * sup + lo
                o_ref[r0:r0 + ch, h * hd:(h + 1) * hd] = (o * _silu(z[r0:r0 + ch, h * hd:(h + 1) * hd])).astype(o_ref.dtype)


def _deltanet(act, bg, z, norm_w, batch, seq):
    nb = seq // DN_BLOCK
    row = lambda b, c: (b * nb + c, 0)
    fixed = lambda b, c: (0, 0)
    return pl.pallas_call(
        _deltanet_kernel,
        grid=(batch, nb),
        in_specs=[
            pl.BlockSpec((DN_BLOCK, 3 * DN_WIDTH), row),
            pl.BlockSpec((DN_BLOCK, LANES), row),
            pl.BlockSpec((DN_BLOCK, DN_WIDTH), row),
            pl.BlockSpec((1, DN_HEAD_DIM), fixed),
        ],
        out_specs=pl.BlockSpec((DN_BLOCK, DN_WIDTH), row),
        out_shape=jax.ShapeDtypeStruct((batch * seq, DN_WIDTH), bf16),
        scratch_shapes=[pltpu.VMEM((DN_HEADS, DN_HEAD_DIM, DN_HEAD_DIM), f32)],
        compiler_params=_params(("arbitrary", "arbitrary")),
        name="deltanet",
    )(act, bg, z, norm_w)


def _dilated_kernel(q_ref, k_ref, v_ref, o_ref, qs_ref, ks_ref, vs_ref, ob_ref, lse_ref, s_ref, p_ref, st_ref, *, seq):
    blk, res = DA_BLOCK, DA_RESIDUES
    per = seq // res
    head0 = lax.broadcasted_iota(jnp.int32, (blk, LANES), 1) < DA_HEAD_DIM
    ai = lax.broadcasted_iota(jnp.int32, (2 * blk, blk), 0) & (blk - 1)
    bi = lax.broadcasted_iota(jnp.int32, (2 * blk, blk), 1)
    ones_blk = jnp.ones((2 * blk, LANES), bf16)

    q_scale = DA_HEAD_DIM ** -0.5 * LOG2_E
    for r in range(res):
        rows = pl.ds(r, per, stride=res)
        qs_ref[r * per:(r + 1) * per, :] = q_ref[0, rows, :] * q_scale
        ks_ref[r * per:(r + 1) * per, :] = k_ref[0, rows, :]
        vs_ref[r * per:(r + 1) * per, :] = v_ref[0, rows, :]

    for br, (window, dil) in enumerate(DA_BRANCHES):
        assert window // dil == blk and res % dil == 0
        runs = res // dil
        run_len = blk // runs
        nblk = seq // (dil * blk)
        shift = run_len.bit_length() - 1

        def pos(a, runs=runs, run_len=run_len, shift=shift):
            return runs * (a & (run_len - 1)) + (a >> shift)

        in_window = jnp.concatenate([pos(bi) >= pos(ai), pos(bi) <= pos(ai)], axis=1)

        n_iter = dil * nblk // DA_UNROLL
        assert n_iter * DA_UNROLL == dil * nblk

        def tile(ref, offs, run_len=run_len, runs=runs):
            parts = [ref[pl.ds(o, run_len), :] for o in offs]
            return parts[0] if runs == 1 else jnp.concatenate(parts, axis=0)

        def blocks_of(it, dil=dil, runs=runs, run_len=run_len, nblk=nblk):
            out = []
            for u in range(DA_UNROLL):
                idx = it * DA_UNROLL + u
                cls = idx // nblk
                n = idx - cls * nblk
                n_prev = jnp.maximum(n - 1, 0)
                cur = [pl.multiple_of((dil * m + cls) * per + run_len * n, SUBLANES) for m in range(runs)]
                prev = [pl.multiple_of((dil * m + cls) * per + run_len * n_prev, SUBLANES) for m in range(runs)]
                out.append((cur, prev, n > 0))
            return out

        def masked_scores(blocks, tile=tile, in_window=in_window):
            out = []
            for cur, prev, has_prev in blocks:
                q = tile(qs_ref, cur).astype(bf16)
                zero = jnp.zeros_like(q)
                q2 = jnp.concatenate([jnp.where(head0, q, zero), jnp.where(head0, zero, q)], axis=0)
                keys = jnp.concatenate([tile(ks_ref, prev), tile(ks_ref, cur)], axis=0)
                s = jnp.where(in_window, _dot_nt(q2, keys), NEG_BIG)
                out.append(jnp.concatenate([jnp.where(has_prev, s[:, :blk], NEG_BIG), s[:, blk:]], axis=1))
            return out

        def softmax_stage(blocks, scores):
            out = []
            for s in scores:
                top = jnp.max(s, -1, keepdims=True)
                out.append((jnp.exp2(s - top).astype(bf16), jnp.where(head0, top[:blk], top[blk:])))
            return out

        def output_stage(blocks, probs, br=br, run_len=run_len, tile=tile):
            for (cur, prev, has_prev), (pb, top) in zip(blocks, probs):
                vals = jnp.concatenate([tile(vs_ref, prev), tile(vs_ref, cur)], axis=0).astype(bf16)
                acc = jnp.dot(pb, jnp.concatenate([vals, ones_blk], axis=1), preferred_element_type=f32)
                den = jnp.where(head0, acc[:blk, LANES:], acc[blk:, LANES:])
                o_blk = jnp.where(head0, acc[:blk, :LANES], acc[blk:, :LANES]) / den
                lse_blk = top + jnp.log2(den)
                for m, o in enumerate(cur):
                    ob_ref[br, pl.ds(o, run_len), :] = o_blk[m * run_len:(m + 1) * run_len]
                    lse_ref[br, pl.ds(o, run_len), :] = lse_blk[m * run_len:(m + 1) * run_len]

        def put_scores(scores):
            for u, sc in enumerate(scores):
                s_ref[u] = sc

        def put_probs(probs):
            for u, (pb, top) in enumerate(probs):
                p_ref[u] = pb
                st_ref[u] = top

        put_probs(softmax_stage(blocks_of(0), masked_scores(blocks_of(0))))
        put_scores(masked_scores(blocks_of(min(1, n_iter - 1))))

        def body(it, carry, n_iter=n_iter, blocks_of=blocks_of, masked_scores=masked_scores,
                 softmax_stage=softmax_stage, output_stage=output_stage, put_scores=put_scores, put_probs=put_probs):
            scores = [s_ref[u] for u in range(DA_UNROLL)]
            probs = [(p_ref[u], st_ref[u]) for u in range(DA_UNROLL)]
            output_stage(blocks_of(it), probs)
            next_probs = softmax_stage(blocks_of(jnp.minimum(it + 1, n_iter - 1)), scores)
            next_scores = masked_scores(blocks_of(jnp.minimum(it + 2, n_iter - 1)))
            put_scores(next_scores)
            put_probs(next_probs)
            return carry

        lax.fori_loop(0, n_iter, body, 0)

    for r in range(res):
        rows = slice(r * per, (r + 1) * per)
        lses = [lse_ref[br, rows, :] for br in range(len(DA_BRANCHES))]
        top = functools.reduce(jnp.maximum, lses)
        wts = [jnp.exp2(l - top) for l in lses]
        num = sum(w * ob_ref[br, rows, :] for br, w in enumerate(wts))
        o_ref[0, pl.ds(r, per, stride=res), :] = num / sum(wts)


def _dilated_attention(qkv3, batch, seq):
    pairs = DA_WIDTH // LANES
    n_br = len(DA_BRANCHES)
    spec = lambda off: pl.BlockSpec((1, seq, LANES), lambda b, p, off=off: (b, 0, off + p))
    return pl.pallas_call(
        functools.partial(_dilated_kernel, seq=seq),
        grid=(batch, pairs),
        in_specs=[spec(0), spec(pairs), spec(2 * pairs)],
        out_specs=pl.BlockSpec((1, seq, LANES), lambda b, p: (b, 0, p)),
        out_shape=jax.ShapeDtypeStruct((batch, seq, DA_WIDTH), f32),
        scratch_shapes=[pltpu.VMEM((seq, LANES), f32)] * 3
        + [pltpu.VMEM((n_br, seq, LANES), f32)] * 2
        + [pltpu.VMEM((DA_UNROLL, 2 * DA_BLOCK, 2 * DA_BLOCK), f32),
           pltpu.VMEM((DA_UNROLL, 2 * DA_BLOCK, 2 * DA_BLOCK), bf16),
           pltpu.VMEM((DA_UNROLL, DA_BLOCK, LANES), f32)],
        compiler_params=_params(("parallel", "parallel")),
        name="dilated_attention",
    )(qkv3, qkv3, qkv3)


def _shortconv_kernel(h_ref, win_ref, cw_ref, wout_ref, g_ref, b_ref, o_ref, tail_ref):
    @pl.when(pl.program_id(1) == 0)
    def _():
        tail_ref[...] = jnp.zeros_like(tail_ref)

    tm = h_ref.shape[0]
    d = D_MODEL
    h = h_ref[...]
    hb = h.astype(bf16)
    gate_b = jnp.dot(hb, win_ref[:, :d], preferred_element_type=f32)
    gate_c = jnp.dot(hb, win_ref[:, d:2 * d], preferred_element_type=f32)
    hid = jnp.dot(hb, win_ref[:, 2 * d:], preferred_element_type=f32)
    u = gate_c * hid
    uu = jnp.concatenate([tail_ref[...], u], axis=0)
    cw = cw_ref[...]
    conv = u * cw[SC_CONV - 1:SC_CONV, :]
    for j in range(SC_CONV - 1):
        back = SC_CONV - 1 - j
        conv = conv + uu[SUBLANES - back:SUBLANES - back + tm, :] * cw[j:j + 1, :]
    tail_ref[...] = u[tm - SUBLANES:, :]
    y = jnp.dot((gate_b * conv).astype(bf16), wout_ref[...], preferred_element_type=f32)
    o_ref[...] = _layer_norm(DEEPNORM_ALPHA * h + y, g_ref[...], b_ref[...])


def _shortconv(h, w_in, conv_w, w_out, g, b, batch, seq):
    tm = SC_TILE
    ns = seq // tm
    row = lambda bb, s: (bb * ns + s, 0)
    fixed = lambda bb, s: (0, 0)
    once = dict(pipeline_mode=pl.Buffered(1))
    return pl.pallas_call(
        _shortconv_kernel,
        grid=(batch, ns),
        in_specs=[
            pl.BlockSpec((tm, D_MODEL), row),
            pl.BlockSpec(w_in.shape, fixed, **once),
            pl.BlockSpec(conv_w.shape, fixed),
            pl.BlockSpec(w_out.shape, fixed, **once),
            pl.BlockSpec((1, D_MODEL), fixed),
            pl.BlockSpec((1, D_MODEL), fixed),
        ],
        out_specs=pl.BlockSpec((tm, D_MODEL), row),
        out_shape=jax.ShapeDtypeStruct((batch * seq, D_MODEL), f32),
        scratch_shapes=[pltpu.VMEM((SUBLANES, D_MODEL), f32)],
        compiler_params=_params(("arbitrary", "arbitrary")),
        name="shortconv",
    )(h, w_in, conv_w, w_out, g, b)


def _route(h2, rwt_ref, rb_ref, cnt_ref):
    tm = h2.shape[0]
    w_hi, w_lo = _split(rwt_ref[...], 2)
    x_hi, x_lo = _split(h2, 2)
    nt = (((1,), (1,)), ((), ()))
    logits = (lax.dot_general(w_hi, x_hi, nt, preferred_element_type=f32)
              + lax.dot_general(w_hi, x_lo, nt, preferred_element_type=f32)
              + lax.dot_general(w_lo, x_hi, nt, preferred_element_type=f32))
    mx = jnp.max(logits, axis=0, keepdims=True)
    ex = jnp.exp(logits - mx)
    scores = ex / jnp.sum(ex, axis=0, keepdims=True)
    sel = scores + rb_ref[...]
    best = jnp.full((1, tm), -jnp.inf, f32)
    bucket = jnp.zeros((1, tm), jnp.int32)
    s_a = jnp.zeros((1, tm), f32)
    s_b = jnp.zeros((1, tm), f32)
    for grp in range(N_GROUPS):
        for p, (i, j) in enumerate(_PAIRS):
            a, b = grp * EXPERTS_PER_GROUP + i, grp * EXPERTS_PER_GROUP + j
            ps = sel[a:a + 1, :] + sel[b:b + 1, :]
            upd = ps > best
            best = jnp.where(upd, ps, best)
            bucket = jnp.where(upd, grp * len(_PAIRS) + p, bucket)
            s_a = jnp.where(upd, scores[a:a + 1, :], s_a)
            s_b = jnp.where(upd, scores[b:b + 1, :], s_b)
    denom = s_a + s_b
    gate_rows = jnp.concatenate([s_a / denom, s_b / denom, jnp.zeros((GATE_LANES - 2, tm), f32)], axis=0)
    gates = jnp.transpose(gate_rows)
    onehot = jnp.where(lax.broadcasted_iota(jnp.int32, (BUCKET_ROWS, tm), 0) == bucket, 1.0, 0.0)
    before = lax.broadcasted_iota(jnp.int32, (tm, tm), 0) < lax.broadcasted_iota(jnp.int32, (tm, tm), 1)
    prefix = jnp.dot(onehot.astype(bf16), jnp.where(before, 1.0, 0.0).astype(bf16), preferred_element_type=f32)
    cnt = cnt_ref[...]
    rank = jnp.sum(onehot * (prefix + cnt[:, 0:1]), axis=0, keepdims=True)
    cnt_ref[...] = cnt + jnp.sum(onehot, axis=1, keepdims=True)
    return gates, bucket, rank.astype(jnp.int32)


def _xattn_kernel(*refs, mix):
    if mix:
        (oa_ref, ob_ref, wmix_ref, g0_ref, b0_ref), refs = refs[:5], refs[5:]
    (h_ref, kv_ref, wq_ref, wo_ref, g_ref, b_ref, rwt_ref, rb_ref,
     hext_ref, route_ref, cnt_out_ref, cnt_ref) = refs

    @pl.when(jnp.logical_and(pl.program_id(0) == 0, pl.program_id(1) == 0))
    def _():
        cnt_ref[...] = jnp.zeros_like(cnt_ref)

    h = h_ref[...]
    tm = h.shape[0]
    if mix:
        y0 = (jnp.dot(oa_ref[...], wmix_ref[:DN_WIDTH], preferred_element_type=f32)
              + jnp.dot(ob_ref[...].astype(bf16), wmix_ref[DN_WIDTH:], preferred_element_type=f32))
        h = _layer_norm(DEEPNORM_ALPHA * h + y0, g0_ref[...], b0_ref[...])
    q = jnp.dot(h.astype(bf16), wq_ref[...], preferred_element_type=f32).astype(bf16)
    heads = [(hd * XA_HEAD_DIM, (hd + 1) * XA_HEAD_DIM) for hd in range(XA_HEADS)]
    scores = [_dot_nt(q[:, lo:hi], kv_ref[:, lo:hi]) for lo, hi in heads]
    probs = [jnp.exp2(s - jnp.max(s, -1, keepdims=True)) for s in scores]
    outs = [(_dot(p, kv_ref[:, D_MODEL + lo:D_MODEL + hi]) / jnp.sum(p, -1, keepdims=True)).astype(bf16)
            for p, (lo, hi) in zip(probs, heads)]
    y = jnp.dot(jnp.concatenate(outs, axis=1), wo_ref[...], preferred_element_type=f32)
    h2 = _layer_norm(DEEPNORM_ALPHA * h + y, g_ref[...], b_ref[...])
    gates, bucket, rank = _route(h2, rwt_ref, rb_ref, cnt_ref)
    hext_ref[:, :D_MODEL] = h2
    hext_ref[:, D_MODEL:] = gates
    route_ref[...] = jnp.concatenate([bucket, rank, jnp.zeros((SUBLANES - 2, tm), jnp.int32)], axis=0)
    cnt_out_ref[...] = cnt_ref[...]


def _xattn_route(h, kv, w_q, w_o, g, b, rwt, rb, batch, seq, n_mem, mix=None):
    tm = XA_TILE
    ns = seq // tm
    t = batch * seq
    row = lambda bb, s: (bb * ns + s, 0)
    fixed = lambda bb, s: (0, 0)
    once = dict(pipeline_mode=pl.Buffered(1))
    mix_specs, mix_args = [], []
    if mix is not None:
        o_a, o_b, w_mix, g0, b0 = mix
        mix_specs = [pl.BlockSpec((tm, DN_WIDTH), row), pl.BlockSpec((tm, DA_WIDTH), row),
                     pl.BlockSpec(w_mix.shape, fixed, **once),
                     pl.BlockSpec((1, D_MODEL), fixed), pl.BlockSpec((1, D_MODEL), fixed)]
        mix_args = [o_a, o_b, w_mix, g0, b0]
    return pl.pallas_call(
        functools.partial(_xattn_kernel, mix=mix is not None),
        grid=(batch, ns),
        in_specs=mix_specs + [
            pl.BlockSpec((tm, D_MODEL), row),
            pl.BlockSpec((n_mem, 2 * D_MODEL), lambda bb, s: (bb, 0)),
            pl.BlockSpec(w_q.shape, fixed, **once),
            pl.BlockSpec(w_o.shape, fixed, **once),
            pl.BlockSpec((1, D_MODEL), fixed),
            pl.BlockSpec((1, D_MODEL), fixed),
            pl.BlockSpec(rwt.shape, fixed),
            pl.BlockSpec(rb.shape, fixed),
        ],
        out_specs=[
            pl.BlockSpec((tm, ROW_EXT), row),
            pl.BlockSpec((SUBLANES, tm), lambda bb, s: (0, bb * ns + s)),
            pl.BlockSpec((BUCKET_ROWS, LANES), fixed),
        ],
        out_shape=[
            jax.ShapeDtypeStruct((t, ROW_EXT), f32),
            jax.ShapeDtypeStruct((SUBLANES, t), jnp.int32),
            jax.ShapeDtypeStruct((BUCKET_ROWS, LANES), f32),
        ],
        scratch_shapes=[pltpu.VMEM((BUCKET_ROWS, LANES), f32)],
        compiler_params=_params(("arbitrary", "arbitrary")),
        name="xattn_route",
    )(*mix_args, h, kv, w_q, w_o, g, b, rwt, rb)


def _dispatch_kernel(dest_ref, fill_ref, hext_ref, xs_ref, inv_ref, zeros_ref, sem, zsem):
    base = pl.program_id(0) * DISPATCH_CHUNK
    n_tiles = xs_ref.shape[0] // MOE_TILE

    @pl.when(pl.program_id(0) == 0)
    def _():
        zeros_ref[...] = jnp.zeros_like(zeros_ref)

        def fill(tile):
            return pltpu.make_async_copy(zeros_ref, xs_ref.at[pl.ds(tile * MOE_TILE, MOE_TILE)], zsem)

        for tile in range(n_tiles):
            @pl.when(fill_ref[tile] > 0)
            def _(tile=tile):
                fill(tile).start()

        def clear(j8, carry):
            for u in range(8):
                inv_ref[j8 * 8 + u] = 0
            return carry

        lax.fori_loop(0, inv_ref.shape[0] // 8, clear, 0)
        for tile in range(n_tiles):
            @pl.when(fill_ref[tile] > 0)
            def _(tile=tile):
                fill(tile).wait()

    for j in range(DISPATCH_CHUNK):
        d = dest_ref[base + j]
        inv_ref[d] = base + j
        pltpu.make_async_copy(hext_ref.at[pl.ds(j, 1)], xs_ref.at[pl.ds(d, 1)], sem).start(priority=j % 2)
    pltpu.make_async_copy(hext_ref, xs_ref.at[pl.ds(0, DISPATCH_CHUNK)], sem).wait()


def _dispatch(dest, fill, hext, n_rows):
    t = hext.shape[0]
    return pl.pallas_call(
        _dispatch_kernel,
        grid_spec=pltpu.PrefetchScalarGridSpec(
            num_scalar_prefetch=2,
            grid=(t // DISPATCH_CHUNK,),
            in_specs=[pl.BlockSpec((DISPATCH_CHUNK, ROW_EXT), lambda i, *_: (i, 0))],
            out_specs=[pl.BlockSpec(memory_space=pl.ANY), pl.BlockSpec(memory_space=pltpu.SMEM)],
            scratch_shapes=[pltpu.VMEM((MOE_TILE, ROW_EXT), f32), pltpu.SemaphoreType.DMA(()),
                            pltpu.SemaphoreType.DMA(())],
        ),
        out_shape=[jax.ShapeDtypeStruct((n_rows, ROW_EXT), f32), jax.ShapeDtypeStruct((n_rows,), jnp.int32)],
        compiler_params=_params(("arbitrary",)),
        name="moe_dispatch",
    )(dest, fill, hext)


def _experts_kernel(ea_ref, eb_ref, nvalid_ref, xs_idx_ref, inv_ref, xs_ref, wga_ref, wua_ref, wda_ref,
                    wgb_ref, wub_ref, wdb_ref, g_ref, b_ref, out_ref, obuf_ref, sem_ref):
    i = pl.program_id(0)
    tm = MOE_TILE
    prev = jnp.maximum(i - 1, 0)
    n_prev = jnp.where(i > 0, nvalid_ref[prev], 0)
    slot_prev = prev % 2

    def send(j):
        return pltpu.make_async_copy(obuf_ref.at[slot_prev, pl.ds(j, 1)],
                                     out_ref.at[pl.ds(inv_ref[prev * tm + j], 1)], sem_ref.at[slot_prev])

    n_cur = jnp.where(i < pl.num_programs(0) - 1, nvalid_ref[jnp.minimum(i, pl.num_programs(0) - 2)], 0)

    def send_full(lo, hi):
        for j in range(lo, hi):
            send(j).start(priority=j % 2)

    def compute(between):
        x32 = xs_ref[:, :D_MODEL]
        x = x32.astype(bf16)
        gates = xs_ref[:, D_MODEL:]

        def hidden(wg_ref, wu_ref, gate):
            hg = jnp.dot(x, wg_ref[0, 0].astype(bf16), preferred_element_type=f32)
            hu = jnp.dot(x, wu_ref[0, 0].astype(bf16), preferred_element_type=f32)
            return (_silu(hg) * hu * gate).astype(bf16)

        between(0)
        h_a = hidden(wga_ref, wua_ref, gates[:, 0:1])
        between(1)
        h_b = hidden(wgb_ref, wub_ref, gates[:, 1:2])
        between(2)
        y = (jnp.dot(h_a, wda_ref[0, 0].astype(bf16), preferred_element_type=f32)
             + jnp.dot(h_b, wdb_ref[0, 0].astype(bf16), preferred_element_type=f32))
        between(3)
        obuf_ref[i % 2] = _layer_norm(DEEPNORM_ALPHA * x32 + y, g_ref[...], b_ref[...])

    usual = jnp.logical_and(n_prev == tm, n_cur > 0)
    quarter = tm // 4

    @pl.when(usual)
    def _():
        compute(lambda k: send_full(k * quarter, (k + 1) * quarter))

    @pl.when(jnp.logical_not(usual))
    def _():
        @pl.when(n_prev == tm)
        def _():
            send_full(0, tm)

        @pl.when(jnp.logical_and(n_prev > 0, n_prev < tm))
        def _():
            for j in range(tm):
                @pl.when(j < n_prev)
                def _(j=j):
                    send(j).start(priority=j % 2)

        @pl.when(n_cur > 0)
        def _():
            compute(lambda k: None)

    for bit in reversed(range(tm.bit_length())):
        size = 1 << bit

        @pl.when((n_prev & size) != 0)
        def _(size=size):
            pltpu.make_async_copy(obuf_ref.at[slot_prev, pl.ds(0, size)], out_ref.at[pl.ds(0, size)],
                                  sem_ref.at[slot_prev]).wait()


def _experts(layer, ea, eb, nvalid, xs_idx, inv, xs, w_gate, w_up, w_down, g, b, t):
    n_tiles = xs.shape[0] // MOE_TILE
    up_a = pl.BlockSpec((1, 1, D_MODEL, D_EXPERT), lambda i, ea, *_: (layer, ea[i], 0, 0))
    up_b = pl.BlockSpec((1, 1, D_MODEL, D_EXPERT), lambda i, ea, eb, *_: (layer, eb[i], 0, 0))
    dn_a = pl.BlockSpec((1, 1, D_EXPERT, D_MODEL), lambda i, ea, *_: (layer, ea[i], 0, 0))
    dn_b = pl.BlockSpec((1, 1, D_EXPERT, D_MODEL), lambda i, ea, eb, *_: (layer, eb[i], 0, 0))
    fixed = lambda i, *_: (0, 0)
    return pl.pallas_call(
        _experts_kernel,
        grid_spec=pltpu.PrefetchScalarGridSpec(
            num_scalar_prefetch=5,
            grid=(n_tiles + 1,),
            in_specs=[pl.BlockSpec((MOE_TILE, ROW_EXT), lambda i, ea, eb, nv, xs_idx, inv: (xs_idx[i], 0)),
                      up_a, up_a, dn_a, up_b, up_b, dn_b,
                      pl.BlockSpec((1, D_MODEL), fixed), pl.BlockSpec((1, D_MODEL), fixed)],
            out_specs=pl.BlockSpec(memory_space=pl.ANY),
            scratch_shapes=[pltpu.VMEM((2, MOE_TILE, D_MODEL), f32), pltpu.SemaphoreType.DMA((2,))],
        ),
        out_shape=jax.ShapeDtypeStruct((t, D_MODEL), f32),
        compiler_params=_params(("arbitrary",)),
        name="moe_experts",
    )(ea, eb, nvalid, xs_idx, inv, xs, w_gate, w_up, w_down, w_gate, w_up, w_down, g, b)


def _moe_plan(route, counts):
    t = route.shape[1]
    n_rows = t + N_BUCKETS * MOE_TILE
    n_tiles = n_rows // MOE_TILE
    cnt = counts[:N_BUCKETS, 0].astype(jnp.int32)
    padded = (cnt + MOE_TILE - 1) // MOE_TILE * MOE_TILE
    ends = jnp.cumsum(padded)
    starts = ends - padded
    buckets = jnp.arange(N_BUCKETS, dtype=jnp.int32)
    dest = jnp.sum(jnp.where(route[0][None, :] == buckets[:, None], starts[:, None], 0), axis=0) + route[1]
    tile_start = jnp.arange(n_tiles + 1, dtype=jnp.int32) * MOE_TILE
    tile_bucket = jnp.sum(tile_start[:, None] >= ends[None, :], axis=1).astype(jnp.int32)
    used = tile_bucket < N_BUCKETS
    tile_bucket = jnp.minimum(tile_bucket, N_BUCKETS - 1)
    nvalid = jnp.where(used, jnp.clip(starts[tile_bucket] + cnt[tile_bucket] - tile_start, 0, MOE_TILE), 0)
    xs_idx = jnp.minimum(jnp.arange(n_tiles + 1, dtype=jnp.int32), jnp.maximum(ends[-1] // MOE_TILE - 1, 0))
    last_bucket = tile_bucket[jnp.maximum(ends[-1] // MOE_TILE - 1, 0)]
    tile_bucket = jnp.where(used, tile_bucket, last_bucket)
    pair_a = jnp.array([g * EXPERTS_PER_GROUP + i for g in range(N_GROUPS) for i, _ in _PAIRS], jnp.int32)
    pair_b = jnp.array([g * EXPERTS_PER_GROUP + j for g in range(N_GROUPS) for _, j in _PAIRS], jnp.int32)
    tiles = jnp.arange(n_tiles, dtype=jnp.int32)
    is_last = jnp.any(jnp.logical_and((tiles[:, None] + 1) * MOE_TILE == ends[None, :], padded[None, :] > 0), axis=1)
    fill = jnp.logical_or(is_last, tiles * MOE_TILE >= ends[-1]).astype(jnp.int32)
    return dict(dest=dest, ea=pair_a[tile_bucket], eb=pair_b[tile_bucket], nvalid=nvalid.astype(jnp.int32),
                xs_idx=xs_idx, fill=fill, n_rows=n_rows)


def _moe(layer, hext, route, counts, w_gate, w_up, w_down, g, b):
    plan = _moe_plan(route, counts)
    xs, inv = _dispatch(plan["dest"], plan["fill"], hext, plan["n_rows"])
    return _experts(layer, plan["ea"], plan["eb"], plan["nvalid"], plan["xs_idx"], inv, xs,
                    w_gate, w_up, w_down, g, b, hext.shape[0])


def kernel(x, mem, ab_w_in, ab_conv_w, ab_a_log, ab_dt_bias, ab_norm_w, ab_w_out, sc_w_in, sc_conv_w, sc_w_out, xa_w_q, xa_w_kv, xa_w_o, router_w, router_b, moe_w_gate, moe_w_up, moe_w_down, ln_g, ln_b):
    batch, seq, d = x.shape
    n_mem = mem.shape[1]
    t = batch * seq
    h = x.reshape(t, d)
    mem2 = mem.reshape(batch * n_mem, d)
    rwt = jnp.transpose(router_w)
    rb = router_b.reshape(N_EXPERTS, 1)
    row = lambda v: v.reshape(1, -1)

    for layer in range(DEPTH):
        i = layer // 2
        if layer % 2 == 0:
            w_in = ab_w_in[i]
            c0, c1, c2 = 3 * DN_WIDTH, 4 * DN_WIDTH, 4 * DN_WIDTH + 2 * DN_HEADS
            w_ba = jnp.pad(w_in[:, c1:c2], ((0, 0), (0, LANES - 2 * DN_HEADS)))
            lane_pad = lambda v: jnp.pad(v.reshape(1, DN_HEADS), ((0, 0), (DN_HEADS, LANES - 2 * DN_HEADS)))
            act, z, bg, qkv_d = _project_ab(h, w_in[:, :c0].astype(bf16), w_in[:, c0:c1].astype(bf16), w_ba.astype(bf16),
                                            w_in[:, c2:].astype(bf16), ab_conv_w[i], lane_pad(ab_a_log[i]),
                                            lane_pad(ab_dt_bias[i]), seq)
            o_a = _deltanet(act, bg, z, row(ab_norm_w[i]), batch, seq)
            o_b = _dilated_attention(qkv_d.reshape(batch, seq, 3 * DA_WIDTH), batch, seq).reshape(t, DA_WIDTH)
            mix = (o_a, o_b, ab_w_out[i].astype(bf16), row(ln_g[layer, 0]), row(ln_b[layer, 0]))
        else:
            h = _shortconv(h, sc_w_in[i].astype(bf16), sc_conv_w[i], sc_w_out[i].astype(bf16),
                           row(ln_g[layer, 0]), row(ln_b[layer, 0]), batch, seq)
            mix = None
        (kv,) = _project(mem2, [xa_w_kv[layer].astype(bf16)], [bf16], n_mem)
        w_q = (xa_w_q[layer] * (XA_HEAD_DIM ** -0.5 * LOG2_E)).astype(bf16)
        hext, route, counts = _xattn_route(h, kv, w_q, xa_w_o[layer].astype(bf16),
                                           row(ln_g[layer, 1]), row(ln_b[layer, 1]), rwt, rb, batch, seq, n_mem, mix)
        h = _moe(layer, hext, route, counts, moe_w_gate, moe_w_up, moe_w_down,
                 row(ln_g[layer, 2]), row(ln_b[layer, 2]))
    return h.reshape(batch, seq, d)
```

```python
import functools

import jax
import jax.numpy as jnp
from jax import lax
from jax.experimental import pallas as pl
from jax.experimental.pallas import tpu as pltpu

D_MODEL = 1024
DEPTH = 2
DN_HEADS = 4
DN_HEAD_DIM = 128
DN_WIDTH = DN_HEADS * DN_HEAD_DIM
DN_CONV = 4
DN_CHUNK = 64
DA_HEADS = 8
DA_HEAD_DIM = 64
DA_WIDTH = DA_HEADS * DA_HEAD_DIM
DA_BRANCHES = ((128, 1), (512, 4), (2048, 16))
DA_BLOCK = 128
DA_RESIDUES = max(d for _, d in DA_BRANCHES)
SC_CONV = 3
XA_HEADS = 4
XA_HEAD_DIM = D_MODEL // XA_HEADS
N_EXPERTS = 16
N_GROUPS = 4
EXPERTS_PER_GROUP = N_EXPERTS // N_GROUPS
D_EXPERT = D_MODEL // 2
DEEPNORM_ALPHA = (2 * DEPTH) ** 0.25
LN_EPS = 1e-5
RMS_EPS = 1e-6

LANES = 128
SUBLANES = 8
VMEM_LIMIT_BYTES = 48 * 1024 * 1024

TOKEN_TILE = 512
XA_TILE = 1024
SC_TILE = 1024
DN_BLOCK = 256
MOE_TILE = 256
DISPATCH_CHUNK = 512
DA_UNROLL = 4
GATE_LANES = LANES
ROW_EXT = D_MODEL + GATE_LANES

_PAIRS = tuple((i, j) for i in range(EXPERTS_PER_GROUP) for j in range(i + 1, EXPERTS_PER_GROUP))
N_BUCKETS = N_GROUPS * len(_PAIRS)
BUCKET_ROWS = 32

NEG_BIG = -1e30
LOG2_E = 1.4426950408889634

bf16 = jnp.bfloat16
f32 = jnp.float32


def _params(semantics):
    return pltpu.CompilerParams(dimension_semantics=semantics, vmem_limit_bytes=VMEM_LIMIT_BYTES)


def _dot(a, b):
    return jnp.dot(a.astype(bf16), b.astype(bf16), preferred_element_type=f32)


def _dot_nt(a, b):
    return lax.dot_general(a.astype(bf16), b.astype(bf16), (((1,), (1,)), ((), ())), preferred_element_type=f32)


def _dot_tn(a, b):
    return lax.dot_general(a.astype(bf16), b.astype(bf16), (((0,), (0,)), ((), ())), preferred_element_type=f32)


def _split(a, parts):
    out = []
    rem = a
    for _ in range(parts):
        p = rem.astype(bf16)
        out.append(p)
        rem = rem - p.astype(f32)
    return out


def _layer_norm(v, g, b):
    mu = jnp.mean(v, -1, keepdims=True)
    c = v - mu
    var = jnp.mean(c * c, -1, keepdims=True)
    return c * lax.rsqrt(var + LN_EPS) * g + b


def _silu(v):
    return v * jax.nn.sigmoid(v)


def _softplus(v):
    return jnp.maximum(v, 0.0) + jnp.log(1.0 + jnp.exp(-jnp.abs(v)))


def _proj_kernel(x_ref, *refs, n_out):
    w_refs, o_refs = refs[:n_out], refs[n_out:]
    xb = x_ref[...].astype(bf16)
    for w_ref, o_ref in zip(w_refs, o_refs):
        n = w_ref.shape[1]
        for c0 in range(0, n, 512):
            c1 = min(c0 + 512, n)
            o_ref[:, c0:c1] = jnp.dot(xb, w_ref[:, c0:c1], preferred_element_type=f32).astype(o_ref.dtype)


def _project(x, ws, out_dtypes, tm):
    t, k = x.shape
    n_out = len(ws)
    return pl.pallas_call(
        functools.partial(_proj_kernel, n_out=n_out),
        grid=(t // tm,),
        in_specs=[pl.BlockSpec((tm, k), lambda i: (i, 0))]
        + [pl.BlockSpec(w.shape, lambda i: (0, 0)) for w in ws],
        out_specs=[pl.BlockSpec((tm, w.shape[1]), lambda i: (i, 0)) for w in ws],
        out_shape=[jax.ShapeDtypeStruct((t, w.shape[1]), dt) for w, dt in zip(ws, out_dtypes)],
        compiler_params=_params(("parallel",)),
        name="project",
    )(x, *ws)


def _proj_ab_kernel(x_ref, wqkv_ref, wz_ref, wba_ref, wd_ref, cw_ref, alog_ref, dtb_ref,
                    act_ref, z_ref, bg_ref, qkvd_ref, xx_ref, *, tiles_per_seq):
    @pl.when(pl.program_id(0) % tiles_per_seq == 0)
    def _():
        xx_ref[0:SUBLANES, :] = jnp.zeros((SUBLANES, xx_ref.shape[1]), f32)

    tm, hd = x_ref.shape[0], DN_HEAD_DIM
    xb = x_ref[...].astype(bf16)
    cw = cw_ref[...]

    def conv_act(c):
        cols = slice(c * hd, (c + 1) * hd)
        conv = xx_ref[SUBLANES:, cols] * cw[DN_CONV - 1:DN_CONV, cols]
        for j in range(DN_CONV - 1):
            back = DN_CONV - 1 - j
            conv = conv + xx_ref[pl.ds(SUBLANES - back, tm), cols] * cw[j:j + 1, cols]
        a = _silu(conv)
        if c < 2 * DN_HEADS:
            scale = hd ** -0.5 if c < DN_HEADS else 1.0
            a = a * (lax.rsqrt(jnp.sum(a * a, -1, keepdims=True) + RMS_EPS) * scale)
        act_ref[:, cols] = a

    for c0 in range(0, 3 * DN_WIDTH, 512):
        xx_ref[SUBLANES:, c0:c0 + 512] = jnp.dot(xb, wqkv_ref[:, c0:c0 + 512], preferred_element_type=f32)
        for c in range(c0 // hd, (c0 + 512) // hd):
            conv_act(c)
    z_ref[...] = jnp.dot(xb, wz_ref[...], preferred_element_type=f32)
    for c0 in range(0, 3 * DA_WIDTH, 512):
        qkvd_ref[:, c0:c0 + 512] = jnp.dot(xb, wd_ref[:, c0:c0 + 512], preferred_element_type=f32)
    ba = jnp.dot(xb, wba_ref[...], preferred_element_type=f32)
    is_beta = lax.broadcasted_iota(jnp.int32, ba.shape, 1) < DN_HEADS
    bg_ref[...] = jnp.where(is_beta, jax.nn.sigmoid(ba), -jnp.exp(alog_ref[...]) * _softplus(ba + dtb_ref[...]))
    xx_ref[0:SUBLANES, :] = xx_ref[tm:tm + SUBLANES, :]


def _project_ab(x, w_qkv, w_z, w_ba, w_d, conv_w, alog_row, dtb_row, seq):
    t, k = x.shape
    tm = TOKEN_TILE
    fixed = lambda i: (0, 0)
    row = lambda i: (i, 0)
    widths = (3 * DN_WIDTH, DN_WIDTH, LANES, 3 * DA_WIDTH)
    return pl.pallas_call(
        functools.partial(_proj_ab_kernel, tiles_per_seq=seq // tm),
        grid=(t // tm,),
        in_specs=[pl.BlockSpec((tm, k), row)]
        + [pl.BlockSpec(w.shape, fixed) for w in (w_qkv, w_z, w_ba, w_d, conv_w, alog_row, dtb_row)],
        out_specs=[pl.BlockSpec((tm, n), row) for n in widths],
        out_shape=[jax.ShapeDtypeStruct((t, n), f32) for n in widths],
        scratch_shapes=[pltpu.VMEM((SUBLANES + tm, 3 * DN_WIDTH), f32)],
        compiler_params=_params(("arbitrary",)),
        name="project_ab",
    )(x, w_qkv, w_z, w_ba, w_d, conv_w, alog_row, dtb_row)


def _deltanet_kernel(act_ref, bg_ref, z_ref, nw_ref, o_ref, state_ref):
    @pl.when(pl.program_id(1) == 0)
    def _():
        state_ref[...] = jnp.zeros_like(state_ref)

    rows, ch, hd, width = DN_BLOCK, DN_CHUNK, DN_HEAD_DIM, DN_WIDTH
    sup = 2 * ch
    n_sup = rows // sup

    act = act_ref[...]
    beta_all = bg_ref[...]
    g_all = beta_all

    ri = lax.broadcasted_iota(jnp.int32, (sup, sup), 0)
    ci = lax.broadcasted_iota(jnp.int32, (sup, sup), 1)
    same = (ri >= ch) == (ci >= ch)
    incl = jnp.logical_and(ri >= ci, same)
    strict = jnp.logical_and(ri > ci, same)
    lower_ones = jnp.where(incl, 1.0, 0.0).astype(bf16)
    eye = jnp.where(ri == ci, 1.0, 0.0)
    first_rows = lax.broadcasted_iota(jnp.int32, (sup, hd), 0) < ch
    z = z_ref[...]
    nw = nw_ref[...]

    chains = [(s, h) for s in range(n_sup) for h in range(DN_HEADS)]
    g_cum_all, g_cum_t = [], []
    for s in range(n_sup):
        g_blk = g_all[s * sup:(s + 1) * sup, :]
        gc = sum(jnp.dot(lower_ones, p, preferred_element_type=f32) for p in _split(g_blk, 2))
        g_cum_all.append(gc)
        g_cum_t.append(jnp.transpose(gc))

    st = {}
    for s, h in chains:
        r0 = s * sup
        q = act[r0:r0 + sup, h * hd:(h + 1) * hd]
        k = act[r0:r0 + sup, width + h * hd:width + (h + 1) * hd]
        v = act[r0:r0 + sup, 2 * width + h * hd:2 * width + (h + 1) * hd]
        beta_b = jnp.broadcast_to(beta_all[r0:r0 + sup, h:h + 1], (sup, hd))
        g_i = jnp.broadcast_to(g_cum_all[s][:, DN_HEADS + h:DN_HEADS + h + 1], (sup, hd))
        g_j = jnp.broadcast_to(g_cum_t[s][DN_HEADS + h:DN_HEADS + h + 1, :], (sup, sup))
        decay = jnp.where(incl, jnp.exp(jnp.minimum(g_i - g_j, 0.0)), 0.0)
        e_g = jnp.exp(g_i)
        g_last = jnp.where(first_rows, g_i[ch - 1:ch, :], g_i[sup - 1:sup, :])
        st[s, h] = dict(q=q, k=k, kb=k * beta_b, vb=v * beta_b, decay=decay, e_g=e_g,
                        k_tail=k * jnp.exp(g_last - g_i),
                        gl=(jnp.exp(g_i[ch - 1:ch, :]), jnp.exp(g_i[sup - 1:sup, :])))

    for c in chains:
        d = st[c]
        d["a"] = jnp.where(strict, _dot_nt(d["kb"], d["k"]) * d["decay"], 0.0)
        d["attn"] = jnp.where(incl, _dot_nt(d["q"], d["k"]) * d["decay"], 0.0)
    for c in chains:
        st[c]["t_inv"] = eye - st[c]["a"]
        st[c]["pw"] = st[c]["a"]
    for _ in range(5):
        for c in chains:
            st[c]["pw"] = _dot(st[c]["pw"], st[c]["pw"])
        for c in chains:
            st[c]["t_inv"] = st[c]["t_inv"] + _dot(st[c]["t_inv"], st[c]["pw"])
    for c in chains:
        d = st[c]
        sol = _dot(d["t_inv"], jnp.concatenate([d["vb"], d["kb"] * d["e_g"]], axis=1))
        d["u"], d["w"] = sol[:, :hd], sol[:, hd:]
    for c in chains:
        d = st[c]
        uw = jnp.concatenate([d["u"], d["w"]], axis=1).astype(bf16)
        mix = jnp.dot(d["attn"].astype(bf16), uw, preferred_element_type=f32)
        d["au"] = mix[:, :hd]
        d["qe"] = d["q"] * d["e_g"] - mix[:, hd:]
        kt = d["k_tail"].astype(bf16)
        zero = jnp.zeros_like(kt)
        d["nu_pw"] = [_dot_tn(jnp.where(first_rows == first, kt, zero), uw)
                      for first in (True, False)]

    for s in range(n_sup):
        for half in range(2):
            for h in range(DN_HEADS):
                d = st[s, h]
                state = state_ref[h]
                lo = half * ch
                o = _dot(d["qe"][lo:lo + ch], state) + d["au"][lo:lo + ch]
                nu_pw = d["nu_pw"][half]
                state_ref[h] = state * d["gl"][half] - _dot(nu_pw[:, hd:], state) + nu_pw[:, :hd]
                o = o * lax.rsqrt(jnp.mean(o * o, -1, keepdims=True) + RMS_EPS) * nw
                r0 = s * sup + lo
                o_ref[r0:r0 + ch, h * hd:(h + 1) * hd] = (o * _silu(z[r0:r0 + ch, h * hd:(h + 1) * hd])).astype(o_ref.dtype)


def _deltanet(act, bg, z, norm_w, batch, seq):
    nb = seq // DN_BLOCK
    row = lambda b, c: (b * nb + c, 0)
    fixed = lambda b, c: (0, 0)
    return pl.pallas_call(
        _deltanet_kernel,
        grid=(batch, nb),
        in_specs=[
            pl.BlockSpec((DN_BLOCK, 3 * DN_WIDTH), row),
            pl.BlockSpec((DN_BLOCK, LANES), row),
            pl.BlockSpec((DN_BLOCK, DN_WIDTH), row),
            pl.BlockSpec((1, DN_HEAD_DIM), fixed),
        ],
        out_specs=pl.BlockSpec((DN_BLOCK, DN_WIDTH), row),
        out_shape=jax.ShapeDtypeStruct((batch * seq, DN_WIDTH), bf16),
        scratch_shapes=[pltpu.VMEM((DN_HEADS, DN_HEAD_DIM, DN_HEAD_DIM), f32)],
        compiler_params=_params(("arbitrary", "arbitrary")),
        name="deltanet",
    )(act, bg, z, norm_w)


def _dilated_kernel(*refs, seq):
    blk, res = DA_BLOCK, DA_RESIDUES
    q_refs, k_refs, v_refs = refs[:res], refs[res:2 * res], refs[2 * res:3 * res]
    o_ref, qs_ref, ks_ref, vs_ref, ob_ref, lse_ref, s_ref, p_ref, st_ref = refs[3 * res:]
    per = seq // res
    head0 = lax.broadcasted_iota(jnp.int32, (blk, LANES), 1) < DA_HEAD_DIM
    ai = lax.broadcasted_iota(jnp.int32, (2 * blk, blk), 0) & (blk - 1)
    bi = lax.broadcasted_iota(jnp.int32, (2 * blk, blk), 1)
    ones_blk = jnp.ones((2 * blk, LANES), bf16)

    q_scale = DA_HEAD_DIM ** -0.5 * LOG2_E
    for r in range(res):
        qs_ref[r * per:(r + 1) * per, :] = q_refs[r][0] * q_scale
        ks_ref[r * per:(r + 1) * per, :] = k_refs[r][0]
        vs_ref[r * per:(r + 1) * per, :] = v_refs[r][0]

    for br, (window, dil) in enumerate(DA_BRANCHES):
        assert window // dil == blk and res % dil == 0
        runs = res // dil
        run_len = blk // runs
        nblk = seq // (dil * blk)
        shift = run_len.bit_length() - 1

        def pos(a, runs=runs, run_len=run_len, shift=shift):
            return runs * (a & (run_len - 1)) + (a >> shift)

        in_window = jnp.concatenate([pos(bi) >= pos(ai), pos(bi) <= pos(ai)], axis=1)

        n_iter = dil * nblk // DA_UNROLL
        assert n_iter * DA_UNROLL == dil * nblk

        def tile(ref, offs, run_len=run_len, runs=runs):
            parts = [ref[pl.ds(o, run_len), :] for o in offs]
            return parts[0] if runs == 1 else jnp.concatenate(parts, axis=0)

        def blocks_of(it, dil=dil, runs=runs, run_len=run_len, nblk=nblk):
            out = []
            for u in range(DA_UNROLL):
                idx = it * DA_UNROLL + u
                cls = idx // nblk
                n = idx - cls * nblk
                n_prev = jnp.maximum(n - 1, 0)
                cur = [pl.multiple_of((dil * m + cls) * per + run_len * n, SUBLANES) for m in range(runs)]
                prev = [pl.multiple_of((dil * m + cls) * per + run_len * n_prev, SUBLANES) for m in range(runs)]
                out.append((cur, prev, n > 0))
            return out

        def masked_scores(blocks, tile=tile, in_window=in_window):
            out = []
            for cur, prev, has_prev in blocks:
                q = tile(qs_ref, cur).astype(bf16)
                zero = jnp.zeros_like(q)
                q2 = jnp.concatenate([jnp.where(head0, q, zero), jnp.where(head0, zero, q)], axis=0)
                keys = jnp.concatenate([tile(ks_ref, prev), tile(ks_ref, cur)], axis=0)
                s = jnp.where(in_window, _dot_nt(q2, keys), NEG_BIG)
                out.append(jnp.concatenate([jnp.where(has_prev, s[:, :blk], NEG_BIG), s[:, blk:]], axis=1))
            return out

        def softmax_stage(blocks, scores):
            out = []
            for s in scores:
                top = jnp.max(s, -1, keepdims=True)
                out.append((jnp.exp2(s - top).astype(bf16), jnp.where(head0, top[:blk], top[blk:])))
            return out

        def output_stage(blocks, probs, br=br, run_len=run_len, tile=tile):
            for (cur, prev, has_prev), (pb, top) in zip(blocks, probs):
                vals = jnp.concatenate([tile(vs_ref, prev), tile(vs_ref, cur)], axis=0).astype(bf16)
                acc = jnp.dot(pb, jnp.concatenate([vals, ones_blk], axis=1), preferred_element_type=f32)
                den = jnp.where(head0, acc[:blk, LANES:], acc[blk:, LANES:])
                o_blk = jnp.where(head0, acc[:blk, :LANES], acc[blk:, :LANES]) / den
                lse_blk = top + jnp.log2(den)
                for m, o in enumerate(cur):
                    ob_ref[br, pl.ds(o, run_len), :] = o_blk[m * run_len:(m + 1) * run_len]
                    lse_ref[br, pl.ds(o, run_len), :] = lse_blk[m * run_len:(m + 1) * run_len]

        def put_scores(scores):
            for u, sc in enumerate(scores):
                s_ref[u] = sc

        def put_probs(probs):
            for u, (pb, top) in enumerate(probs):
                p_ref[u] = pb
                st_ref[u] = top

        put_probs(softmax_stage(blocks_of(0), masked_scores(blocks_of(0))))
        put_scores(masked_scores(blocks_of(min(1, n_iter - 1))))

        def body(it, carry, n_iter=n_iter, blocks_of=blocks_of, masked_scores=masked_scores,
                 softmax_stage=softmax_stage, output_stage=output_stage, put_scores=put_scores, put_probs=put_probs):
            scores = [s_ref[u] for u in range(DA_UNROLL)]
            probs = [(p_ref[u], st_ref[u]) for u in range(DA_UNROLL)]
            output_stage(blocks_of(it), probs)
            next_probs = softmax_stage(blocks_of(jnp.minimum(it + 1, n_iter - 1)), scores)
            next_scores = masked_scores(blocks_of(jnp.minimum(it + 2, n_iter - 1)))
            put_scores(next_scores)
            put_probs(next_probs)
            return carry

        lax.fori_loop(0, n_iter, body, 0)

    for r in range(res):
        rows = slice(r * per, (r + 1) * per)
        lses = [lse_ref[br, rows, :] for br in range(len(DA_BRANCHES))]
        top = functools.reduce(jnp.maximum, lses)
        wts = [jnp.exp2(l - top) for l in lses]
        num = sum(w * ob_ref[br, rows, :] for br, w in enumerate(wts))
        o_ref[0, pl.ds(r, per, stride=res), :] = num / sum(wts)


def _dilated_attention(qkv, batch, seq):
    pairs = DA_WIDTH // LANES
    n_br = len(DA_BRANCHES)
    res = DA_RESIDUES
    per = seq // res
    row_blocks = 3 * pairs
    view = qkv.reshape(batch, per, res * 3 * DA_WIDTH)
    spec = lambda r, off: pl.BlockSpec((1, per, LANES), lambda b, p, r=r, off=off: (b, 0, r * row_blocks + off + p))
    in_specs = [spec(r, off) for off in (0, pairs, 2 * pairs) for r in range(res)]
    return pl.pallas_call(
        functools.partial(_dilated_kernel, seq=seq),
        grid=(batch, pairs),
        in_specs=in_specs,
        out_specs=pl.BlockSpec((1, seq, LANES), lambda b, p: (b, 0, p)),
        out_shape=jax.ShapeDtypeStruct((batch, seq, DA_WIDTH), f32),
        scratch_shapes=[pltpu.VMEM((seq, LANES), f32)] * 3
        + [pltpu.VMEM((n_br, seq, LANES), f32)] * 2
        + [pltpu.VMEM((DA_UNROLL, 2 * DA_BLOCK, 2 * DA_BLOCK), f32),
           pltpu.VMEM((DA_UNROLL, 2 * DA_BLOCK, 2 * DA_BLOCK), bf16),
           pltpu.VMEM((DA_UNROLL, DA_BLOCK, LANES), f32)],
        compiler_params=_params(("parallel", "parallel")),
        name="dilated_attention",
    )(*([view] * len(in_specs)))


def _shortconv_kernel(h_ref, win_ref, cw_ref, wout_ref, g_ref, b_ref, o_ref, tail_ref):
    @pl.when(pl.program_id(1) == 0)
    def _():
        tail_ref[...] = jnp.zeros_like(tail_ref)

    tm = h_ref.shape[0]
    d = D_MODEL
    h = h_ref[...]
    hb = h.astype(bf16)
    gate_b = jnp.dot(hb, win_ref[:, :d], preferred_element_type=f32)
    gate_c = jnp.dot(hb, win_ref[:, d:2 * d], preferred_element_type=f32)
    hid = jnp.dot(hb, win_ref[:, 2 * d:], preferred_element_type=f32)
    u = gate_c * hid
    uu = jnp.concatenate([tail_ref[...], u], axis=0)
    cw = cw_ref[...]
    conv = u * cw[SC_CONV - 1:SC_CONV, :]
    for j in range(SC_CONV - 1):
        back = SC_CONV - 1 - j
        conv = conv + uu[SUBLANES - back:SUBLANES - back + tm, :] * cw[j:j + 1, :]
    tail_ref[...] = u[tm - SUBLANES:, :]
    y = jnp.dot((gate_b * conv).astype(bf16), wout_ref[...], preferred_element_type=f32)
    o_ref[...] = _layer_norm(DEEPNORM_ALPHA * h + y, g_ref[...], b_ref[...])


def _shortconv(h, w_in, conv_w, w_out, g, b, batch, seq):
    tm = SC_TILE
    ns = seq // tm
    row = lambda bb, s: (bb * ns + s, 0)
    fixed = lambda bb, s: (0, 0)
    once = dict(pipeline_mode=pl.Buffered(1))
    return pl.pallas_call(
        _shortconv_kernel,
        grid=(batch, ns),
        in_specs=[
            pl.BlockSpec((tm, D_MODEL), row),
            pl.BlockSpec(w_in.shape, fixed, **once),
            pl.BlockSpec(conv_w.shape, fixed),
            pl.BlockSpec(w_out.shape, fixed, **once),
            pl.BlockSpec((1, D_MODEL), fixed),
            pl.BlockSpec((1, D_MODEL), fixed),
        ],
        out_specs=pl.BlockSpec((tm, D_MODEL), row),
        out_shape=jax.ShapeDtypeStruct((batch * seq, D_MODEL), f32),
        scratch_shapes=[pltpu.VMEM((SUBLANES, D_MODEL), f32)],
        compiler_params=_params(("arbitrary", "arbitrary")),
        name="shortconv",
    )(h, w_in, conv_w, w_out, g, b)


def _route(h2, rwt_ref, rb_ref, cnt_ref):
    tm = h2.shape[0]
    w_hi, w_lo = _split(rwt_ref[...], 2)
    x_hi, x_lo = _split(h2, 2)
    nt = (((1,), (1,)), ((), ()))
    logits = (lax.dot_general(w_hi, x_hi, nt, preferred_element_type=f32)
              + lax.dot_general(w_hi, x_lo, nt, preferred_element_type=f32)
              + lax.dot_general(w_lo, x_hi, nt, preferred_element_type=f32))
    mx = jnp.max(logits, axis=0, keepdims=True)
    ex = jnp.exp(logits - mx)
    scores = ex / jnp.sum(ex, axis=0, keepdims=True)
    sel = scores + rb_ref[...]
    best = jnp.full((1, tm), -jnp.inf, f32)
    bucket = jnp.zeros((1, tm), jnp.int32)
    s_a = jnp.zeros((1, tm), f32)
    s_b = jnp.zeros((1, tm), f32)
    for grp in range(N_GROUPS):
        for p, (i, j) in enumerate(_PAIRS):
            a, b = grp * EXPERTS_PER_GROUP + i, grp * EXPERTS_PER_GROUP + j
            ps = sel[a:a + 1, :] + sel[b:b + 1, :]
            upd = ps > best
            best = jnp.where(upd, ps, best)
            bucket = jnp.where(upd, grp * len(_PAIRS) + p, bucket)
            s_a = jnp.where(upd, scores[a:a + 1, :], s_a)
            s_b = jnp.where(upd, scores[b:b + 1, :], s_b)
    denom = s_a + s_b
    gate_rows = jnp.concatenate([s_a / denom, s_b / denom, jnp.zeros((GATE_LANES - 2, tm), f32)], axis=0)
    gates = jnp.transpose(gate_rows)
    onehot = jnp.where(lax.broadcasted_iota(jnp.int32, (BUCKET_ROWS, tm), 0) == bucket, 1.0, 0.0)
    before = lax.broadcasted_iota(jnp.int32, (tm, tm), 0) < lax.broadcasted_iota(jnp.int32, (tm, tm), 1)
    prefix = jnp.dot(onehot.astype(bf16), jnp.where(before, 1.0, 0.0).astype(bf16), preferred_element_type=f32)
    cnt = cnt_ref[...]
    rank = jnp.sum(onehot * (prefix + cnt[:, 0:1]), axis=0, keepdims=True)
    cnt_ref[...] = cnt + jnp.sum(onehot, axis=1, keepdims=True)
    return gates, bucket, rank.astype(jnp.int32)


def _xattn_kernel(*refs, mix):
    if mix:
        (oa_ref, ob_ref, wmix_ref, g0_ref, b0_ref), refs = refs[:5], refs[5:]
    (h_ref, kv_ref, wq_ref, wo_ref, g_ref, b_ref, rwt_ref, rb_ref,
     hext_ref, route_ref, cnt_out_ref, cnt_ref) = refs

    @pl.when(jnp.logical_and(pl.program_id(0) == 0, pl.program_id(1) == 0))
    def _():
        cnt_ref[...] = jnp.zeros_like(cnt_ref)

    h = h_ref[...]
    tm = h.shape[0]
    if mix:
        y0 = (jnp.dot(oa_ref[...], wmix_ref[:DN_WIDTH], preferred_element_type=f32)
              + jnp.dot(ob_ref[...].astype(bf16), wmix_ref[DN_WIDTH:], preferred_element_type=f32))
        h = _layer_norm(DEEPNORM_ALPHA * h + y0, g0_ref[...], b0_ref[...])
    q = jnp.dot(h.astype(bf16), wq_ref[...], preferred_element_type=f32).astype(bf16)
    heads = [(hd * XA_HEAD_DIM, (hd + 1) * XA_HEAD_DIM) for hd in range(XA_HEADS)]
    scores = [_dot_nt(q[:, lo:hi], kv_ref[:, lo:hi]) for lo, hi in heads]
    probs = [jnp.exp2(s - jnp.max(s, -1, keepdims=True)) for s in scores]
    outs = [(_dot(p, kv_ref[:, D_MODEL + lo:D_MODEL + hi]) / jnp.sum(p, -1, keepdims=True)).astype(bf16)
            for p, (lo, hi) in zip(probs, heads)]
    y = jnp.dot(jnp.concatenate(outs, axis=1), wo_ref[...], preferred_element_type=f32)
    h2 = _layer_norm(DEEPNORM_ALPHA * h + y, g_ref[...], b_ref[...])
    gates, bucket, rank = _route(h2, rwt_ref, rb_ref, cnt_ref)
    hext_ref[:, :D_MODEL] = h2
    hext_ref[:, D_MODEL:] = gates
    route_ref[...] = jnp.concatenate([bucket, rank, jnp.zeros((SUBLANES - 2, tm), jnp.int32)], axis=0)
    cnt_out_ref[...] = cnt_ref[...]


def _xattn_route(h, kv, w_q, w_o, g, b, rwt, rb, batch, seq, n_mem, mix=None):
    tm = XA_TILE
    ns = seq // tm
    t = batch * seq
    row = lambda bb, s: (bb * ns + s, 0)
    fixed = lambda bb, s: (0, 0)
    once = dict(pipeline_mode=pl.Buffered(1))
    mix_specs, mix_args = [], []
    if mix is not None:
        o_a, o_b, w_mix, g0, b0 = mix
        mix_specs = [pl.BlockSpec((tm, DN_WIDTH), row), pl.BlockSpec((tm, DA_WIDTH), row),
                     pl.BlockSpec(w_mix.shape, fixed, **once),
                     pl.BlockSpec((1, D_MODEL), fixed), pl.BlockSpec((1, D_MODEL), fixed)]
        mix_args = [o_a, o_b, w_mix, g0, b0]
    return pl.pallas_call(
        functools.partial(_xattn_kernel, mix=mix is not None),
        grid=(batch, ns),
        in_specs=mix_specs + [
            pl.BlockSpec((tm, D_MODEL), row),
            pl.BlockSpec((n_mem, 2 * D_MODEL), lambda bb, s: (bb, 0)),
            pl.BlockSpec(w_q.shape, fixed, **once),
            pl.BlockSpec(w_o.shape, fixed, **once),
            pl.BlockSpec((1, D_MODEL), fixed),
            pl.BlockSpec((1, D_MODEL), fixed),
            pl.BlockSpec(rwt.shape, fixed),
            pl.BlockSpec(rb.shape, fixed),
        ],
        out_specs=[
            pl.BlockSpec((tm, ROW_EXT), row),
            pl.BlockSpec((SUBLANES, tm), lambda bb, s: (0, bb * ns + s)),
            pl.BlockSpec((BUCKET_ROWS, LANES), fixed),
        ],
        out_shape=[
            jax.ShapeDtypeStruct((t, ROW_EXT), f32),
            jax.ShapeDtypeStruct((SUBLANES, t), jnp.int32),
            jax.ShapeDtypeStruct((BUCKET_ROWS, LANES), f32),
        ],
        scratch_shapes=[pltpu.VMEM((BUCKET_ROWS, LANES), f32)],
        compiler_params=_params(("arbitrary", "arbitrary")),
        name="xattn_route",
    )(*mix_args, h, kv, w_q, w_o, g, b, rwt, rb)


def _dispatch_kernel(dest_ref, fill_ref, hext_ref, xs_ref, inv_ref, zeros_ref, sem, zsem):
    base = pl.program_id(0) * DISPATCH_CHUNK
    n_tiles = xs_ref.shape[0] // MOE_TILE

    @pl.when(pl.program_id(0) == 0)
    def _():
        zeros_ref[...] = jnp.zeros_like(zeros_ref)

        def fill(tile):
            return pltpu.make_async_copy(zeros_ref, xs_ref.at[pl.ds(tile * MOE_TILE, MOE_TILE)], zsem)

        for tile in range(n_tiles):
            @pl.when(fill_ref[tile] > 0)
            def _(tile=tile):
                fill(tile).start()

        def clear(j8, carry):
            for u in range(8):
                inv_ref[j8 * 8 + u] = 0
            return carry

        lax.fori_loop(0, inv_ref.shape[0] // 8, clear, 0)
        for tile in range(n_tiles):
            @pl.when(fill_ref[tile] > 0)
            def _(tile=tile):
                fill(tile).wait()

    for j in range(DISPATCH_CHUNK):
        d = dest_ref[base + j]
        inv_ref[d] = base + j
        pltpu.make_async_copy(hext_ref.at[pl.ds(j, 1)], xs_ref.at[pl.ds(d, 1)], sem).start(priority=j % 2)
    pltpu.make_async_copy(hext_ref, xs_ref.at[pl.ds(0, DISPATCH_CHUNK)], sem).wait()


def _dispatch(dest, fill, hext, n_rows):
    t = hext.shape[0]
    return pl.pallas_call(
        _dispatch_kernel,
        grid_spec=pltpu.PrefetchScalarGridSpec(
            num_scalar_prefetch=2,
            grid=(t // DISPATCH_CHUNK,),
            in_specs=[pl.BlockSpec((DISPATCH_CHUNK, ROW_EXT), lambda i, *_: (i, 0))],
            out_specs=[pl.BlockSpec(memory_space=pl.ANY), pl.BlockSpec(memory_space=pltpu.SMEM)],
            scratch_shapes=[pltpu.VMEM((MOE_TILE, ROW_EXT), f32), pltpu.SemaphoreType.DMA(()),
                            pltpu.SemaphoreType.DMA(())],
        ),
        out_shape=[jax.ShapeDtypeStruct((n_rows, ROW_EXT), f32), jax.ShapeDtypeStruct((n_rows,), jnp.int32)],
        compiler_params=_params(("arbitrary",)),
        name="moe_dispatch",
    )(dest, fill, hext)


def _experts_kernel(ea_ref, eb_ref, nvalid_ref, xs_idx_ref, inv_ref, xs_ref, wga_ref, wua_ref, wda_ref,
                    wgb_ref, wub_ref, wdb_ref, g_ref, b_ref, out_ref, obuf_ref, sem_ref):
    i = pl.program_id(0)
    tm = MOE_TILE
    prev = jnp.maximum(i - 1, 0)
    n_prev = jnp.where(i > 0, nvalid_ref[prev], 0)
    slot_prev = prev % 2

    def send(j):
        return pltpu.make_async_copy(obuf_ref.at[slot_prev, pl.ds(j, 1)],
                                     out_ref.at[pl.ds(inv_ref[prev * tm + j], 1)], sem_ref.at[slot_prev])

    n_cur = jnp.where(i < pl.num_programs(0) - 1, nvalid_ref[jnp.minimum(i, pl.num_programs(0) - 2)], 0)

    def send_full(lo, hi):
        for j in range(lo, hi):
            send(j).start(priority=j % 2)

    def compute(between):
        x32 = xs_ref[:, :D_MODEL]
        x = x32.astype(bf16)
        gates = xs_ref[:, D_MODEL:]

        def hidden(wg_ref, wu_ref, gate):
            hg = jnp.dot(x, wg_ref[0, 0].astype(bf16), preferred_element_type=f32)
            hu = jnp.dot(x, wu_ref[0, 0].astype(bf16), preferred_element_type=f32)
            return (_silu(hg) * hu * gate).astype(bf16)

        between(0)
        h_a = hidden(wga_ref, wua_ref, gates[:, 0:1])
        between(1)
        h_b = hidden(wgb_ref, wub_ref, gates[:, 1:2])
        between(2)
        y = (jnp.dot(h_a, wda_ref[0, 0].astype(bf16), preferred_element_type=f32)
             + jnp.dot(h_b, wdb_ref[0, 0].astype(bf16), preferred_element_type=f32))
        between(3)
        obuf_ref[i % 2] = _layer_norm(DEEPNORM_ALPHA * x32 + y, g_ref[...], b_ref[...])

    usual = jnp.logical_and(n_prev == tm, n_cur > 0)
    quarter = tm // 4

    @pl.when(usual)
    def _():
        compute(lambda k: send_full(k * quarter, (k + 1) * quarter))

    @pl.when(jnp.logical_not(usual))
    def _():
        @pl.when(n_prev == tm)
        def _():
            send_full(0, tm)

        @pl.when(jnp.logical_and(n_prev > 0, n_prev < tm))
        def _():
            for j in range(tm):
                @pl.when(j < n_prev)
                def _(j=j):
                    send(j).start(priority=j % 2)

        @pl.when(n_cur > 0)
        def _():
            compute(lambda k: None)

    for bit in reversed(range(tm.bit_length())):
        size = 1 << bit

        @pl.when((n_prev & size) != 0)
        def _(size=size):
            pltpu.make_async_copy(obuf_ref.at[slot_prev, pl.ds(0, size)], out_ref.at[pl.ds(0, size)],
                                  sem_ref.at[slot_prev]).wait()


def _experts(layer, ea, eb, nvalid, xs_idx, inv, xs, w_gate, w_up, w_down, g, b, t):
    n_tiles = xs.shape[0] // MOE_TILE
    up_a = pl.BlockSpec((1, 1, D_MODEL, D_EXPERT), lambda i, ea, *_: (layer, ea[i], 0, 0))
    up_b = pl.BlockSpec((1, 1, D_MODEL, D_EXPERT), lambda i, ea, eb, *_: (layer, eb[i], 0, 0))
    dn_a = pl.BlockSpec((1, 1, D_EXPERT, D_MODEL), lambda i, ea, *_: (layer, ea[i], 0, 0))
    dn_b = pl.BlockSpec((1, 1, D_EXPERT, D_MODEL), lambda i, ea, eb, *_: (layer, eb[i], 0, 0))
    fixed = lambda i, *_: (0, 0)
    return pl.pallas_call(
        _experts_kernel,
        grid_spec=pltpu.PrefetchScalarGridSpec(
            num_scalar_prefetch=5,
            grid=(n_tiles + 1,),
            in_specs=[pl.BlockSpec((MOE_TILE, ROW_EXT), lambda i, ea, eb, nv, xs_idx, inv: (xs_idx[i], 0)),
                      up_a, up_a, dn_a, up_b, up_b, dn_b,
                      pl.BlockSpec((1, D_MODEL), fixed), pl.BlockSpec((1, D_MODEL), fixed)],
            out_specs=pl.BlockSpec(memory_space=pl.ANY),
            scratch_shapes=[pltpu.VMEM((2, MOE_TILE, D_MODEL), f32), pltpu.SemaphoreType.DMA((2,))],
        ),
        out_shape=jax.ShapeDtypeStruct((t, D_MODEL), f32),
        compiler_params=_params(("arbitrary",)),
        name="moe_experts",
    )(ea, eb, nvalid, xs_idx, inv, xs, w_gate, w_up, w_down, w_gate, w_up, w_down, g, b)


def _moe_plan(route, counts):
    t = route.shape[1]
    n_rows = t + N_BUCKETS * MOE_TILE
    n_tiles = n_rows // MOE_TILE
    cnt = counts[:N_BUCKETS, 0].astype(jnp.int32)
    padded = (cnt + MOE_TILE - 1) // MOE_TILE * MOE_TILE
    ends = jnp.cumsum(padded)
    starts = ends - padded
    buckets = jnp.arange(N_BUCKETS, dtype=jnp.int32)
    dest = jnp.sum(jnp.where(route[0][None, :] == buckets[:, None], starts[:, None], 0), axis=0) + route[1]
    tile_start = jnp.arange(n_tiles + 1, dtype=jnp.int32) * MOE_TILE
    tile_bucket = jnp.sum(tile_start[:, None] >= ends[None, :], axis=1).astype(jnp.int32)
    used = tile_bucket < N_BUCKETS
    tile_bucket = jnp.minimum(tile_bucket, N_BUCKETS - 1)
    nvalid = jnp.where(used, jnp.clip(starts[tile_bucket] + cnt[tile_bucket] - tile_start, 0, MOE_TILE), 0)
    xs_idx = jnp.minimum(jnp.arange(n_tiles + 1, dtype=jnp.int32), jnp.maximum(ends[-1] // MOE_TILE - 1, 0))
    last_bucket = tile_bucket[jnp.maximum(ends[-1] // MOE_TILE - 1, 0)]
    tile_bucket = jnp.where(used, tile_bucket, last_bucket)
    pair_a = jnp.array([g * EXPERTS_PER_GROUP + i for g in range(N_GROUPS) for i, _ in _PAIRS], jnp.int32)
    pair_b = jnp.array([g * EXPERTS_PER_GROUP + j for g in range(N_GROUPS) for _, j in _PAIRS], jnp.int32)
    tiles = jnp.arange(n_tiles, dtype=jnp.int32)
    is_last = jnp.any(jnp.logical_and((tiles[:, None] + 1) * MOE_TILE == ends[None, :], padded[None, :] > 0), axis=1)
    fill = jnp.logical_or(is_last, tiles * MOE_TILE >= ends[-1]).astype(jnp.int32)
    return dict(dest=dest, ea=pair_a[tile_bucket], eb=pair_b[tile_bucket], nvalid=nvalid.astype(jnp.int32),
                xs_idx=xs_idx, fill=fill, n_rows=n_rows)


def _moe(layer, hext, route, counts, w_gate, w_up, w_down, g, b):
    plan = _moe_plan(route, counts)
    xs, inv = _dispatch(plan["dest"], plan["fill"], hext, plan["n_rows"])
    return _experts(layer, plan["ea"], plan["eb"], plan["nvalid"], plan["xs_idx"], inv, xs,
                    w_gate, w_up, w_down, g, b, hext.shape[0])


def kernel(x, mem, ab_w_in, ab_conv_w, ab_a_log, ab_dt_bias, ab_norm_w, ab_w_out, sc_w_in, sc_conv_w, sc_w_out, xa_w_q, xa_w_kv, xa_w_o, router_w, router_b, moe_w_gate, moe_w_up, moe_w_down, ln_g, ln_b):
    batch, seq, d = x.shape
    n_mem = mem.shape[1]
    t = batch * seq
    h = x.reshape(t, d)
    mem2 = mem.reshape(batch * n_mem, d)
    rwt = jnp.transpose(router_w)
    rb = router_b.reshape(N_EXPERTS, 1)
    row = lambda v: v.reshape(1, -1)

    for layer in range(DEPTH):
        i = layer // 2
        if layer % 2 == 0:
            w_in = ab_w_in[i]
            c0, c1, c2 = 3 * DN_WIDTH, 4 * DN_WIDTH, 4 * DN_WIDTH + 2 * DN_HEADS
            w_ba = jnp.pad(w_in[:, c1:c2], ((0, 0), (0, LANES - 2 * DN_HEADS)))
            lane_pad = lambda v: jnp.pad(v.reshape(1, DN_HEADS), ((0, 0), (DN_HEADS, LANES - 2 * DN_HEADS)))
            act, z, bg, qkv_d = _project_ab(h, w_in[:, :c0].astype(bf16), w_in[:, c0:c1].astype(bf16), w_ba.astype(bf16),
                                            w_in[:, c2:].astype(bf16), ab_conv_w[i], lane_pad(ab_a_log[i]),
                                            lane_pad(ab_dt_bias[i]), seq)
            o_a = _deltanet(act, bg, z, row(ab_norm_w[i]), batch, seq)
            o_b = _dilated_attention(qkv_d, batch, seq).reshape(t, DA_WIDTH)
            mix = (o_a, o_b, ab_w_out[i].astype(bf16), row(ln_g[layer, 0]), row(ln_b[layer, 0]))
        else:
            h = _shortconv(h, sc_w_in[i].astype(bf16), sc_conv_w[i], sc_w_out[i].astype(bf16),
                           row(ln_g[layer, 0]), row(ln_b[layer, 0]), batch, seq)
            mix = None
        (kv,) = _project(mem2, [xa_w_kv[layer].astype(bf16)], [bf16], n_mem)
        w_q = (xa_w_q[layer] * (XA_HEAD_DIM ** -0.5 * LOG2_E)).astype(bf16)
        hext, route, counts = _xattn_route(h, kv, w_q, xa_w_o[layer].astype(bf16),
                                           row(ln_g[layer, 1]), row(ln_b[layer, 1]), rwt, rb, batch, seq, n_mem, mix)
        h = _moe(layer, hext, route, counts, moe_w_gate, moe_w_up, moe_w_down,
                 row(ln_g[layer, 2]), row(ln_b[layer, 2]))
    return h.reshape(batch, seq, d)
```

```python
import functools

import jax
import jax.numpy as jnp
from jax import lax
from jax.experimental import pallas as pl
from jax.experimental.pallas import tpu as pltpu

D_MODEL = 1024
DEPTH = 2
DN_HEADS = 4
DN_HEAD_DIM = 128
DN_WIDTH = DN_HEADS * DN_HEAD_DIM
DN_CONV = 4
DN_CHUNK = 64
DA_HEADS = 8
DA_HEAD_DIM = 64
DA_WIDTH = DA_HEADS * DA_HEAD_DIM
DA_BRANCHES = ((128, 1), (512, 4), (2048, 16))
DA_BLOCK = 128
DA_RESIDUES = max(d for _, d in DA_BRANCHES)
SC_CONV = 3
XA_HEADS = 4
XA_HEAD_DIM = D_MODEL // XA_HEADS
N_EXPERTS = 16
N_GROUPS = 4
EXPERTS_PER_GROUP = N_EXPERTS // N_GROUPS
D_EXPERT = D_MODEL // 2
DEEPNORM_ALPHA = (2 * DEPTH) ** 0.25
LN_EPS = 1e-5
RMS_EPS = 1e-6

LANES = 128
SUBLANES = 8
VMEM_LIMIT_BYTES = 48 * 1024 * 1024

TOKEN_TILE = 512
XA_TILE = 1024
SC_TILE = 1024
DN_BLOCK = 512
MOE_TILE = 256
DISPATCH_CHUNK = 512
DA_UNROLL = 4
GATE_LANES = LANES
ROW_EXT = D_MODEL + GATE_LANES

_PAIRS = tuple((i, j) for i in range(EXPERTS_PER_GROUP) for j in range(i + 1, EXPERTS_PER_GROUP))
N_BUCKETS = N_GROUPS * len(_PAIRS)
BUCKET_ROWS = 32

NEG_BIG = -1e30
LOG2_E = 1.4426950408889634

bf16 = jnp.bfloat16
f32 = jnp.float32


def _params(semantics):
    return pltpu.CompilerParams(dimension_semantics=semantics, vmem_limit_bytes=VMEM_LIMIT_BYTES)


def _dot(a, b):
    return jnp.dot(a.astype(bf16), b.astype(bf16), preferred_element_type=f32)


def _dot_nt(a, b):
    return lax.dot_general(a.astype(bf16), b.astype(bf16), (((1,), (1,)), ((), ())), preferred_element_type=f32)


def _dot_tn(a, b):
    return lax.dot_general(a.astype(bf16), b.astype(bf16), (((0,), (0,)), ((), ())), preferred_element_type=f32)


def _split(a, parts):
    out = []
    rem = a
    for _ in range(parts):
        p = rem.astype(bf16)
        out.append(p)
        rem = rem - p.astype(f32)
    return out


def _layer_norm(v, g, b):
    mu = jnp.mean(v, -1, keepdims=True)
    c = v - mu
    var = jnp.mean(c * c, -1, keepdims=True)
    return c * lax.rsqrt(var + LN_EPS) * g + b


def _silu(v):
    return v * jax.nn.sigmoid(v)


def _softplus(v):
    return jnp.maximum(v, 0.0) + jnp.log(1.0 + jnp.exp(-jnp.abs(v)))


def _proj_kernel(x_ref, *refs, n_out):
    w_refs, o_refs = refs[:n_out], refs[n_out:]
    xb = x_ref[...].astype(bf16)
    for w_ref, o_ref in zip(w_refs, o_refs):
        n = w_ref.shape[1]
        for c0 in range(0, n, 512):
            c1 = min(c0 + 512, n)
            o_ref[:, c0:c1] = jnp.dot(xb, w_ref[:, c0:c1], preferred_element_type=f32).astype(o_ref.dtype)


def _project(x, ws, out_dtypes, tm):
    t, k = x.shape
    n_out = len(ws)
    return pl.pallas_call(
        functools.partial(_proj_kernel, n_out=n_out),
        grid=(t // tm,),
        in_specs=[pl.BlockSpec((tm, k), lambda i: (i, 0))]
        + [pl.BlockSpec(w.shape, lambda i: (0, 0)) for w in ws],
        out_specs=[pl.BlockSpec((tm, w.shape[1]), lambda i: (i, 0)) for w in ws],
        out_shape=[jax.ShapeDtypeStruct((t, w.shape[1]), dt) for w, dt in zip(ws, out_dtypes)],
        compiler_params=_params(("parallel",)),
        name="project",
    )(x, *ws)


def _proj_ab_kernel(x_ref, wqkv_ref, wz_ref, wba_ref, wd_ref, cw_ref, alog_ref, dtb_ref,
                    act_ref, z_ref, bg_ref, qkvd_ref, xx_ref, *, tiles_per_seq):
    @pl.when(pl.program_id(0) % tiles_per_seq == 0)
    def _():
        xx_ref[0:SUBLANES, :] = jnp.zeros((SUBLANES, xx_ref.shape[1]), f32)

    tm, hd = x_ref.shape[0], DN_HEAD_DIM
    xb = x_ref[...].astype(bf16)
    cw = cw_ref[...]

    def conv_act(c):
        cols = slice(c * hd, (c + 1) * hd)
        conv = xx_ref[SUBLANES:, cols] * cw[DN_CONV - 1:DN_CONV, cols]
        for j in range(DN_CONV - 1):
            back = DN_CONV - 1 - j
            conv = conv + xx_ref[pl.ds(SUBLANES - back, tm), cols] * cw[j:j + 1, cols]
        a = _silu(conv)
        if c < 2 * DN_HEADS:
            scale = hd ** -0.5 if c < DN_HEADS else 1.0
            a = a * (lax.rsqrt(jnp.sum(a * a, -1, keepdims=True) + RMS_EPS) * scale)
        act_ref[:, cols] = a

    for c0 in range(0, 3 * DN_WIDTH, 512):
        xx_ref[SUBLANES:, c0:c0 + 512] = jnp.dot(xb, wqkv_ref[:, c0:c0 + 512], preferred_element_type=f32)
        for c in range(c0 // hd, (c0 + 512) // hd):
            conv_act(c)
    z_ref[...] = jnp.dot(xb, wz_ref[...], preferred_element_type=f32)
    for c0 in range(0, 3 * DA_WIDTH, 512):
        qkvd_ref[:, c0:c0 + 512] = jnp.dot(xb, wd_ref[:, c0:c0 + 512], preferred_element_type=f32)
    ba = jnp.dot(xb, wba_ref[...], preferred_element_type=f32)
    is_beta = lax.broadcasted_iota(jnp.int32, ba.shape, 1) < DN_HEADS
    bg_ref[...] = jnp.where(is_beta, jax.nn.sigmoid(ba), -jnp.exp(alog_ref[...]) * _softplus(ba + dtb_ref[...]))
    xx_ref[0:SUBLANES, :] = xx_ref[tm:tm + SUBLANES, :]


def _project_ab(x, w_qkv, w_z, w_ba, w_d, conv_w, alog_row, dtb_row, seq):
    t, k = x.shape
    tm = TOKEN_TILE
    fixed = lambda i: (0, 0)
    row = lambda i: (i, 0)
    widths = (3 * DN_WIDTH, DN_WIDTH, LANES, 3 * DA_WIDTH)
    return pl.pallas_call(
        functools.partial(_proj_ab_kernel, tiles_per_seq=seq // tm),
        grid=(t // tm,),
        in_specs=[pl.BlockSpec((tm, k), row)]
        + [pl.BlockSpec(w.shape, fixed) for w in (w_qkv, w_z, w_ba, w_d, conv_w, alog_row, dtb_row)],
        out_specs=[pl.BlockSpec((tm, n), row) for n in widths],
        out_shape=[jax.ShapeDtypeStruct((t, n), f32) for n in widths],
        scratch_shapes=[pltpu.VMEM((SUBLANES + tm, 3 * DN_WIDTH), f32)],
        compiler_params=_params(("arbitrary",)),
        name="project_ab",
    )(x, w_qkv, w_z, w_ba, w_d, conv_w, alog_row, dtb_row)


def _deltanet_kernel(act_ref, bg_ref, z_ref, nw_ref, o_ref, state_ref):
    @pl.when(pl.program_id(1) == 0)
    def _():
        state_ref[...] = jnp.zeros_like(state_ref)

    rows, ch, hd, width = DN_BLOCK, DN_CHUNK, DN_HEAD_DIM, DN_WIDTH
    sup = 2 * ch
    n_sup = rows // sup

    act = act_ref[...]
    beta_all = bg_ref[...]
    g_all = beta_all

    ri = lax.broadcasted_iota(jnp.int32, (sup, sup), 0)
    ci = lax.broadcasted_iota(jnp.int32, (sup, sup), 1)
    same = (ri >= ch) == (ci >= ch)
    incl = jnp.logical_and(ri >= ci, same)
    strict = jnp.logical_and(ri > ci, same)
    lower_ones = jnp.where(incl, 1.0, 0.0).astype(bf16)
    eye = jnp.where(ri == ci, 1.0, 0.0)
    first_rows = lax.broadcasted_iota(jnp.int32, (sup, hd), 0) < ch
    z = z_ref[...]
    nw = nw_ref[...]

    chains = [(s, h) for s in range(n_sup) for h in range(DN_HEADS)]
    g_cum_all, g_cum_t = [], []
    for s in range(n_sup):
        g_blk = g_all[s * sup:(s + 1) * sup, :]
        gc = sum(jnp.dot(lower_ones, p, preferred_element_type=f32) for p in _split(g_blk, 2))
        g_cum_all.append(gc)
        g_cum_t.append(jnp.transpose(gc))

    st = {}
    for s, h in chains:
        r0 = s * sup
        q = act[r0:r0 + sup, h * hd:(h + 1) * hd]
        k = act[r0:r0 + sup, width + h * hd:width + (h + 1) * hd]
        v = act[r0:r0 + sup, 2 * width + h * hd:2 * width + (h + 1) * hd]
        beta_b = jnp.broadcast_to(beta_all[r0:r0 + sup, h:h + 1], (sup, hd))
        g_i = jnp.broadcast_to(g_cum_all[s][:, DN_HEADS + h:DN_HEADS + h + 1], (sup, hd))
        g_j = jnp.broadcast_to(g_cum_t[s][DN_HEADS + h:DN_HEADS + h + 1, :], (sup, sup))
        decay = jnp.where(incl, jnp.exp(jnp.minimum(g_i - g_j, 0.0)), 0.0)
        e_g = jnp.exp(g_i)
        g_last = jnp.where(first_rows, g_i[ch - 1:ch, :], g_i[sup - 1:sup, :])
        st[s, h] = dict(q=q, k=k, kb=k * beta_b, vb=v * beta_b, decay=decay, e_g=e_g,
                        k_tail=k * jnp.exp(g_last - g_i),
                        gl=(jnp.exp(g_i[ch - 1:ch, :]), jnp.exp(g_i[sup - 1:sup, :])))

    for c in chains:
        d = st[c]
        d["a"] = jnp.where(strict, _dot_nt(d["kb"], d["k"]) * d["decay"], 0.0)
        d["attn"] = jnp.where(incl, _dot_nt(d["q"], d["k"]) * d["decay"], 0.0)
    for c in chains:
        st[c]["t_inv"] = eye - st[c]["a"]
        st[c]["pw"] = st[c]["a"]
    for _ in range(5):
        for c in chains:
            st[c]["pw"] = _dot(st[c]["pw"], st[c]["pw"])
        for c in chains:
            st[c]["t_inv"] = st[c]["t_inv"] + _dot(st[c]["t_inv"], st[c]["pw"])
    for c in chains:
        d = st[c]
        sol = _dot(d["t_inv"], jnp.concatenate([d["vb"], d["kb"] * d["e_g"]], axis=1))
        d["u"], d["w"] = sol[:, :hd], sol[:, hd:]
    for c in chains:
        d = st[c]
        uw = jnp.concatenate([d["u"], d["w"]], axis=1).astype(bf16)
        mix = jnp.dot(d["attn"].astype(bf16), uw, preferred_element_type=f32)
        d["au"] = mix[:, :hd]
        d["qe"] = d["q"] * d["e_g"] - mix[:, hd:]
        kt = d["k_tail"].astype(bf16)
        zero = jnp.zeros_like(kt)
        d["nu_pw"] = [_dot_tn(jnp.where(first_rows == first, kt, zero), uw)
                      for first in (True, False)]

    for s in range(n_sup):
        for half in range(2):
            for h in range(DN_HEADS):
                d = st[s, h]
                state = state_ref[h]
                lo = half * ch
                o = _dot(d["qe"][lo:lo + ch], state) + d["au"][lo:lo + ch]
                nu_pw = d["nu_pw"][half]
                state_ref[h] = state * d["gl"][half] - _dot(nu_pw[:, hd:], state) + nu_pw[:, :hd]
                o = o * lax.rsqrt(jnp.mean(o * o, -1, keepdims=True) + RMS_EPS) * nw
                r0 = s * sup + lo
                o_ref[r0:r0 + ch, h * hd:(h + 1) * hd] = (o * _silu(z[r0:r0 + ch, h * hd:(h + 1) * hd])).astype(o_ref.dtype)


def _deltanet(act, bg, z, norm_w, batch, seq):
    nb = seq // DN_BLOCK
    row = lambda b, c: (b * nb + c, 0)
    fixed = lambda b, c: (0, 0)
    return pl.pallas_call(
        _deltanet_kernel,
        grid=(batch, nb),
        in_specs=[
            pl.BlockSpec((DN_BLOCK, 3 * DN_WIDTH), row),
            pl.BlockSpec((DN_BLOCK, LANES), row),
            pl.BlockSpec((DN_BLOCK, DN_WIDTH), row),
            pl.BlockSpec((1, DN_HEAD_DIM), fixed),
        ],
        out_specs=pl.BlockSpec((DN_BLOCK, DN_WIDTH), row),
        out_shape=jax.ShapeDtypeStruct((batch * seq, DN_WIDTH), bf16),
        scratch_shapes=[pltpu.VMEM((DN_HEADS, DN_HEAD_DIM, DN_HEAD_DIM), f32)],
        compiler_params=_params(("arbitrary", "arbitrary")),
        name="deltanet",
    )(act, bg, z, norm_w)


def _dilated_kernel(q_ref, k_ref, v_ref, o_ref, qs_ref, ks_ref, vs_ref, ob_ref, lse_ref, s_ref, p_ref, st_ref, *, seq):
    blk, res = DA_BLOCK, DA_RESIDUES
    per = seq // res
    head0 = lax.broadcasted_iota(jnp.int32, (blk, LANES), 1) < DA_HEAD_DIM
    ai = lax.broadcasted_iota(jnp.int32, (2 * blk, blk), 0) & (blk - 1)
    bi = lax.broadcasted_iota(jnp.int32, (2 * blk, blk), 1)
    ones_blk = jnp.ones((2 * blk, LANES), bf16)

    q_scale = DA_HEAD_DIM ** -0.5 * LOG2_E
    for r in range(res):
        rows = pl.ds(r, per, stride=res)
        qs_ref[r * per:(r + 1) * per, :] = q_ref[0, rows, :] * q_scale
        ks_ref[r * per:(r + 1) * per, :] = k_ref[0, rows, :]
        vs_ref[r * per:(r + 1) * per, :] = v_ref[0, rows, :]

    for br, (window, dil) in enumerate(DA_BRANCHES):
        assert window // dil == blk and res % dil == 0
        runs = res // dil
        run_len = blk // runs
        nblk = seq // (dil * blk)
        shift = run_len.bit_length() - 1

        def pos(a, runs=runs, run_len=run_len, shift=shift):
            return runs * (a & (run_len - 1)) + (a >> shift)

        in_window = jnp.concatenate([pos(bi) >= pos(ai), pos(bi) <= pos(ai)], axis=1)

        n_iter = dil * nblk // DA_UNROLL
        assert n_iter * DA_UNROLL == dil * nblk

        def tile(ref, offs, run_len=run_len, runs=runs):
            parts = [ref[pl.ds(o, run_len), :] for o in offs]
            return parts[0] if runs == 1 else jnp.concatenate(parts, axis=0)

        def blocks_of(it, dil=dil, runs=runs, run_len=run_len, nblk=nblk):
            out = []
            for u in range(DA_UNROLL):
                idx = it * DA_UNROLL + u
                cls = idx // nblk
                n = idx - cls * nblk
                n_prev = jnp.maximum(n - 1, 0)
                cur = [pl.multiple_of((dil * m + cls) * per + run_len * n, SUBLANES) for m in range(runs)]
                prev = [pl.multiple_of((dil * m + cls) * per + run_len * n_prev, SUBLANES) for m in range(runs)]
                out.append((cur, prev, n > 0))
            return out

        def masked_scores(blocks, tile=tile, in_window=in_window):
            out = []
            for cur, prev, has_prev in blocks:
                q = tile(qs_ref, cur).astype(bf16)
                zero = jnp.zeros_like(q)
                q2 = jnp.concatenate([jnp.where(head0, q, zero), jnp.where(head0, zero, q)], axis=0)
                keys = jnp.concatenate([tile(ks_ref, prev), tile(ks_ref, cur)], axis=0)
                s = jnp.where(in_window, _dot_nt(q2, keys), NEG_BIG)
                out.append(jnp.concatenate([jnp.where(has_prev, s[:, :blk], NEG_BIG), s[:, blk:]], axis=1))
            return out

        def softmax_stage(blocks, scores):
            out = []
            for s in scores:
                top = jnp.max(s, -1, keepdims=True)
                out.append((jnp.exp2(s - top).astype(bf16), jnp.where(head0, top[:blk], top[blk:])))
            return out

        def output_stage(blocks, probs, br=br, run_len=run_len, tile=tile):
            for (cur, prev, has_prev), (pb, top) in zip(blocks, probs):
                vals = jnp.concatenate([tile(vs_ref, prev), tile(vs_ref, cur)], axis=0).astype(bf16)
                acc = jnp.dot(pb, jnp.concatenate([vals, ones_blk], axis=1), preferred_element_type=f32)
                den = jnp.where(head0, acc[:blk, LANES:], acc[blk:, LANES:])
                o_blk = jnp.where(head0, acc[:blk, :LANES], acc[blk:, :LANES]) / den
                lse_blk = top + jnp.log2(den)
                for m, o in enumerate(cur):
                    ob_ref[br, pl.ds(o, run_len), :] = o_blk[m * run_len:(m + 1) * run_len]
                    lse_ref[br, pl.ds(o, run_len), :] = lse_blk[m * run_len:(m + 1) * run_len]

        def put_scores(scores):
            for u, sc in enumerate(scores):
                s_ref[u] = sc

        def put_probs(probs):
            for u, (pb, top) in enumerate(probs):
                p_ref[u] = pb
                st_ref[u] = top

        put_probs(softmax_stage(blocks_of(0), masked_scores(blocks_of(0))))
        put_scores(masked_scores(blocks_of(min(1, n_iter - 1))))

        def body(it, carry, n_iter=n_iter, blocks_of=blocks_of, masked_scores=masked_scores,
                 softmax_stage=softmax_stage, output_stage=output_stage, put_scores=put_scores, put_probs=put_probs):
            scores = [s_ref[u] for u in range(DA_UNROLL)]
            probs = [(p_ref[u], st_ref[u]) for u in range(DA_UNROLL)]
            output_stage(blocks_of(it), probs)
            next_probs = softmax_stage(blocks_of(jnp.minimum(it + 1, n_iter - 1)), scores)
            next_scores = masked_scores(blocks_of(jnp.minimum(it + 2, n_iter - 1)))
            put_scores(next_scores)
            put_probs(next_probs)
            return carry

        lax.fori_loop(0, n_iter, body, 0)

    for r in range(res):
        rows = slice(r * per, (r + 1) * per)
        lses = [lse_ref[br, rows, :] for br in range(len(DA_BRANCHES))]
        top = functools.reduce(jnp.maximum, lses)
        wts = [jnp.exp2(l - top) for l in lses]
        num = sum(w * ob_ref[br, rows, :] for br, w in enumerate(wts))
        o_ref[0, pl.ds(r, per, stride=res), :] = num / sum(wts)


def _dilated_attention(qkv3, batch, seq):
    pairs = DA_WIDTH // LANES
    n_br = len(DA_BRANCHES)
    spec = lambda off: pl.BlockSpec((1, seq, LANES), lambda b, p, off=off: (b, 0, off + p))
    return pl.pallas_call(
        functools.partial(_dilated_kernel, seq=seq),
        grid=(batch, pairs),
        in_specs=[spec(0), spec(pairs), spec(2 * pairs)],
        out_specs=pl.BlockSpec((1, seq, LANES), lambda b, p: (b, 0, p)),
        out_shape=jax.ShapeDtypeStruct((batch, seq, DA_WIDTH), f32),
        scratch_shapes=[pltpu.VMEM((seq, LANES), f32)] * 3
        + [pltpu.VMEM((n_br, seq, LANES), f32)] * 2
        + [pltpu.VMEM((DA_UNROLL, 2 * DA_BLOCK, 2 * DA_BLOCK), f32),
           pltpu.VMEM((DA_UNROLL, 2 * DA_BLOCK, 2 * DA_BLOCK), bf16),
           pltpu.VMEM((DA_UNROLL, DA_BLOCK, LANES), f32)],
        compiler_params=_params(("parallel", "parallel")),
        name="dilated_attention",
    )(qkv3, qkv3, qkv3)


def _shortconv_kernel(h_ref, win_ref, cw_ref, wout_ref, g_ref, b_ref, o_ref, tail_ref):
    @pl.when(pl.program_id(1) == 0)
    def _():
        tail_ref[...] = jnp.zeros_like(tail_ref)

    tm = h_ref.shape[0]
    d = D_MODEL
    h = h_ref[...]
    hb = h.astype(bf16)
    gate_b = jnp.dot(hb, win_ref[:, :d], preferred_element_type=f32)
    gate_c = jnp.dot(hb, win_ref[:, d:2 * d], preferred_element_type=f32)
    hid = jnp.dot(hb, win_ref[:, 2 * d:], preferred_element_type=f32)
    u = gate_c * hid
    uu = jnp.concatenate([tail_ref[...], u], axis=0)
    cw = cw_ref[...]
    conv = u * cw[SC_CONV - 1:SC_CONV, :]
    for j in range(SC_CONV - 1):
        back = SC_CONV - 1 - j
        conv = conv + uu[SUBLANES - back:SUBLANES - back + tm, :] * cw[j:j + 1, :]
    tail_ref[...] = u[tm - SUBLANES:, :]
    y = jnp.dot((gate_b * conv).astype(bf16), wout_ref[...], preferred_element_type=f32)
    o_ref[...] = _layer_norm(DEEPNORM_ALPHA * h + y, g_ref[...], b_ref[...])


def _shortconv(h, w_in, conv_w, w_out, g, b, batch, seq):
    tm = SC_TILE
    ns = seq // tm
    row = lambda bb, s: (bb * ns + s, 0)
    fixed = lambda bb, s: (0, 0)
    once = dict(pipeline_mode=pl.Buffered(1))
    return pl.pallas_call(
        _shortconv_kernel,
        grid=(batch, ns),
        in_specs=[
            pl.BlockSpec((tm, D_MODEL), row),
            pl.BlockSpec(w_in.shape, fixed, **once),
            pl.BlockSpec(conv_w.shape, fixed),
            pl.BlockSpec(w_out.shape, fixed, **once),
            pl.BlockSpec((1, D_MODEL), fixed),
            pl.BlockSpec((1, D_MODEL), fixed),
        ],
        out_specs=pl.BlockSpec((tm, D_MODEL), row),
        out_shape=jax.ShapeDtypeStruct((batch * seq, D_MODEL), f32),
        scratch_shapes=[pltpu.VMEM((SUBLANES, D_MODEL), f32)],
        compiler_params=_params(("arbitrary", "arbitrary")),
        name="shortconv",
    )(h, w_in, conv_w, w_out, g, b)


def _route(h2, rwt_ref, rb_ref, cnt_ref):
    tm = h2.shape[0]
    w_hi, w_lo = _split(rwt_ref[...], 2)
    x_hi, x_lo = _split(h2, 2)
    nt = (((1,), (1,)), ((), ()))
    logits = (lax.dot_general(w_hi, x_hi, nt, preferred_element_type=f32)
              + lax.dot_general(w_hi, x_lo, nt, preferred_element_type=f32)
              + lax.dot_general(w_lo, x_hi, nt, preferred_element_type=f32))
    mx = jnp.max(logits, axis=0, keepdims=True)
    ex = jnp.exp(logits - mx)
    scores = ex / jnp.sum(ex, axis=0, keepdims=True)
    sel = scores + rb_ref[...]
    best = jnp.full((1, tm), -jnp.inf, f32)
    bucket = jnp.zeros((1, tm), jnp.int32)
    s_a = jnp.zeros((1, tm), f32)
    s_b = jnp.zeros((1, tm), f32)
    for grp in range(N_GROUPS):
        for p, (i, j) in enumerate(_PAIRS):
            a, b = grp * EXPERTS_PER_GROUP + i, grp * EXPERTS_PER_GROUP + j
            ps = sel[a:a + 1, :] + sel[b:b + 1, :]
            upd = ps > best
            best = jnp.where(upd, ps, best)
            bucket = jnp.where(upd, grp * len(_PAIRS) + p, bucket)
            s_a = jnp.where(upd, scores[a:a + 1, :], s_a)
            s_b = jnp.where(upd, scores[b:b + 1, :], s_b)
    denom = s_a + s_b
    gate_rows = jnp.concatenate([s_a / denom, s_b / denom, jnp.zeros((GATE_LANES - 2, tm), f32)], axis=0)
    gates = jnp.transpose(gate_rows)
    onehot = jnp.where(lax.broadcasted_iota(jnp.int32, (BUCKET_ROWS, tm), 0) == bucket, 1.0, 0.0)
    before = lax.broadcasted_iota(jnp.int32, (tm, tm), 0) < lax.broadcasted_iota(jnp.int32, (tm, tm), 1)
    prefix = jnp.dot(onehot.astype(bf16), jnp.where(before, 1.0, 0.0).astype(bf16), preferred_element_type=f32)
    cnt = cnt_ref[...]
    rank = jnp.sum(onehot * (prefix + cnt[:, 0:1]), axis=0, keepdims=True)
    cnt_ref[...] = cnt + jnp.sum(onehot, axis=1, keepdims=True)
    return gates, bucket, rank.astype(jnp.int32)


def _xattn_kernel(*refs, mix):
    if mix:
        (oa_ref, ob_ref, wmix_ref, g0_ref, b0_ref), refs = refs[:5], refs[5:]
    (h_ref, kv_ref, wq_ref, wo_ref, g_ref, b_ref, rwt_ref, rb_ref,
     hext_ref, route_ref, cnt_out_ref, cnt_ref) = refs

    @pl.when(jnp.logical_and(pl.program_id(0) == 0, pl.program_id(1) == 0))
    def _():
        cnt_ref[...] = jnp.zeros_like(cnt_ref)

    h = h_ref[...]
    tm = h.shape[0]
    if mix:
        y0 = (jnp.dot(oa_ref[...], wmix_ref[:DN_WIDTH], preferred_element_type=f32)
              + jnp.dot(ob_ref[...].astype(bf16), wmix_ref[DN_WIDTH:], preferred_element_type=f32))
        h = _layer_norm(DEEPNORM_ALPHA * h + y0, g0_ref[...], b0_ref[...])
    q = jnp.dot(h.astype(bf16), wq_ref[...], preferred_element_type=f32).astype(bf16)
    heads = [(hd * XA_HEAD_DIM, (hd + 1) * XA_HEAD_DIM) for hd in range(XA_HEADS)]
    scores = [_dot_nt(q[:, lo:hi], kv_ref[:, lo:hi]) for lo, hi in heads]
    probs = [jnp.exp2(s - jnp.max(s, -1, keepdims=True)) for s in scores]
    outs = [(_dot(p, kv_ref[:, D_MODEL + lo:D_MODEL + hi]) / jnp.sum(p, -1, keepdims=True)).astype(bf16)
            for p, (lo, hi) in zip(probs, heads)]
    y = jnp.dot(jnp.concatenate(outs, axis=1), wo_ref[...], preferred_element_type=f32)
    h2 = _layer_norm(DEEPNORM_ALPHA * h + y, g_ref[...], b_ref[...])
    gates, bucket, rank = _route(h2, rwt_ref, rb_ref, cnt_ref)
    hext_ref[:, :D_MODEL] = h2
    hext_ref[:, D_MODEL:] = gates
    route_ref[...] = jnp.concatenate([bucket, rank, jnp.zeros((SUBLANES - 2, tm), jnp.int32)], axis=0)
    cnt_out_ref[...] = cnt_ref[...]


def _xattn_route(h, kv, w_q, w_o, g, b, rwt, rb, batch, seq, n_mem, mix=None):
    tm = XA_TILE
    ns = seq // tm
    t = batch * seq
    row = lambda bb, s: (bb * ns + s, 0)
    fixed = lambda bb, s: (0, 0)
    once = dict(pipeline_mode=pl.Buffered(1))
    mix_specs, mix_args = [], []
    if mix is not None:
        o_a, o_b, w_mix, g0, b0 = mix
        mix_specs = [pl.BlockSpec((tm, DN_WIDTH), row), pl.BlockSpec((tm, DA_WIDTH), row),
                     pl.BlockSpec(w_mix.shape, fixed, **once),
                     pl.BlockSpec((1, D_MODEL), fixed), pl.BlockSpec((1, D_MODEL), fixed)]
        mix_args = [o_a, o_b, w_mix, g0, b0]
    return pl.pallas_call(
        functools.partial(_xattn_kernel, mix=mix is not None),
        grid=(batch, ns),
        in_specs=mix_specs + [
            pl.BlockSpec((tm, D_MODEL), row),
            pl.BlockSpec((n_mem, 2 * D_MODEL), lambda bb, s: (bb, 0)),
            pl.BlockSpec(w_q.shape, fixed, **once),
            pl.BlockSpec(w_o.shape, fixed, **once),
            pl.BlockSpec((1, D_MODEL), fixed),
            pl.BlockSpec((1, D_MODEL), fixed),
            pl.BlockSpec(rwt.shape, fixed),
            pl.BlockSpec(rb.shape, fixed),
        ],
        out_specs=[
            pl.BlockSpec((tm, ROW_EXT), row),
            pl.BlockSpec((SUBLANES, tm), lambda bb, s: (0, bb * ns + s)),
            pl.BlockSpec((BUCKET_ROWS, LANES), fixed),
        ],
        out_shape=[
            jax.ShapeDtypeStruct((t, ROW_EXT), f32),
            jax.ShapeDtypeStruct((SUBLANES, t), jnp.int32),
            jax.ShapeDtypeStruct((BUCKET_ROWS, LANES), f32),
        ],
        scratch_shapes=[pltpu.VMEM((BUCKET_ROWS, LANES), f32)],
        compiler_params=_params(("arbitrary", "arbitrary")),
        name="xattn_route",
    )(*mix_args, h, kv, w_q, w_o, g, b, rwt, rb)


def _dispatch_kernel(dest_ref, fill_ref, hext_ref, xs_ref, inv_ref, zeros_ref, sem, zsem):
    base = pl.program_id(0) * DISPATCH_CHUNK
    n_tiles = xs_ref.shape[0] // MOE_TILE

    @pl.when(pl.program_id(0) == 0)
    def _():
        zeros_ref[...] = jnp.zeros_like(zeros_ref)

        def fill(tile):
            return pltpu.make_async_copy(zeros_ref, xs_ref.at[pl.ds(tile * MOE_TILE, MOE_TILE)], zsem)

        for tile in range(n_tiles):
            @pl.when(fill_ref[tile] > 0)
            def _(tile=tile):
                fill(tile).start()

        def clear(j8, carry):
            for u in range(8):
                inv_ref[j8 * 8 + u] = 0
            return carry

        lax.fori_loop(0, inv_ref.shape[0] // 8, clear, 0)
        for tile in range(n_tiles):
            @pl.when(fill_ref[tile] > 0)
            def _(tile=tile):
                fill(tile).wait()

    for j in range(DISPATCH_CHUNK):
        d = dest_ref[base + j]
        inv_ref[d] = base + j
        pltpu.make_async_copy(hext_ref.at[pl.ds(j, 1)], xs_ref.at[pl.ds(d, 1)], sem).start(priority=j % 2)
    pltpu.make_async_copy(hext_ref, xs_ref.at[pl.ds(0, DISPATCH_CHUNK)], sem).wait()


def _dispatch(dest, fill, hext, n_rows):
    t = hext.shape[0]
    return pl.pallas_call(
        _dispatch_kernel,
        grid_spec=pltpu.PrefetchScalarGridSpec(
            num_scalar_prefetch=2,
            grid=(t // DISPATCH_CHUNK,),
            in_specs=[pl.BlockSpec((DISPATCH_CHUNK, ROW_EXT), lambda i, *_: (i, 0))],
            out_specs=[pl.BlockSpec(memory_space=pl.ANY), pl.BlockSpec(memory_space=pltpu.SMEM)],
            scratch_shapes=[pltpu.VMEM((MOE_TILE, ROW_EXT), f32), pltpu.SemaphoreType.DMA(()),
                            pltpu.SemaphoreType.DMA(())],
        ),
        out_shape=[jax.ShapeDtypeStruct((n_rows, ROW_EXT), f32), jax.ShapeDtypeStruct((n_rows,), jnp.int32)],
        compiler_params=_params(("arbitrary",)),
        name="moe_dispatch",
    )(dest, fill, hext)


def _experts_kernel(ea_ref, eb_ref, nvalid_ref, xs_idx_ref, inv_ref, xs_ref, wga_ref, wua_ref, wda_ref,
                    wgb_ref, wub_ref, wdb_ref, g_ref, b_ref, out_ref, obuf_ref, sem_ref):
    i = pl.program_id(0)
    tm = MOE_TILE
    prev = jnp.maximum(i - 1, 0)
    n_prev = jnp.where(i > 0, nvalid_ref[prev], 0)
    slot_prev = prev % 2

    def send(j):
        return pltpu.make_async_copy(obuf_ref.at[slot_prev, pl.ds(j, 1)],
                                     out_ref.at[pl.ds(inv_ref[prev * tm + j], 1)], sem_ref.at[slot_prev])

    n_cur = jnp.where(i < pl.num_programs(0) - 1, nvalid_ref[jnp.minimum(i, pl.num_programs(0) - 2)], 0)

    def send_full(lo, hi):
        for j in range(lo, hi):
            send(j).start(priority=j % 2)

    def compute(between):
        x32 = xs_ref[:, :D_MODEL]
        x = x32.astype(bf16)
        gates = xs_ref[:, D_MODEL:]

        def hidden(wg_ref, wu_ref, gate):
            hg = jnp.dot(x, wg_ref[0, 0].astype(bf16), preferred_element_type=f32)
            hu = jnp.dot(x, wu_ref[0, 0].astype(bf16), preferred_element_type=f32)
            return (_silu(hg) * hu * gate).astype(bf16)

        between(0)
        h_a = hidden(wga_ref, wua_ref, gates[:, 0:1])
        between(1)
        h_b = hidden(wgb_ref, wub_ref, gates[:, 1:2])
        between(2)
        y = (jnp.dot(h_a, wda_ref[0, 0].astype(bf16), preferred_element_type=f32)
             + jnp.dot(h_b, wdb_ref[0, 0].astype(bf16), preferred_element_type=f32))
        between(3)
        obuf_ref[i % 2] = _layer_norm(DEEPNORM_ALPHA * x32 + y, g_ref[...], b_ref[...])

    usual = jnp.logical_and(n_prev == tm, n_cur > 0)
    quarter = tm // 4

    @pl.when(usual)
    def _():
        compute(lambda k: send_full(k * quarter, (k + 1) * quarter))

    @pl.when(jnp.logical_not(usual))
    def _():
        @pl.when(n_prev == tm)
        def _():
            send_full(0, tm)

        @pl.when(jnp.logical_and(n_prev > 0, n_prev < tm))
        def _():
            for j in range(tm):
                @pl.when(j < n_prev)
                def _(j=j):
                    send(j).start(priority=j % 2)

        @pl.when(n_cur > 0)
        def _():
            compute(lambda k: None)

    for bit in reversed(range(tm.bit_length())):
        size = 1 << bit

        @pl.when((n_prev & size) != 0)
        def _(size=size):
            pltpu.make_async_copy(obuf_ref.at[slot_prev, pl.ds(0, size)], out_ref.at[pl.ds(0, size)],
                                  sem_ref.at[slot_prev]).wait()


def _experts(layer, ea, eb, nvalid, xs_idx, inv, xs, w_gate, w_up, w_down, g, b, t):
    n_tiles = xs.shape[0] // MOE_TILE
    up_a = pl.BlockSpec((1, 1, D_MODEL, D_EXPERT), lambda i, ea, *_: (layer, ea[i], 0, 0))
    up_b = pl.BlockSpec((1, 1, D_MODEL, D_EXPERT), lambda i, ea, eb, *_: (layer, eb[i], 0, 0))
    dn_a = pl.BlockSpec((1, 1, D_EXPERT, D_MODEL), lambda i, ea, *_: (layer, ea[i], 0, 0))
    dn_b = pl.BlockSpec((1, 1, D_EXPERT, D_MODEL), lambda i, ea, eb, *_: (layer, eb[i], 0, 0))
    fixed = lambda i, *_: (0, 0)
    return pl.pallas_call(
        _experts_kernel,
        grid_spec=pltpu.PrefetchScalarGridSpec(
            num_scalar_prefetch=5,
            grid=(n_tiles + 1,),
            in_specs=[pl.BlockSpec((MOE_TILE, ROW_EXT), lambda i, ea, eb, nv, xs_idx, inv: (xs_idx[i], 0)),
                      up_a, up_a, dn_a, up_b, up_b, dn_b,
                      pl.BlockSpec((1, D_MODEL), fixed), pl.BlockSpec((1, D_MODEL), fixed)],
            out_specs=pl.BlockSpec(memory_space=pl.ANY),
            scratch_shapes=[pltpu.VMEM((2, MOE_TILE, D_MODEL), f32), pltpu.SemaphoreType.DMA((2,))],
        ),
        out_shape=jax.ShapeDtypeStruct((t, D_MODEL), f32),
        compiler_params=_params(("arbitrary",)),
        name="moe_experts",
    )(ea, eb, nvalid, xs_idx, inv, xs, w_gate, w_up, w_down, w_gate, w_up, w_down, g, b)


def _moe_plan(route, counts):
    t = route.shape[1]
    n_rows = t + N_BUCKETS * MOE_TILE
    n_tiles = n_rows // MOE_TILE
    cnt = counts[:N_BUCKETS, 0].astype(jnp.int32)
    padded = (cnt + MOE_TILE - 1) // MOE_TILE * MOE_TILE
    ends = jnp.cumsum(padded)
    starts = ends - padded
    buckets = jnp.arange(N_BUCKETS, dtype=jnp.int32)
    dest = jnp.sum(jnp.where(route[0][None, :] == buckets[:, None], starts[:, None], 0), axis=0) + route[1]
    tile_start = jnp.arange(n_tiles + 1, dtype=jnp.int32) * MOE_TILE
    tile_bucket = jnp.sum(tile_start[:, None] >= ends[None, :], axis=1).astype(jnp.int32)
    used = tile_bucket < N_BUCKETS
    tile_bucket = jnp.minimum(tile_bucket, N_BUCKETS - 1)
    nvalid = jnp.where(used, jnp.clip(starts[tile_bucket] + cnt[tile_bucket] - tile_start, 0, MOE_TILE), 0)
    xs_idx = jnp.minimum(jnp.arange(n_tiles + 1, dtype=jnp.int32), jnp.maximum(ends[-1] // MOE_TILE - 1, 0))
    last_bucket = tile_bucket[jnp.maximum(ends[-1] // MOE_TILE - 1, 0)]
    tile_bucket = jnp.where(used, tile_bucket, last_bucket)
    pair_a = jnp.array([g * EXPERTS_PER_GROUP + i for g in range(N_GROUPS) for i, _ in _PAIRS], jnp.int32)
    pair_b = jnp.array([g * EXPERTS_PER_GROUP + j for g in range(N_GROUPS) for _, j in _PAIRS], jnp.int32)
    tiles = jnp.arange(n_tiles, dtype=jnp.int32)
    is_last = jnp.any(jnp.logical_and((tiles[:, None] + 1) * MOE_TILE == ends[None, :], padded[None, :] > 0), axis=1)
    fill = jnp.logical_or(is_last, tiles * MOE_TILE >= ends[-1]).astype(jnp.int32)
    return dict(dest=dest, ea=pair_a[tile_bucket], eb=pair_b[tile_bucket], nvalid=nvalid.astype(jnp.int32),
                xs_idx=xs_idx, fill=fill, n_rows=n_rows)


def _moe(layer, hext, route, counts, w_gate, w_up, w_down, g, b):
    plan = _moe_plan(route, counts)
    xs, inv = _dispatch(plan["dest"], plan["fill"], hext, plan["n_rows"])
    return _experts(layer, plan["ea"], plan["eb"], plan["nvalid"], plan["xs_idx"], inv, xs,
                    w_gate, w_up, w_down, g, b, hext.shape[0])


def kernel(x, mem, ab_w_in, ab_conv_w, ab_a_log, ab_dt_bias, ab_norm_w, ab_w_out, sc_w_in, sc_conv_w, sc_w_out, xa_w_q, xa_w_kv, xa_w_o, router_w, router_b, moe_w_gate, moe_w_up, moe_w_down, ln_g, ln_b):
    batch, seq, d = x.shape
    n_mem = mem.shape[1]
    t = batch * seq
    h = x.reshape(t, d)
    mem2 = mem.reshape(batch * n_mem, d)
    rwt = jnp.transpose(router_w)
    rb = router_b.reshape(N_EXPERTS, 1)
    row = lambda v: v.reshape(1, -1)

    for layer in range(DEPTH):
        i = layer // 2
        if layer % 2 == 0:
            w_in = ab_w_in[i]
            c0, c1, c2 = 3 * DN_WIDTH, 4 * DN_WIDTH, 4 * DN_WIDTH + 2 * DN_HEADS
            w_ba = jnp.pad(w_in[:, c1:c2], ((0, 0), (0, LANES - 2 * DN_HEADS)))
            lane_pad = lambda v: jnp.pad(v.reshape(1, DN_HEADS), ((0, 0), (DN_HEADS, LANES - 2 * DN_HEADS)))
            act, z, bg, qkv_d = _project_ab(h, w_in[:, :c0].astype(bf16), w_in[:, c0:c1].astype(bf16), w_ba.astype(bf16),
                                            w_in[:, c2:].astype(bf16), ab_conv_w[i], lane_pad(ab_a_log[i]),
                                            lane_pad(ab_dt_bias[i]), seq)
            o_a = _deltanet(act, bg, z, row(ab_norm_w[i]), batch, seq)
            o_b = _dilated_attention(qkv_d.reshape(batch, seq, 3 * DA_WIDTH), batch, seq).reshape(t, DA_WIDTH)
            mix = (o_a, o_b, ab_w_out[i].astype(bf16), row(ln_g[layer, 0]), row(ln_b[layer, 0]))
        else:
            h = _shortconv(h, sc_w_in[i].astype(bf16), sc_conv_w[i], sc_w_out[i].astype(bf16),
                           row(ln_g[layer, 0]), row(ln_b[layer, 0]), batch, seq)
            mix = None
        (kv,) = _project(mem2, [xa_w_kv[layer].astype(bf16)], [bf16], n_mem)
        w_q = (xa_w_q[layer] * (XA_HEAD_DIM ** -0.5 * LOG2_E)).astype(bf16)
        hext, route, counts = _xattn_route(h, kv, w_q, xa_w_o[layer].astype(bf16),
                                           row(ln_g[layer, 1]), row(ln_b[layer, 1]), rwt, rb, batch, seq, n_mem, mix)
        h = _moe(layer, hext, route, counts, moe_w_gate, moe_w_up, moe_w_down,
                 row(ln_g[layer, 2]), row(ln_b[layer, 2]))
    return h.reshape(batch, seq, d)
```

```python
import functools

import jax
import jax.numpy as jnp
from jax import lax
from jax.experimental import pallas as pl
from jax.experimental.pallas import tpu as pltpu

D_MODEL = 1024
DEPTH = 2
DN_HEADS = 4
DN_HEAD_DIM = 128
DN_WIDTH = DN_HEADS * DN_HEAD_DIM
DN_CONV = 4
DN_CHUNK = 64
DA_HEADS = 8
DA_HEAD_DIM = 64
DA_WIDTH = DA_HEADS * DA_HEAD_DIM
DA_BRANCHES = ((128, 1), (512, 4), (2048, 16))
DA_BLOCK = 128
DA_RESIDUES = max(d for _, d in DA_BRANCHES)
SC_CONV = 3
XA_HEADS = 4
XA_HEAD_DIM = D_MODEL // XA_HEADS
N_EXPERTS = 16
N_GROUPS = 4
EXPERTS_PER_GROUP = N_EXPERTS // N_GROUPS
D_EXPERT = D_MODEL // 2
DEEPNORM_ALPHA = (2 * DEPTH) ** 0.25
LN_EPS = 1e-5
RMS_EPS = 1e-6

LANES = 128
SUBLANES = 8
VMEM_LIMIT_BYTES = 48 * 1024 * 1024

TOKEN_TILE = 512
XA_TILE = 1024
SC_TILE = 1024
DN_BLOCK = 512
MOE_TILE = 256
DISPATCH_CHUNK = 1024
DA_UNROLL = 4
GATE_LANES = LANES
ROW_EXT = D_MODEL + GATE_LANES

_PAIRS = tuple((i, j) for i in range(EXPERTS_PER_GROUP) for j in range(i + 1, EXPERTS_PER_GROUP))
N_BUCKETS = N_GROUPS * len(_PAIRS)
BUCKET_ROWS = 32

NEG_BIG = -1e30
LOG2_E = 1.4426950408889634

bf16 = jnp.bfloat16
f32 = jnp.float32


def _params(semantics):
    return pltpu.CompilerParams(dimension_semantics=semantics, vmem_limit_bytes=VMEM_LIMIT_BYTES)


def _dot(a, b):
    return jnp.dot(a.astype(bf16), b.astype(bf16), preferred_element_type=f32)


def _dot_nt(a, b):
    return lax.dot_general(a.astype(bf16), b.astype(bf16), (((1,), (1,)), ((), ())), preferred_element_type=f32)


def _dot_tn(a, b):
    return lax.dot_general(a.astype(bf16), b.astype(bf16), (((0,), (0,)), ((), ())), preferred_element_type=f32)


def _split(a, parts):
    out = []
    rem = a
    for _ in range(parts):
        p = rem.astype(bf16)
        out.append(p)
        rem = rem - p.astype(f32)
    return out


def _layer_norm(v, g, b):
    mu = jnp.mean(v, -1, keepdims=True)
    c = v - mu
    var = jnp.mean(c * c, -1, keepdims=True)
    return c * lax.rsqrt(var + LN_EPS) * g + b


def _silu(v):
    return v * jax.nn.sigmoid(v)


def _softplus(v):
    return jnp.maximum(v, 0.0) + jnp.log(1.0 + jnp.exp(-jnp.abs(v)))


def _proj_ab_kernel(x_ref, wqkv_ref, wz_ref, wba_ref, wd_ref, cw_ref, alog_ref, dtb_ref,
                    act_ref, z_ref, bg_ref, qkvd_ref, xx_ref, *, tiles_per_seq):
    @pl.when(pl.program_id(0) % tiles_per_seq == 0)
    def _():
        xx_ref[0:SUBLANES, :] = jnp.zeros((SUBLANES, xx_ref.shape[1]), f32)

    tm, hd = x_ref.shape[0], DN_HEAD_DIM
    xb = x_ref[...].astype(bf16)
    cw = cw_ref[...]

    def conv_act(c):
        cols = slice(c * hd, (c + 1) * hd)
        conv = xx_ref[SUBLANES:, cols] * cw[DN_CONV - 1:DN_CONV, cols]
        for j in range(DN_CONV - 1):
            back = DN_CONV - 1 - j
            conv = conv + xx_ref[pl.ds(SUBLANES - back, tm), cols] * cw[j:j + 1, cols]
        a = _silu(conv)
        if c < 2 * DN_HEADS:
            scale = hd ** -0.5 if c < DN_HEADS else 1.0
            a = a * (lax.rsqrt(jnp.sum(a * a, -1, keepdims=True) + RMS_EPS) * scale)
        act_ref[:, cols] = a

    for c0 in range(0, 3 * DN_WIDTH, 512):
        xx_ref[SUBLANES:, c0:c0 + 512] = jnp.dot(xb, wqkv_ref[:, c0:c0 + 512], preferred_element_type=f32)
        for c in range(c0 // hd, (c0 + 512) // hd):
            conv_act(c)
    z_ref[...] = jnp.dot(xb, wz_ref[...], preferred_element_type=f32)
    for c0 in range(0, 3 * DA_WIDTH, 512):
        qkvd_ref[:, c0:c0 + 512] = jnp.dot(xb, wd_ref[:, c0:c0 + 512], preferred_element_type=f32)
    ba = jnp.dot(xb, wba_ref[...], preferred_element_type=f32)
    is_beta = lax.broadcasted_iota(jnp.int32, ba.shape, 1) < DN_HEADS
    bg_ref[...] = jnp.where(is_beta, jax.nn.sigmoid(ba), -jnp.exp(alog_ref[...]) * _softplus(ba + dtb_ref[...]))
    xx_ref[0:SUBLANES, :] = xx_ref[tm:tm + SUBLANES, :]


def _project_ab(x, w_qkv, w_z, w_ba, w_d, conv_w, alog_row, dtb_row, seq):
    t, k = x.shape
    tm = TOKEN_TILE
    fixed = lambda i: (0, 0)
    row = lambda i: (i, 0)
    widths = (3 * DN_WIDTH, DN_WIDTH, LANES, 3 * DA_WIDTH)
    return pl.pallas_call(
        functools.partial(_proj_ab_kernel, tiles_per_seq=seq // tm),
        grid=(t // tm,),
        in_specs=[pl.BlockSpec((tm, k), row)]
        + [pl.BlockSpec(w.shape, fixed) for w in (w_qkv, w_z, w_ba, w_d, conv_w, alog_row, dtb_row)],
        out_specs=[pl.BlockSpec((tm, n), row) for n in widths],
        out_shape=[jax.ShapeDtypeStruct((t, n), f32) for n in widths],
        scratch_shapes=[pltpu.VMEM((SUBLANES + tm, 3 * DN_WIDTH), f32)],
        compiler_params=_params(("arbitrary",)),
        name="project_ab",
    )(x, w_qkv, w_z, w_ba, w_d, conv_w, alog_row, dtb_row)


def _deltanet_kernel(act_ref, bg_ref, z_ref, nw_ref, o_ref, state_ref):
    @pl.when(pl.program_id(1) == 0)
    def _():
        state_ref[...] = jnp.zeros_like(state_ref)

    rows, ch, hd, width = DN_BLOCK, DN_CHUNK, DN_HEAD_DIM, DN_WIDTH
    sup = 2 * ch
    n_sup = rows // sup

    act = act_ref[...]
    beta_all = bg_ref[...]
    g_all = beta_all

    ri = lax.broadcasted_iota(jnp.int32, (sup, sup), 0)
    ci = lax.broadcasted_iota(jnp.int32, (sup, sup), 1)
    same = (ri >= ch) == (ci >= ch)
    incl = jnp.logical_and(ri >= ci, same)
    strict = jnp.logical_and(ri > ci, same)
    lower_ones = jnp.where(incl, 1.0, 0.0).astype(bf16)
    eye = jnp.where(ri == ci, 1.0, 0.0)
    first_rows = lax.broadcasted_iota(jnp.int32, (sup, hd), 0) < ch
    z = z_ref[...]
    nw = nw_ref[...]

    chains = [(s, h) for s in range(n_sup) for h in range(DN_HEADS)]
    g_cum_all, g_cum_t = [], []
    for s in range(n_sup):
        g_blk = g_all[s * sup:(s + 1) * sup, :]
        gc = sum(jnp.dot(lower_ones, p, preferred_element_type=f32) for p in _split(g_blk, 2))
        g_cum_all.append(gc)
        g_cum_t.append(jnp.transpose(gc))

    st = {}
    for s, h in chains:
        r0 = s * sup
        q = act[r0:r0 + sup, h * hd:(h + 1) * hd]
        k = act[r0:r0 + sup, width + h * hd:width + (h + 1) * hd]
        v = act[r0:r0 + sup, 2 * width + h * hd:2 * width + (h + 1) * hd]
        beta_b = jnp.broadcast_to(beta_all[r0:r0 + sup, h:h + 1], (sup, hd))
        g_i = jnp.broadcast_to(g_cum_all[s][:, DN_HEADS + h:DN_HEADS + h + 1], (sup, hd))
        g_j = jnp.broadcast_to(g_cum_t[s][DN_HEADS + h:DN_HEADS + h + 1, :], (sup, sup))
        decay = jnp.where(incl, jnp.exp(jnp.minimum(g_i - g_j, 0.0)), 0.0)
        e_g = jnp.exp(g_i)
        g_last = jnp.where(first_rows, g_i[ch - 1:ch, :], g_i[sup - 1:sup, :])
        st[s, h] = dict(q=q, k=k, kb=k * beta_b, vb=v * beta_b, decay=decay, e_g=e_g,
                        k_tail=k * jnp.exp(g_last - g_i),
                        gl=(jnp.exp(g_i[ch - 1:ch, :]), jnp.exp(g_i[sup - 1:sup, :])))

    for c in chains:
        d = st[c]
        d["a"] = jnp.where(strict, _dot_nt(d["kb"], d["k"]) * d["decay"], 0.0)
        d["attn"] = jnp.where(incl, _dot_nt(d["q"], d["k"]) * d["decay"], 0.0)
    for c in chains:
        st[c]["t_inv"] = eye - st[c]["a"]
        st[c]["pw"] = st[c]["a"]
    for _ in range(5):
        for c in chains:
            st[c]["pw"] = _dot(st[c]["pw"], st[c]["pw"])
        for c in chains:
            st[c]["t_inv"] = st[c]["t_inv"] + _dot(st[c]["t_inv"], st[c]["pw"])
    for c in chains:
        d = st[c]
        sol = _dot(d["t_inv"], jnp.concatenate([d["vb"], d["kb"] * d["e_g"]], axis=1))
        d["u"], d["w"] = sol[:, :hd], sol[:, hd:]
    for c in chains:
        d = st[c]
        uw = jnp.concatenate([d["u"], d["w"]], axis=1).astype(bf16)
        mix = jnp.dot(d["attn"].astype(bf16), uw, preferred_element_type=f32)
        d["au"] = mix[:, :hd]
        d["qe"] = d["q"] * d["e_g"] - mix[:, hd:]
        kt = d["k_tail"].astype(bf16)
        zero = jnp.zeros_like(kt)
        d["nu_pw"] = [_dot_tn(jnp.where(first_rows == first, kt, zero), uw)
                      for first in (True, False)]

    for s in range(n_sup):
        for half in range(2):
            for h in range(DN_HEADS):
                d = st[s, h]
                state = state_ref[h]
                lo = half * ch
                o = _dot(d["qe"][lo:lo + ch], state) + d["au"][lo:lo + ch]
                nu_pw = d["nu_pw"][half]
                state_ref[h] = state * d["gl"][half] - _dot(nu_pw[:, hd:], state) + nu_pw[:, :hd]
                o = o * lax.rsqrt(jnp.mean(o * o, -1, keepdims=True) + RMS_EPS) * nw
                r0 = s * sup + lo
                o_ref[r0:r0 + ch, h * hd:(h + 1) * hd] = (o * _silu(z[r0:r0 + ch, h * hd:(h + 1) * hd])).astype(o_ref.dtype)


def _deltanet(act, bg, z, norm_w, batch, seq):
    nb = seq // DN_BLOCK
    row = lambda b, c: (b * nb + c, 0)
    fixed = lambda b, c: (0, 0)
    return pl.pallas_call(
        _deltanet_kernel,
        grid=(batch, nb),
        in_specs=[
            pl.BlockSpec((DN_BLOCK, 3 * DN_WIDTH), row),
            pl.BlockSpec((DN_BLOCK, LANES), row),
            pl.BlockSpec((DN_BLOCK, DN_WIDTH), row),
            pl.BlockSpec((1, DN_HEAD_DIM), fixed),
        ],
        out_specs=pl.BlockSpec((DN_BLOCK, DN_WIDTH), row),
        out_shape=jax.ShapeDtypeStruct((batch * seq, DN_WIDTH), bf16),
        scratch_shapes=[pltpu.VMEM((DN_HEADS, DN_HEAD_DIM, DN_HEAD_DIM), f32)],
        compiler_params=_params(("arbitrary", "arbitrary")),
        name="deltanet",
    )(act, bg, z, norm_w)


def _dilated_kernel(q_ref, k_ref, v_ref, o_ref, qs_ref, ks_ref, vs_ref, ob_ref, lse_ref, s_ref, p_ref, st_ref, *, seq):
    blk, res = DA_BLOCK, DA_RESIDUES
    per = seq // res
    head0 = lax.broadcasted_iota(jnp.int32, (blk, LANES), 1) < DA_HEAD_DIM
    ai = lax.broadcasted_iota(jnp.int32, (2 * blk, blk), 0) & (blk - 1)
    bi = lax.broadcasted_iota(jnp.int32, (2 * blk, blk), 1)
    ones_blk = jnp.ones((2 * blk, LANES), bf16)

    q_scale = DA_HEAD_DIM ** -0.5 * LOG2_E
    for r in range(res):
        rows = pl.ds(r, per, stride=res)
        qs_ref[r * per:(r + 1) * per, :] = q_ref[0, rows, :] * q_scale
        ks_ref[r * per:(r + 1) * per, :] = k_ref[0, rows, :]
        vs_ref[r * per:(r + 1) * per, :] = v_ref[0, rows, :]

    for br, (window, dil) in enumerate(DA_BRANCHES):
        assert window // dil == blk and res % dil == 0
        runs = res // dil
        run_len = blk // runs
        nblk = seq // (dil * blk)
        shift = run_len.bit_length() - 1

        def pos(a, runs=runs, run_len=run_len, shift=shift):
            return runs * (a & (run_len - 1)) + (a >> shift)

        in_window = jnp.concatenate([pos(bi) >= pos(ai), pos(bi) <= pos(ai)], axis=1)

        n_iter = dil * nblk // DA_UNROLL
        assert n_iter * DA_UNROLL == dil * nblk

        def tile(ref, offs, run_len=run_len, runs=runs):
            parts = [ref[pl.ds(o, run_len), :] for o in offs]
            return parts[0] if runs == 1 else jnp.concatenate(parts, axis=0)

        def blocks_of(it, dil=dil, runs=runs, run_len=run_len, nblk=nblk):
            out = []
            for u in range(DA_UNROLL):
                idx = it * DA_UNROLL + u
                cls = idx // nblk
                n = idx - cls * nblk
                n_prev = jnp.maximum(n - 1, 0)
                cur = [pl.multiple_of((dil * m + cls) * per + run_len * n, SUBLANES) for m in range(runs)]
                prev = [pl.multiple_of((dil * m + cls) * per + run_len * n_prev, SUBLANES) for m in range(runs)]
                out.append((cur, prev, n > 0))
            return out

        def masked_scores(blocks, tile=tile, in_window=in_window):
            out = []
            for cur, prev, has_prev in blocks:
                q = tile(qs_ref, cur).astype(bf16)
                zero = jnp.zeros_like(q)
                q2 = jnp.concatenate([jnp.where(head0, q, zero), jnp.where(head0, zero, q)], axis=0)
                keys = jnp.concatenate([tile(ks_ref, prev), tile(ks_ref, cur)], axis=0)
                s = jnp.where(in_window, _dot_nt(q2, keys), NEG_BIG)
                out.append(jnp.concatenate([jnp.where(has_prev, s[:, :blk], NEG_BIG), s[:, blk:]], axis=1))
            return out

        def softmax_stage(blocks, scores):
            out = []
            for s in scores:
                top = jnp.max(s, -1, keepdims=True)
                out.append((jnp.exp2(s - top).astype(bf16), jnp.where(head0, top[:blk], top[blk:])))
            return out

        def output_stage(blocks, probs, br=br, run_len=run_len, tile=tile):
            for (cur, prev, has_prev), (pb, top) in zip(blocks, probs):
                vals = jnp.concatenate([tile(vs_ref, prev), tile(vs_ref, cur)], axis=0).astype(bf16)
                acc = jnp.dot(pb, jnp.concatenate([vals, ones_blk], axis=1), preferred_element_type=f32)
                den = jnp.where(head0, acc[:blk, LANES:], acc[blk:, LANES:])
                o_blk = jnp.where(head0, acc[:blk, :LANES], acc[blk:, :LANES]) / den
                lse_blk = top + jnp.log2(den)
                for m, o in enumerate(cur):
                    ob_ref[br, pl.ds(o, run_len), :] = o_blk[m * run_len:(m + 1) * run_len]
                    lse_ref[br, pl.ds(o, run_len), :] = lse_blk[m * run_len:(m + 1) * run_len]

        def put_scores(scores):
            for u, sc in enumerate(scores):
                s_ref[u] = sc

        def put_probs(probs):
            for u, (pb, top) in enumerate(probs):
                p_ref[u] = pb
                st_ref[u] = top

        put_probs(softmax_stage(blocks_of(0), masked_scores(blocks_of(0))))
        put_scores(masked_scores(blocks_of(min(1, n_iter - 1))))

        def body(it, carry, n_iter=n_iter, blocks_of=blocks_of, masked_scores=masked_scores,
                 softmax_stage=softmax_stage, output_stage=output_stage, put_scores=put_scores, put_probs=put_probs):
            scores = [s_ref[u] for u in range(DA_UNROLL)]
            probs = [(p_ref[u], st_ref[u]) for u in range(DA_UNROLL)]
            output_stage(blocks_of(it), probs)
            next_probs = softmax_stage(blocks_of(jnp.minimum(it + 1, n_iter - 1)), scores)
            next_scores = masked_scores(blocks_of(jnp.minimum(it + 2, n_iter - 1)))
            put_scores(next_scores)
            put_probs(next_probs)
            return carry

        lax.fori_loop(0, n_iter, body, 0)

    for r in range(res):
        rows = slice(r * per, (r + 1) * per)
        lses = [lse_ref[br, rows, :] for br in range(len(DA_BRANCHES))]
        top = functools.reduce(jnp.maximum, lses)
        wts = [jnp.exp2(l - top) for l in lses]
        num = sum(w * ob_ref[br, rows, :] for br, w in enumerate(wts))
        o_ref[0, pl.ds(r, per, stride=res), :] = num / sum(wts)


def _dilated_attention(qkv3, batch, seq):
    pairs = DA_WIDTH // LANES
    n_br = len(DA_BRANCHES)
    spec = lambda off: pl.BlockSpec((1, seq, LANES), lambda b, p, off=off: (b, 0, off + p))
    return pl.pallas_call(
        functools.partial(_dilated_kernel, seq=seq),
        grid=(batch, pairs),
        in_specs=[spec(0), spec(pairs), spec(2 * pairs)],
        out_specs=pl.BlockSpec((1, seq, LANES), lambda b, p: (b, 0, p)),
        out_shape=jax.ShapeDtypeStruct((batch, seq, DA_WIDTH), f32),
        scratch_shapes=[pltpu.VMEM((seq, LANES), f32)] * 3
        + [pltpu.VMEM((n_br, seq, LANES), f32)] * 2
        + [pltpu.VMEM((DA_UNROLL, 2 * DA_BLOCK, 2 * DA_BLOCK), f32),
           pltpu.VMEM((DA_UNROLL, 2 * DA_BLOCK, 2 * DA_BLOCK), bf16),
           pltpu.VMEM((DA_UNROLL, DA_BLOCK, LANES), f32)],
        compiler_params=_params(("parallel", "parallel")),
        name="dilated_attention",
    )(qkv3, qkv3, qkv3)


def _shortconv_kernel(h_ref, win_ref, cw_ref, wout_ref, g_ref, b_ref, o_ref, tail_ref):
    @pl.when(pl.program_id(1) == 0)
    def _():
        tail_ref[...] = jnp.zeros_like(tail_ref)

    tm = h_ref.shape[0]
    d = D_MODEL
    h = h_ref[...]
    hb = h.astype(bf16)
    gate_b = jnp.dot(hb, win_ref[:, :d], preferred_element_type=f32)
    gate_c = jnp.dot(hb, win_ref[:, d:2 * d], preferred_element_type=f32)
    hid = jnp.dot(hb, win_ref[:, 2 * d:], preferred_element_type=f32)
    u = gate_c * hid
    uu = jnp.concatenate([tail_ref[...], u], axis=0)
    cw = cw_ref[...]
    conv = u * cw[SC_CONV - 1:SC_CONV, :]
    for j in range(SC_CONV - 1):
        back = SC_CONV - 1 - j
        conv = conv + uu[SUBLANES - back:SUBLANES - back + tm, :] * cw[j:j + 1, :]
    tail_ref[...] = u[tm - SUBLANES:, :]
    y = jnp.dot((gate_b * conv).astype(bf16), wout_ref[...], preferred_element_type=f32)
    o_ref[...] = _layer_norm(DEEPNORM_ALPHA * h + y, g_ref[...], b_ref[...])


def _shortconv(h, w_in, conv_w, w_out, g, b, batch, seq):
    tm = SC_TILE
    ns = seq // tm
    row = lambda bb, s: (bb * ns + s, 0)
    fixed = lambda bb, s: (0, 0)
    once = dict(pipeline_mode=pl.Buffered(1))
    return pl.pallas_call(
        _shortconv_kernel,
        grid=(batch, ns),
        in_specs=[
            pl.BlockSpec((tm, D_MODEL), row),
            pl.BlockSpec(w_in.shape, fixed, **once),
            pl.BlockSpec(conv_w.shape, fixed),
            pl.BlockSpec(w_out.shape, fixed, **once),
            pl.BlockSpec((1, D_MODEL), fixed),
            pl.BlockSpec((1, D_MODEL), fixed),
        ],
        out_specs=pl.BlockSpec((tm, D_MODEL), row),
        out_shape=jax.ShapeDtypeStruct((batch * seq, D_MODEL), f32),
        scratch_shapes=[pltpu.VMEM((SUBLANES, D_MODEL), f32)],
        compiler_params=_params(("arbitrary", "arbitrary")),
        name="shortconv",
    )(h, w_in, conv_w, w_out, g, b)


def _route(h2, rwt_ref, rb_ref, cnt_ref):
    tm = h2.shape[0]
    w_hi, w_lo = _split(rwt_ref[...], 2)
    x_hi, x_lo = _split(h2, 2)
    nt = (((1,), (1,)), ((), ()))
    logits = (lax.dot_general(w_hi, x_hi, nt, preferred_element_type=f32)
              + lax.dot_general(w_hi, x_lo, nt, preferred_element_type=f32)
              + lax.dot_general(w_lo, x_hi, nt, preferred_element_type=f32))
    mx = jnp.max(logits, axis=0, keepdims=True)
    ex = jnp.exp(logits - mx)
    scores = ex / jnp.sum(ex, axis=0, keepdims=True)
    sel = scores + rb_ref[...]
    best = jnp.full((1, tm), -jnp.inf, f32)
    bucket = jnp.zeros((1, tm), jnp.int32)
    s_a = jnp.zeros((1, tm), f32)
    s_b = jnp.zeros((1, tm), f32)
    for grp in range(N_GROUPS):
        for p, (i, j) in enumerate(_PAIRS):
            a, b = grp * EXPERTS_PER_GROUP + i, grp * EXPERTS_PER_GROUP + j
            ps = sel[a:a + 1, :] + sel[b:b + 1, :]
            upd = ps > best
            best = jnp.where(upd, ps, best)
            bucket = jnp.where(upd, grp * len(_PAIRS) + p, bucket)
            s_a = jnp.where(upd, scores[a:a + 1, :], s_a)
            s_b = jnp.where(upd, scores[b:b + 1, :], s_b)
    denom = s_a + s_b
    gate_rows = jnp.concatenate([s_a / denom, s_b / denom, jnp.zeros((GATE_LANES - 2, tm), f32)], axis=0)
    gates = jnp.transpose(gate_rows)
    onehot = jnp.where(lax.broadcasted_iota(jnp.int32, (BUCKET_ROWS, tm), 0) == bucket, 1.0, 0.0)
    before = lax.broadcasted_iota(jnp.int32, (tm, tm), 0) < lax.broadcasted_iota(jnp.int32, (tm, tm), 1)
    prefix = jnp.dot(onehot.astype(bf16), jnp.where(before, 1.0, 0.0).astype(bf16), preferred_element_type=f32)
    cnt = cnt_ref[...]
    rank = jnp.sum(onehot * (prefix + cnt[:, 0:1]), axis=0, keepdims=True)
    cnt_ref[...] = cnt + jnp.sum(onehot, axis=1, keepdims=True)
    return gates, bucket, rank.astype(jnp.int32)


def _xattn_kernel(*refs, mix):
    if mix:
        (oa_ref, ob_ref, wmix_ref, g0_ref, b0_ref), refs = refs[:5], refs[5:]
    (h_ref, mem_ref, wkv_ref, wq_ref, wo_ref, g_ref, b_ref, rwt_ref, rb_ref,
     hext_ref, route_ref, cnt_out_ref, cnt_ref, kv_ref) = refs

    @pl.when(jnp.logical_and(pl.program_id(0) == 0, pl.program_id(1) == 0))
    def _():
        cnt_ref[...] = jnp.zeros_like(cnt_ref)

    @pl.when(pl.program_id(1) == 0)
    def _():
        kv_ref[...] = jnp.dot(mem_ref[...].astype(bf16), wkv_ref[...], preferred_element_type=f32).astype(bf16)

    h = h_ref[...]
    tm = h.shape[0]
    if mix:
        y0 = (jnp.dot(oa_ref[...], wmix_ref[:DN_WIDTH], preferred_element_type=f32)
              + jnp.dot(ob_ref[...].astype(bf16), wmix_ref[DN_WIDTH:], preferred_element_type=f32))
        h = _layer_norm(DEEPNORM_ALPHA * h + y0, g0_ref[...], b0_ref[...])
    q = jnp.dot(h.astype(bf16), wq_ref[...], preferred_element_type=f32).astype(bf16)
    heads = [(hd * XA_HEAD_DIM, (hd + 1) * XA_HEAD_DIM) for hd in range(XA_HEADS)]
    scores = [_dot_nt(q[:, lo:hi], kv_ref[:, lo:hi]) for lo, hi in heads]
    probs = [jnp.exp2(s - jnp.max(s, -1, keepdims=True)) for s in scores]
    outs = [(_dot(p, kv_ref[:, D_MODEL + lo:D_MODEL + hi]) / jnp.sum(p, -1, keepdims=True)).astype(bf16)
            for p, (lo, hi) in zip(probs, heads)]
    y = jnp.dot(jnp.concatenate(outs, axis=1), wo_ref[...], preferred_element_type=f32)
    h2 = _layer_norm(DEEPNORM_ALPHA * h + y, g_ref[...], b_ref[...])
    gates, bucket, rank = _route(h2, rwt_ref, rb_ref, cnt_ref)
    hext_ref[:, :D_MODEL] = h2
    hext_ref[:, D_MODEL:] = gates
    route_ref[...] = jnp.concatenate([bucket, rank, jnp.zeros((SUBLANES - 2, tm), jnp.int32)], axis=0)
    cnt_out_ref[...] = cnt_ref[...]


def _xattn_route(h, mem, w_kv, w_q, w_o, g, b, rwt, rb, batch, seq, n_mem, mix=None):
    tm = XA_TILE
    ns = seq // tm
    t = batch * seq
    row = lambda bb, s: (bb * ns + s, 0)
    fixed = lambda bb, s: (0, 0)
    once = dict(pipeline_mode=pl.Buffered(1))
    mix_specs, mix_args = [], []
    if mix is not None:
        o_a, o_b, w_mix, g0, b0 = mix
        mix_specs = [pl.BlockSpec((tm, DN_WIDTH), row), pl.BlockSpec((tm, DA_WIDTH), row),
                     pl.BlockSpec(w_mix.shape, fixed, **once),
                     pl.BlockSpec((1, D_MODEL), fixed), pl.BlockSpec((1, D_MODEL), fixed)]
        mix_args = [o_a, o_b, w_mix, g0, b0]
    return pl.pallas_call(
        functools.partial(_xattn_kernel, mix=mix is not None),
        grid=(batch, ns),
        in_specs=mix_specs + [
            pl.BlockSpec((tm, D_MODEL), row),
            pl.BlockSpec((n_mem, D_MODEL), lambda bb, s: (bb, 0)),
            pl.BlockSpec(w_kv.shape, fixed, **once),
            pl.BlockSpec(w_q.shape, fixed, **once),
            pl.BlockSpec(w_o.shape, fixed, **once),
            pl.BlockSpec((1, D_MODEL), fixed),
            pl.BlockSpec((1, D_MODEL), fixed),
            pl.BlockSpec(rwt.shape, fixed),
            pl.BlockSpec(rb.shape, fixed),
        ],
        out_specs=[
            pl.BlockSpec((tm, ROW_EXT), row),
            pl.BlockSpec((SUBLANES, tm), lambda bb, s: (0, bb * ns + s)),
            pl.BlockSpec((BUCKET_ROWS, LANES), fixed),
        ],
        out_shape=[
            jax.ShapeDtypeStruct((t, ROW_EXT), f32),
            jax.ShapeDtypeStruct((SUBLANES, t), jnp.int32),
            jax.ShapeDtypeStruct((BUCKET_ROWS, LANES), f32),
        ],
        scratch_shapes=[pltpu.VMEM((BUCKET_ROWS, LANES), f32), pltpu.VMEM((n_mem, 2 * D_MODEL), bf16)],
        compiler_params=_params(("arbitrary", "arbitrary")),
        name="xattn_route",
    )(*mix_args, h, mem, w_kv, w_q, w_o, g, b, rwt, rb)


def _dispatch_kernel(dest_ref, fill_ref, hext_ref, xs_ref, inv_ref, zeros_ref, sem, zsem):
    base = pl.program_id(0) * DISPATCH_CHUNK
    n_tiles = xs_ref.shape[0] // MOE_TILE

    @pl.when(pl.program_id(0) == 0)
    def _():
        zeros_ref[...] = jnp.zeros_like(zeros_ref)

        def fill(tile):
            return pltpu.make_async_copy(zeros_ref, xs_ref.at[pl.ds(tile * MOE_TILE, MOE_TILE)], zsem)

        for tile in range(n_tiles):
            @pl.when(fill_ref[tile] > 0)
            def _(tile=tile):
                fill(tile).start()

        def clear(j8, carry):
            for u in range(8):
                inv_ref[j8 * 8 + u] = 0
            return carry

        lax.fori_loop(0, inv_ref.shape[0] // 8, clear, 0)
        for tile in range(n_tiles):
            @pl.when(fill_ref[tile] > 0)
            def _(tile=tile):
                fill(tile).wait()

    for j in range(DISPATCH_CHUNK):
        d = dest_ref[base + j]
        inv_ref[d] = base + j
        pltpu.make_async_copy(hext_ref.at[pl.ds(j, 1)], xs_ref.at[pl.ds(d, 1)], sem).start(priority=j % 2)
    pltpu.make_async_copy(hext_ref, xs_ref.at[pl.ds(0, DISPATCH_CHUNK)], sem).wait()


def _dispatch(dest, fill, hext, n_rows):
    t = hext.shape[0]
    return pl.pallas_call(
        _dispatch_kernel,
        grid_spec=pltpu.PrefetchScalarGridSpec(
            num_scalar_prefetch=2,
            grid=(t // DISPATCH_CHUNK,),
            in_specs=[pl.BlockSpec((DISPATCH_CHUNK, ROW_EXT), lambda i, *_: (i, 0))],
            out_specs=[pl.BlockSpec(memory_space=pl.ANY), pl.BlockSpec(memory_space=pltpu.SMEM)],
            scratch_shapes=[pltpu.VMEM((MOE_TILE, ROW_EXT), f32), pltpu.SemaphoreType.DMA(()),
                            pltpu.SemaphoreType.DMA(())],
        ),
        out_shape=[jax.ShapeDtypeStruct((n_rows, ROW_EXT), f32), jax.ShapeDtypeStruct((n_rows,), jnp.int32)],
        compiler_params=_params(("arbitrary",)),
        name="moe_dispatch",
    )(dest, fill, hext)


def _experts_kernel(ea_ref, eb_ref, nvalid_ref, xs_idx_ref, inv_ref, xs_ref, wga_ref, wua_ref, wda_ref,
                    wgb_ref, wub_ref, wdb_ref, g_ref, b_ref, out_ref, obuf_ref, sem_ref):
    i = pl.program_id(0)
    tm = MOE_TILE
    prev = jnp.maximum(i - 1, 0)
    n_prev = jnp.where(i > 0, nvalid_ref[prev], 0)
    slot_prev = prev % 2

    def send(j):
        return pltpu.make_async_copy(obuf_ref.at[slot_prev, pl.ds(j, 1)],
                                     out_ref.at[pl.ds(inv_ref[prev * tm + j], 1)], sem_ref.at[slot_prev])

    n_cur = jnp.where(i < pl.num_programs(0) - 1, nvalid_ref[jnp.minimum(i, pl.num_programs(0) - 2)], 0)

    def send_full(lo, hi):
        for j in range(lo, hi):
            send(j).start(priority=j % 2)

    def compute(between):
        x32 = xs_ref[:, :D_MODEL]
        x = x32.astype(bf16)
        gates = xs_ref[:, D_MODEL:]

        def hidden(wg_ref, wu_ref, gate):
            hg = jnp.dot(x, wg_ref[0, 0].astype(bf16), preferred_element_type=f32)
            hu = jnp.dot(x, wu_ref[0, 0].astype(bf16), preferred_element_type=f32)
            return (_silu(hg) * hu * gate).astype(bf16)

        between(0)
        h_a = hidden(wga_ref, wua_ref, gates[:, 0:1])
        between(1)
        h_b = hidden(wgb_ref, wub_ref, gates[:, 1:2])
        between(2)
        y = (jnp.dot(h_a, wda_ref[0, 0].astype(bf16), preferred_element_type=f32)
             + jnp.dot(h_b, wdb_ref[0, 0].astype(bf16), preferred_element_type=f32))
        between(3)
        obuf_ref[i % 2] = _layer_norm(DEEPNORM_ALPHA * x32 + y, g_ref[...], b_ref[...])

    usual = jnp.logical_and(n_prev == tm, n_cur > 0)
    quarter = tm // 4

    @pl.when(usual)
    def _():
        compute(lambda k: send_full(k * quarter, (k + 1) * quarter))

    @pl.when(jnp.logical_not(usual))
    def _():
        @pl.when(n_prev == tm)
        def _():
            send_full(0, tm)

        @pl.when(jnp.logical_and(n_prev > 0, n_prev < tm))
        def _():
            for j in range(tm):
                @pl.when(j < n_prev)
                def _(j=j):
                    send(j).start(priority=j % 2)

        @pl.when(n_cur > 0)
        def _():
            compute(lambda k: None)

    for bit in reversed(range(tm.bit_length())):
        size = 1 << bit

        @pl.when((n_prev & size) != 0)
        def _(size=size):
            pltpu.make_async_copy(obuf_ref.at[slot_prev, pl.ds(0, size)], out_ref.at[pl.ds(0, size)],
                                  sem_ref.at[slot_prev]).wait()


def _experts(layer, ea, eb, nvalid, xs_idx, inv, xs, w_gate, w_up, w_down, g, b, t):
    n_tiles = xs.shape[0] // MOE_TILE
    up_a = pl.BlockSpec((1, 1, D_MODEL, D_EXPERT), lambda i, ea, *_: (layer, ea[i], 0, 0))
    up_b = pl.BlockSpec((1, 1, D_MODEL, D_EXPERT), lambda i, ea, eb, *_: (layer, eb[i], 0, 0))
    dn_a = pl.BlockSpec((1, 1, D_EXPERT, D_MODEL), lambda i, ea, *_: (layer, ea[i], 0, 0))
    dn_b = pl.BlockSpec((1, 1, D_EXPERT, D_MODEL), lambda i, ea, eb, *_: (layer, eb[i], 0, 0))
    fixed = lambda i, *_: (0, 0)
    return pl.pallas_call(
        _experts_kernel,
        grid_spec=pltpu.PrefetchScalarGridSpec(
            num_scalar_prefetch=5,
            grid=(n_tiles + 1,),
            in_specs=[pl.BlockSpec((MOE_TILE, ROW_EXT), lambda i, ea, eb, nv, xs_idx, inv: (xs_idx[i], 0)),
                      up_a, up_a, dn_a, up_b, up_b, dn_b,
                      pl.BlockSpec((1, D_MODEL), fixed), pl.BlockSpec((1, D_MODEL), fixed)],
            out_specs=pl.BlockSpec(memory_space=pl.ANY),
            scratch_shapes=[pltpu.VMEM((2, MOE_TILE, D_MODEL), f32), pltpu.SemaphoreType.DMA((2,))],
        ),
        out_shape=jax.ShapeDtypeStruct((t, D_MODEL), f32),
        compiler_params=_params(("arbitrary",)),
        name="moe_experts",
    )(ea, eb, nvalid, xs_idx, inv, xs, w_gate, w_up, w_down, w_gate, w_up, w_down, g, b)


def _moe_plan(route, counts):
    t = route.shape[1]
    n_rows = t + N_BUCKETS * MOE_TILE
    n_tiles = n_rows // MOE_TILE
    cnt = counts[:N_BUCKETS, 0].astype(jnp.int32)
    padded = (cnt + MOE_TILE - 1) // MOE_TILE * MOE_TILE
    ends = jnp.cumsum(padded)
    starts = ends - padded
    buckets = jnp.arange(N_BUCKETS, dtype=jnp.int32)
    dest = jnp.sum(jnp.where(route[0][None, :] == buckets[:, None], starts[:, None], 0), axis=0) + route[1]
    tile_start = jnp.arange(n_tiles + 1, dtype=jnp.int32) * MOE_TILE
    tile_bucket = jnp.sum(tile_start[:, None] >= ends[None, :], axis=1).astype(jnp.int32)
    used = tile_bucket < N_BUCKETS
    tile_bucket = jnp.minimum(tile_bucket, N_BUCKETS - 1)
    nvalid = jnp.where(used, jnp.clip(starts[tile_bucket] + cnt[tile_bucket] - tile_start, 0, MOE_TILE), 0)
    xs_idx = jnp.minimum(jnp.arange(n_tiles + 1, dtype=jnp.int32), jnp.maximum(ends[-1] // MOE_TILE - 1, 0))
    last_bucket = tile_bucket[jnp.maximum(ends[-1] // MOE_TILE - 1, 0)]
    tile_bucket = jnp.where(used, tile_bucket, last_bucket)
    pair_a = jnp.array([g * EXPERTS_PER_GROUP + i for g in range(N_GROUPS) for i, _ in _PAIRS], jnp.int32)
    pair_b = jnp.array([g * EXPERTS_PER_GROUP + j for g in range(N_GROUPS) for _, j in _PAIRS], jnp.int32)
    tiles = jnp.arange(n_tiles, dtype=jnp.int32)
    is_last = jnp.any(jnp.logical_and((tiles[:, None] + 1) * MOE_TILE == ends[None, :], padded[None, :] > 0), axis=1)
    fill = jnp.logical_or(is_last, tiles * MOE_TILE >= ends[-1]).astype(jnp.int32)
    return dict(dest=dest, ea=pair_a[tile_bucket], eb=pair_b[tile_bucket], nvalid=nvalid.astype(jnp.int32),
                xs_idx=xs_idx, fill=fill, n_rows=n_rows)


def _moe(layer, hext, route, counts, w_gate, w_up, w_down, g, b):
    plan = _moe_plan(route, counts)
    xs, inv = _dispatch(plan["dest"], plan["fill"], hext, plan["n_rows"])
    return _experts(layer, plan["ea"], plan["eb"], plan["nvalid"], plan["xs_idx"], inv, xs,
                    w_gate, w_up, w_down, g, b, hext.shape[0])


def kernel(x, mem, ab_w_in, ab_conv_w, ab_a_log, ab_dt_bias, ab_norm_w, ab_w_out, sc_w_in, sc_conv_w, sc_w_out, xa_w_q, xa_w_kv, xa_w_o, router_w, router_b, moe_w_gate, moe_w_up, moe_w_down, ln_g, ln_b):
    batch, seq, d = x.shape
    n_mem = mem.shape[1]
    t = batch * seq
    h = x.reshape(t, d)
    mem2 = mem.reshape(batch * n_mem, d)
    rwt = jnp.transpose(router_w)
    rb = router_b.reshape(N_EXPERTS, 1)
    row = lambda v: v.reshape(1, -1)

    for layer in range(DEPTH):
        i = layer // 2
        if layer % 2 == 0:
            w_in = ab_w_in[i]
            c0, c1, c2 = 3 * DN_WIDTH, 4 * DN_WIDTH, 4 * DN_WIDTH + 2 * DN_HEADS
            w_ba = jnp.pad(w_in[:, c1:c2], ((0, 0), (0, LANES - 2 * DN_HEADS)))
            lane_pad = lambda v: jnp.pad(v.reshape(1, DN_HEADS), ((0, 0), (DN_HEADS, LANES - 2 * DN_HEADS)))
            act, z, bg, qkv_d = _project_ab(h, w_in[:, :c0].astype(bf16), w_in[:, c0:c1].astype(bf16), w_ba.astype(bf16),
                                            w_in[:, c2:].astype(bf16), ab_conv_w[i], lane_pad(ab_a_log[i]),
                                            lane_pad(ab_dt_bias[i]), seq)
            o_a = _deltanet(act, bg, z, row(ab_norm_w[i]), batch, seq)
            o_b = _dilated_attention(qkv_d.reshape(batch, seq, 3 * DA_WIDTH), batch, seq).reshape(t, DA_WIDTH)
            mix = (o_a, o_b, ab_w_out[i].astype(bf16), row(ln_g[layer, 0]), row(ln_b[layer, 0]))
        else:
            h = _shortconv(h, sc_w_in[i].astype(bf16), sc_conv_w[i], sc_w_out[i].astype(bf16),
                           row(ln_g[layer, 0]), row(ln_b[layer, 0]), batch, seq)
            mix = None
        w_q = (xa_w_q[layer] * (XA_HEAD_DIM ** -0.5 * LOG2_E)).astype(bf16)
        hext, route, counts = _xattn_route(h, mem2, xa_w_kv[layer].astype(bf16), w_q, xa_w_o[layer].astype(bf16),
                                           row(ln_g[layer, 1]), row(ln_b[layer, 1]), rwt, rb, batch, seq, n_mem, mix)
        h = _moe(layer, hext, route, counts, moe_w_gate, moe_w_up, moe_w_down,
                 row(ln_g[layer, 2]), row(ln_b[layer, 2]))
    return h.reshape(batch, seq, d)
```

```python
import functools

import jax
import jax.numpy as jnp
from jax import lax
from jax.experimental import pallas as pl
from jax.experimental.pallas import tpu as pltpu

D_MODEL = 1024
DEPTH = 2
DN_HEADS = 4
DN_HEAD_DIM = 128
DN_WIDTH = DN_HEADS * DN_HEAD_DIM
DN_CONV = 4
DN_CHUNK = 64
DA_HEADS = 8
DA_HEAD_DIM = 64
DA_WIDTH = DA_HEADS * DA_HEAD_DIM
DA_BRANCHES = ((128, 1), (512, 4), (2048, 16))
DA_BLOCK = 128
DA_RESIDUES = max(d for _, d in DA_BRANCHES)
SC_CONV = 3
XA_HEADS = 4
XA_HEAD_DIM = D_MODEL // XA_HEADS
N_EXPERTS = 16
N_GROUPS = 4
EXPERTS_PER_GROUP = N_EXPERTS // N_GROUPS
D_EXPERT = D_MODEL // 2
DEEPNORM_ALPHA = (2 * DEPTH) ** 0.25
LN_EPS = 1e-5
RMS_EPS = 1e-6

LANES = 128
SUBLANES = 8
VMEM_LIMIT_BYTES = 48 * 1024 * 1024

TOKEN_TILE = 512
PROJ_COLS = 512
XA_TILE = 1024
SC_TILE = 1024
DN_BLOCK = 512
MOE_TILE = 256
DISPATCH_CHUNK = 2048
DA_UNROLL = 4
GATE_LANES = LANES
ROW_EXT = D_MODEL + GATE_LANES

_PAIRS = tuple((i, j) for i in range(EXPERTS_PER_GROUP) for j in range(i + 1, EXPERTS_PER_GROUP))
N_BUCKETS = N_GROUPS * len(_PAIRS)
BUCKET_ROWS = 32

NEG_BIG = -1e30
LOG2_E = 1.4426950408889634

bf16 = jnp.bfloat16
f32 = jnp.float32


def _params(semantics):
    return pltpu.CompilerParams(dimension_semantics=semantics, vmem_limit_bytes=VMEM_LIMIT_BYTES)


def _dot(a, b):
    return jnp.dot(a.astype(bf16), b.astype(bf16), preferred_element_type=f32)


def _dot_nt(a, b):
    return lax.dot_general(a.astype(bf16), b.astype(bf16), (((1,), (1,)), ((), ())), preferred_element_type=f32)


def _dot_tn(a, b):
    return lax.dot_general(a.astype(bf16), b.astype(bf16), (((0,), (0,)), ((), ())), preferred_element_type=f32)


def _split(a, parts):
    out = []
    rem = a
    for _ in range(parts):
        p = rem.astype(bf16)
        out.append(p)
        rem = rem - p.astype(f32)
    return out


def _layer_norm(v, g, b):
    mu = jnp.mean(v, -1, keepdims=True)
    c = v - mu
    var = jnp.mean(c * c, -1, keepdims=True)
    return c * lax.rsqrt(var + LN_EPS) * g + b


def _silu(v):
    return v * jax.nn.sigmoid(v)


def _softplus(v):
    return jnp.maximum(v, 0.0) + jnp.log(1.0 + jnp.exp(-jnp.abs(v)))


def _proj_ab_kernel(x_ref, wqkv_ref, wz_ref, wba_ref, wd_ref, cw_ref, alog_ref, dtb_ref,
                    act_ref, z_ref, bg_ref, qkvd_ref, xx_ref, *, tiles_per_seq):
    @pl.when(pl.program_id(0) % tiles_per_seq == 0)
    def _():
        xx_ref[0:SUBLANES, :] = jnp.zeros((SUBLANES, xx_ref.shape[1]), f32)

    tm, hd = x_ref.shape[0], DN_HEAD_DIM
    xb = x_ref[...].astype(bf16)
    cw = cw_ref[...]

    def conv_act(c):
        cols = slice(c * hd, (c + 1) * hd)
        conv = xx_ref[SUBLANES:, cols] * cw[DN_CONV - 1:DN_CONV, cols]
        for j in range(DN_CONV - 1):
            back = DN_CONV - 1 - j
            conv = conv + xx_ref[pl.ds(SUBLANES - back, tm), cols] * cw[j:j + 1, cols]
        a = _silu(conv)
        if c < 2 * DN_HEADS:
            scale = hd ** -0.5 if c < DN_HEADS else 1.0
            a = a * (lax.rsqrt(jnp.sum(a * a, -1, keepdims=True) + RMS_EPS) * scale)
        act_ref[:, cols] = a

    cc = PROJ_COLS
    for c0 in range(0, 3 * DN_WIDTH, cc):
        xx_ref[SUBLANES:, c0:c0 + cc] = jnp.dot(xb, wqkv_ref[:, c0:c0 + cc], preferred_element_type=f32)
        for c in range(c0 // hd, (c0 + cc) // hd):
            conv_act(c)
    z_ref[...] = jnp.dot(xb, wz_ref[...], preferred_element_type=f32)
    for c0 in range(0, 3 * DA_WIDTH, cc):
        qkvd_ref[:, c0:c0 + cc] = jnp.dot(xb, wd_ref[:, c0:c0 + cc], preferred_element_type=f32)
    ba = jnp.dot(xb, wba_ref[...], preferred_element_type=f32)
    is_beta = lax.broadcasted_iota(jnp.int32, ba.shape, 1) < DN_HEADS
    bg_ref[...] = jnp.where(is_beta, jax.nn.sigmoid(ba), -jnp.exp(alog_ref[...]) * _softplus(ba + dtb_ref[...]))
    xx_ref[0:SUBLANES, :] = xx_ref[tm:tm + SUBLANES, :]


def _project_ab(x, w_qkv, w_z, w_ba, w_d, conv_w, alog_row, dtb_row, seq):
    t, k = x.shape
    tm = TOKEN_TILE
    fixed = lambda i: (0, 0)
    row = lambda i: (i, 0)
    widths = (3 * DN_WIDTH, DN_WIDTH, LANES, 3 * DA_WIDTH)
    return pl.pallas_call(
        functools.partial(_proj_ab_kernel, tiles_per_seq=seq // tm),
        grid=(t // tm,),
        in_specs=[pl.BlockSpec((tm, k), row)]
        + [pl.BlockSpec(w.shape, fixed) for w in (w_qkv, w_z, w_ba, w_d, conv_w, alog_row, dtb_row)],
        out_specs=[pl.BlockSpec((tm, n), row) for n in widths],
        out_shape=[jax.ShapeDtypeStruct((t, n), f32) for n in widths],
        scratch_shapes=[pltpu.VMEM((SUBLANES + tm, 3 * DN_WIDTH), f32)],
        compiler_params=_params(("arbitrary",)),
        name="project_ab",
    )(x, w_qkv, w_z, w_ba, w_d, conv_w, alog_row, dtb_row)


def _deltanet_kernel(act_ref, bg_ref, z_ref, nw_ref, o_ref, state_ref):
    @pl.when(pl.program_id(1) == 0)
    def _():
        state_ref[...] = jnp.zeros_like(state_ref)

    rows, ch, hd, width = DN_BLOCK, DN_CHUNK, DN_HEAD_DIM, DN_WIDTH
    sup = 2 * ch
    n_sup = rows // sup

    act = act_ref[...]
    beta_all = bg_ref[...]
    g_all = beta_all

    ri = lax.broadcasted_iota(jnp.int32, (sup, sup), 0)
    ci = lax.broadcasted_iota(jnp.int32, (sup, sup), 1)
    same = (ri >= ch) == (ci >= ch)
    incl = jnp.logical_and(ri >= ci, same)
    strict = jnp.logical_and(ri > ci, same)
    lower_ones = jnp.where(incl, 1.0, 0.0).astype(bf16)
    eye = jnp.where(ri == ci, 1.0, 0.0)
    first_rows = lax.broadcasted_iota(jnp.int32, (sup, hd), 0) < ch
    z = z_ref[...]
    nw = nw_ref[...]

    chains = [(s, h) for s in range(n_sup) for h in range(DN_HEADS)]
    g_cum_all, g_cum_t = [], []
    for s in range(n_sup):
        g_blk = g_all[s * sup:(s + 1) * sup, :]
        gc = sum(jnp.dot(lower_ones, p, preferred_element_type=f32) for p in _split(g_blk, 2))
        g_cum_all.append(gc)
        g_cum_t.append(jnp.transpose(gc))

    st = {}
    for s, h in chains:
        r0 = s * sup
        q = act[r0:r0 + sup, h * hd:(h + 1) * hd]
        k = act[r0:r0 + sup, width + h * hd:width + (h + 1) * hd]
        v = act[r0:r0 + sup, 2 * width + h * hd:2 * width + (h + 1) * hd]
        beta_b = jnp.broadcast_to(beta_all[r0:r0 + sup, h:h + 1], (sup, hd))
        g_i = jnp.broadcast_to(g_cum_all[s][:, DN_HEADS + h:DN_HEADS + h + 1], (sup, hd))
        g_j = jnp.broadcast_to(g_cum_t[s][DN_HEADS + h:DN_HEADS + h + 1, :], (sup, sup))
        decay = jnp.where(incl, jnp.exp(jnp.minimum(g_i - g_j, 0.0)), 0.0)
        e_g = jnp.exp(g_i)
        g_last = jnp.where(first_rows, g_i[ch - 1:ch, :], g_i[sup - 1:sup, :])
        st[s, h] = dict(q=q, k=k, kb=k * beta_b, vb=v * beta_b, decay=decay, e_g=e_g,
                        k_tail=k * jnp.exp(g_last - g_i),
                        gl=(jnp.exp(g_i[ch - 1:ch, :]), jnp.exp(g_i[sup - 1:sup, :])))

    for c in chains:
        d = st[c]
        d["a"] = jnp.where(strict, _dot_nt(d["kb"], d["k"]) * d["decay"], 0.0)
        d["attn"] = jnp.where(incl, _dot_nt(d["q"], d["k"]) * d["decay"], 0.0)
    for c in chains:
        st[c]["t_inv"] = eye - st[c]["a"]
        st[c]["pw"] = st[c]["a"]
    for _ in range(5):
        for c in chains:
            st[c]["pw"] = _dot(st[c]["pw"], st[c]["pw"])
        for c in chains:
            st[c]["t_inv"] = st[c]["t_inv"] + _dot(st[c]["t_inv"], st[c]["pw"])
    for c in chains:
        d = st[c]
        sol = _dot(d["t_inv"], jnp.concatenate([d["vb"], d["kb"] * d["e_g"]], axis=1))
        d["u"], d["w"] = sol[:, :hd], sol[:, hd:]
    for c in chains:
        d = st[c]
        uw = jnp.concatenate([d["u"], d["w"]], axis=1).astype(bf16)
        mix = jnp.dot(d["attn"].astype(bf16), uw, preferred_element_type=f32)
        d["au"] = mix[:, :hd]
        d["qe"] = d["q"] * d["e_g"] - mix[:, hd:]
        kt = d["k_tail"].astype(bf16)
        zero = jnp.zeros_like(kt)
        d["nu_pw"] = [_dot_tn(jnp.where(first_rows == first, kt, zero), uw)
                      for first in (True, False)]

    for s in range(n_sup):
        for half in range(2):
            for h in range(DN_HEADS):
                d = st[s, h]
                state = state_ref[h]
                lo = half * ch
                o = _dot(d["qe"][lo:lo + ch], state) + d["au"][lo:lo + ch]
                nu_pw = d["nu_pw"][half]
                state_ref[h] = state * d["gl"][half] - _dot(nu_pw[:, hd:], state) + nu_pw[:, :hd]
                o = o * lax.rsqrt(jnp.mean(o * o, -1, keepdims=True) + RMS_EPS) * nw
                r0 = s * sup + lo
                o_ref[r0:r0 + ch, h * hd:(h + 1) * hd] = (o * _silu(z[r0:r0 + ch, h * hd:(h + 1) * hd])).astype(o_ref.dtype)


def _deltanet(act, bg, z, norm_w, batch, seq):
    nb = seq // DN_BLOCK
    row = lambda b, c: (b * nb + c, 0)
    fixed = lambda b, c: (0, 0)
    return pl.pallas_call(
        _deltanet_kernel,
        grid=(batch, nb),
        in_specs=[
            pl.BlockSpec((DN_BLOCK, 3 * DN_WIDTH), row),
            pl.BlockSpec((DN_BLOCK, LANES), row),
            pl.BlockSpec((DN_BLOCK, DN_WIDTH), row),
            pl.BlockSpec((1, DN_HEAD_DIM), fixed),
        ],
        out_specs=pl.BlockSpec((DN_BLOCK, DN_WIDTH), row),
        out_shape=jax.ShapeDtypeStruct((batch * seq, DN_WIDTH), bf16),
        scratch_shapes=[pltpu.VMEM((DN_HEADS, DN_HEAD_DIM, DN_HEAD_DIM), f32)],
        compiler_params=_params(("arbitrary", "arbitrary")),
        name="deltanet",
    )(act, bg, z, norm_w)


def _dilated_kernel(q_ref, k_ref, v_ref, o_ref, qs_ref, ks_ref, vs_ref, ob_ref, lse_ref, s_ref, p_ref, st_ref, *, seq):
    blk, res = DA_BLOCK, DA_RESIDUES
    per = seq // res
    head0 = lax.broadcasted_iota(jnp.int32, (blk, LANES), 1) < DA_HEAD_DIM
    ai = lax.broadcasted_iota(jnp.int32, (2 * blk, blk), 0) & (blk - 1)
    bi = lax.broadcasted_iota(jnp.int32, (2 * blk, blk), 1)
    ones_blk = jnp.ones((2 * blk, LANES), bf16)

    q_scale = DA_HEAD_DIM ** -0.5 * LOG2_E
    for r in range(res):
        rows = pl.ds(r, per, stride=res)
        qs_ref[r * per:(r + 1) * per, :] = q_ref[0, rows, :] * q_scale
        ks_ref[r * per:(r + 1) * per, :] = k_ref[0, rows, :]
        vs_ref[r * per:(r + 1) * per, :] = v_ref[0, rows, :]

    for br, (window, dil) in enumerate(DA_BRANCHES):
        assert window // dil == blk and res % dil == 0
        runs = res // dil
        run_len = blk // runs
        nblk = seq // (dil * blk)
        shift = run_len.bit_length() - 1

        def pos(a, runs=runs, run_len=run_len, shift=shift):
            return runs * (a & (run_len - 1)) + (a >> shift)

        in_window = jnp.concatenate([pos(bi) >= pos(ai), pos(bi) <= pos(ai)], axis=1)

        n_iter = dil * nblk // DA_UNROLL
        assert n_iter * DA_UNROLL == dil * nblk

        def tile(ref, offs, run_len=run_len, runs=runs):
            parts = [ref[pl.ds(o, run_len), :] for o in offs]
            return parts[0] if runs == 1 else jnp.concatenate(parts, axis=0)

        def blocks_of(it, dil=dil, runs=runs, run_len=run_len, nblk=nblk):
            out = []
            for u in range(DA_UNROLL):
                idx = it * DA_UNROLL + u
                cls = idx // nblk
                n = idx - cls * nblk
                n_prev = jnp.maximum(n - 1, 0)
                cur = [pl.multiple_of((dil * m + cls) * per + run_len * n, SUBLANES) for m in range(runs)]
                prev = [pl.multiple_of((dil * m + cls) * per + run_len * n_prev, SUBLANES) for m in range(runs)]
                out.append((cur, prev, n > 0))
            return out

        def masked_scores(blocks, tile=tile, in_window=in_window):
            out = []
            for cur, prev, has_prev in blocks:
                q = tile(qs_ref, cur).astype(bf16)
                zero = jnp.zeros_like(q)
                q2 = jnp.concatenate([jnp.where(head0, q, zero), jnp.where(head0, zero, q)], axis=0)
                keys = jnp.concatenate([tile(ks_ref, prev), tile(ks_ref, cur)], axis=0)
                s = jnp.where(in_window, _dot_nt(q2, keys), NEG_BIG)
                out.append(jnp.concatenate([jnp.where(has_prev, s[:, :blk], NEG_BIG), s[:, blk:]], axis=1))
            return out

        def softmax_stage(scores):
            out = []
            for s in scores:
                top = jnp.max(s, -1, keepdims=True)
                out.append((jnp.exp2(s - top).astype(bf16), jnp.where(head0, top[:blk], top[blk:])))
            return out

        def output_stage(blocks, probs, br=br, run_len=run_len, tile=tile):
            for (cur, prev, has_prev), (pb, top) in zip(blocks, probs):
                vals = jnp.concatenate([tile(vs_ref, prev), tile(vs_ref, cur)], axis=0).astype(bf16)
                acc = jnp.dot(pb, jnp.concatenate([vals, ones_blk], axis=1), preferred_element_type=f32)
                den = jnp.where(head0, acc[:blk, LANES:], acc[blk:, LANES:])
                o_blk = jnp.where(head0, acc[:blk, :LANES], acc[blk:, :LANES]) / den
                lse_blk = top + jnp.log2(den)
                for m, o in enumerate(cur):
                    ob_ref[br, pl.ds(o, run_len), :] = o_blk[m * run_len:(m + 1) * run_len]
                    lse_ref[br, pl.ds(o, run_len), :] = lse_blk[m * run_len:(m + 1) * run_len]

        def put_scores(scores):
            for u, sc in enumerate(scores):
                s_ref[u] = sc

        def put_probs(probs):
            for u, (pb, top) in enumerate(probs):
                p_ref[u] = pb
                st_ref[u] = top

        put_probs(softmax_stage(masked_scores(blocks_of(0))))
        put_scores(masked_scores(blocks_of(min(1, n_iter - 1))))

        def body(it, carry, n_iter=n_iter, blocks_of=blocks_of, masked_scores=masked_scores,
                 softmax_stage=softmax_stage, output_stage=output_stage, put_scores=put_scores, put_probs=put_probs):
            scores = [s_ref[u] for u in range(DA_UNROLL)]
            probs = [(p_ref[u], st_ref[u]) for u in range(DA_UNROLL)]
            output_stage(blocks_of(it), probs)
            next_probs = softmax_stage(scores)
            next_scores = masked_scores(blocks_of(jnp.minimum(it + 2, n_iter - 1)))
            put_scores(next_scores)
            put_probs(next_probs)
            return carry

        lax.fori_loop(0, n_iter, body, 0)

    for r in range(res):
        rows = slice(r * per, (r + 1) * per)
        lses = [lse_ref[br, rows, :] for br in range(len(DA_BRANCHES))]
        top = functools.reduce(jnp.maximum, lses)
        wts = [jnp.exp2(l - top) for l in lses]
        num = sum(w * ob_ref[br, rows, :] for br, w in enumerate(wts))
        o_ref[0, pl.ds(r, per, stride=res), :] = num / sum(wts)


def _dilated_attention(qkv3, batch, seq):
    pairs = DA_WIDTH // LANES
    n_br = len(DA_BRANCHES)
    spec = lambda off: pl.BlockSpec((1, seq, LANES), lambda b, p, off=off: (b, 0, off + p))
    return pl.pallas_call(
        functools.partial(_dilated_kernel, seq=seq),
        grid=(batch, pairs),
        in_specs=[spec(0), spec(pairs), spec(2 * pairs)],
        out_specs=pl.BlockSpec((1, seq, LANES), lambda b, p: (b, 0, p)),
        out_shape=jax.ShapeDtypeStruct((batch, seq, DA_WIDTH), f32),
        scratch_shapes=[pltpu.VMEM((seq, LANES), f32)] * 3
        + [pltpu.VMEM((n_br, seq, LANES), f32)] * 2
        + [pltpu.VMEM((DA_UNROLL, 2 * DA_BLOCK, 2 * DA_BLOCK), f32),
           pltpu.VMEM((DA_UNROLL, 2 * DA_BLOCK, 2 * DA_BLOCK), bf16),
           pltpu.VMEM((DA_UNROLL, DA_BLOCK, LANES), f32)],
        compiler_params=_params(("parallel", "parallel")),
        name="dilated_attention",
    )(qkv3, qkv3, qkv3)


def _shortconv_kernel(h_ref, win_ref, cw_ref, wout_ref, g_ref, b_ref, o_ref, tail_ref):
    @pl.when(pl.program_id(1) == 0)
    def _():
        tail_ref[...] = jnp.zeros_like(tail_ref)

    tm = h_ref.shape[0]
    d = D_MODEL
    h = h_ref[...]
    hb = h.astype(bf16)
    gate_b = jnp.dot(hb, win_ref[:, :d], preferred_element_type=f32)
    gate_c = jnp.dot(hb, win_ref[:, d:2 * d], preferred_element_type=f32)
    hid = jnp.dot(hb, win_ref[:, 2 * d:], preferred_element_type=f32)
    u = gate_c * hid
    uu = jnp.concatenate([tail_ref[...], u], axis=0)
    cw = cw_ref[...]
    conv = u * cw[SC_CONV - 1:SC_CONV, :]
    for j in range(SC_CONV - 1):
        back = SC_CONV - 1 - j
        conv = conv + uu[SUBLANES - back:SUBLANES - back + tm, :] * cw[j:j + 1, :]
    tail_ref[...] = u[tm - SUBLANES:, :]
    y = jnp.dot((gate_b * conv).astype(bf16), wout_ref[...], preferred_element_type=f32)
    o_ref[...] = _layer_norm(DEEPNORM_ALPHA * h + y, g_ref[...], b_ref[...])


def _shortconv(h, w_in, conv_w, w_out, g, b, batch, seq):
    tm = SC_TILE
    ns = seq // tm
    row = lambda bb, s: (bb * ns + s, 0)
    fixed = lambda bb, s: (0, 0)
    once = dict(pipeline_mode=pl.Buffered(1))
    return pl.pallas_call(
        _shortconv_kernel,
        grid=(batch, ns),
        in_specs=[
            pl.BlockSpec((tm, D_MODEL), row),
            pl.BlockSpec(w_in.shape, fixed, **once),
            pl.BlockSpec(conv_w.shape, fixed),
            pl.BlockSpec(w_out.shape, fixed, **once),
            pl.BlockSpec((1, D_MODEL), fixed),
            pl.BlockSpec((1, D_MODEL), fixed),
        ],
        out_specs=pl.BlockSpec((tm, D_MODEL), row),
        out_shape=jax.ShapeDtypeStruct((batch * seq, D_MODEL), f32),
        scratch_shapes=[pltpu.VMEM((SUBLANES, D_MODEL), f32)],
        compiler_params=_params(("arbitrary", "arbitrary")),
        name="shortconv",
    )(h, w_in, conv_w, w_out, g, b)


def _route(h2, rwt_ref, rb_ref, cnt_ref):
    tm = h2.shape[0]
    w_hi, w_lo = _split(rwt_ref[...], 2)
    x_hi, x_lo = _split(h2, 2)
    nt = (((1,), (1,)), ((), ()))
    logits = (lax.dot_general(w_hi, x_hi, nt, preferred_element_type=f32)
              + lax.dot_general(w_hi, x_lo, nt, preferred_element_type=f32)
              + lax.dot_general(w_lo, x_hi, nt, preferred_element_type=f32))
    mx = jnp.max(logits, axis=0, keepdims=True)
    ex = jnp.exp(logits - mx)
    scores = ex / jnp.sum(ex, axis=0, keepdims=True)
    sel = scores + rb_ref[...]
    best = jnp.full((1, tm), -jnp.inf, f32)
    bucket = jnp.zeros((1, tm), jnp.int32)
    s_a = jnp.zeros((1, tm), f32)
    s_b = jnp.zeros((1, tm), f32)
    for grp in range(N_GROUPS):
        for p, (i, j) in enumerate(_PAIRS):
            a, b = grp * EXPERTS_PER_GROUP + i, grp * EXPERTS_PER_GROUP + j
            ps = sel[a:a + 1, :] + sel[b:b + 1, :]
            upd = ps > best
            best = jnp.where(upd, ps, best)
            bucket = jnp.where(upd, grp * len(_PAIRS) + p, bucket)
            s_a = jnp.where(upd, scores[a:a + 1, :], s_a)
            s_b = jnp.where(upd, scores[b:b + 1, :], s_b)
    denom = s_a + s_b
    gate_rows = jnp.concatenate([s_a / denom, s_b / denom, jnp.zeros((GATE_LANES - 2, tm), f32)], axis=0)
    gates = jnp.transpose(gate_rows)
    onehot = jnp.where(lax.broadcasted_iota(jnp.int32, (BUCKET_ROWS, tm), 0) == bucket, 1.0, 0.0)
    before = lax.broadcasted_iota(jnp.int32, (tm, tm), 0) < lax.broadcasted_iota(jnp.int32, (tm, tm), 1)
    prefix = jnp.dot(onehot.astype(bf16), jnp.where(before, 1.0, 0.0).astype(bf16), preferred_element_type=f32)
    cnt = cnt_ref[...]
    rank = jnp.sum(onehot * (prefix + cnt[:, 0:1]), axis=0, keepdims=True)
    cnt_ref[...] = cnt + jnp.sum(onehot, axis=1, keepdims=True)
    return gates, bucket, rank.astype(jnp.int32)


def _xattn_kernel(*refs, mix):
    if mix:
        (oa_ref, ob_ref, wmix_ref, g0_ref, b0_ref), refs = refs[:5], refs[5:]
    (h_ref, mem_ref, wkv_ref, wq_ref, wo_ref, g_ref, b_ref, rwt_ref, rb_ref,
     hext_ref, route_ref, cnt_out_ref, cnt_ref, kv_ref) = refs

    @pl.when(jnp.logical_and(pl.program_id(0) == 0, pl.program_id(1) == 0))
    def _():
        cnt_ref[...] = jnp.zeros_like(cnt_ref)

    @pl.when(pl.program_id(1) == 0)
    def _():
        kv_ref[...] = jnp.dot(mem_ref[...].astype(bf16), wkv_ref[...], preferred_element_type=f32).astype(bf16)

    h = h_ref[...]
    tm = h.shape[0]
    if mix:
        y0 = (jnp.dot(oa_ref[...], wmix_ref[:DN_WIDTH], preferred_element_type=f32)
              + jnp.dot(ob_ref[...].astype(bf16), wmix_ref[DN_WIDTH:], preferred_element_type=f32))
        h = _layer_norm(DEEPNORM_ALPHA * h + y0, g0_ref[...], b0_ref[...])
    q = jnp.dot(h.astype(bf16), wq_ref[...], preferred_element_type=f32).astype(bf16)
    heads = [(hd * XA_HEAD_DIM, (hd + 1) * XA_HEAD_DIM) for hd in range(XA_HEADS)]
    scores = [_dot_nt(q[:, lo:hi], kv_ref[:, lo:hi]) for lo, hi in heads]
    probs = [jnp.exp2(s - jnp.max(s, -1, keepdims=True)) for s in scores]
    outs = [(_dot(p, kv_ref[:, D_MODEL + lo:D_MODEL + hi]) / jnp.sum(p, -1, keepdims=True)).astype(bf16)
            for p, (lo, hi) in zip(probs, heads)]
    y = jnp.dot(jnp.concatenate(outs, axis=1), wo_ref[...], preferred_element_type=f32)
    h2 = _layer_norm(DEEPNORM_ALPHA * h + y, g_ref[...], b_ref[...])
    gates, bucket, rank = _route(h2, rwt_ref, rb_ref, cnt_ref)
    hext_ref[:, :D_MODEL] = h2
    hext_ref[:, D_MODEL:] = gates
    route_ref[...] = jnp.concatenate([bucket, rank, jnp.zeros((SUBLANES - 2, tm), jnp.int32)], axis=0)
    cnt_out_ref[...] = cnt_ref[...]


def _xattn_route(h, mem, w_kv, w_q, w_o, g, b, rwt, rb, batch, seq, n_mem, mix=None):
    tm = XA_TILE
    ns = seq // tm
    t = batch * seq
    row = lambda bb, s: (bb * ns + s, 0)
    fixed = lambda bb, s: (0, 0)
    once = dict(pipeline_mode=pl.Buffered(1))
    mix_specs, mix_args = [], []
    if mix is not None:
        o_a, o_b, w_mix, g0, b0 = mix
        mix_specs = [pl.BlockSpec((tm, DN_WIDTH), row), pl.BlockSpec((tm, DA_WIDTH), row),
                     pl.BlockSpec(w_mix.shape, fixed, **once),
                     pl.BlockSpec((1, D_MODEL), fixed), pl.BlockSpec((1, D_MODEL), fixed)]
        mix_args = [o_a, o_b, w_mix, g0, b0]
    return pl.pallas_call(
        functools.partial(_xattn_kernel, mix=mix is not None),
        grid=(batch, ns),
        in_specs=mix_specs + [
            pl.BlockSpec((tm, D_MODEL), row),
            pl.BlockSpec((n_mem, D_MODEL), lambda bb, s: (bb, 0)),
            pl.BlockSpec(w_kv.shape, fixed, **once),
            pl.BlockSpec(w_q.shape, fixed, **once),
            pl.BlockSpec(w_o.shape, fixed, **once),
            pl.BlockSpec((1, D_MODEL), fixed),
            pl.BlockSpec((1, D_MODEL), fixed),
            pl.BlockSpec(rwt.shape, fixed),
            pl.BlockSpec(rb.shape, fixed),
        ],
        out_specs=[
            pl.BlockSpec((tm, ROW_EXT), row),
            pl.BlockSpec((SUBLANES, tm), lambda bb, s: (0, bb * ns + s)),
            pl.BlockSpec((BUCKET_ROWS, LANES), fixed),
        ],
        out_shape=[
            jax.ShapeDtypeStruct((t, ROW_EXT), f32),
            jax.ShapeDtypeStruct((SUBLANES, t), jnp.int32),
            jax.ShapeDtypeStruct((BUCKET_ROWS, LANES), f32),
        ],
        scratch_shapes=[pltpu.VMEM((BUCKET_ROWS, LANES), f32), pltpu.VMEM((n_mem, 2 * D_MODEL), bf16)],
        compiler_params=_params(("arbitrary", "arbitrary")),
        name="xattn_route",
    )(*mix_args, h, mem, w_kv, w_q, w_o, g, b, rwt, rb)


def _dispatch_kernel(dest_ref, fill_ref, hext_ref, xs_ref, inv_ref, zeros_ref, sem, zsem):
    base = pl.program_id(0) * DISPATCH_CHUNK
    n_tiles = xs_ref.shape[0] // MOE_TILE

    @pl.when(pl.program_id(0) == 0)
    def _():
        zeros_ref[...] = jnp.zeros_like(zeros_ref)

        def fill(tile):
            return pltpu.make_async_copy(zeros_ref, xs_ref.at[pl.ds(tile * MOE_TILE, MOE_TILE)], zsem)

        for tile in range(n_tiles):
            @pl.when(fill_ref[tile] > 0)
            def _(tile=tile):
                fill(tile).start()

        def clear(j8, carry):
            for u in range(8):
                inv_ref[j8 * 8 + u] = 0
            return carry

        lax.fori_loop(0, inv_ref.shape[0] // 8, clear, 0)
        for tile in range(n_tiles):
            @pl.when(fill_ref[tile] > 0)
            def _(tile=tile):
                fill(tile).wait()

    for j in range(DISPATCH_CHUNK):
        d = dest_ref[base + j]
        inv_ref[d] = base + j
        pltpu.make_async_copy(hext_ref.at[pl.ds(j, 1)], xs_ref.at[pl.ds(d, 1)], sem).start(priority=j % 2)
    pltpu.make_async_copy(hext_ref, xs_ref.at[pl.ds(0, DISPATCH_CHUNK)], sem).wait()


def _dispatch(dest, fill, hext, n_rows):
    t = hext.shape[0]
    return pl.pallas_call(
        _dispatch_kernel,
        grid_spec=pltpu.PrefetchScalarGridSpec(
            num_scalar_prefetch=2,
            grid=(t // DISPATCH_CHUNK,),
            in_specs=[pl.BlockSpec((DISPATCH_CHUNK, ROW_EXT), lambda i, *_: (i, 0))],
            out_specs=[pl.BlockSpec(memory_space=pl.ANY), pl.BlockSpec(memory_space=pltpu.SMEM)],
            scratch_shapes=[pltpu.VMEM((MOE_TILE, ROW_EXT), f32), pltpu.SemaphoreType.DMA(()),
                            pltpu.SemaphoreType.DMA(())],
        ),
        out_shape=[jax.ShapeDtypeStruct((n_rows, ROW_EXT), f32), jax.ShapeDtypeStruct((n_rows,), jnp.int32)],
        compiler_params=_params(("arbitrary",)),
        name="moe_dispatch",
    )(dest, fill, hext)


def _experts_kernel(ea_ref, eb_ref, nvalid_ref, xs_idx_ref, inv_ref, xs_ref, wga_ref, wua_ref, wda_ref,
                    wgb_ref, wub_ref, wdb_ref, g_ref, b_ref, out_ref, obuf_ref, sem_ref):
    i = pl.program_id(0)
    tm = MOE_TILE
    prev = jnp.maximum(i - 1, 0)
    n_prev = jnp.where(i > 0, nvalid_ref[prev], 0)
    slot_prev = prev % 2

    def send(j):
        return pltpu.make_async_copy(obuf_ref.at[slot_prev, pl.ds(j, 1)],
                                     out_ref.at[pl.ds(inv_ref[prev * tm + j], 1)], sem_ref.at[slot_prev])

    n_cur = jnp.where(i < pl.num_programs(0) - 1, nvalid_ref[jnp.minimum(i, pl.num_programs(0) - 2)], 0)

    def send_full(lo, hi):
        for j in range(lo, hi):
            send(j).start(priority=j % 2)

    def compute(between):
        x32 = xs_ref[:, :D_MODEL]
        x = x32.astype(bf16)
        gates = xs_ref[:, D_MODEL:]

        def hidden(wg_ref, wu_ref, gate):
            hg = jnp.dot(x, wg_ref[0, 0].astype(bf16), preferred_element_type=f32)
            hu = jnp.dot(x, wu_ref[0, 0].astype(bf16), preferred_element_type=f32)
            return (_silu(hg) * hu * gate).astype(bf16)

        between(0)
        h_a = hidden(wga_ref, wua_ref, gates[:, 0:1])
        between(1)
        h_b = hidden(wgb_ref, wub_ref, gates[:, 1:2])
        between(2)
        y = (jnp.dot(h_a, wda_ref[0, 0].astype(bf16), preferred_element_type=f32)
             + jnp.dot(h_b, wdb_ref[0, 0].astype(bf16), preferred_element_type=f32))
        between(3)
        obuf_ref[i % 2] = _layer_norm(DEEPNORM_ALPHA * x32 + y, g_ref[...], b_ref[...])

    usual = jnp.logical_and(n_prev == tm, n_cur > 0)
    quarter = tm // 4

    @pl.when(usual)
    def _():
        compute(lambda k: send_full(k * quarter, (k + 1) * quarter))

    @pl.when(jnp.logical_not(usual))
    def _():
        @pl.when(n_prev == tm)
        def _():
            send_full(0, tm)

        @pl.when(jnp.logical_and(n_prev > 0, n_prev < tm))
        def _():
            for j in range(tm):
                @pl.when(j < n_prev)
                def _(j=j):
                    send(j).start(priority=j % 2)

        @pl.when(n_cur > 0)
        def _():
            compute(lambda k: None)

    for bit in reversed(range(tm.bit_length())):
        size = 1 << bit

        @pl.when((n_prev & size) != 0)
        def _(size=size):
            pltpu.make_async_copy(obuf_ref.at[slot_prev, pl.ds(0, size)], out_ref.at[pl.ds(0, size)],
                                  sem_ref.at[slot_prev]).wait()


def _experts(layer, ea, eb, nvalid, xs_idx, inv, xs, w_gate, w_up, w_down, g, b, t):
    n_tiles = xs.shape[0] // MOE_TILE
    up_a = pl.BlockSpec((1, 1, D_MODEL, D_EXPERT), lambda i, ea, *_: (layer, ea[i], 0, 0))
    up_b = pl.BlockSpec((1, 1, D_MODEL, D_EXPERT), lambda i, ea, eb, *_: (layer, eb[i], 0, 0))
    dn_a = pl.BlockSpec((1, 1, D_EXPERT, D_MODEL), lambda i, ea, *_: (layer, ea[i], 0, 0))
    dn_b = pl.BlockSpec((1, 1, D_EXPERT, D_MODEL), lambda i, ea, eb, *_: (layer, eb[i], 0, 0))
    fixed = lambda i, *_: (0, 0)
    return pl.pallas_call(
        _experts_kernel,
        grid_spec=pltpu.PrefetchScalarGridSpec(
            num_scalar_prefetch=5,
            grid=(n_tiles + 1,),
            in_specs=[pl.BlockSpec((MOE_TILE, ROW_EXT), lambda i, ea, eb, nv, xs_idx, inv: (xs_idx[i], 0)),
                      up_a, up_a, dn_a, up_b, up_b, dn_b,
                      pl.BlockSpec((1, D_MODEL), fixed), pl.BlockSpec((1, D_MODEL), fixed)],
            out_specs=pl.BlockSpec(memory_space=pl.ANY),
            scratch_shapes=[pltpu.VMEM((2, MOE_TILE, D_MODEL), f32), pltpu.SemaphoreType.DMA((2,))],
        ),
        out_shape=jax.ShapeDtypeStruct((t, D_MODEL), f32),
        compiler_params=_params(("arbitrary",)),
        name="moe_experts",
    )(ea, eb, nvalid, xs_idx, inv, xs, w_gate, w_up, w_down, w_gate, w_up, w_down, g, b)


def _moe_plan(route, counts):
    t = route.shape[1]
    n_rows = t + N_BUCKETS * MOE_TILE
    n_tiles = n_rows // MOE_TILE
    cnt = counts[:N_BUCKETS, 0].astype(jnp.int32)
    padded = (cnt + MOE_TILE - 1) // MOE_TILE * MOE_TILE
    ends = jnp.cumsum(padded)
    starts = ends - padded
    buckets = jnp.arange(N_BUCKETS, dtype=jnp.int32)
    dest = jnp.sum(jnp.where(route[0][None, :] == buckets[:, None], starts[:, None], 0), axis=0) + route[1]
    tile_start = jnp.arange(n_tiles + 1, dtype=jnp.int32) * MOE_TILE
    tile_bucket = jnp.sum(tile_start[:, None] >= ends[None, :], axis=1).astype(jnp.int32)
    used = tile_bucket < N_BUCKETS
    tile_bucket = jnp.minimum(tile_bucket, N_BUCKETS - 1)
    nvalid = jnp.where(used, jnp.clip(starts[tile_bucket] + cnt[tile_bucket] - tile_start, 0, MOE_TILE), 0)
    xs_idx = jnp.minimum(jnp.arange(n_tiles + 1, dtype=jnp.int32), jnp.maximum(ends[-1] // MOE_TILE - 1, 0))
    last_bucket = tile_bucket[jnp.maximum(ends[-1] // MOE_TILE - 1, 0)]
    tile_bucket = jnp.where(used, tile_bucket, last_bucket)
    pair_a = jnp.array([g * EXPERTS_PER_GROUP + i for g in range(N_GROUPS) for i, _ in _PAIRS], jnp.int32)
    pair_b = jnp.array([g * EXPERTS_PER_GROUP + j for g in range(N_GROUPS) for _, j in _PAIRS], jnp.int32)
    tiles = jnp.arange(n_tiles, dtype=jnp.int32)
    is_last = jnp.any(jnp.logical_and((tiles[:, None] + 1) * MOE_TILE == ends[None, :], padded[None, :] > 0), axis=1)
    fill = jnp.logical_or(is_last, tiles * MOE_TILE >= ends[-1]).astype(jnp.int32)
    return dict(dest=dest, ea=pair_a[tile_bucket], eb=pair_b[tile_bucket], nvalid=nvalid.astype(jnp.int32),
                xs_idx=xs_idx, fill=fill, n_rows=n_rows)


def _moe(layer, hext, route, counts, w_gate, w_up, w_down, g, b):
    plan = _moe_plan(route, counts)
    xs, inv = _dispatch(plan["dest"], plan["fill"], hext, plan["n_rows"])
    return _experts(layer, plan["ea"], plan["eb"], plan["nvalid"], plan["xs_idx"], inv, xs,
                    w_gate, w_up, w_down, g, b, hext.shape[0])


def kernel(x, mem, ab_w_in, ab_conv_w, ab_a_log, ab_dt_bias, ab_norm_w, ab_w_out, sc_w_in, sc_conv_w, sc_w_out, xa_w_q, xa_w_kv, xa_w_o, router_w, router_b, moe_w_gate, moe_w_up, moe_w_down, ln_g, ln_b):
    batch, seq, d = x.shape
    n_mem = mem.shape[1]
    t = batch * seq
    h = x.reshape(t, d)
    mem2 = mem.reshape(batch * n_mem, d)
    rwt = jnp.transpose(router_w)
    rb = router_b.reshape(N_EXPERTS, 1)
    row = lambda v: v.reshape(1, -1)

    for layer in range(DEPTH):
        i = layer // 2
        if layer % 2 == 0:
            w_in = ab_w_in[i]
            c0, c1, c2 = 3 * DN_WIDTH, 4 * DN_WIDTH, 4 * DN_WIDTH + 2 * DN_HEADS
            w_ba = jnp.pad(w_in[:, c1:c2], ((0, 0), (0, LANES - 2 * DN_HEADS)))
            lane_pad = lambda v: jnp.pad(v.reshape(1, DN_HEADS), ((0, 0), (DN_HEADS, LANES - 2 * DN_HEADS)))
            act, z, bg, qkv_d = _project_ab(h, w_in[:, :c0].astype(bf16), w_in[:, c0:c1].astype(bf16), w_ba.astype(bf16),
                                            w_in[:, c2:].astype(bf16), ab_conv_w[i], lane_pad(ab_a_log[i]),
                                            lane_pad(ab_dt_bias[i]), seq)
            o_a = _deltanet(act, bg, z, row(ab_norm_w[i]), batch, seq)
            o_b = _dilated_attention(qkv_d.reshape(batch, seq, 3 * DA_WIDTH), batch, seq).reshape(t, DA_WIDTH)
            mix = (o_a, o_b, ab_w_out[i].astype(bf16), row(ln_g[layer, 0]), row(ln_b[layer, 0]))
        else:
            h = _shortconv(h, sc_w_in[i].astype(bf16), sc_conv_w[i], sc_w_out[i].astype(bf16),
                           row(ln_g[layer, 0]), row(ln_b[layer, 0]), batch, seq)
            mix = None
        w_q = (xa_w_q[layer] * (XA_HEAD_DIM ** -0.5 * LOG2_E)).astype(bf16)
        hext, route, counts = _xattn_route(h, mem2, xa_w_kv[layer].astype(bf16), w_q, xa_w_o[layer].astype(bf16),
                                           row(ln_g[layer, 1]), row(ln_b[layer, 1]), rwt, rb, batch, seq, n_mem, mix)
        h = _moe(layer, hext, route, counts, moe_w_gate, moe_w_up, moe_w_down,
                 row(ln_g[layer, 2]), row(ln_b[layer, 2]))
    return h.reshape(batch, seq, d)
```

```python
import functools

import jax
import jax.numpy as jnp
from jax import lax
from jax.experimental import pallas as pl
from jax.experimental.pallas import tpu as pltpu

D_MODEL = 1024
DEPTH = 2
DN_HEADS = 4
DN_HEAD_DIM = 128
DN_WIDTH = DN_HEADS * DN_HEAD_DIM
DN_CONV = 4
DN_CHUNK = 64
DA_HEADS = 8
DA_HEAD_DIM = 64
DA_WIDTH = DA_HEADS * DA_HEAD_DIM
DA_BRANCHES = ((128, 1), (512, 4), (2048, 16))
DA_BLOCK = 128
DA_RESIDUES = max(d for _, d in DA_BRANCHES)
SC_CONV = 3
XA_HEADS = 4
XA_HEAD_DIM = D_MODEL // XA_HEADS
N_EXPERTS = 16
N_GROUPS = 4
EXPERTS_PER_GROUP = N_EXPERTS // N_GROUPS
D_EXPERT = D_MODEL // 2
DEEPNORM_ALPHA = (2 * DEPTH) ** 0.25
LN_EPS = 1e-5
RMS_EPS = 1e-6

LANES = 128
SUBLANES = 8
VMEM_LIMIT_BYTES = 48 * 1024 * 1024

TOKEN_TILE = 512
PROJ_COLS = 512
XA_TILE = 1024
SC_TILE = 1024
DN_BLOCK = 512
MOE_TILE = 256
DISPATCH_CHUNK = 2048
DA_UNROLL = 4
GATE_LANES = LANES
ROW_EXT = D_MODEL + GATE_LANES

_PAIRS = tuple((i, j) for i in range(EXPERTS_PER_GROUP) for j in range(i + 1, EXPERTS_PER_GROUP))
N_BUCKETS = N_GROUPS * len(_PAIRS)
BUCKET_ROWS = 32

NEG_BIG = -1e30
LOG2_E = 1.4426950408889634

bf16 = jnp.bfloat16
f32 = jnp.float32


def _params(semantics):
    return pltpu.CompilerParams(dimension_semantics=semantics, vmem_limit_bytes=VMEM_LIMIT_BYTES)


def _dot(a, b):
    return jnp.dot(a.astype(bf16), b.astype(bf16), preferred_element_type=f32)


def _dot_nt(a, b):
    return lax.dot_general(a.astype(bf16), b.astype(bf16), (((1,), (1,)), ((), ())), preferred_element_type=f32)


def _dot_tn(a, b):
    return lax.dot_general(a.astype(bf16), b.astype(bf16), (((0,), (0,)), ((), ())), preferred_element_type=f32)


def _split(a, parts):
    out = []
    rem = a
    for _ in range(parts):
        p = rem.astype(bf16)
        out.append(p)
        rem = rem - p.astype(f32)
    return out


def _layer_norm(v, g, b):
    mu = jnp.mean(v, -1, keepdims=True)
    c = v - mu
    var = jnp.mean(c * c, -1, keepdims=True)
    return c * lax.rsqrt(var + LN_EPS) * g + b


def _silu(v):
    return v * jax.nn.sigmoid(v)


def _softplus(v):
    return jnp.maximum(v, 0.0) + jnp.log(1.0 + jnp.exp(-jnp.abs(v)))


def _proj_ab_kernel(x_ref, wqkv_ref, wz_ref, wba_ref, wd_ref, cw_ref, alog_ref, dtb_ref,
                    act_ref, z_ref, bg_ref, qkvd_ref, xx_ref, *, tiles_per_seq):
    @pl.when(pl.program_id(0) % tiles_per_seq == 0)
    def _():
        xx_ref[0:SUBLANES, :] = jnp.zeros((SUBLANES, xx_ref.shape[1]), f32)

    tm, hd = x_ref.shape[0], DN_HEAD_DIM
    xb = x_ref[...].astype(bf16)
    cw = cw_ref[...]

    def conv_act(c):
        cols = slice(c * hd, (c + 1) * hd)
        conv = xx_ref[SUBLANES:, cols] * cw[DN_CONV - 1:DN_CONV, cols]
        for j in range(DN_CONV - 1):
            back = DN_CONV - 1 - j
            conv = conv + xx_ref[pl.ds(SUBLANES - back, tm), cols] * cw[j:j + 1, cols]
        a = _silu(conv)
        if c < 2 * DN_HEADS:
            scale = hd ** -0.5 if c < DN_HEADS else 1.0
            a = a * (lax.rsqrt(jnp.sum(a * a, -1, keepdims=True) + RMS_EPS) * scale)
        act_ref[:, cols] = a

    cc = PROJ_COLS
    for c0 in range(0, 3 * DN_WIDTH, cc):
        xx_ref[SUBLANES:, c0:c0 + cc] = jnp.dot(xb, wqkv_ref[:, c0:c0 + cc], preferred_element_type=f32)
        for c in range(c0 // hd, (c0 + cc) // hd):
            conv_act(c)
    z_ref[...] = jnp.dot(xb, wz_ref[...], preferred_element_type=f32)
    for c0 in range(0, 3 * DA_WIDTH, cc):
        qkvd_ref[:, c0:c0 + cc] = jnp.dot(xb, wd_ref[:, c0:c0 + cc], preferred_element_type=f32)
    ba = jnp.dot(xb, wba_ref[...], preferred_element_type=f32)
    is_beta = lax.broadcasted_iota(jnp.int32, ba.shape, 1) < DN_HEADS
    bg_ref[...] = jnp.where(is_beta, jax.nn.sigmoid(ba), -jnp.exp(alog_ref[...]) * _softplus(ba + dtb_ref[...]))
    xx_ref[0:SUBLANES, :] = xx_ref[tm:tm + SUBLANES, :]


def _project_ab(x, w_qkv, w_z, w_ba, w_d, conv_w, alog_row, dtb_row, seq):
    t, k = x.shape
    tm = TOKEN_TILE
    fixed = lambda i: (0, 0)
    row = lambda i: (i, 0)
    widths = (3 * DN_WIDTH, DN_WIDTH, LANES, 3 * DA_WIDTH)
    return pl.pallas_call(
        functools.partial(_proj_ab_kernel, tiles_per_seq=seq // tm),
        grid=(t // tm,),
        in_specs=[pl.BlockSpec((tm, k), row)]
        + [pl.BlockSpec(w.shape, fixed) for w in (w_qkv, w_z, w_ba, w_d, conv_w, alog_row, dtb_row)],
        out_specs=[pl.BlockSpec((tm, n), row) for n in widths],
        out_shape=[jax.ShapeDtypeStruct((t, n), f32) for n in widths],
        scratch_shapes=[pltpu.VMEM((SUBLANES + tm, 3 * DN_WIDTH), f32)],
        compiler_params=_params(("arbitrary",)),
        name="project_ab",
    )(x, w_qkv, w_z, w_ba, w_d, conv_w, alog_row, dtb_row)


def _deltanet_kernel(act_ref, bg_ref, z_ref, nw_ref, o_ref, state_ref):
    @pl.when(pl.program_id(1) == 0)
    def _():
        state_ref[...] = jnp.zeros_like(state_ref)

    rows, ch, hd, width = DN_BLOCK, DN_CHUNK, DN_HEAD_DIM, DN_WIDTH
    sup = 2 * ch
    n_sup = rows // sup

    act = act_ref[...]
    beta_all = bg_ref[...]
    g_all = beta_all

    ri = lax.broadcasted_iota(jnp.int32, (sup, sup), 0)
    ci = lax.broadcasted_iota(jnp.int32, (sup, sup), 1)
    same = (ri >= ch) == (ci >= ch)
    incl = jnp.logical_and(ri >= ci, same)
    strict = jnp.logical_and(ri > ci, same)
    lower_ones = jnp.where(incl, 1.0, 0.0).astype(bf16)
    eye = jnp.where(ri == ci, 1.0, 0.0)
    first_rows = lax.broadcasted_iota(jnp.int32, (sup, hd), 0) < ch
    z = z_ref[...]
    nw = nw_ref[...]

    chains = [(s, h) for s in range(n_sup) for h in range(DN_HEADS)]
    g_cum_all, g_cum_t = [], []
    for s in range(n_sup):
        g_blk = g_all[s * sup:(s + 1) * sup, :]
        gc = sum(jnp.dot(lower_ones, p, preferred_element_type=f32) for p in _split(g_blk, 2))
        g_cum_all.append(gc)
        g_cum_t.append(jnp.transpose(gc))

    st = {}
    for s, h in chains:
        r0 = s * sup
        q = act[r0:r0 + sup, h * hd:(h + 1) * hd]
        k = act[r0:r0 + sup, width + h * hd:width + (h + 1) * hd]
        v = act[r0:r0 + sup, 2 * width + h * hd:2 * width + (h + 1) * hd]
        beta_b = jnp.broadcast_to(beta_all[r0:r0 + sup, h:h + 1], (sup, hd))
        g_i = jnp.broadcast_to(g_cum_all[s][:, DN_HEADS + h:DN_HEADS + h + 1], (sup, hd))
        g_j = jnp.broadcast_to(g_cum_t[s][DN_HEADS + h:DN_HEADS + h + 1, :], (sup, sup))
        decay = jnp.where(incl, jnp.exp(jnp.minimum(g_i - g_j, 0.0)), 0.0)
        e_g = jnp.exp(g_i)
        g_last = jnp.where(first_rows, g_i[ch - 1:ch, :], g_i[sup - 1:sup, :])
        st[s, h] = dict(q=q, k=k, kb=k * beta_b, vb=v * beta_b, decay=decay, e_g=e_g,
                        k_tail=k * jnp.exp(g_last - g_i),
                        gl=(jnp.exp(g_i[ch - 1:ch, :]), jnp.exp(g_i[sup - 1:sup, :])))

    for c in chains:
        d = st[c]
        d["a"] = jnp.where(strict, _dot_nt(d["kb"], d["k"]) * d["decay"], 0.0)
        d["attn"] = jnp.where(incl, _dot_nt(d["q"], d["k"]) * d["decay"], 0.0)
    for c in chains:
        st[c]["t_inv"] = eye - st[c]["a"]
        st[c]["pw"] = st[c]["a"]
    for _ in range(5):
        for c in chains:
            st[c]["pw"] = _dot(st[c]["pw"], st[c]["pw"])
        for c in chains:
            st[c]["t_inv"] = st[c]["t_inv"] + _dot(st[c]["t_inv"], st[c]["pw"])
    for c in chains:
        d = st[c]
        sol = _dot(d["t_inv"], jnp.concatenate([d["vb"], d["kb"] * d["e_g"]], axis=1))
        d["u"], d["w"] = sol[:, :hd], sol[:, hd:]
    for c in chains:
        d = st[c]
        uw = jnp.concatenate([d["u"], d["w"]], axis=1).astype(bf16)
        mix = jnp.dot(d["attn"].astype(bf16), uw, preferred_element_type=f32)
        d["au"] = mix[:, :hd]
        d["qe"] = d["q"] * d["e_g"] - mix[:, hd:]
        kt = d["k_tail"].astype(bf16)
        zero = jnp.zeros_like(kt)
        d["nu_pw"] = [_dot_tn(jnp.where(first_rows == first, kt, zero), uw)
                      for first in (True, False)]

    for s in range(n_sup):
        for half in range(2):
            for h in range(DN_HEADS):
                d = st[s, h]
                state = state_ref[h]
                lo = half * ch
                o = _dot(d["qe"][lo:lo + ch], state) + d["au"][lo:lo + ch]
                nu_pw = d["nu_pw"][half]
                state_ref[h] = state * d["gl"][half] - _dot(nu_pw[:, hd:], state) + nu_pw[:, :hd]
                o = o * lax.rsqrt(jnp.mean(o * o, -1, keepdims=True) + RMS_EPS) * nw
                r0 = s * sup + lo
                o_ref[r0:r0 + ch, h * hd:(h + 1) * hd] = (o * _silu(z[r0:r0 + ch, h * hd:(h + 1) * hd])).astype(o_ref.dtype)


def _deltanet(act, bg, z, norm_w, batch, seq):
    nb = seq // DN_BLOCK
    row = lambda b, c: (b * nb + c, 0)
    fixed = lambda b, c: (0, 0)
    return pl.pallas_call(
        _deltanet_kernel,
        grid=(batch, nb),
        in_specs=[
            pl.BlockSpec((DN_BLOCK, 3 * DN_WIDTH), row),
            pl.BlockSpec((DN_BLOCK, LANES), row),
            pl.BlockSpec((DN_BLOCK, DN_WIDTH), row),
            pl.BlockSpec((1, DN_HEAD_DIM), fixed),
        ],
        out_specs=pl.BlockSpec((DN_BLOCK, DN_WIDTH), row),
        out_shape=jax.ShapeDtypeStruct((batch * seq, DN_WIDTH), bf16),
        scratch_shapes=[pltpu.VMEM((DN_HEADS, DN_HEAD_DIM, DN_HEAD_DIM), f32)],
        compiler_params=_params(("arbitrary", "arbitrary")),
        name="deltanet",
    )(act, bg, z, norm_w)


def _dilated_kernel(q_ref, k_ref, v_ref, o_ref, qs_ref, ks_ref, vs_ref, ob_ref, lse_ref, s_ref, p_ref, st_ref, *, seq):
    blk, res = DA_BLOCK, DA_RESIDUES
    per = seq // res
    head0 = lax.broadcasted_iota(jnp.int32, (blk, LANES), 1) < DA_HEAD_DIM
    ai = lax.broadcasted_iota(jnp.int32, (2 * blk, blk), 0) & (blk - 1)
    bi = lax.broadcasted_iota(jnp.int32, (2 * blk, blk), 1)
    ones_blk = jnp.ones((2 * blk, LANES), bf16)

    q_scale = DA_HEAD_DIM ** -0.5 * LOG2_E
    for r in range(res):
        rows = pl.ds(r, per, stride=res)
        qs_ref[r * per:(r + 1) * per, :] = q_ref[0, rows, :] * q_scale
        ks_ref[r * per:(r + 1) * per, :] = k_ref[0, rows, :]
        vs_ref[r * per:(r + 1) * per, :] = v_ref[0, rows, :]

    for br, (window, dil) in enumerate(DA_BRANCHES):
        assert window // dil == blk and res % dil == 0
        runs = res // dil
        run_len = blk // runs
        nblk = seq // (dil * blk)
        shift = run_len.bit_length() - 1

        def pos(a, runs=runs, run_len=run_len, shift=shift):
            return runs * (a & (run_len - 1)) + (a >> shift)

        in_window = jnp.concatenate([pos(bi) >= pos(ai), pos(bi) <= pos(ai)], axis=1)

        n_iter = dil * nblk // DA_UNROLL
        assert n_iter * DA_UNROLL == dil * nblk

        def tile(ref, offs, run_len=run_len, runs=runs):
            parts = [ref[pl.ds(o, run_len), :] for o in offs]
            return parts[0] if runs == 1 else jnp.concatenate(parts, axis=0)

        def blocks_of(it, dil=dil, runs=runs, run_len=run_len, nblk=nblk):
            out = []
            for u in range(DA_UNROLL):
                idx = it * DA_UNROLL + u
                cls = idx // nblk
                n = idx - cls * nblk
                n_prev = jnp.maximum(n - 1, 0)
                cur = [pl.multiple_of((dil * m + cls) * per + run_len * n, SUBLANES) for m in range(runs)]
                prev = [pl.multiple_of((dil * m + cls) * per + run_len * n_prev, SUBLANES) for m in range(runs)]
                out.append((cur, prev, n > 0))
            return out

        def masked_scores(blocks, tile=tile, in_window=in_window):
            out = []
            for cur, prev, has_prev in blocks:
                q = tile(qs_ref, cur).astype(bf16)
                zero = jnp.zeros_like(q)
                q2 = jnp.concatenate([jnp.where(head0, q, zero), jnp.where(head0, zero, q)], axis=0)
                keys = jnp.concatenate([tile(ks_ref, prev), tile(ks_ref, cur)], axis=0)
                s = jnp.where(in_window, _dot_nt(q2, keys), NEG_BIG)
                out.append(jnp.concatenate([jnp.where(has_prev, s[:, :blk], NEG_BIG), s[:, blk:]], axis=1))
            return out

        def softmax_stage(scores):
            out = []
            for s in scores:
                top = jnp.max(s, -1, keepdims=True)
                out.append((jnp.exp2(s - top).astype(bf16), jnp.where(head0, top[:blk], top[blk:])))
            return out

        def output_stage(blocks, probs, br=br, run_len=run_len, tile=tile):
            for (cur, prev, has_prev), (pb, top) in zip(blocks, probs):
                vals = jnp.concatenate([tile(vs_ref, prev), tile(vs_ref, cur)], axis=0).astype(bf16)
                acc = jnp.dot(pb, jnp.concatenate([vals, ones_blk], axis=1), preferred_element_type=f32)
                den = jnp.where(head0, acc[:blk, LANES:], acc[blk:, LANES:])
                o_blk = jnp.where(head0, acc[:blk, :LANES], acc[blk:, :LANES]) / den
                lse_blk = top + jnp.log2(den)
                for m, o in enumerate(cur):
                    ob_ref[br, pl.ds(o, run_len), :] = o_blk[m * run_len:(m + 1) * run_len]
                    lse_ref[br, pl.ds(o, run_len), :] = lse_blk[m * run_len:(m + 1) * run_len]

        def put_scores(scores):
            for u, sc in enumerate(scores):
                s_ref[u] = sc

        def put_probs(probs):
            for u, (pb, top) in enumerate(probs):
                p_ref[u] = pb
                st_ref[u] = top

        put_probs(softmax_stage(masked_scores(blocks_of(0))))
        put_scores(masked_scores(blocks_of(min(1, n_iter - 1))))

        def body(it, carry, n_iter=n_iter, blocks_of=blocks_of, masked_scores=masked_scores,
                 softmax_stage=softmax_stage, output_stage=output_stage, put_scores=put_scores, put_probs=put_probs):
            scores = [s_ref[u] for u in range(DA_UNROLL)]
            probs = [(p_ref[u], st_ref[u]) for u in range(DA_UNROLL)]
            output_stage(blocks_of(it), probs)
            next_probs = softmax_stage(scores)
            next_scores = masked_scores(blocks_of(jnp.minimum(it + 2, n_iter - 1)))
            put_scores(next_scores)
            put_probs(next_probs)
            return carry

        lax.fori_loop(0, n_iter, body, 0)

    for r in range(res):
        rows = slice(r * per, (r + 1) * per)
        lses = [lse_ref[br, rows, :] for br in range(len(DA_BRANCHES))]
        top = functools.reduce(jnp.maximum, lses)
        wts = [jnp.exp2(l - top) for l in lses]
        num = sum(w * ob_ref[br, rows, :] for br, w in enumerate(wts))
        o_ref[0, pl.ds(r, per, stride=res), :] = num / sum(wts)


def _dilated_attention(qkv3, batch, seq):
    pairs = DA_WIDTH // LANES
    n_br = len(DA_BRANCHES)
    spec = lambda off: pl.BlockSpec((1, seq, LANES), lambda b, p, off=off: (b, 0, off + p))
    return pl.pallas_call(
        functools.partial(_dilated_kernel, seq=seq),
        grid=(batch, pairs),
        in_specs=[spec(0), spec(pairs), spec(2 * pairs)],
        out_specs=pl.BlockSpec((1, seq, LANES), lambda b, p: (b, 0, p)),
        out_shape=jax.ShapeDtypeStruct((batch, seq, DA_WIDTH), f32),
        scratch_shapes=[pltpu.VMEM((seq, LANES), f32)] * 3
        + [pltpu.VMEM((n_br, seq, LANES), f32)] * 2
        + [pltpu.VMEM((DA_UNROLL, 2 * DA_BLOCK, 2 * DA_BLOCK), f32),
           pltpu.VMEM((DA_UNROLL, 2 * DA_BLOCK, 2 * DA_BLOCK), bf16),
           pltpu.VMEM((DA_UNROLL, DA_BLOCK, LANES), f32)],
        compiler_params=_params(("parallel", "parallel")),
        name="dilated_attention",
    )(qkv3, qkv3, qkv3)


def _shortconv_kernel(h_ref, win_ref, cw_ref, wout_ref, g_ref, b_ref, o_ref, tail_ref):
    @pl.when(pl.program_id(1) == 0)
    def _():
        tail_ref[...] = jnp.zeros_like(tail_ref)

    tm = h_ref.shape[0]
    d = D_MODEL
    h = h_ref[...]
    hb = h.astype(bf16)
    gate_b = jnp.dot(hb, win_ref[:, :d], preferred_element_type=f32)
    gate_c = jnp.dot(hb, win_ref[:, d:2 * d], preferred_element_type=f32)
    hid = jnp.dot(hb, win_ref[:, 2 * d:], preferred_element_type=f32)
    u = gate_c * hid
    uu = jnp.concatenate([tail_ref[...], u], axis=0)
    cw = cw_ref[...]
    conv = u * cw[SC_CONV - 1:SC_CONV, :]
    for j in range(SC_CONV - 1):
        back = SC_CONV - 1 - j
        conv = conv + uu[SUBLANES - back:SUBLANES - back + tm, :] * cw[j:j + 1, :]
    tail_ref[...] = u[tm - SUBLANES:, :]
    y = jnp.dot((gate_b * conv).astype(bf16), wout_ref[...], preferred_element_type=f32)
    o_ref[...] = _layer_norm(DEEPNORM_ALPHA * h + y, g_ref[...], b_ref[...])


def _shortconv(h, w_in, conv_w, w_out, g, b, batch, seq):
    tm = SC_TILE
    ns = seq // tm
    row = lambda bb, s: (bb * ns + s, 0)
    fixed = lambda bb, s: (0, 0)
    once = dict(pipeline_mode=pl.Buffered(1))
    return pl.pallas_call(
        _shortconv_kernel,
        grid=(batch, ns),
        in_specs=[
            pl.BlockSpec((tm, D_MODEL), row),
            pl.BlockSpec(w_in.shape, fixed, **once),
            pl.BlockSpec(conv_w.shape, fixed),
            pl.BlockSpec(w_out.shape, fixed, **once),
            pl.BlockSpec((1, D_MODEL), fixed),
            pl.BlockSpec((1, D_MODEL), fixed),
        ],
        out_specs=pl.BlockSpec((tm, D_MODEL), row),
        out_shape=jax.ShapeDtypeStruct((batch * seq, D_MODEL), f32),
        scratch_shapes=[pltpu.VMEM((SUBLANES, D_MODEL), f32)],
        compiler_params=_params(("arbitrary", "arbitrary")),
        name="shortconv",
    )(h, w_in, conv_w, w_out, g, b)


def _route(h2, rwt_ref, rb_ref, cnt_ref, before_ref):
    tm = h2.shape[0]
    w_hi, w_lo = _split(rwt_ref[...], 2)
    x_hi, x_lo = _split(h2, 2)
    nt = (((1,), (1,)), ((), ()))
    logits = (lax.dot_general(w_hi, x_hi, nt, preferred_element_type=f32)
              + lax.dot_general(w_hi, x_lo, nt, preferred_element_type=f32)
              + lax.dot_general(w_lo, x_hi, nt, preferred_element_type=f32))
    mx = jnp.max(logits, axis=0, keepdims=True)
    ex = jnp.exp(logits - mx)
    scores = ex / jnp.sum(ex, axis=0, keepdims=True)
    sel = scores + rb_ref[...]
    best = jnp.full((1, tm), -jnp.inf, f32)
    bucket = jnp.zeros((1, tm), jnp.int32)
    s_a = jnp.zeros((1, tm), f32)
    s_b = jnp.zeros((1, tm), f32)
    for grp in range(N_GROUPS):
        for p, (i, j) in enumerate(_PAIRS):
            a, b = grp * EXPERTS_PER_GROUP + i, grp * EXPERTS_PER_GROUP + j
            ps = sel[a:a + 1, :] + sel[b:b + 1, :]
            upd = ps > best
            best = jnp.where(upd, ps, best)
            bucket = jnp.where(upd, grp * len(_PAIRS) + p, bucket)
            s_a = jnp.where(upd, scores[a:a + 1, :], s_a)
            s_b = jnp.where(upd, scores[b:b + 1, :], s_b)
    denom = s_a + s_b
    gate_rows = jnp.concatenate([s_a / denom, s_b / denom, jnp.zeros((GATE_LANES - 2, tm), f32)], axis=0)
    gates = jnp.transpose(gate_rows)
    onehot = jnp.where(lax.broadcasted_iota(jnp.int32, (BUCKET_ROWS, tm), 0) == bucket, 1.0, 0.0)
    prefix = jnp.dot(onehot.astype(bf16), before_ref[...], preferred_element_type=f32)
    cnt = cnt_ref[...]
    rank = jnp.sum(onehot * (prefix + cnt[:, 0:1]), axis=0, keepdims=True)
    cnt_ref[...] = cnt + jnp.sum(onehot, axis=1, keepdims=True)
    return gates, bucket, rank.astype(jnp.int32)


def _xattn_kernel(*refs, mix):
    if mix:
        (oa_ref, ob_ref, wmix_ref, g0_ref, b0_ref), refs = refs[:5], refs[5:]
    (h_ref, mem_ref, wkv_ref, wq_ref, wo_ref, g_ref, b_ref, rwt_ref, rb_ref,
     hext_ref, route_ref, cnt_out_ref, cnt_ref, kv_ref, before_ref) = refs

    @pl.when(jnp.logical_and(pl.program_id(0) == 0, pl.program_id(1) == 0))
    def _():
        cnt_ref[...] = jnp.zeros_like(cnt_ref)
        n = before_ref.shape[0]
        earlier = lax.broadcasted_iota(jnp.int32, (n, n), 0) < lax.broadcasted_iota(jnp.int32, (n, n), 1)
        before_ref[...] = jnp.where(earlier, 1.0, 0.0).astype(bf16)

    @pl.when(pl.program_id(1) == 0)
    def _():
        kv_ref[...] = jnp.dot(mem_ref[...].astype(bf16), wkv_ref[...], preferred_element_type=f32).astype(bf16)

    h = h_ref[...]
    tm = h.shape[0]
    if mix:
        y0 = (jnp.dot(oa_ref[...], wmix_ref[:DN_WIDTH], preferred_element_type=f32)
              + jnp.dot(ob_ref[...].astype(bf16), wmix_ref[DN_WIDTH:], preferred_element_type=f32))
        h = _layer_norm(DEEPNORM_ALPHA * h + y0, g0_ref[...], b0_ref[...])
    q = jnp.dot(h.astype(bf16), wq_ref[...], preferred_element_type=f32).astype(bf16)
    heads = [(hd * XA_HEAD_DIM, (hd + 1) * XA_HEAD_DIM) for hd in range(XA_HEADS)]
    scores = [_dot_nt(q[:, lo:hi], kv_ref[:, lo:hi]) for lo, hi in heads]
    probs = [jnp.exp2(s - jnp.max(s, -1, keepdims=True)) for s in scores]
    outs = [(_dot(p, kv_ref[:, D_MODEL + lo:D_MODEL + hi]) / jnp.sum(p, -1, keepdims=True)).astype(bf16)
            for p, (lo, hi) in zip(probs, heads)]
    y = jnp.dot(jnp.concatenate(outs, axis=1), wo_ref[...], preferred_element_type=f32)
    h2 = _layer_norm(DEEPNORM_ALPHA * h + y, g_ref[...], b_ref[...])
    gates, bucket, rank = _route(h2, rwt_ref, rb_ref, cnt_ref, before_ref)
    hext_ref[:, :D_MODEL] = h2
    hext_ref[:, D_MODEL:] = gates
    route_ref[...] = jnp.concatenate([bucket, rank, jnp.zeros((SUBLANES - 2, tm), jnp.int32)], axis=0)
    cnt_out_ref[...] = cnt_ref[...]


def _xattn_route(h, mem, w_kv, w_q, w_o, g, b, rwt, rb, batch, seq, n_mem, mix=None):
    tm = XA_TILE
    ns = seq // tm
    t = batch * seq
    row = lambda bb, s: (bb * ns + s, 0)
    fixed = lambda bb, s: (0, 0)
    once = dict(pipeline_mode=pl.Buffered(1))
    mix_specs, mix_args = [], []
    if mix is not None:
        o_a, o_b, w_mix, g0, b0 = mix
        mix_specs = [pl.BlockSpec((tm, DN_WIDTH), row), pl.BlockSpec((tm, DA_WIDTH), row),
                     pl.BlockSpec(w_mix.shape, fixed, **once),
                     pl.BlockSpec((1, D_MODEL), fixed), pl.BlockSpec((1, D_MODEL), fixed)]
        mix_args = [o_a, o_b, w_mix, g0, b0]
    return pl.pallas_call(
        functools.partial(_xattn_kernel, mix=mix is not None),
        grid=(batch, ns),
        in_specs=mix_specs + [
            pl.BlockSpec((tm, D_MODEL), row),
            pl.BlockSpec((n_mem, D_MODEL), lambda bb, s: (bb, 0)),
            pl.BlockSpec(w_kv.shape, fixed, **once),
            pl.BlockSpec(w_q.shape, fixed, **once),
            pl.BlockSpec(w_o.shape, fixed, **once),
            pl.BlockSpec((1, D_MODEL), fixed),
            pl.BlockSpec((1, D_MODEL), fixed),
            pl.BlockSpec(rwt.shape, fixed),
            pl.BlockSpec(rb.shape, fixed),
        ],
        out_specs=[
            pl.BlockSpec((tm, ROW_EXT), row),
            pl.BlockSpec((SUBLANES, tm), lambda bb, s: (0, bb * ns + s)),
            pl.BlockSpec((BUCKET_ROWS, LANES), fixed),
        ],
        out_shape=[
            jax.ShapeDtypeStruct((t, ROW_EXT), f32),
            jax.ShapeDtypeStruct((SUBLANES, t), jnp.int32),
            jax.ShapeDtypeStruct((BUCKET_ROWS, LANES), f32),
        ],
        scratch_shapes=[pltpu.VMEM((BUCKET_ROWS, LANES), f32), pltpu.VMEM((n_mem, 2 * D_MODEL), bf16),
                        pltpu.VMEM((tm, tm), bf16)],
        compiler_params=_params(("arbitrary", "arbitrary")),
        name="xattn_route",
    )(*mix_args, h, mem, w_kv, w_q, w_o, g, b, rwt, rb)


def _dispatch_kernel(dest_ref, fill_ref, hext_ref, xs_ref, inv_ref, zeros_ref, sem, zsem):
    base = pl.program_id(0) * DISPATCH_CHUNK
    n_tiles = xs_ref.shape[0] // MOE_TILE

    @pl.when(pl.program_id(0) == 0)
    def _():
        zeros_ref[...] = jnp.zeros_like(zeros_ref)

        def fill(tile):
            return pltpu.make_async_copy(zeros_ref, xs_ref.at[pl.ds(tile * MOE_TILE, MOE_TILE)], zsem)

        for tile in range(n_tiles):
            @pl.when(fill_ref[tile] > 0)
            def _(tile=tile):
                fill(tile).start()

        def clear(j8, carry):
            for u in range(8):
                inv_ref[j8 * 8 + u] = 0
            return carry

        lax.fori_loop(0, inv_ref.shape[0] // 8, clear, 0)
        for tile in range(n_tiles):
            @pl.when(fill_ref[tile] > 0)
            def _(tile=tile):
                fill(tile).wait()

    for j in range(DISPATCH_CHUNK):
        d = dest_ref[base + j]
        inv_ref[d] = base + j
        pltpu.make_async_copy(hext_ref.at[pl.ds(j, 1)], xs_ref.at[pl.ds(d, 1)], sem).start(priority=j % 2)
    pltpu.make_async_copy(hext_ref, xs_ref.at[pl.ds(0, DISPATCH_CHUNK)], sem).wait()


def _dispatch(dest, fill, hext, n_rows):
    t = hext.shape[0]
    return pl.pallas_call(
        _dispatch_kernel,
        grid_spec=pltpu.PrefetchScalarGridSpec(
            num_scalar_prefetch=2,
            grid=(t // DISPATCH_CHUNK,),
            in_specs=[pl.BlockSpec((DISPATCH_CHUNK, ROW_EXT), lambda i, *_: (i, 0))],
            out_specs=[pl.BlockSpec(memory_space=pl.ANY), pl.BlockSpec(memory_space=pltpu.SMEM)],
            scratch_shapes=[pltpu.VMEM((MOE_TILE, ROW_EXT), f32), pltpu.SemaphoreType.DMA(()),
                            pltpu.SemaphoreType.DMA(())],
        ),
        out_shape=[jax.ShapeDtypeStruct((n_rows, ROW_EXT), f32), jax.ShapeDtypeStruct((n_rows,), jnp.int32)],
        compiler_params=_params(("arbitrary",)),
        name="moe_dispatch",
    )(dest, fill, hext)


def _experts_kernel(ea_ref, eb_ref, nvalid_ref, xs_idx_ref, inv_ref, xs_ref, wga_ref, wua_ref, wda_ref,
                    wgb_ref, wub_ref, wdb_ref, g_ref, b_ref, out_ref, obuf_ref, sem_ref):
    i = pl.program_id(0)
    tm = MOE_TILE
    prev = jnp.maximum(i - 1, 0)
    n_prev = jnp.where(i > 0, nvalid_ref[prev], 0)
    slot_prev = prev % 2

    def send(j):
        return pltpu.make_async_copy(obuf_ref.at[slot_prev, pl.ds(j, 1)],
                                     out_ref.at[pl.ds(inv_ref[prev * tm + j], 1)], sem_ref.at[slot_prev])

    n_cur = jnp.where(i < pl.num_programs(0) - 1, nvalid_ref[jnp.minimum(i, pl.num_programs(0) - 2)], 0)

    def send_full(lo, hi):
        for j in range(lo, hi):
            send(j).start(priority=j % 2)

    def compute(between):
        x32 = xs_ref[:, :D_MODEL]
        x = x32.astype(bf16)
        gates = xs_ref[:, D_MODEL:]

        def hidden(wg_ref, wu_ref, gate):
            hg = jnp.dot(x, wg_ref[0, 0].astype(bf16), preferred_element_type=f32)
            hu = jnp.dot(x, wu_ref[0, 0].astype(bf16), preferred_element_type=f32)
            return (_silu(hg) * hu * gate).astype(bf16)

        between(0)
        h_a = hidden(wga_ref, wua_ref, gates[:, 0:1])
        between(1)
        h_b = hidden(wgb_ref, wub_ref, gates[:, 1:2])
        between(2)
        y = (jnp.dot(h_a, wda_ref[0, 0].astype(bf16), preferred_element_type=f32)
             + jnp.dot(h_b, wdb_ref[0, 0].astype(bf16), preferred_element_type=f32))
        between(3)
        obuf_ref[i % 2] = _layer_norm(DEEPNORM_ALPHA * x32 + y, g_ref[...], b_ref[...])

    usual = jnp.logical_and(n_prev == tm, n_cur > 0)
    quarter = tm // 4

    @pl.when(usual)
    def _():
        compute(lambda k: send_full(k * quarter, (k + 1) * quarter))

    @pl.when(jnp.logical_not(usual))
    def _():
        @pl.when(n_prev == tm)
        def _():
            send_full(0, tm)

        @pl.when(jnp.logical_and(n_prev > 0, n_prev < tm))
        def _():
            for j in range(tm):
                @pl.when(j < n_prev)
                def _(j=j):
                    send(j).start(priority=j % 2)

        @pl.when(n_cur > 0)
        def _():
            compute(lambda k: None)

    for bit in reversed(range(tm.bit_length())):
        size = 1 << bit

        @pl.when((n_prev & size) != 0)
        def _(size=size):
            pltpu.make_async_copy(obuf_ref.at[slot_prev, pl.ds(0, size)], out_ref.at[pl.ds(0, size)],
                                  sem_ref.at[slot_prev]).wait()


def _experts(layer, ea, eb, nvalid, xs_idx, inv, xs, w_gate, w_up, w_down, g, b, t):
    n_tiles = xs.shape[0] // MOE_TILE
    up_a = pl.BlockSpec((1, 1, D_MODEL, D_EXPERT), lambda i, ea, *_: (layer, ea[i], 0, 0))
    up_b = pl.BlockSpec((1, 1, D_MODEL, D_EXPERT), lambda i, ea, eb, *_: (layer, eb[i], 0, 0))
    dn_a = pl.BlockSpec((1, 1, D_EXPERT, D_MODEL), lambda i, ea, *_: (layer, ea[i], 0, 0))
    dn_b = pl.BlockSpec((1, 1, D_EXPERT, D_MODEL), lambda i, ea, eb, *_: (layer, eb[i], 0, 0))
    fixed = lambda i, *_: (0, 0)
    return pl.pallas_call(
        _experts_kernel,
        grid_spec=pltpu.PrefetchScalarGridSpec(
            num_scalar_prefetch=5,
            grid=(n_tiles + 1,),
            in_specs=[pl.BlockSpec((MOE_TILE, ROW_EXT), lambda i, ea, eb, nv, xs_idx, inv: (xs_idx[i], 0)),
                      up_a, up_a, dn_a, up_b, up_b, dn_b,
                      pl.BlockSpec((1, D_MODEL), fixed), pl.BlockSpec((1, D_MODEL), fixed)],
            out_specs=pl.BlockSpec(memory_space=pl.ANY),
            scratch_shapes=[pltpu.VMEM((2, MOE_TILE, D_MODEL), f32), pltpu.SemaphoreType.DMA((2,))],
        ),
        out_shape=jax.ShapeDtypeStruct((t, D_MODEL), f32),
        compiler_params=_params(("arbitrary",)),
        name="moe_experts",
    )(ea, eb, nvalid, xs_idx, inv, xs, w_gate, w_up, w_down, w_gate, w_up, w_down, g, b)


def _moe_plan(route, counts):
    t = route.shape[1]
    n_rows = t + N_BUCKETS * MOE_TILE
    n_tiles = n_rows // MOE_TILE
    cnt = counts[:N_BUCKETS, 0].astype(jnp.int32)
    padded = (cnt + MOE_TILE - 1) // MOE_TILE * MOE_TILE
    ends = jnp.cumsum(padded)
    starts = ends - padded
    buckets = jnp.arange(N_BUCKETS, dtype=jnp.int32)
    dest = jnp.sum(jnp.where(route[0][None, :] == buckets[:, None], starts[:, None], 0), axis=0) + route[1]
    tile_start = jnp.arange(n_tiles + 1, dtype=jnp.int32) * MOE_TILE
    tile_bucket = jnp.sum(tile_start[:, None] >= ends[None, :], axis=1).astype(jnp.int32)
    used = tile_bucket < N_BUCKETS
    tile_bucket = jnp.minimum(tile_bucket, N_BUCKETS - 1)
    nvalid = jnp.where(used, jnp.clip(starts[tile_bucket] + cnt[tile_bucket] - tile_start, 0, MOE_TILE), 0)
    xs_idx = jnp.minimum(jnp.arange(n_tiles + 1, dtype=jnp.int32), jnp.maximum(ends[-1] // MOE_TILE - 1, 0))
    last_bucket = tile_bucket[jnp.maximum(ends[-1] // MOE_TILE - 1, 0)]
    tile_bucket = jnp.where(used, tile_bucket, last_bucket)
    pair_a = jnp.array([g * EXPERTS_PER_GROUP + i for g in range(N_GROUPS) for i, _ in _PAIRS], jnp.int32)
    pair_b = jnp.array([g * EXPERTS_PER_GROUP + j for g in range(N_GROUPS) for _, j in _PAIRS], jnp.int32)
    tiles = jnp.arange(n_tiles, dtype=jnp.int32)
    is_last = jnp.any(jnp.logical_and((tiles[:, None] + 1) * MOE_TILE == ends[None, :], padded[None, :] > 0), axis=1)
    fill = jnp.logical_or(is_last, tiles * MOE_TILE >= ends[-1]).astype(jnp.int32)
    return dict(dest=dest, ea=pair_a[tile_bucket], eb=pair_b[tile_bucket], nvalid=nvalid.astype(jnp.int32),
                xs_idx=xs_idx, fill=fill, n_rows=n_rows)


def _moe(layer, hext, route, counts, w_gate, w_up, w_down, g, b):
    plan = _moe_plan(route, counts)
    xs, inv = _dispatch(plan["dest"], plan["fill"], hext, plan["n_rows"])
    return _experts(layer, plan["ea"], plan["eb"], plan["nvalid"], plan["xs_idx"], inv, xs,
                    w_gate, w_up, w_down, g, b, hext.shape[0])


def kernel(x, mem, ab_w_in, ab_conv_w, ab_a_log, ab_dt_bias, ab_norm_w, ab_w_out, sc_w_in, sc_conv_w, sc_w_out, xa_w_q, xa_w_kv, xa_w_o, router_w, router_b, moe_w_gate, moe_w_up, moe_w_down, ln_g, ln_b):
    batch, seq, d = x.shape
    n_mem = mem.shape[1]
    t = batch * seq
    h = x.reshape(t, d)
    mem2 = mem.reshape(batch * n_mem, d)
    rwt = jnp.transpose(router_w)
    rb = router_b.reshape(N_EXPERTS, 1)
    row = lambda v: v.reshape(1, -1)

    for layer in range(DEPTH):
        i = layer // 2
        if layer % 2 == 0:
            w_in = ab_w_in[i]
            c0, c1, c2 = 3 * DN_WIDTH, 4 * DN_WIDTH, 4 * DN_WIDTH + 2 * DN_HEADS
            w_ba = jnp.pad(w_in[:, c1:c2], ((0, 0), (0, LANES - 2 * DN_HEADS)))
            lane_pad = lambda v: jnp.pad(v.reshape(1, DN_HEADS), ((0, 0), (DN_HEADS, LANES - 2 * DN_HEADS)))
            act, z, bg, qkv_d = _project_ab(h, w_in[:, :c0].astype(bf16), w_in[:, c0:c1].astype(bf16), w_ba.astype(bf16),
                                            w_in[:, c2:].astype(bf16), ab_conv_w[i], lane_pad(ab_a_log[i]),
                                            lane_pad(ab_dt_bias[i]), seq)
            o_a = _deltanet(act, bg, z, row(ab_norm_w[i]), batch, seq)
            o_b = _dilated_attention(qkv_d.reshape(batch, seq, 3 * DA_WIDTH), batch, seq).reshape(t, DA_WIDTH)
            mix = (o_a, o_b, ab_w_out[i].astype(bf16), row(ln_g[layer, 0]), row(ln_b[layer, 0]))
        else:
            h = _shortconv(h, sc_w_in[i].astype(bf16), sc_conv_w[i], sc_w_out[i].astype(bf16),
                           row(ln_g[layer, 0]), row(ln_b[layer, 0]), batch, seq)
            mix = None
        w_q = (xa_w_q[layer] * (XA_HEAD_DIM ** -0.5 * LOG2_E)).astype(bf16)
        hext, route, counts = _xattn_route(h, mem2, xa_w_kv[layer].astype(bf16), w_q, xa_w_o[layer].astype(bf16),
                                           row(ln_g[layer, 1]), row(ln_b[layer, 1]), rwt, rb, batch, seq, n_mem, mix)
        h = _moe(layer, hext, route, counts, moe_w_gate, moe_w_up, moe_w_down,
                 row(ln_g[layer, 2]), row(ln_b[layer, 2]))
    return h.reshape(batch, seq, d)
```
